```python
import math
import jax
import jax.numpy as jnp
from jax import lax
import numpy as np

D_MODEL = 1024
BATCH = 8
SEQ = 2048
DEPTH = 4
DEC_BATCH = 128
DEC_SEQ = 4
PAST_LEN = 16384
PAGE_SIZE = 128

N_MIXERS = 3
N_S5_LAYERS = (DEPTH + 2) // 3
N_LRU_LAYERS = (DEPTH + 1) // 3
N_GDN_LAYERS = DEPTH // 3
RMS_EPS = 1e-6
L2_EPS = 1e-6
S5_WIDTH = D_MODEL
S5_GROUP_CH = 16
S5_GROUPS = S5_WIDTH // S5_GROUP_CH
S5_STATE = 64
S5_SCAN_CHUNK = 128
LRU_BLOCK = 128
LRU_WIDTH = (4 * D_MODEL // 3) // LRU_BLOCK * LRU_BLOCK
LRU_BLOCKS = LRU_WIDTH // LRU_BLOCK
LRU_C = 8.0
CONV_WIDTH = 4
GDN_DK = 128
GDN_DV = 128
GDN_HEADS = D_MODEL // 128
GDN_KEY_DIM = GDN_HEADS * GDN_DK
GDN_VAL_DIM = GDN_HEADS * GDN_DV
GDN_CONV_DIM = 2 * GDN_KEY_DIM + GDN_VAL_DIM
GDN_PROJ_DIM = GDN_CONV_DIM + GDN_VAL_DIM + 2 * GDN_HEADS
GDN_CHUNK = 64
FFN_HIDDEN = ((8 * D_MODEL // 3 + 127) // 128) * 128
FFN_CONV_WIDTH = 3

kernel_name = 'hybrid_s5_rglru_gdn_convffn_step'


def rmsnorm(x, g):
    xf = x.astype(jnp.float32)
    y = xf * lax.rsqrt(jnp.mean(xf * xf, axis=-1, keepdims=True) + RMS_EPS)
    return (y * g.astype(jnp.float32)).astype(x.dtype)


def l2norm(x):
    xf = x.astype(jnp.float32)
    return xf * lax.rsqrt(jnp.sum(xf * xf, axis=-1, keepdims=True) + L2_EPS)


def causal_dwconv(x, buf, w, b=None):
    width, seq = w.shape[0], x.shape[1]
    xp = jnp.concatenate([buf.astype(x.dtype), x], axis=1)
    y = xp[:, 0:seq] * w[0]
    for k in range(1, width):
        y = y + xp[:, k:k + seq] * w[k]
    if b is not None:
        y = y + b
    return y, xp[:, seq:]


def linear_scan(a, b, h0):
    b = b.at[:, 0].add(a[:, 0] * h0)

    def combine(left, right):
        a_l, b_l = left
        a_r, b_r = right
        return a_r * a_l, a_r * b_l + b_r

    _, h = lax.associative_scan(combine, (a, b), axis=1)
    return h


def s5_mixer(x, h0_re, h0_im, w_in, a_re, a_im, log_dt, b_re, b_im, c_re, c_im, d, w_glu):
    bsz, seq, _ = x.shape
    f32 = jnp.float32
    u = (x @ w_in).astype(f32)
    lam = lax.complex(a_re.astype(f32), a_im.astype(f32))
    dt = jnp.exp(log_dt.astype(f32))[:, None]
    a_bar = jnp.exp(lam * dt)
    b_bar = ((a_bar - 1.0) / lam)[..., None] * lax.complex(b_re.astype(f32), b_im.astype(f32))
    c_mat = lax.complex(c_re.astype(f32), c_im.astype(f32))
    h0 = lax.complex(h0_re.astype(f32), h0_im.astype(f32))
    chunk = math.gcd(seq, S5_SCAN_CHUNK)
    n_chunks = seq // chunk
    u_chunks = u.reshape(bsz, n_chunks, chunk, S5_GROUPS, S5_GROUP_CH).transpose(1, 0, 2, 3, 4)

    def step(h, u_c):
        bu = jnp.einsum('blgc,gpc->blgp', u_c.astype(jnp.complex64), b_bar)
        hs = linear_scan(jnp.broadcast_to(a_bar, bu.shape), bu, h)
        y_c = jnp.einsum('blgp,gcp->blgc', hs, c_mat).real
        return hs[:, -1], y_c

    h_last, ys = lax.scan(step, h0, u_chunks)
    y = ys.transpose(1, 0, 2, 3, 4).reshape(bsz, seq, S5_WIDTH) + d.astype(f32) * u
    y = jax.nn.gelu(y).astype(x.dtype)
    val, gate = jnp.split(y @ w_glu, 2, axis=-1)
    return val * jax.nn.sigmoid(gate), jnp.real(h_last), jnp.imag(h_last)


def rglru_mixer(x, h0, conv_buf, w_in, conv_w, conv_b, w_ga, b_ga, w_gx, b_gx, lam, w_out):
    bsz, seq, _ = x.shape
    f32 = jnp.float32
    gate_br, x_br = jnp.split(x @ w_in, 2, axis=-1)
    xc, new_buf = causal_dwconv(x_br, conv_buf, conv_w, conv_b)
    xb = xc.reshape(bsz, seq, LRU_BLOCKS, LRU_BLOCK)
    r = jax.nn.sigmoid(jnp.einsum('blnd,nde->blne', xb, w_ga).reshape(bsz, seq, LRU_WIDTH).astype(f32) + b_ga.astype(f32))
    i = jax.nn.sigmoid(jnp.einsum('blnd,nde->blne', xb, w_gx).reshape(bsz, seq, LRU_WIDTH).astype(f32) + b_gx.astype(f32))
    log_a = -LRU_C * r * jax.nn.softplus(-lam.astype(f32))
    a = jnp.exp(log_a)
    b = jnp.sqrt(-jnp.expm1(2.0 * log_a)) * i * xc.astype(f32)
    h = linear_scan(a, b, h0.astype(f32))
    y = (jax.nn.gelu(gate_br.astype(f32)) * h).astype(x.dtype)
    return y @ w_out, h[:, -1], new_buf


def chunked_gated_delta(q, k, v, g, beta, s0):
    bsz, seq, heads, dk = q.shape
    dv = v.shape[-1]
    chunk = math.gcd(seq, GDN_CHUNK)
    n_chunks = seq // chunk

    def blocks(t):
        return t.reshape(bsz, n_chunks, chunk, heads, t.shape[-1]).transpose(1, 0, 3, 2, 4)

    def blocks_h(t):
        return t.reshape(bsz, n_chunks, chunk, heads).transpose(1, 0, 3, 2)

    causal = jnp.tril(jnp.ones((chunk, chunk), dtype=bool))
    strict = jnp.tril(jnp.ones((chunk, chunk), dtype=bool), k=-1)
    eye = jnp.eye(chunk, dtype=jnp.float32)

    def step(s, inp):
        q_c, k_c, v_c, g_c, b_c = inp
        cum = jnp.cumsum(g_c, axis=-1)
        decay = jnp.exp(jnp.where(causal, cum[..., :, None] - cum[..., None, :], -jnp.inf))
        k_beta = k_c * b_c[..., None]
        m = jnp.where(strict, jnp.einsum('bhid,bhjd->bhij', k_beta, k_c) * decay, 0.0)
        rhs = jnp.concatenate([v_c * b_c[..., None], k_beta * jnp.exp(cum)[..., None]], axis=-1)
        sol = lax.linalg.triangular_solve(m + eye, rhs, left_side=True, lower=True, unit_diagonal=True)
        u_c, w_c = sol[..., :dv], sol[..., dv:]
        v_new = u_c - jnp.einsum('bhck,bhkv->bhcv', w_c, s)
        attn = jnp.einsum('bhik,bhjk->bhij', q_c, k_c) * decay
        o_c = jnp.einsum('bhck,bhkv->bhcv', q_c * jnp.exp(cum)[..., None], s) + jnp.einsum('bhij,bhjv->bhiv', attn, v_new)
        g_last = cum[..., -1:]
        s_new = s * jnp.exp(g_last)[..., None] + jnp.einsum('bhck,bhcv->bhkv', k_c * jnp.exp(g_last - cum)[..., None], v_new)
        return s_new, o_c

    s_last, o = lax.scan(step, s0, (blocks(q), blocks(k), blocks(v), blocks_h(g), blocks_h(beta)))
    return o.transpose(1, 0, 3, 2, 4).reshape(bsz, seq, heads, dv), s_last


def gdn_mixer(x, s0, conv_buf, w_in, conv_w, a_log, dt_bias, norm_w, w_out):
    bsz, seq, _ = x.shape
    f32 = jnp.float32
    proj = x @ w_in
    qkv, z, a_in, b_in = jnp.split(proj, [GDN_CONV_DIM, GDN_CONV_DIM + GDN_VAL_DIM, GDN_CONV_DIM + GDN_VAL_DIM + GDN_HEADS], axis=-1)
    qkv, new_buf = causal_dwconv(qkv, conv_buf, conv_w)
    qkv = jax.nn.silu(qkv)
    q, k, v = jnp.split(qkv, [GDN_KEY_DIM, 2 * GDN_KEY_DIM], axis=-1)
    q = l2norm(q.reshape(bsz, seq, GDN_HEADS, GDN_DK)) * (GDN_DK ** -0.5)
    k = l2norm(k.reshape(bsz, seq, GDN_HEADS, GDN_DK))
    v = v.reshape(bsz, seq, GDN_HEADS, GDN_DV).astype(f32)
    beta = jax.nn.sigmoid(b_in.astype(f32))
    g = -jnp.exp(a_log.astype(f32)) * jax.nn.softplus(a_in.astype(f32) + dt_bias.astype(f32))
    o, s_last = chunked_gated_delta(q, k, v, g, beta, s0.astype(f32))
    o = rmsnorm(o, norm_w) * jax.nn.silu(z.reshape(bsz, seq, GDN_HEADS, GDN_DV).astype(f32))
    return o.reshape(bsz, seq, GDN_VAL_DIM).astype(x.dtype) @ w_out, s_last, new_buf


def conv_ffn(x, buf, w_up, conv_w, conv_b, w_down):
    h, new_buf = causal_dwconv(x @ w_up, buf, conv_w, conv_b)
    a, b = jnp.split(h, 2, axis=-1)
    return (jax.nn.gelu(a) * b) @ w_down, new_buf


def trunk(x, s5_re, s5_im, lru_h, lru_conv, gdn_s, gdn_conv, ffn_conv, p):
    o_s5_re, o_s5_im, o_lru, o_lru_conv, o_gdn, o_gdn_conv, o_ffn_conv = [], [], [], [], [], [], []
    for i in range(DEPTH):
        kind, j = i % N_MIXERS, i // N_MIXERS
        xn = rmsnorm(x, p['norm_mix'][i])
        if kind == 0:
            y, hr, hi = s5_mixer(xn, s5_re[j], s5_im[j], p['s5_w_in'][j], p['s5_a_re'][j], p['s5_a_im'][j],
                                 p['s5_log_dt'][j], p['s5_b_re'][j], p['s5_b_im'][j], p['s5_c_re'][j],
                                 p['s5_c_im'][j], p['s5_d'][j], p['s5_w_glu'][j])
            o_s5_re.append(hr)
            o_s5_im.append(hi)
        elif kind == 1:
            y, h, buf = rglru_mixer(xn, lru_h[j], lru_conv[j], p['lru_w_in'][j], p['lru_conv_w'][j], p['lru_conv_b'][j],
                                    p['lru_w_gate_a'][j], p['lru_b_gate_a'][j], p['lru_w_gate_x'][j],
                                    p['lru_b_gate_x'][j], p['lru_lambda'][j], p['lru_w_out'][j])
            o_lru.append(h)
            o_lru_conv.append(buf)
        else:
            y, s, buf = gdn_mixer(xn, gdn_s[j], gdn_conv[j], p['gdn_w_in'][j], p['gdn_conv_w'][j], p['gdn_a_log'][j],
                                  p['gdn_dt_bias'][j], p['gdn_norm'][j], p['gdn_w_out'][j])
            o_gdn.append(s)
            o_gdn_conv.append(buf)
        x = x + y.astype(x.dtype)
        xn = rmsnorm(x, p['norm_ffn'][i])
        y, buf = conv_ffn(xn, ffn_conv[i], p['ffn_w_up'][i], p['ffn_conv_w'][i], p['ffn_conv_b'][i], p['ffn_w_down'][i])
        o_ffn_conv.append(buf)
        x = x + y.astype(x.dtype)
    x = rmsnorm(x, p['norm_final'])
    return (x, jnp.stack(o_s5_re), jnp.stack(o_s5_im), jnp.stack(o_lru), jnp.stack(o_lru_conv),
            jnp.stack(o_gdn), jnp.stack(o_gdn_conv), jnp.stack(o_ffn_conv))


def setup_inputs(seed: int = 0) -> dict:
    key = jax.random.key(seed)
    keys = jax.random.split(key, 64)
    counter = [0]

    def nk():
        counter[0] += 1
        return keys[counter[0] - 1]

    def nrm(shape, scale):
        return scale * jax.random.normal(nk(), shape, jnp.float32)

    def unif(shape, lo, hi):
        return jax.random.uniform(nk(), shape, jnp.float32, lo, hi)

    f2 = 2 * FFN_HIDDEN
    lru_s = unif((N_LRU_LAYERS, LRU_WIDTH), 0.9, 0.999) ** (1.0 / LRU_C)
    gdn_dt = jnp.exp(unif((N_GDN_LAYERS, GDN_HEADS), math.log(1e-3), math.log(1e-1)))
    return {
        'x_prompt': nrm((BATCH, SEQ, D_MODEL), 1.0),
        'x_sample': nrm((DEC_BATCH, DEC_SEQ, D_MODEL), 1.0),
        'state_s5_re': nrm((N_S5_LAYERS, DEC_BATCH, S5_GROUPS, S5_STATE), 0.1),
        'state_s5_im': nrm((N_S5_LAYERS, DEC_BATCH, S5_GROUPS, S5_STATE), 0.1),
        'state_lru': nrm((N_LRU_LAYERS, DEC_BATCH, LRU_WIDTH), 0.5),
        'state_lru_conv': nrm((N_LRU_LAYERS, DEC_BATCH, CONV_WIDTH - 1, LRU_WIDTH), 1.0),
        'state_gdn': nrm((N_GDN_LAYERS, DEC_BATCH, GDN_HEADS, GDN_DK, GDN_DV), 0.1),
        'state_gdn_conv': nrm((N_GDN_LAYERS, DEC_BATCH, CONV_WIDTH - 1, GDN_CONV_DIM), 1.0),
        'state_ffn_conv': nrm((DEPTH, DEC_BATCH, FFN_CONV_WIDTH - 1, f2), 1.0),
        'norm_mix': 1.0 + nrm((DEPTH, D_MODEL), 0.02),
        'norm_ffn': 1.0 + nrm((DEPTH, D_MODEL), 0.02),
        'norm_final': 1.0 + nrm((D_MODEL,), 0.02),
        's5_w_in': nrm((N_S5_LAYERS, D_MODEL, S5_WIDTH), D_MODEL ** -0.5),
        's5_a_re': -0.5 + nrm((N_S5_LAYERS, S5_GROUPS, S5_STATE), 0.01),
        's5_a_im': jnp.pi * jnp.arange(S5_STATE, dtype=jnp.float32) + nrm((N_S5_LAYERS, S5_GROUPS, S5_STATE), 0.01),
        's5_log_dt': unif((N_S5_LAYERS, S5_GROUPS), math.log(1e-3), math.log(1e-1)),
        's5_b_re': nrm((N_S5_LAYERS, S5_GROUPS, S5_STATE, S5_GROUP_CH), (2 * S5_GROUP_CH) ** -0.5),
        's5_b_im': nrm((N_S5_LAYERS, S5_GROUPS, S5_STATE, S5_GROUP_CH), (2 * S5_GROUP_CH) ** -0.5),
        's5_c_re': nrm((N_S5_LAYERS, S5_GROUPS, S5_GROUP_CH, S5_STATE), S5_STATE ** -0.5),
        's5_c_im': nrm((N_S5_LAYERS, S5_GROUPS, S5_GROUP_CH, S5_STATE), S5_STATE ** -0.5),
        's5_d': nrm((N_S5_LAYERS, S5_WIDTH), 1.0),
        's5_w_glu': nrm((N_S5_LAYERS, S5_WIDTH, 2 * D_MODEL), S5_WIDTH ** -0.5),
        'lru_w_in': nrm((N_LRU_LAYERS, D_MODEL, 2 * LRU_WIDTH), D_MODEL ** -0.5),
        'lru_conv_w': nrm((N_LRU_LAYERS, CONV_WIDTH, LRU_WIDTH), CONV_WIDTH ** -0.5),
        'lru_conv_b': nrm((N_LRU_LAYERS, LRU_WIDTH), 0.01),
        'lru_w_gate_a': nrm((N_LRU_LAYERS, LRU_BLOCKS, LRU_BLOCK, LRU_BLOCK), LRU_BLOCK ** -0.5),
        'lru_b_gate_a': nrm((N_LRU_LAYERS, LRU_WIDTH), 0.01),
        'lru_w_gate_x': nrm((N_LRU_LAYERS, LRU_BLOCKS, LRU_BLOCK, LRU_BLOCK), LRU_BLOCK ** -0.5),
        'lru_b_gate_x': nrm((N_LRU_LAYERS, LRU_WIDTH), 0.01),
        'lru_lambda': jnp.log(lru_s) - jnp.log1p(-lru_s),
        'lru_w_out': nrm((N_LRU_LAYERS, LRU_WIDTH, D_MODEL), LRU_WIDTH ** -0.5),
        'gdn_w_in': nrm((N_GDN_LAYERS, D_MODEL, GDN_PROJ_DIM), D_MODEL ** -0.5),
        'gdn_conv_w': nrm((N_GDN_LAYERS, CONV_WIDTH, GDN_CONV_DIM), CONV_WIDTH ** -0.5),
        'gdn_a_log': jnp.log(unif((N_GDN_LAYERS, GDN_HEADS), 1.0, 16.0)),
        'gdn_dt_bias': gdn_dt + jnp.log(-jnp.expm1(-gdn_dt)),
        'gdn_norm': 1.0 + nrm((N_GDN_LAYERS, GDN_DV), 0.02),
        'gdn_w_out': nrm((N_GDN_LAYERS, GDN_VAL_DIM, D_MODEL), GDN_VAL_DIM ** -0.5),
        'ffn_w_up': nrm((DEPTH, D_MODEL, f2), D_MODEL ** -0.5),
        'ffn_conv_w': nrm((DEPTH, FFN_CONV_WIDTH, f2), FFN_CONV_WIDTH ** -0.5),
        'ffn_conv_b': nrm((DEPTH, f2), 0.01),
        'ffn_w_down': nrm((DEPTH, FFN_HIDDEN, D_MODEL), FFN_HIDDEN ** -0.5),
    }


def reference(x_prompt, x_sample, state_s5_re, state_s5_im, state_lru, state_lru_conv, state_gdn, state_gdn_conv,
              state_ffn_conv, norm_mix, norm_ffn, norm_final, s5_w_in, s5_a_re, s5_a_im, s5_log_dt, s5_b_re, s5_b_im,
              s5_c_re, s5_c_im, s5_d, s5_w_glu, lru_w_in, lru_conv_w, lru_conv_b, lru_w_gate_a, lru_b_gate_a,
              lru_w_gate_x, lru_b_gate_x, lru_lambda, lru_w_out, gdn_w_in, gdn_conv_w, gdn_a_log, gdn_dt_bias,
              gdn_norm, gdn_w_out, ffn_w_up, ffn_conv_w, ffn_conv_b, ffn_w_down):
    p = {
        'norm_mix': norm_mix, 'norm_ffn': norm_ffn, 'norm_final': norm_final,
        's5_w_in': s5_w_in, 's5_a_re': s5_a_re, 's5_a_im': s5_a_im, 's5_log_dt': s5_log_dt,
        's5_b_re': s5_b_re, 's5_b_im': s5_b_im, 's5_c_re': s5_c_re, 's5_c_im': s5_c_im,
        's5_d': s5_d, 's5_w_glu': s5_w_glu,
        'lru_w_in': lru_w_in, 'lru_conv_w': lru_conv_w, 'lru_conv_b': lru_conv_b,
        'lru_w_gate_a': lru_w_gate_a, 'lru_b_gate_a': lru_b_gate_a, 'lru_w_gate_x': lru_w_gate_x,
        'lru_b_gate_x': lru_b_gate_x, 'lru_lambda': lru_lambda, 'lru_w_out': lru_w_out,
        'gdn_w_in': gdn_w_in, 'gdn_conv_w': gdn_conv_w, 'gdn_a_log': gdn_a_log, 'gdn_dt_bias': gdn_dt_bias,
        'gdn_norm': gdn_norm, 'gdn_w_out': gdn_w_out,
        'ffn_w_up': ffn_w_up, 'ffn_conv_w': ffn_conv_w, 'ffn_conv_b': ffn_conv_b, 'ffn_w_down': ffn_w_down,
    }
    bsz = x_prompt.shape[0]
    f32 = jnp.float32
    dt = x_prompt.dtype
    (y_prompt, p_s5_re, p_s5_im, p_lru, p_lru_conv, p_gdn, p_gdn_conv, p_ffn_conv) = trunk(
        x_prompt,
        jnp.zeros((N_S5_LAYERS, bsz, S5_GROUPS, S5_STATE), f32),
        jnp.zeros((N_S5_LAYERS, bsz, S5_GROUPS, S5_STATE), f32),
        jnp.zeros((N_LRU_LAYERS, bsz, LRU_WIDTH), f32),
        jnp.zeros((N_LRU_LAYERS, bsz, CONV_WIDTH - 1, LRU_WIDTH), dt),
        jnp.zeros((N_GDN_LAYERS, bsz, GDN_HEADS, GDN_DK, GDN_DV), f32),
        jnp.zeros((N_GDN_LAYERS, bsz, CONV_WIDTH - 1, GDN_CONV_DIM), dt),
        jnp.zeros((DEPTH, bsz, FFN_CONV_WIDTH - 1, 2 * FFN_HIDDEN), dt),
        p)
    (y_sample, s_s5_re, s_s5_im, s_lru, s_lru_conv, s_gdn, s_gdn_conv, s_ffn_conv) = trunk(
        x_sample, state_s5_re, state_s5_im, state_lru, state_lru_conv, state_gdn, state_gdn_conv,
        state_ffn_conv, p)
    return (y_prompt, y_sample, p_s5_re, p_s5_im, p_lru, p_lru_conv, p_gdn, p_gdn_conv, p_ffn_conv,
            s_s5_re, s_s5_im, s_lru, s_lru_conv, s_gdn, s_gdn_conv, s_ffn_conv)
```

```python
import functools
import math

import jax
import jax.numpy as jnp
from jax import lax
from jax.experimental import pallas as pl
from jax.experimental.pallas import tpu as pltpu

F32 = jnp.float32
BF16 = jnp.bfloat16

D_MODEL = 1024
RMS_EPS = 1e-6
L2_EPS = 1e-6
S5_GROUPS = 64
S5_STATE = 64
S5_GROUP_CH = 16
S5_COLS = S5_GROUPS * S5_STATE
S5_KB = 8
LRU_WIDTH = 1280
LRU_BLOCK = 128
LRU_BLOCKS = LRU_WIDTH // LRU_BLOCK
LRU_C = 8.0
CONV_WIDTH = 4
GDN_HEADS = 8
GDN_DK = 128
GDN_DV = 128
GDN_KEY_DIM = GDN_HEADS * GDN_DK
GDN_CONV_DIM = 3 * GDN_KEY_DIM
GDN_CHUNK = 64
GDN_PROJ_PAD = 4224
FFN_HIDDEN = 2816
FFN_CONV_WIDTH = 3
FFN_TN = 256
VMEM_LIMIT_BYTES = 56 * 1024 * 1024


def _cparams(sem):
    return pltpu.CompilerParams(dimension_semantics=sem, vmem_limit_bytes=VMEM_LIMIT_BYTES)


def _rms(x, g):
    ms = jnp.mean(x * x, axis=-1, keepdims=True)
    return x * lax.rsqrt(ms + RMS_EPS) * g


def _softplus(x):
    return jnp.maximum(x, 0.0) + jnp.log1p(jnp.exp(-jnp.abs(x)))


def _expm1(x):
    u = jnp.exp(x)
    small = jnp.abs(x) < 0.5
    usable = small & (u != 1.0)
    ratio = (u - 1.0) * x / jnp.log(jnp.where(usable, u, 2.0))
    return jnp.where(small, jnp.where(usable, ratio, x), u - 1.0)


def _dot(a, b):
    return jnp.dot(a.astype(BF16), b.astype(BF16), preferred_element_type=F32)


def _dot_nt(a, b):
    return lax.dot_general(a.astype(BF16), b.astype(BF16), (((1,), (1,)), ((), ())),
                           preferred_element_type=F32)


def _split2(a):
    hi = a.astype(BF16)
    lo = (a - hi.astype(F32)).astype(BF16)
    return hi, lo


def _split3(a):
    hi = a.astype(BF16)
    r = a - hi.astype(F32)
    mid = r.astype(BF16)
    lo = (r - mid.astype(F32)).astype(BF16)
    return hi, mid, lo


def _dot3(a, b):
    ah, al = _split2(a)
    bh, bl = _split2(b)
    d = functools.partial(jnp.dot, preferred_element_type=F32)
    return d(ah, bh) + d(al, bh) + d(ah, bl)


def _norm_mm_kernel(x_ref, g_ref, w_ref, o_ref, xn_ref):
    @pl.when(pl.program_id(1) == 0)
    def _():
        xn_ref[...] = _rms(x_ref[...], g_ref[...]).astype(BF16)

    o_ref[...] = jnp.dot(xn_ref[...], w_ref[...], preferred_element_type=F32)


def _norm_mm(x, g, w, tm, tn):
    rows, n = x.shape[0], w.shape[1]
    return pl.pallas_call(
        _norm_mm_kernel,
        grid=(rows // tm, n // tn),
        in_specs=[pl.BlockSpec((tm, D_MODEL), lambda i, j: (i, 0)),
                  pl.BlockSpec((1, D_MODEL), lambda i, j: (0, 0)),
                  pl.BlockSpec((D_MODEL, tn), lambda i, j: (0, j))],
        out_specs=pl.BlockSpec((tm, tn), lambda i, j: (i, j)),
        out_shape=jax.ShapeDtypeStruct((rows, n), F32),
        scratch_shapes=[pltpu.VMEM((tm, D_MODEL), BF16)],
        compiler_params=_cparams(("parallel", "arbitrary")),
        name="norm_mm",
    )(x, g, w)


def _mm_res_kernel(a_ref, w_ref, r_ref, o_ref):
    o_ref[...] = r_ref[...] + jnp.dot(a_ref[...], w_ref[...], preferred_element_type=F32)


def _mm_res(a, w, res, tm, tn):
    rows, k = a.shape
    n = w.shape[1]
    return pl.pallas_call(
        _mm_res_kernel,
        grid=(rows // tm, n // tn),
        in_specs=[pl.BlockSpec((tm, k), lambda i, j: (i, 0)),
                  pl.BlockSpec((k, tn), lambda i, j: (0, j)),
                  pl.BlockSpec((tm, tn), lambda i, j: (i, j))],
        out_specs=pl.BlockSpec((tm, tn), lambda i, j: (i, j)),
        out_shape=jax.ShapeDtypeStruct((rows, n), F32),
        compiler_params=_cparams(("parallel", "parallel")),
        name="mm_res",
    )(a, w, res)


def _mm_glu_res_kernel(a_ref, wv_ref, wg_ref, r_ref, o_ref):
    a = a_ref[...]
    val = jnp.dot(a, wv_ref[...], preferred_element_type=F32)
    gate = jnp.dot(a, wg_ref[...], preferred_element_type=F32)
    o_ref[...] = r_ref[...] + val * jax.nn.sigmoid(gate)


def _mm_glu_res(a, w, res, tm, tn):
    rows, k = a.shape
    n = w.shape[1] // 2
    nj = n // tn
    return pl.pallas_call(
        _mm_glu_res_kernel,
        grid=(rows // tm, nj),
        in_specs=[pl.BlockSpec((tm, k), lambda i, j: (i, 0)),
                  pl.BlockSpec((k, tn), lambda i, j: (0, j)),
                  pl.BlockSpec((k, tn), lambda i, j: (0, nj + j)),
                  pl.BlockSpec((tm, tn), lambda i, j: (i, j))],
        out_specs=pl.BlockSpec((tm, tn), lambda i, j: (i, j)),
        out_shape=jax.ShapeDtypeStruct((rows, n), F32),
        compiler_params=_cparams(("parallel", "parallel")),
        name="mm_glu_res",
    )(a, w, w, res)


def _s5_core_kernel(u_ref, h0re_ref, h0im_ref, bre_ref, bim_ref, cre_ref, cim_ref, are_ref, aim_ref,
                    d_ref, y_ref, hre_out, him_out, hre_s, him_s, *, nb, rows):
    i = pl.program_id(0)

    @pl.when(i == 0)
    def _():
        hre_s[0:nb, :] = h0re_ref[...]
        him_s[0:nb, :] = h0im_ref[...]

    u = u_ref[...]
    ub = u.astype(BF16)
    kw = S5_COLS // S5_KB
    uw = D_MODEL // S5_KB
    for kb in range(S5_KB):
        ukb = ub[:, kb * uw:(kb + 1) * uw]
        hre_s[nb:nb + rows, kb * kw:(kb + 1) * kw] = jnp.dot(ukb, bre_ref[kb], preferred_element_type=F32)
        him_s[nb:nb + rows, kb * kw:(kb + 1) * kw] = jnp.dot(ukb, bim_ref[kb], preferred_element_type=F32)

    are = are_ref[...]
    aim = aim_ref[...]

    def step(t, carry):
        r0 = pl.multiple_of(t * nb, nb)
        r1 = pl.multiple_of(t * nb + nb, nb)
        pr = hre_s[pl.ds(r0, nb), :]
        pi = him_s[pl.ds(r0, nb), :]
        br = hre_s[pl.ds(r1, nb), :]
        bi = him_s[pl.ds(r1, nb), :]
        hre_s[pl.ds(r1, nb), :] = are * pr - aim * pi + br
        him_s[pl.ds(r1, nb), :] = are * pi + aim * pr + bi
        return carry

    lax.fori_loop(0, rows // nb, step, 0)

    for kb in range(S5_KB):
        hr = hre_s[nb:nb + rows, kb * kw:(kb + 1) * kw].astype(BF16)
        hi = him_s[nb:nb + rows, kb * kw:(kb + 1) * kw].astype(BF16)
        yk = (jnp.dot(hr, cre_ref[kb], preferred_element_type=F32)
              - jnp.dot(hi, cim_ref[kb], preferred_element_type=F32))
        yk = yk + d_ref[:, kb * uw:(kb + 1) * uw] * u[:, kb * uw:(kb + 1) * uw]
        y_ref[:, kb * uw:(kb + 1) * uw] = jax.nn.gelu(yk).astype(BF16)

    last_re = hre_s[rows:rows + nb, :]
    last_im = him_s[rows:rows + nb, :]
    hre_s[0:nb, :] = last_re
    him_s[0:nb, :] = last_im
    hre_out[...] = last_re
    him_out[...] = last_im


def _s5_core(u, h0re, h0im, bre, bim, cre, cim, are, aim, d, nb, rows):
    total = u.shape[0]
    full = lambda shape: pl.BlockSpec(shape, lambda i: (0,) * len(shape))
    return pl.pallas_call(
        functools.partial(_s5_core_kernel, nb=nb, rows=rows),
        grid=(total // rows,),
        in_specs=[pl.BlockSpec((rows, D_MODEL), lambda i: (i, 0)),
                  full((nb, S5_COLS)), full((nb, S5_COLS)),
                  full(bre.shape), full(bim.shape), full(cre.shape), full(cim.shape),
                  full((1, S5_COLS)), full((1, S5_COLS)), full((1, D_MODEL))],
        out_specs=[pl.BlockSpec((rows, D_MODEL), lambda i: (i, 0)),
                   full((nb, S5_COLS)), full((nb, S5_COLS))],
        out_shape=[jax.ShapeDtypeStruct((total, D_MODEL), BF16),
                   jax.ShapeDtypeStruct((nb, S5_COLS), F32),
                   jax.ShapeDtypeStruct((nb, S5_COLS), F32)],
        scratch_shapes=[pltpu.VMEM((nb + rows, S5_COLS), F32),
                        pltpu.VMEM((nb + rows, S5_COLS), F32)],
        compiler_params=_cparams(("arbitrary",)),
        name="s5_core",
    )(u, h0re, h0im, bre, bim, cre, cim, are, aim, d)


def _lru_core_kernel(gate_ref, xbr_ref, prev_ref, h0_ref, cw_ref, cb_ref, wga_ref, bga_ref, wgx_ref,
                     bgx_ref, lam_ref, y_ref, hout_ref, cout_ref, xp_s, h_s, a_s, *, nb, rows):
    i = pl.program_id(0)
    hist = (CONV_WIDTH - 1) * nb

    @pl.when(i == 0)
    def _():
        xp_s[0:hist, :] = prev_ref[...]
        h_s[0:nb, :] = h0_ref[...]

    xp_s[hist:hist + rows, :] = xbr_ref[...]
    xc = xp_s[0:rows, :] * cw_ref[0:1, :]
    for k in range(1, CONV_WIDTH):
        xc = xc + xp_s[k * nb:k * nb + rows, :] * cw_ref[k:k + 1, :]
    xc = xc + cb_ref[...]
    c8 = -LRU_C * _softplus(-lam_ref[...])
    for n in range(LRU_BLOCKS):
        sl = slice(n * LRU_BLOCK, (n + 1) * LRU_BLOCK)
        xcn = xc[:, sl]
        xcb = xcn.astype(BF16)
        r = jax.nn.sigmoid(jnp.dot(xcb, wga_ref[n], preferred_element_type=F32) + bga_ref[:, sl])
        ig = jax.nn.sigmoid(jnp.dot(xcb, wgx_ref[n], preferred_element_type=F32) + bgx_ref[:, sl])
        log_a = c8[:, sl] * r
        a_s[:, sl] = jnp.exp(log_a)
        h_s[nb:nb + rows, sl] = jnp.sqrt(-_expm1(2.0 * log_a)) * ig * xcn

    def step(t, carry):
        r0 = pl.multiple_of(t * nb, nb)
        r1 = pl.multiple_of(t * nb + nb, nb)
        h_s[pl.ds(r1, nb), :] = a_s[pl.ds(r0, nb), :] * h_s[pl.ds(r0, nb), :] + h_s[pl.ds(r1, nb), :]
        return carry

    lax.fori_loop(0, rows // nb, step, 0)

    y_ref[...] = (jax.nn.gelu(gate_ref[...]) * h_s[nb:nb + rows, :]).astype(BF16)
    tail = xp_s[rows:rows + hist, :]
    last = h_s[rows:rows + nb, :]
    xp_s[0:hist, :] = tail
    h_s[0:nb, :] = last
    cout_ref[...] = tail
    hout_ref[...] = last


def _lru_core(proj, prev, h0, cw, cb, wga, bga, wgx, bgx, lam, nb, rows):
    total = proj.shape[0]
    hist = (CONV_WIDTH - 1) * nb
    full = lambda shape: pl.BlockSpec(shape, lambda i: (0,) * len(shape))
    return pl.pallas_call(
        functools.partial(_lru_core_kernel, nb=nb, rows=rows),
        grid=(total // rows,),
        in_specs=[pl.BlockSpec((rows, LRU_WIDTH), lambda i: (i, 0)),
                  pl.BlockSpec((rows, LRU_WIDTH), lambda i: (i, 1)),
                  full((hist, LRU_WIDTH)), full((nb, LRU_WIDTH)),
                  full((CONV_WIDTH, LRU_WIDTH)), full((1, LRU_WIDTH)),
                  full(wga.shape), full((1, LRU_WIDTH)), full(wgx.shape), full((1, LRU_WIDTH)),
                  full((1, LRU_WIDTH))],
        out_specs=[pl.BlockSpec((rows, LRU_WIDTH), lambda i: (i, 0)),
                   full((nb, LRU_WIDTH)), full((hist, LRU_WIDTH))],
        out_shape=[jax.ShapeDtypeStruct((total, LRU_WIDTH), BF16),
                   jax.ShapeDtypeStruct((nb, LRU_WIDTH), F32),
                   jax.ShapeDtypeStruct((hist, LRU_WIDTH), F32)],
        scratch_shapes=[pltpu.VMEM((hist + rows, LRU_WIDTH), F32),
                        pltpu.VMEM((nb + rows, LRU_WIDTH), F32),
                        pltpu.VMEM((rows, LRU_WIDTH), F32)],
        compiler_params=_cparams(("arbitrary",)),
        name="lru_core",
    )(proj, proj, prev, h0, cw, cb, wga, bga, wgx, bgx, lam)


def _gdn_prep_kernel(qkv_ref, ab_ref, prev_ref, cw_ref, alog_ref, dtb_ref,
                     q_ref, k_ref, v_ref, g_ref, beta_ref, cout_ref, xp_s, *, nb, rows):
    i = pl.program_id(0)
    hist = (CONV_WIDTH - 1) * nb

    @pl.when(i == 0)
    def _():
        xp_s[0:hist, :] = prev_ref[...]

    xp_s[hist:hist + rows, :] = qkv_ref[...]
    for part, out in enumerate((q_ref, k_ref, v_ref)):
        for h in range(GDN_HEADS):
            lo = part * GDN_KEY_DIM + h * GDN_DK
            sl = slice(lo, lo + GDN_DK)
            acc = xp_s[0:rows, sl] * cw_ref[0:1, sl]
            for k in range(1, CONV_WIDTH):
                acc = acc + xp_s[k * nb:k * nb + rows, sl] * cw_ref[k:k + 1, sl]
            s = acc * jax.nn.sigmoid(acc)
            if part < 2:
                s = s * lax.rsqrt(jnp.sum(s * s, axis=-1, keepdims=True) + L2_EPS)
            if part == 0:
                s = s * (GDN_DK ** -0.5)
            out[:, h * GDN_DK:(h + 1) * GDN_DK] = s
    ab = ab_ref[...]
    g_ref[...] = -jnp.exp(alog_ref[...]) * _softplus(ab + dtb_ref[...])
    beta_ref[...] = jax.nn.sigmoid(ab)
    tail = xp_s[rows:rows + hist, :]
    xp_s[0:hist, :] = tail
    cout_ref[...] = tail


def _gdn_prep(proj, prev, cw, alog, dtb, nb, rows):
    total = proj.shape[0]
    hist = (CONV_WIDTH - 1) * nb
    full = lambda shape: pl.BlockSpec(shape, lambda i: (0,) * len(shape))
    tile = lambda n: pl.BlockSpec((rows, n), lambda i: (i, 0))
    return pl.pallas_call(
        functools.partial(_gdn_prep_kernel, nb=nb, rows=rows),
        grid=(total // rows,),
        in_specs=[tile(GDN_CONV_DIM),
                  pl.BlockSpec((rows, 128), lambda i: (i, (GDN_CONV_DIM + GDN_KEY_DIM) // 128)),
                  full((hist, GDN_CONV_DIM)), full((CONV_WIDTH, GDN_CONV_DIM)),
                  full((1, 128)), full((1, 128))],
        out_specs=[tile(GDN_KEY_DIM), tile(GDN_KEY_DIM), tile(GDN_KEY_DIM), tile(128), tile(128),
                   full((hist, GDN_CONV_DIM))],
        out_shape=[jax.ShapeDtypeStruct((total, GDN_KEY_DIM), F32)] * 3
                  + [jax.ShapeDtypeStruct((total, 128), F32)] * 2
                  + [jax.ShapeDtypeStruct((hist, GDN_CONV_DIM), F32)],
        scratch_shapes=[pltpu.VMEM((hist + rows, GDN_CONV_DIM), F32)],
        compiler_params=_cparams(("arbitrary",)),
        name="gdn_prep",
    )(proj, proj, prev, cw, alog, dtb)


def _unit_lower_inverse(m, ri, ci, chunk):
    eye = (ri == ci).astype(F32)
    rb, cb = ri >> 3, ci >> 3
    n1 = jnp.where(rb == cb, -m, 0.0)
    n2 = _dot3(n1, n1)
    n4 = _dot3(n2, n2)
    t = _dot3(_dot3(eye + n1, eye + n2), eye + n4)
    shift = 3
    while (1 << shift) < chunk:
        pair = ((ri >> (shift + 1)) == (ci >> (shift + 1))) & ((ri >> shift) != (ci >> shift))
        off = jnp.where(pair, m, 0.0)
        t = t - _dot3(_dot3(t, off), t)
        shift += 1
    return t


def _gdn_core_kernel(q_ref, k_ref, v_ref, z_ref, g_ref, beta_ref, s0_ref, nw_ref, o_ref, s_ref, *, chunk):
    c = pl.program_id(1)

    @pl.when(c == 0)
    def _():
        s_ref[...] = s0_ref[...]

    ri = lax.broadcasted_iota(jnp.int32, (chunk, chunk), 0)
    ci = lax.broadcasted_iota(jnp.int32, (chunk, chunk), 1)
    causal = ri >= ci
    strict = ri > ci
    tril = causal.astype(BF16)
    e_r = lax.broadcasted_iota(jnp.int32, (128, 128), 0)
    e_c = lax.broadcasted_iota(jnp.int32, (128, 128), 1)
    eye128 = (e_r == e_c).astype(BF16)
    dotf = functools.partial(jnp.dot, preferred_element_type=F32)

    g = g_ref[...]
    beta = beta_ref[...]
    g3 = _split3(g)
    cum = dotf(tril, g3[0]) + dotf(tril, g3[1]) + dotf(tril, g3[2])
    c3 = _split3(cum)
    nt = lambda a, b: lax.dot_general(a, b, (((1,), (1,)), ((), ())), preferred_element_type=F32)
    cum_t = nt(eye128, c3[0]) + nt(eye128, c3[1]) + nt(eye128, c3[2])
    ecum = jnp.exp(cum)
    g_last = cum[chunk - 1:chunk, :]
    e_last = jnp.exp(g_last)
    e_rest = jnp.exp(g_last - cum)
    nw = nw_ref[...]

    for h in range(GDN_HEADS):
        sl = slice(h * GDN_DK, (h + 1) * GDN_DK)
        qh, kh, vh = q_ref[:, sl], k_ref[:, sl], v_ref[:, sl]
        bh = beta[:, GDN_HEADS + h:GDN_HEADS + h + 1]
        cum_h = cum[:, h:h + 1]
        decay = jnp.exp(jnp.where(causal, cum_h - cum_t[h:h + 1, :], -jnp.inf))
        k_beta = kh * bh
        m = jnp.where(strict, _dot_nt(k_beta, kh) * decay, 0.0)
        t = _unit_lower_inverse(m, ri, ci, chunk)
        u = _dot3(t, vh * bh)
        w = _dot3(t, k_beta * ecum[:, h:h + 1])
        s = s_ref[h]
        v_new = u - _dot(w, s)
        attn = _dot_nt(qh, kh) * decay
        o = _dot(qh * ecum[:, h:h + 1], s) + _dot(attn, v_new)
        k_dec_t = nt(eye128, (kh * e_rest[:, h:h + 1]).astype(BF16))
        s_ref[h] = s * e_last[:, h:h + 1] + _dot(k_dec_t, v_new)
        on = o * lax.rsqrt(jnp.mean(o * o, axis=-1, keepdims=True) + RMS_EPS) * nw
        zh = z_ref[:, sl]
        o_ref[:, sl] = (on * (zh * jax.nn.sigmoid(zh))).astype(BF16)


def _gdn_core(q, k, v, z, g, beta, s0, nw, chunk):
    nb, lp = q.shape[0], q.shape[1]
    seq = lambda n: pl.BlockSpec((None, chunk, n), lambda b, c: (b, c, 0))
    st = pl.BlockSpec((None, GDN_HEADS, GDN_DK, GDN_DV), lambda b, c: (b, 0, 0, 0))
    return pl.pallas_call(
        functools.partial(_gdn_core_kernel, chunk=chunk),
        grid=(nb, lp // chunk),
        in_specs=[seq(GDN_KEY_DIM), seq(GDN_KEY_DIM), seq(GDN_KEY_DIM), seq(GDN_KEY_DIM),
                  seq(128), seq(128), st, pl.BlockSpec((1, GDN_DV), lambda b, c: (0, 0))],
        out_specs=[seq(GDN_KEY_DIM), st],
        out_shape=[jax.ShapeDtypeStruct((nb, lp, GDN_KEY_DIM), BF16),
                   jax.ShapeDtypeStruct(s0.shape, F32)],
        compiler_params=_cparams(("parallel", "arbitrary")),
        name="gdn_core",
    )(q, k, v, z, g, beta, s0, nw)


def _ffn_kernel(x_ref, g_ref, wa_ref, wb_ref, cwa_ref, cwb_ref, cba_ref, cbb_ref, pa_ref, pb_ref, wd_ref,
                gf_ref, o_ref, ca_out, cb_out, xn_s, acc_s, hpa_s, hpb_s, cara_s, carb_s,
                *, nb, tm, final_norm):
    i = pl.program_id(0)
    j = pl.program_id(1)
    hist = (FFN_CONV_WIDTH - 1) * nb

    @pl.when(j == 0)
    def _():
        xn_s[...] = _rms(x_ref[...], g_ref[...]).astype(BF16)
        acc_s[...] = jnp.zeros_like(acc_s)

    @pl.when(i == 0)
    def _():
        hpa_s[0:hist, :] = pa_ref[...]
        hpb_s[0:hist, :] = pb_ref[...]

    @pl.when(i > 0)
    def _():
        hpa_s[0:hist, :] = cara_s[j]
        hpb_s[0:hist, :] = carb_s[j]

    xn = xn_s[...]
    hpa_s[hist:hist + tm, :] = jnp.dot(xn, wa_ref[...], preferred_element_type=F32)
    hpb_s[hist:hist + tm, :] = jnp.dot(xn, wb_ref[...], preferred_element_type=F32)
    tail_a = hpa_s[tm:tm + hist, :]
    tail_b = hpb_s[tm:tm + hist, :]
    cara_s[j] = tail_a
    carb_s[j] = tail_b
    ca_out[...] = tail_a
    cb_out[...] = tail_b

    def conv(hp_s, cw_ref, cb_ref):
        y = hp_s[0:tm, :] * cw_ref[0:1, :]
        for k in range(1, FFN_CONV_WIDTH):
            y = y + hp_s[k * nb:k * nb + tm, :] * cw_ref[k:k + 1, :]
        return y + cb_ref[...]

    act = (jax.nn.gelu(conv(hpa_s, cwa_ref, cba_ref)) * conv(hpb_s, cwb_ref, cbb_ref)).astype(BF16)
    acc_s[...] += jnp.dot(act, wd_ref[...], preferred_element_type=F32)

    @pl.when(j == pl.num_programs(1) - 1)
    def _():
        y = x_ref[...] + acc_s[...]
        if final_norm:
            y = _rms(y, gf_ref[...])
        o_ref[...] = y


def _ffn(x, g, w_up, cw, cb, prev, w_down, g_final, nb, tm, final_norm):
    rows = x.shape[0]
    tn = FFN_TN
    nj = FFN_HIDDEN // tn
    hist = (FFN_CONV_WIDTH - 1) * nb
    col_a = lambda r: pl.BlockSpec((r, tn), lambda i, j: (0, j))
    col_b = lambda r: pl.BlockSpec((r, tn), lambda i, j: (0, nj + j))
    vec = pl.BlockSpec((1, D_MODEL), lambda i, j: (0, 0))
    return pl.pallas_call(
        functools.partial(_ffn_kernel, nb=nb, tm=tm, final_norm=final_norm),
        grid=(rows // tm, nj),
        in_specs=[pl.BlockSpec((tm, D_MODEL), lambda i, j: (i, 0)), vec,
                  col_a(D_MODEL), col_b(D_MODEL),
                  col_a(FFN_CONV_WIDTH), col_b(FFN_CONV_WIDTH), col_a(1), col_b(1),
                  col_a(hist), col_b(hist),
                  pl.BlockSpec((tn, D_MODEL), lambda i, j: (j, 0)), vec],
        out_specs=[pl.BlockSpec((tm, D_MODEL), lambda i, j: (i, 0)),
                   pl.BlockSpec((hist, tn), lambda i, j: (0, j)),
                   pl.BlockSpec((hist, tn), lambda i, j: (0, j))],
        out_shape=[jax.ShapeDtypeStruct((rows, D_MODEL), F32),
                   jax.ShapeDtypeStruct((hist, FFN_HIDDEN), F32),
                   jax.ShapeDtypeStruct((hist, FFN_HIDDEN), F32)],
        scratch_shapes=[pltpu.VMEM((tm, D_MODEL), BF16), pltpu.VMEM((tm, D_MODEL), F32),
                        pltpu.VMEM((hist + tm, tn), F32), pltpu.VMEM((hist + tm, tn), F32),
                        pltpu.VMEM((nj, hist, tn), F32), pltpu.VMEM((nj, hist, tn), F32)],
        compiler_params=_cparams(("arbitrary", "arbitrary")),
        name="conv_ffn",
    )(x, g, w_up, w_up, cw, cw, cb, cb, prev, prev, w_down, g_final)


def _s5_params(a_re, a_im, log_dt, b_re, b_im, c_re, c_im):
    lam = lax.complex(a_re.astype(F32), a_im.astype(F32))
    dt = jnp.exp(log_dt.astype(F32))[:, None]
    a_bar = jnp.exp(lam * dt)
    b_bar = ((a_bar - 1.0) / lam)[..., None] * lax.complex(b_re.astype(F32), b_im.astype(F32))
    eye = jnp.eye(S5_GROUPS // S5_KB, dtype=F32)

    def b_blocks(b):
        b = b.reshape(S5_KB, S5_GROUPS // S5_KB, S5_STATE, S5_GROUP_CH)
        return jnp.einsum('kgpc,gh->kgchp', b, eye).reshape(S5_KB, D_MODEL // S5_KB, S5_COLS // S5_KB).astype(BF16)

    def c_blocks(c):
        c = c.reshape(S5_KB, S5_GROUPS // S5_KB, S5_GROUP_CH, S5_STATE)
        return jnp.einsum('kgcp,gh->kgphc', c, eye).reshape(S5_KB, S5_COLS // S5_KB, D_MODEL // S5_KB).astype(BF16)

    return (b_blocks(jnp.real(b_bar)), b_blocks(jnp.imag(b_bar)),
            c_blocks(c_re.astype(F32)), c_blocks(c_im.astype(F32)),
            jnp.real(a_bar).reshape(1, S5_COLS), jnp.imag(a_bar).reshape(1, S5_COLS))


def _to_time_major(a):
    return a.transpose(1, 0, 2).reshape(a.shape[0] * a.shape[1], a.shape[2])


def _from_time_major(a, nb):
    return a.reshape(a.shape[0] // nb, nb, a.shape[1]).transpose(1, 0, 2)


def _trunk(x, nb, seq, s5_re, s5_im, lru_h, lru_conv, gdn_s, gdn_conv, ffn_conv, p):
    total = seq * nb
    tm = min(total, 1024)
    rows = 512
    o_s5_re, o_s5_im, o_lru, o_lru_conv, o_gdn, o_gdn_conv, o_ffn_conv = [], [], [], [], [], [], []
    depth = p['norm_mix'].shape[0]
    for i in range(depth):
        kind, j = i % 3, i // 3
        g_mix = p['norm_mix'][i].reshape(1, D_MODEL)
        if kind == 0:
            bre, bim, cre, cim, are, aim = _s5_params(p['s5_a_re'][j], p['s5_a_im'][j], p['s5_log_dt'][j],
                                                      p['s5_b_re'][j], p['s5_b_im'][j], p['s5_c_re'][j],
                                                      p['s5_c_im'][j])
            u = _norm_mm(x, g_mix, p['s5_w_in'][j].astype(BF16), tm, 512)
            y, hre, him = _s5_core(u, s5_re[j].reshape(nb, S5_COLS), s5_im[j].reshape(nb, S5_COLS),
                                   bre, bim, cre, cim, are, aim, p['s5_d'][j].reshape(1, D_MODEL), nb, rows)
            x = _mm_glu_res(y, p['s5_w_glu'][j].astype(BF16), x, tm, 512)
            o_s5_re.append(hre.reshape(nb, S5_GROUPS, S5_STATE))
            o_s5_im.append(him.reshape(nb, S5_GROUPS, S5_STATE))
        elif kind == 1:
            proj = _norm_mm(x, g_mix, p['lru_w_in'][j].astype(BF16), tm, 512)
            y, h_new, conv_new = _lru_core(
                proj, _to_time_major(lru_conv[j]), lru_h[j],
                p['lru_conv_w'][j], p['lru_conv_b'][j].reshape(1, LRU_WIDTH),
                p['lru_w_gate_a'][j].astype(BF16), p['lru_b_gate_a'][j].reshape(1, LRU_WIDTH),
                p['lru_w_gate_x'][j].astype(BF16), p['lru_b_gate_x'][j].reshape(1, LRU_WIDTH),
                p['lru_lambda'][j].reshape(1, LRU_WIDTH), nb, rows)
            x = _mm_res(y, p['lru_w_out'][j].astype(BF16), x, tm, 512)
            o_lru.append(h_new)
            o_lru_conv.append(_from_time_major(conv_new, nb))
        else:
            w_in = p['gdn_w_in'][j]
            w_pad = jnp.pad(w_in, ((0, 0), (0, GDN_PROJ_PAD - w_in.shape[1]))).astype(BF16)
            proj = _norm_mm(x, g_mix, w_pad, tm, 384)
            pad8 = lambda a: jnp.pad(a.reshape(1, GDN_HEADS), ((0, 0), (0, 128 - GDN_HEADS)))
            q, k, v, g, beta, conv_new = _gdn_prep(proj, _to_time_major(gdn_conv[j]), p['gdn_conv_w'][j],
                                                   pad8(p['gdn_a_log'][j]), pad8(p['gdn_dt_bias'][j]), nb, rows)
            chunk = GDN_CHUNK
            lp = -(-seq // chunk) * chunk
            z = proj[:, GDN_CONV_DIM:GDN_CONV_DIM + GDN_KEY_DIM]

            def bm(a):
                a = a.reshape(seq, nb, a.shape[1]).transpose(1, 0, 2)
                return jnp.pad(a, ((0, 0), (0, lp - seq), (0, 0)))

            o, s_new = _gdn_core(bm(q), bm(k), bm(v), bm(z), bm(g), bm(beta), gdn_s[j],
                                 p['gdn_norm'][j].reshape(1, GDN_DV), chunk)
            o = o[:, :seq].transpose(1, 0, 2).reshape(total, GDN_KEY_DIM)
            x = _mm_res(o, p['gdn_w_out'][j].astype(BF16), x, tm, 512)
            o_gdn.append(s_new)
            o_gdn_conv.append(_from_time_major(conv_new, nb))
        x, ca, cb = _ffn(x, p['norm_ffn'][i].reshape(1, D_MODEL), p['ffn_w_up'][i].astype(BF16),
                         p['ffn_conv_w'][i], p['ffn_conv_b'][i].reshape(1, 2 * FFN_HIDDEN),
                         _to_time_major(ffn_conv[i]), p['ffn_w_down'][i].astype(BF16),
                         p['norm_final'].reshape(1, D_MODEL), nb, tm, i == depth - 1)
        o_ffn_conv.append(_from_time_major(jnp.concatenate([ca, cb], axis=1), nb))
    return (x, jnp.stack(o_s5_re), jnp.stack(o_s5_im), jnp.stack(o_lru), jnp.stack(o_lru_conv),
            jnp.stack(o_gdn), jnp.stack(o_gdn_conv), jnp.stack(o_ffn_conv))


def kernel(x_prompt, x_sample, state_s5_re, state_s5_im, state_lru, state_lru_conv, state_gdn, state_gdn_conv, state_ffn_conv, norm_mix, norm_ffn, norm_final, s5_w_in, s5_a_re, s5_a_im, s5_log_dt, s5_b_re, s5_b_im, s5_c_re, s5_c_im, s5_d, s5_w_glu, lru_w_in, lru_conv_w, lru_conv_b, lru_w_gate_a, lru_b_gate_a, lru_w_gate_x, lru_b_gate_x, lru_lambda, lru_w_out, gdn_w_in, gdn_conv_w, gdn_a_log, gdn_dt_bias, gdn_norm, gdn_w_out, ffn_w_up, ffn_conv_w, ffn_conv_b, ffn_w_down):
    p = dict(norm_mix=norm_mix, norm_ffn=norm_ffn, norm_final=norm_final, s5_w_in=s5_w_in, s5_a_re=s5_a_re,
             s5_a_im=s5_a_im, s5_log_dt=s5_log_dt, s5_b_re=s5_b_re, s5_b_im=s5_b_im, s5_c_re=s5_c_re,
             s5_c_im=s5_c_im, s5_d=s5_d, s5_w_glu=s5_w_glu, lru_w_in=lru_w_in, lru_conv_w=lru_conv_w,
             lru_conv_b=lru_conv_b, lru_w_gate_a=lru_w_gate_a, lru_b_gate_a=lru_b_gate_a,
             lru_w_gate_x=lru_w_gate_x, lru_b_gate_x=lru_b_gate_x, lru_lambda=lru_lambda, lru_w_out=lru_w_out,
             gdn_w_in=gdn_w_in, gdn_conv_w=gdn_conv_w, gdn_a_log=gdn_a_log, gdn_dt_bias=gdn_dt_bias,
             gdn_norm=gdn_norm, gdn_w_out=gdn_w_out, ffn_w_up=ffn_w_up, ffn_conv_w=ffn_conv_w,
             ffn_conv_b=ffn_conv_b, ffn_w_down=ffn_w_down)
    outs = []
    for x, states in (
            (x_prompt, None),
            (x_sample, (state_s5_re, state_s5_im, state_lru, state_lru_conv, state_gdn, state_gdn_conv,
                        state_ffn_conv))):
        nb, seq, _ = x.shape
        if states is None:
            states = tuple(jnp.zeros((s.shape[0], nb) + s.shape[2:], F32) for s in (
                state_s5_re, state_s5_im, state_lru, state_lru_conv, state_gdn, state_gdn_conv, state_ffn_conv))
        res = _trunk(_to_time_major(x), nb, seq, *states, p)
        outs.append((_from_time_major(res[0], nb),) + tuple(res[1:]))
    (y_p, *st_p), (y_s, *st_s) = outs
    return (y_p, y_s, *st_p, *st_s)
```

```python
import functools
import math

import jax
import jax.numpy as jnp
from jax import lax
from jax.experimental import pallas as pl
from jax.experimental.pallas import tpu as pltpu

F32 = jnp.float32
BF16 = jnp.bfloat16

D_MODEL = 1024
RMS_EPS = 1e-6
L2_EPS = 1e-6
S5_GROUPS = 64
S5_STATE = 64
S5_GROUP_CH = 16
S5_COLS = S5_GROUPS * S5_STATE
S5_KB = 8
LRU_WIDTH = 1280
LRU_BLOCK = 128
LRU_BLOCKS = LRU_WIDTH // LRU_BLOCK
LRU_C = 8.0
CONV_WIDTH = 4
GDN_HEADS = 8
GDN_DK = 128
GDN_DV = 128
GDN_KEY_DIM = GDN_HEADS * GDN_DK
GDN_CONV_DIM = 3 * GDN_KEY_DIM
GDN_CHUNK = 64
GDN_PROJ_PAD = 4224
FFN_HIDDEN = 2816
FFN_CONV_WIDTH = 3
FFN_TN = 256
VMEM_LIMIT_BYTES = 56 * 1024 * 1024


def _cparams(sem):
    return pltpu.CompilerParams(dimension_semantics=sem, vmem_limit_bytes=VMEM_LIMIT_BYTES)


def _rms(x, g):
    ms = jnp.mean(x * x, axis=-1, keepdims=True)
    return x * lax.rsqrt(ms + RMS_EPS) * g


def _softplus(x):
    return jnp.maximum(x, 0.0) + jnp.log1p(jnp.exp(-jnp.abs(x)))


def _expm1(x):
    u = jnp.exp(x)
    small = jnp.abs(x) < 0.5
    usable = small & (u != 1.0)
    ratio = (u - 1.0) * x / jnp.log(jnp.where(usable, u, 2.0))
    return jnp.where(small, jnp.where(usable, ratio, x), u - 1.0)


def _dot(a, b):
    return jnp.dot(a.astype(BF16), b.astype(BF16), preferred_element_type=F32)


def _dot_nt(a, b):
    return lax.dot_general(a.astype(BF16), b.astype(BF16), (((1,), (1,)), ((), ())),
                           preferred_element_type=F32)


def _split2(a):
    hi = a.astype(BF16)
    lo = (a - hi.astype(F32)).astype(BF16)
    return hi, lo


def _split3(a):
    hi = a.astype(BF16)
    r = a - hi.astype(F32)
    mid = r.astype(BF16)
    lo = (r - mid.astype(F32)).astype(BF16)
    return hi, mid, lo


def _dot3(a, b):
    ah, al = _split2(a)
    bh, bl = _split2(b)
    d = functools.partial(jnp.dot, preferred_element_type=F32)
    return d(ah, bh) + d(al, bh) + d(ah, bl)


def _norm_mm_kernel(x_ref, g_ref, w_ref, o_ref, xn_ref):
    @pl.when(pl.program_id(1) == 0)
    def _():
        xn_ref[...] = _rms(x_ref[...], g_ref[...]).astype(BF16)

    o_ref[...] = jnp.dot(xn_ref[...], w_ref[...], preferred_element_type=F32)


def _norm_mm(x, g, w, tm, tn):
    rows, n = x.shape[0], w.shape[1]
    return pl.pallas_call(
        _norm_mm_kernel,
        grid=(rows // tm, n // tn),
        in_specs=[pl.BlockSpec((tm, D_MODEL), lambda i, j: (i, 0)),
                  pl.BlockSpec((1, D_MODEL), lambda i, j: (0, 0)),
                  pl.BlockSpec((D_MODEL, tn), lambda i, j: (0, j))],
        out_specs=pl.BlockSpec((tm, tn), lambda i, j: (i, j)),
        out_shape=jax.ShapeDtypeStruct((rows, n), F32),
        scratch_shapes=[pltpu.VMEM((tm, D_MODEL), BF16)],
        compiler_params=_cparams(("parallel", "arbitrary")),
        name="norm_mm",
    )(x, g, w)


def _mm_res_kernel(a_ref, w_ref, r_ref, o_ref):
    o_ref[...] = r_ref[...] + jnp.dot(a_ref[...], w_ref[...], preferred_element_type=F32)


def _mm_res(a, w, res, tm, tn):
    rows, k = a.shape
    n = w.shape[1]
    return pl.pallas_call(
        _mm_res_kernel,
        grid=(rows // tm, n // tn),
        in_specs=[pl.BlockSpec((tm, k), lambda i, j: (i, 0)),
                  pl.BlockSpec((k, tn), lambda i, j: (0, j)),
                  pl.BlockSpec((tm, tn), lambda i, j: (i, j))],
        out_specs=pl.BlockSpec((tm, tn), lambda i, j: (i, j)),
        out_shape=jax.ShapeDtypeStruct((rows, n), F32),
        compiler_params=_cparams(("parallel", "parallel")),
        name="mm_res",
    )(a, w, res)


def _mm_glu_res_kernel(a_ref, wv_ref, wg_ref, r_ref, o_ref):
    a = a_ref[...]
    val = jnp.dot(a, wv_ref[...], preferred_element_type=F32)
    gate = jnp.dot(a, wg_ref[...], preferred_element_type=F32)
    o_ref[...] = r_ref[...] + val * jax.nn.sigmoid(gate)


def _mm_glu_res(a, w, res, tm, tn):
    rows, k = a.shape
    n = w.shape[1] // 2
    nj = n // tn
    return pl.pallas_call(
        _mm_glu_res_kernel,
        grid=(rows // tm, nj),
        in_specs=[pl.BlockSpec((tm, k), lambda i, j: (i, 0)),
                  pl.BlockSpec((k, tn), lambda i, j: (0, j)),
                  pl.BlockSpec((k, tn), lambda i, j: (0, nj + j)),
                  pl.BlockSpec((tm, tn), lambda i, j: (i, j))],
        out_specs=pl.BlockSpec((tm, tn), lambda i, j: (i, j)),
        out_shape=jax.ShapeDtypeStruct((rows, n), F32),
        compiler_params=_cparams(("parallel", "parallel")),
        name="mm_glu_res",
    )(a, w, w, res)


def _s5_core_kernel(u_ref, h0re_ref, h0im_ref, bre_ref, bim_ref, cre_ref, cim_ref, are_ref, aim_ref,
                    d_ref, y_ref, hre_out, him_out, hre_s, him_s, *, nb, rows):
    i = pl.program_id(0)

    @pl.when(i == 0)
    def _():
        hre_s[0:nb, :] = h0re_ref[...]
        him_s[0:nb, :] = h0im_ref[...]

    u = u_ref[...]
    ub = u.astype(BF16)
    kw = S5_COLS // S5_KB
    uw = D_MODEL // S5_KB
    for kb in range(S5_KB):
        ukb = ub[:, kb * uw:(kb + 1) * uw]
        hre_s[nb:nb + rows, kb * kw:(kb + 1) * kw] = jnp.dot(ukb, bre_ref[kb], preferred_element_type=F32)
        him_s[nb:nb + rows, kb * kw:(kb + 1) * kw] = jnp.dot(ukb, bim_ref[kb], preferred_element_type=F32)

    are = are_ref[...]
    aim = aim_ref[...]

    def step(t, carry):
        r0 = pl.multiple_of(t * nb, nb)
        r1 = pl.multiple_of(t * nb + nb, nb)
        pr = hre_s[pl.ds(r0, nb), :]
        pi = him_s[pl.ds(r0, nb), :]
        br = hre_s[pl.ds(r1, nb), :]
        bi = him_s[pl.ds(r1, nb), :]
        hre_s[pl.ds(r1, nb), :] = are * pr - aim * pi + br
        him_s[pl.ds(r1, nb), :] = are * pi + aim * pr + bi
        return carry

    lax.fori_loop(0, rows // nb, step, 0)

    for kb in range(S5_KB):
        hr = hre_s[nb:nb + rows, kb * kw:(kb + 1) * kw].astype(BF16)
        hi = him_s[nb:nb + rows, kb * kw:(kb + 1) * kw].astype(BF16)
        yk = (jnp.dot(hr, cre_ref[kb], preferred_element_type=F32)
              - jnp.dot(hi, cim_ref[kb], preferred_element_type=F32))
        yk = yk + d_ref[:, kb * uw:(kb + 1) * uw] * u[:, kb * uw:(kb + 1) * uw]
        y_ref[:, kb * uw:(kb + 1) * uw] = jax.nn.gelu(yk).astype(BF16)

    last_re = hre_s[rows:rows + nb, :]
    last_im = him_s[rows:rows + nb, :]
    hre_s[0:nb, :] = last_re
    him_s[0:nb, :] = last_im
    hre_out[...] = last_re
    him_out[...] = last_im


def _s5_core(u, h0re, h0im, bre, bim, cre, cim, are, aim, d, nb, rows):
    total = u.shape[0]
    full = lambda shape: pl.BlockSpec(shape, lambda i: (0,) * len(shape))
    return pl.pallas_call(
        functools.partial(_s5_core_kernel, nb=nb, rows=rows),
        grid=(total // rows,),
        in_specs=[pl.BlockSpec((rows, D_MODEL), lambda i: (i, 0)),
                  full((nb, S5_COLS)), full((nb, S5_COLS)),
                  full(bre.shape), full(bim.shape), full(cre.shape), full(cim.shape),
                  full((1, S5_COLS)), full((1, S5_COLS)), full((1, D_MODEL))],
        out_specs=[pl.BlockSpec((rows, D_MODEL), lambda i: (i, 0)),
                   full((nb, S5_COLS)), full((nb, S5_COLS))],
        out_shape=[jax.ShapeDtypeStruct((total, D_MODEL), BF16),
                   jax.ShapeDtypeStruct((nb, S5_COLS), F32),
                   jax.ShapeDtypeStruct((nb, S5_COLS), F32)],
        scratch_shapes=[pltpu.VMEM((nb + rows, S5_COLS), F32),
                        pltpu.VMEM((nb + rows, S5_COLS), F32)],
        compiler_params=_cparams(("arbitrary",)),
        name="s5_core",
    )(u, h0re, h0im, bre, bim, cre, cim, are, aim, d)


def _lru_core_kernel(gate_ref, xbr_ref, prev_ref, h0_ref, cw_ref, cb_ref, wga_ref, bga_ref, wgx_ref,
                     bgx_ref, lam_ref, y_ref, hout_ref, cout_ref, xp_s, h_s, a_s, *, nb, rows):
    i = pl.program_id(0)
    hist = (CONV_WIDTH - 1) * nb

    @pl.when(i == 0)
    def _():
        xp_s[0:hist, :] = prev_ref[...]
        h_s[0:nb, :] = h0_ref[...]

    xp_s[hist:hist + rows, :] = xbr_ref[...]
    xc = xp_s[0:rows, :] * cw_ref[0:1, :]
    for k in range(1, CONV_WIDTH):
        xc = xc + xp_s[k * nb:k * nb + rows, :] * cw_ref[k:k + 1, :]
    xc = xc + cb_ref[...]
    c8 = -LRU_C * _softplus(-lam_ref[...])
    for n in range(LRU_BLOCKS):
        sl = slice(n * LRU_BLOCK, (n + 1) * LRU_BLOCK)
        xcn = xc[:, sl]
        xcb = xcn.astype(BF16)
        r = jax.nn.sigmoid(jnp.dot(xcb, wga_ref[n], preferred_element_type=F32) + bga_ref[:, sl])
        ig = jax.nn.sigmoid(jnp.dot(xcb, wgx_ref[n], preferred_element_type=F32) + bgx_ref[:, sl])
        log_a = c8[:, sl] * r
        a_s[:, sl] = jnp.exp(log_a)
        h_s[nb:nb + rows, sl] = jnp.sqrt(-_expm1(2.0 * log_a)) * ig * xcn

    def step(t, carry):
        r0 = pl.multiple_of(t * nb, nb)
        r1 = pl.multiple_of(t * nb + nb, nb)
        h_s[pl.ds(r1, nb), :] = a_s[pl.ds(r0, nb), :] * h_s[pl.ds(r0, nb), :] + h_s[pl.ds(r1, nb), :]
        return carry

    lax.fori_loop(0, rows // nb, step, 0)

    y_ref[...] = (jax.nn.gelu(gate_ref[...]) * h_s[nb:nb + rows, :]).astype(BF16)
    tail = xp_s[rows:rows + hist, :]
    last = h_s[rows:rows + nb, :]
    xp_s[0:hist, :] = tail
    h_s[0:nb, :] = last
    cout_ref[...] = tail
    hout_ref[...] = last


def _lru_core(proj, prev, h0, cw, cb, wga, bga, wgx, bgx, lam, nb, rows):
    total = proj.shape[0]
    hist = (CONV_WIDTH - 1) * nb
    full = lambda shape: pl.BlockSpec(shape, lambda i: (0,) * len(shape))
    return pl.pallas_call(
        functools.partial(_lru_core_kernel, nb=nb, rows=rows),
        grid=(total // rows,),
        in_specs=[pl.BlockSpec((rows, LRU_WIDTH), lambda i: (i, 0)),
                  pl.BlockSpec((rows, LRU_WIDTH), lambda i: (i, 1)),
                  full((hist, LRU_WIDTH)), full((nb, LRU_WIDTH)),
                  full((CONV_WIDTH, LRU_WIDTH)), full((1, LRU_WIDTH)),
                  full(wga.shape), full((1, LRU_WIDTH)), full(wgx.shape), full((1, LRU_WIDTH)),
                  full((1, LRU_WIDTH))],
        out_specs=[pl.BlockSpec((rows, LRU_WIDTH), lambda i: (i, 0)),
                   full((nb, LRU_WIDTH)), full((hist, LRU_WIDTH))],
        out_shape=[jax.ShapeDtypeStruct((total, LRU_WIDTH), BF16),
                   jax.ShapeDtypeStruct((nb, LRU_WIDTH), F32),
                   jax.ShapeDtypeStruct((hist, LRU_WIDTH), F32)],
        scratch_shapes=[pltpu.VMEM((hist + rows, LRU_WIDTH), F32),
                        pltpu.VMEM((nb + rows, LRU_WIDTH), F32),
                        pltpu.VMEM((rows, LRU_WIDTH), F32)],
        compiler_params=_cparams(("arbitrary",)),
        name="lru_core",
    )(proj, proj, prev, h0, cw, cb, wga, bga, wgx, bgx, lam)


def _gdn_prep_kernel(qkv_ref, ab_ref, prev_ref, cw_ref, alog_ref, dtb_ref,
                     q_ref, k_ref, v_ref, g_ref, beta_ref, cout_ref, xp_s, *, nb, rows):
    i = pl.program_id(0)
    hist = (CONV_WIDTH - 1) * nb

    @pl.when(i == 0)
    def _():
        xp_s[0:hist, :] = prev_ref[...]

    xp_s[hist:hist + rows, :] = qkv_ref[...]
    for part, out in enumerate((q_ref, k_ref, v_ref)):
        for h in range(GDN_HEADS):
            lo = part * GDN_KEY_DIM + h * GDN_DK
            sl = slice(lo, lo + GDN_DK)
            acc = xp_s[0:rows, sl] * cw_ref[0:1, sl]
            for k in range(1, CONV_WIDTH):
                acc = acc + xp_s[k * nb:k * nb + rows, sl] * cw_ref[k:k + 1, sl]
            s = acc * jax.nn.sigmoid(acc)
            if part < 2:
                s = s * lax.rsqrt(jnp.sum(s * s, axis=-1, keepdims=True) + L2_EPS)
            if part == 0:
                s = s * (GDN_DK ** -0.5)
            out[:, h * GDN_DK:(h + 1) * GDN_DK] = s
    ab = ab_ref[...]
    g_ref[...] = -jnp.exp(alog_ref[...]) * _softplus(ab + dtb_ref[...])
    beta_ref[...] = jax.nn.sigmoid(ab)
    tail = xp_s[rows:rows + hist, :]
    xp_s[0:hist, :] = tail
    cout_ref[...] = tail


def _gdn_prep(proj, prev, cw, alog, dtb, nb, rows):
    total = proj.shape[0]
    hist = (CONV_WIDTH - 1) * nb
    full = lambda shape: pl.BlockSpec(shape, lambda i: (0,) * len(shape))
    tile = lambda n: pl.BlockSpec((rows, n), lambda i: (i, 0))
    return pl.pallas_call(
        functools.partial(_gdn_prep_kernel, nb=nb, rows=rows),
        grid=(total // rows,),
        in_specs=[tile(GDN_CONV_DIM),
                  pl.BlockSpec((rows, 128), lambda i: (i, (GDN_CONV_DIM + GDN_KEY_DIM) // 128)),
                  full((hist, GDN_CONV_DIM)), full((CONV_WIDTH, GDN_CONV_DIM)),
                  full((1, 128)), full((1, 128))],
        out_specs=[tile(GDN_KEY_DIM), tile(GDN_KEY_DIM), tile(GDN_KEY_DIM), tile(128), tile(128),
                   full((hist, GDN_CONV_DIM))],
        out_shape=[jax.ShapeDtypeStruct((total, GDN_KEY_DIM), F32)] * 3
                  + [jax.ShapeDtypeStruct((total, 128), F32)] * 2
                  + [jax.ShapeDtypeStruct((hist, GDN_CONV_DIM), F32)],
        scratch_shapes=[pltpu.VMEM((hist + rows, GDN_CONV_DIM), F32)],
        compiler_params=_cparams(("arbitrary",)),
        name="gdn_prep",
    )(proj, proj, prev, cw, alog, dtb)


def _unit_lower_inverse(m, ri, ci, chunk):
    eye = (ri == ci).astype(F32)
    rb, cb = ri >> 3, ci >> 3
    n1 = jnp.where(rb == cb, -m, 0.0)
    n2 = _dot(n1, n1)
    n4 = _dot(n2, n2)
    t = _dot(_dot(eye + n1, eye + n2), eye + n4)
    shift = 3
    while (1 << shift) < chunk:
        pair = ((ri >> (shift + 1)) == (ci >> (shift + 1))) & ((ri >> shift) != (ci >> shift))
        off = jnp.where(pair, m, 0.0)
        t = t - _dot(_dot(t, off), t)
        shift += 1
    return t


def _gdn_core_kernel(q_ref, k_ref, v_ref, z_ref, g_ref, beta_ref, s0_ref, nw_ref, o_ref, s_ref, *, chunk):
    c = pl.program_id(1)

    @pl.when(c == 0)
    def _():
        s_ref[...] = s0_ref[...]

    ri = lax.broadcasted_iota(jnp.int32, (chunk, chunk), 0)
    ci = lax.broadcasted_iota(jnp.int32, (chunk, chunk), 1)
    causal = ri >= ci
    strict = ri > ci
    tril = causal.astype(BF16)
    e_r = lax.broadcasted_iota(jnp.int32, (128, 128), 0)
    e_c = lax.broadcasted_iota(jnp.int32, (128, 128), 1)
    eye128 = (e_r == e_c).astype(BF16)
    dotf = functools.partial(jnp.dot, preferred_element_type=F32)

    g = g_ref[...]
    beta = beta_ref[...]
    g3 = _split3(g)
    cum = dotf(tril, g3[0]) + dotf(tril, g3[1]) + dotf(tril, g3[2])
    c3 = _split3(cum)
    nt = lambda a, b: lax.dot_general(a, b, (((1,), (1,)), ((), ())), preferred_element_type=F32)
    cum_t = nt(eye128, c3[0]) + nt(eye128, c3[1]) + nt(eye128, c3[2])
    ecum = jnp.exp(cum)
    g_last = cum[chunk - 1:chunk, :]
    e_last = jnp.exp(g_last)
    e_rest = jnp.exp(g_last - cum)
    nw = nw_ref[...]

    for h in range(GDN_HEADS):
        sl = slice(h * GDN_DK, (h + 1) * GDN_DK)
        qh, kh, vh = q_ref[:, sl], k_ref[:, sl], v_ref[:, sl]
        bh = beta[:, GDN_HEADS + h:GDN_HEADS + h + 1]
        cum_h = cum[:, h:h + 1]
        decay = jnp.exp(jnp.where(causal, cum_h - cum_t[h:h + 1, :], -jnp.inf))
        k_beta = kh * bh
        ak = _dot_nt(jnp.concatenate([k_beta, qh], axis=0), kh)
        m = jnp.where(strict, ak[:chunk] * decay, 0.0)
        t = _unit_lower_inverse(m, ri, ci, chunk)
        sol = _dot(t, jnp.concatenate([vh * bh, k_beta * ecum[:, h:h + 1]], axis=1))
        u, w = sol[:, :GDN_DV], sol[:, GDN_DV:]
        s = s_ref[h]
        ws = _dot(jnp.concatenate([w, qh * ecum[:, h:h + 1]], axis=0), s)
        v_new = u - ws[:chunk]
        o = ws[chunk:] + _dot(ak[chunk:] * decay, v_new)
        k_dec_t = nt(eye128, (kh * e_rest[:, h:h + 1]).astype(BF16))
        s_ref[h] = s * e_last[:, h:h + 1] + _dot(k_dec_t, v_new)
        on = o * lax.rsqrt(jnp.mean(o * o, axis=-1, keepdims=True) + RMS_EPS) * nw
        zh = z_ref[:, sl]
        o_ref[:, sl] = (on * (zh * jax.nn.sigmoid(zh))).astype(BF16)


def _gdn_core(q, k, v, z, g, beta, s0, nw, chunk):
    nb, lp = q.shape[0], q.shape[1]
    seq = lambda n: pl.BlockSpec((None, chunk, n), lambda b, c: (b, c, 0))
    st = pl.BlockSpec((None, GDN_HEADS, GDN_DK, GDN_DV), lambda b, c: (b, 0, 0, 0))
    return pl.pallas_call(
        functools.partial(_gdn_core_kernel, chunk=chunk),
        grid=(nb, lp // chunk),
        in_specs=[seq(GDN_KEY_DIM), seq(GDN_KEY_DIM), seq(GDN_KEY_DIM), seq(GDN_KEY_DIM),
                  seq(128), seq(128), st, pl.BlockSpec((1, GDN_DV), lambda b, c: (0, 0))],
        out_specs=[seq(GDN_KEY_DIM), st],
        out_shape=[jax.ShapeDtypeStruct((nb, lp, GDN_KEY_DIM), BF16),
                   jax.ShapeDtypeStruct(s0.shape, F32)],
        compiler_params=_cparams(("parallel", "arbitrary")),
        name="gdn_core",
    )(q, k, v, z, g, beta, s0, nw)


def _ffn_kernel(x_ref, g_ref, wa_ref, wb_ref, cwa_ref, cwb_ref, cba_ref, cbb_ref, pa_ref, pb_ref, wd_ref,
                gf_ref, o_ref, ca_out, cb_out, xn_s, acc_s, hpa_s, hpb_s, cara_s, carb_s,
                *, nb, tm, final_norm):
    i = pl.program_id(0)
    j = pl.program_id(1)
    hist = (FFN_CONV_WIDTH - 1) * nb

    @pl.when(j == 0)
    def _():
        xn_s[...] = _rms(x_ref[...], g_ref[...]).astype(BF16)
        acc_s[...] = jnp.zeros_like(acc_s)

    @pl.when(i == 0)
    def _():
        hpa_s[0:hist, :] = pa_ref[...]
        hpb_s[0:hist, :] = pb_ref[...]

    @pl.when(i > 0)
    def _():
        hpa_s[0:hist, :] = cara_s[j]
        hpb_s[0:hist, :] = carb_s[j]

    xn = xn_s[...]
    hpa_s[hist:hist + tm, :] = jnp.dot(xn, wa_ref[...], preferred_element_type=F32)
    hpb_s[hist:hist + tm, :] = jnp.dot(xn, wb_ref[...], preferred_element_type=F32)
    tail_a = hpa_s[tm:tm + hist, :]
    tail_b = hpb_s[tm:tm + hist, :]
    cara_s[j] = tail_a
    carb_s[j] = tail_b
    ca_out[...] = tail_a
    cb_out[...] = tail_b

    def conv(hp_s, cw_ref, cb_ref):
        y = hp_s[0:tm, :] * cw_ref[0:1, :]
        for k in range(1, FFN_CONV_WIDTH):
            y = y + hp_s[k * nb:k * nb + tm, :] * cw_ref[k:k + 1, :]
        return y + cb_ref[...]

    act = (jax.nn.gelu(conv(hpa_s, cwa_ref, cba_ref)) * conv(hpb_s, cwb_ref, cbb_ref)).astype(BF16)
    acc_s[...] += jnp.dot(act, wd_ref[...], preferred_element_type=F32)

    @pl.when(j == pl.num_programs(1) - 1)
    def _():
        y = x_ref[...] + acc_s[...]
        if final_norm:
            y = _rms(y, gf_ref[...])
        o_ref[...] = y


def _ffn(x, g, w_up, cw, cb, prev, w_down, g_final, nb, tm, final_norm):
    rows = x.shape[0]
    tn = FFN_TN
    nj = FFN_HIDDEN // tn
    hist = (FFN_CONV_WIDTH - 1) * nb
    col_a = lambda r: pl.BlockSpec((r, tn), lambda i, j: (0, j))
    col_b = lambda r: pl.BlockSpec((r, tn), lambda i, j: (0, nj + j))
    vec = pl.BlockSpec((1, D_MODEL), lambda i, j: (0, 0))
    return pl.pallas_call(
        functools.partial(_ffn_kernel, nb=nb, tm=tm, final_norm=final_norm),
        grid=(rows // tm, nj),
        in_specs=[pl.BlockSpec((tm, D_MODEL), lambda i, j: (i, 0)), vec,
                  col_a(D_MODEL), col_b(D_MODEL),
                  col_a(FFN_CONV_WIDTH), col_b(FFN_CONV_WIDTH), col_a(1), col_b(1),
                  col_a(hist), col_b(hist),
                  pl.BlockSpec((tn, D_MODEL), lambda i, j: (j, 0)), vec],
        out_specs=[pl.BlockSpec((tm, D_MODEL), lambda i, j: (i, 0)),
                   pl.BlockSpec((hist, tn), lambda i, j: (i, j)),
                   pl.BlockSpec((hist, tn), lambda i, j: (i, j))],
        out_shape=[jax.ShapeDtypeStruct((rows, D_MODEL), F32),
                   jax.ShapeDtypeStruct((rows // tm * hist, FFN_HIDDEN), F32),
                   jax.ShapeDtypeStruct((rows // tm * hist, FFN_HIDDEN), F32)],
        scratch_shapes=[pltpu.VMEM((tm, D_MODEL), BF16), pltpu.VMEM((tm, D_MODEL), F32),
                        pltpu.VMEM((hist + tm, tn), F32), pltpu.VMEM((hist + tm, tn), F32),
                        pltpu.VMEM((nj, hist, tn), F32), pltpu.VMEM((nj, hist, tn), F32)],
        compiler_params=_cparams(("arbitrary", "arbitrary")),
        name="conv_ffn",
    )(x, g, w_up, w_up, cw, cw, cb, cb, prev, prev, w_down, g_final)


def _s5_disc_kernel(are_ref, aim_ref, ldt_ref, bre_ref, bim_ref, abr_ref, abi_ref, bbr_ref, bbi_ref):
    a_re, a_im = are_ref[...], aim_ref[...]
    dt = jnp.exp(ldt_ref[...])
    mag = jnp.exp(a_re * dt)
    ar = mag * jnp.cos(a_im * dt)
    ai = mag * jnp.sin(a_im * dt)
    den = a_re * a_re + a_im * a_im
    nr = ar - 1.0
    cr = (nr * a_re + ai * a_im) / den
    ci = (ai * a_re - nr * a_im) / den
    b_re, b_im = bre_ref[...], bim_ref[...]
    abr_ref[...] = ar
    abi_ref[...] = ai
    bbr_ref[...] = cr * b_re - ci * b_im
    bbi_ref[...] = cr * b_im + ci * b_re


def _s5_params(a_re, a_im, log_dt, b_re, b_im, c_re, c_im):
    rep = lambda a: jnp.repeat(a.astype(F32), S5_GROUP_CH, axis=0)
    rows_gc = lambda b: b.astype(F32).transpose(0, 2, 1).reshape(D_MODEL, S5_STATE)
    ldt = jnp.broadcast_to(log_dt.astype(F32)[:, None], (S5_GROUPS, S5_STATE))
    sds = jax.ShapeDtypeStruct((D_MODEL, S5_STATE), F32)
    abr, abi, bbr, bbi = pl.pallas_call(_s5_disc_kernel, out_shape=[sds] * 4, name="s5_discretize")(
        rep(a_re), rep(a_im), rep(ldt), rows_gc(b_re), rows_gc(b_im))
    eye = jnp.eye(S5_GROUPS // S5_KB, dtype=F32)

    def b_blocks(b):
        b = b.reshape(S5_KB, S5_GROUPS // S5_KB, S5_GROUP_CH, S5_STATE)
        return jnp.einsum('kgcp,gh->kgchp', b, eye).reshape(S5_KB, D_MODEL // S5_KB, S5_COLS // S5_KB).astype(BF16)

    def c_blocks(c):
        c = c.astype(F32).reshape(S5_KB, S5_GROUPS // S5_KB, S5_GROUP_CH, S5_STATE)
        return jnp.einsum('kgcp,gh->kgphc', c, eye).reshape(S5_KB, S5_COLS // S5_KB, D_MODEL // S5_KB).astype(BF16)

    return (b_blocks(bbr), b_blocks(bbi), c_blocks(c_re), c_blocks(c_im),
            abr[::S5_GROUP_CH].reshape(1, S5_COLS), abi[::S5_GROUP_CH].reshape(1, S5_COLS))


def _to_time_major(a):
    return a.transpose(1, 0, 2).reshape(a.shape[0] * a.shape[1], a.shape[2])


def _from_time_major(a, nb):
    return a.reshape(a.shape[0] // nb, nb, a.shape[1]).transpose(1, 0, 2)


def _trunk(x, nb, seq, s5_re, s5_im, lru_h, lru_conv, gdn_s, gdn_conv, ffn_conv, p):
    total = seq * nb
    tm = min(total, 1024)
    rows = 512
    o_s5_re, o_s5_im, o_lru, o_lru_conv, o_gdn, o_gdn_conv, o_ffn_conv = [], [], [], [], [], [], []
    depth = p['norm_mix'].shape[0]
    for i in range(depth):
        kind, j = i % 3, i // 3
        g_mix = p['norm_mix'][i].reshape(1, D_MODEL)
        if kind == 0:
            bre, bim, cre, cim, are, aim = p['s5_disc'][j]
            u = _norm_mm(x, g_mix, p['s5_w_in'][j].astype(BF16), tm, 512)
            y, hre, him = _s5_core(u, s5_re[j].reshape(nb, S5_COLS), s5_im[j].reshape(nb, S5_COLS),
                                   bre, bim, cre, cim, are, aim, p['s5_d'][j].reshape(1, D_MODEL), nb, rows)
            x = _mm_glu_res(y, p['s5_w_glu'][j].astype(BF16), x, tm, 512)
            o_s5_re.append(hre.reshape(nb, S5_GROUPS, S5_STATE))
            o_s5_im.append(him.reshape(nb, S5_GROUPS, S5_STATE))
        elif kind == 1:
            proj = _norm_mm(x, g_mix, p['lru_w_in'][j].astype(BF16), tm, 512)
            y, h_new, conv_new = _lru_core(
                proj, _to_time_major(lru_conv[j]), lru_h[j],
                p['lru_conv_w'][j], p['lru_conv_b'][j].reshape(1, LRU_WIDTH),
                p['lru_w_gate_a'][j].astype(BF16), p['lru_b_gate_a'][j].reshape(1, LRU_WIDTH),
                p['lru_w_gate_x'][j].astype(BF16), p['lru_b_gate_x'][j].reshape(1, LRU_WIDTH),
                p['lru_lambda'][j].reshape(1, LRU_WIDTH), nb, rows)
            x = _mm_res(y, p['lru_w_out'][j].astype(BF16), x, tm, 512)
            o_lru.append(h_new)
            o_lru_conv.append(_from_time_major(conv_new, nb))
        else:
            w_in = p['gdn_w_in'][j]
            w_pad = jnp.pad(w_in, ((0, 0), (0, GDN_PROJ_PAD - w_in.shape[1]))).astype(BF16)
            proj = _norm_mm(x, g_mix, w_pad, tm, 384)
            pad8 = lambda a: jnp.pad(a.reshape(1, GDN_HEADS), ((0, 0), (0, 128 - GDN_HEADS)))
            q, k, v, g, beta, conv_new = _gdn_prep(proj, _to_time_major(gdn_conv[j]), p['gdn_conv_w'][j],
                                                   pad8(p['gdn_a_log'][j]), pad8(p['gdn_dt_bias'][j]), nb, rows)
            chunk = GDN_CHUNK if seq >= GDN_CHUNK else 8
            lp = -(-seq // chunk) * chunk
            z = proj[:, GDN_CONV_DIM:GDN_CONV_DIM + GDN_KEY_DIM]

            def bm(a):
                a = a.reshape(seq, nb, a.shape[1]).transpose(1, 0, 2)
                return jnp.pad(a, ((0, 0), (0, lp - seq), (0, 0)))

            o, s_new = _gdn_core(bm(q), bm(k), bm(v), bm(z), bm(g), bm(beta), gdn_s[j],
                                 p['gdn_norm'][j].reshape(1, GDN_DV), chunk)
            o = o[:, :seq].transpose(1, 0, 2).reshape(total, GDN_KEY_DIM)
            x = _mm_res(o, p['gdn_w_out'][j].astype(BF16), x, tm, 512)
            o_gdn.append(s_new)
            o_gdn_conv.append(_from_time_major(conv_new, nb))
        x, ca, cb = _ffn(x, p['norm_ffn'][i].reshape(1, D_MODEL), p['ffn_w_up'][i].astype(BF16),
                         p['ffn_conv_w'][i], p['ffn_conv_b'][i].reshape(1, 2 * FFN_HIDDEN),
                         _to_time_major(ffn_conv[i]), p['ffn_w_down'][i].astype(BF16),
                         p['norm_final'].reshape(1, D_MODEL), nb, tm, i == depth - 1)
        hist = (FFN_CONV_WIDTH - 1) * nb
        o_ffn_conv.append(_from_time_major(jnp.concatenate([ca[-hist:], cb[-hist:]], axis=1), nb))
    return (x, jnp.stack(o_s5_re), jnp.stack(o_s5_im), jnp.stack(o_lru), jnp.stack(o_lru_conv),
            jnp.stack(o_gdn), jnp.stack(o_gdn_conv), jnp.stack(o_ffn_conv))


def kernel(x_prompt, x_sample, state_s5_re, state_s5_im, state_lru, state_lru_conv, state_gdn, state_gdn_conv, state_ffn_conv, norm_mix, norm_ffn, norm_final, s5_w_in, s5_a_re, s5_a_im, s5_log_dt, s5_b_re, s5_b_im, s5_c_re, s5_c_im, s5_d, s5_w_glu, lru_w_in, lru_conv_w, lru_conv_b, lru_w_gate_a, lru_b_gate_a, lru_w_gate_x, lru_b_gate_x, lru_lambda, lru_w_out, gdn_w_in, gdn_conv_w, gdn_a_log, gdn_dt_bias, gdn_norm, gdn_w_out, ffn_w_up, ffn_conv_w, ffn_conv_b, ffn_w_down):
    p = dict(norm_mix=norm_mix, norm_ffn=norm_ffn, norm_final=norm_final, s5_w_in=s5_w_in, s5_a_re=s5_a_re,
             s5_a_im=s5_a_im, s5_log_dt=s5_log_dt, s5_b_re=s5_b_re, s5_b_im=s5_b_im, s5_c_re=s5_c_re,
             s5_c_im=s5_c_im, s5_d=s5_d, s5_w_glu=s5_w_glu, lru_w_in=lru_w_in, lru_conv_w=lru_conv_w,
             lru_conv_b=lru_conv_b, lru_w_gate_a=lru_w_gate_a, lru_b_gate_a=lru_b_gate_a,
             lru_w_gate_x=lru_w_gate_x, lru_b_gate_x=lru_b_gate_x, lru_lambda=lru_lambda, lru_w_out=lru_w_out,
             gdn_w_in=gdn_w_in, gdn_conv_w=gdn_conv_w, gdn_a_log=gdn_a_log, gdn_dt_bias=gdn_dt_bias,
             gdn_norm=gdn_norm, gdn_w_out=gdn_w_out, ffn_w_up=ffn_w_up, ffn_conv_w=ffn_conv_w,
             ffn_conv_b=ffn_conv_b, ffn_w_down=ffn_w_down)
    p['s5_disc'] = [_s5_params(s5_a_re[j], s5_a_im[j], s5_log_dt[j], s5_b_re[j], s5_b_im[j], s5_c_re[j],
                               s5_c_im[j]) for j in range(s5_a_re.shape[0])]
    outs = []
    for x, states in (
            (x_prompt, None),
            (x_sample, (state_s5_re, state_s5_im, state_lru, state_lru_conv, state_gdn, state_gdn_conv,
                        state_ffn_conv))):
        nb, seq, _ = x.shape
        if states is None:
            states = tuple(jnp.zeros((s.shape[0], nb) + s.shape[2:], F32) for s in (
                state_s5_re, state_s5_im, state_lru, state_lru_conv, state_gdn, state_gdn_conv, state_ffn_conv))
        res = _trunk(_to_time_major(x), nb, seq, *states, p)
        outs.append((_from_time_major(res[0], nb),) + tuple(res[1:]))
    (y_p, *st_p), (y_s, *st_s) = outs
    return (y_p, y_s, *st_p, *st_s)
```

```python
import functools
import math

import jax
import jax.numpy as jnp
from jax import lax
from jax.experimental import pallas as pl
from jax.experimental.pallas import tpu as pltpu

F32 = jnp.float32
BF16 = jnp.bfloat16

D_MODEL = 1024
RMS_EPS = 1e-6
L2_EPS = 1e-6
S5_GROUPS = 64
S5_STATE = 64
S5_GROUP_CH = 16
S5_COLS = S5_GROUPS * S5_STATE
S5_KB = 8
LRU_WIDTH = 1280
LRU_BLOCK = 128
LRU_BLOCKS = LRU_WIDTH // LRU_BLOCK
LRU_C = 8.0
CONV_WIDTH = 4
GDN_HEADS = 8
GDN_DK = 128
GDN_DV = 128
GDN_KEY_DIM = GDN_HEADS * GDN_DK
GDN_CONV_DIM = 3 * GDN_KEY_DIM
GDN_CHUNK = 64
GDN_PROJ_PAD = 4224
FFN_HIDDEN = 2816
FFN_CONV_WIDTH = 3
FFN_TN = 256
VMEM_LIMIT_BYTES = 56 * 1024 * 1024


def _cparams(sem):
    return pltpu.CompilerParams(dimension_semantics=sem, vmem_limit_bytes=VMEM_LIMIT_BYTES)


def _rms(x, g):
    ms = jnp.mean(x * x, axis=-1, keepdims=True)
    return x * lax.rsqrt(ms + RMS_EPS) * g


def _softplus(x):
    return jnp.maximum(x, 0.0) + jnp.log1p(jnp.exp(-jnp.abs(x)))


def _expm1(x):
    u = jnp.exp(x)
    small = jnp.abs(x) < 0.5
    usable = small & (u != 1.0)
    ratio = (u - 1.0) * x / jnp.log(jnp.where(usable, u, 2.0))
    return jnp.where(small, jnp.where(usable, ratio, x), u - 1.0)


def _dot(a, b):
    return jnp.dot(a.astype(BF16), b.astype(BF16), preferred_element_type=F32)


def _dot_nt(a, b):
    return lax.dot_general(a.astype(BF16), b.astype(BF16), (((1,), (1,)), ((), ())),
                           preferred_element_type=F32)


def _split2(a):
    hi = a.astype(BF16)
    lo = (a - hi.astype(F32)).astype(BF16)
    return hi, lo


def _split3(a):
    hi = a.astype(BF16)
    r = a - hi.astype(F32)
    mid = r.astype(BF16)
    lo = (r - mid.astype(F32)).astype(BF16)
    return hi, mid, lo


def _dot3(a, b):
    ah, al = _split2(a)
    bh, bl = _split2(b)
    d = functools.partial(jnp.dot, preferred_element_type=F32)
    return d(ah, bh) + d(al, bh) + d(ah, bl)


def _norm_mm_kernel(x_ref, g_ref, w_ref, o_ref, xn_ref):
    @pl.when(pl.program_id(1) == 0)
    def _():
        xn_ref[...] = _rms(x_ref[...], g_ref[...]).astype(BF16)

    o_ref[...] = jnp.dot(xn_ref[...], w_ref[...], preferred_element_type=F32)


def _norm_mm(x, g, w, tm, tn):
    rows, n = x.shape[0], w.shape[1]
    return pl.pallas_call(
        _norm_mm_kernel,
        grid=(rows // tm, n // tn),
        in_specs=[pl.BlockSpec((tm, D_MODEL), lambda i, j: (i, 0)),
                  pl.BlockSpec((1, D_MODEL), lambda i, j: (0, 0)),
                  pl.BlockSpec((D_MODEL, tn), lambda i, j: (0, j))],
        out_specs=pl.BlockSpec((tm, tn), lambda i, j: (i, j)),
        out_shape=jax.ShapeDtypeStruct((rows, n), F32),
        scratch_shapes=[pltpu.VMEM((tm, D_MODEL), BF16)],
        compiler_params=_cparams(("parallel", "arbitrary")),
        name="norm_mm",
    )(x, g, w)


def _mm_res_kernel(a_ref, w_ref, r_ref, o_ref):
    o_ref[...] = r_ref[...] + jnp.dot(a_ref[...], w_ref[...], preferred_element_type=F32)


def _mm_res(a, w, res, tm, tn):
    rows, k = a.shape
    n = w.shape[1]
    return pl.pallas_call(
        _mm_res_kernel,
        grid=(rows // tm, n // tn),
        in_specs=[pl.BlockSpec((tm, k), lambda i, j: (i, 0)),
                  pl.BlockSpec((k, tn), lambda i, j: (0, j)),
                  pl.BlockSpec((tm, tn), lambda i, j: (i, j))],
        out_specs=pl.BlockSpec((tm, tn), lambda i, j: (i, j)),
        out_shape=jax.ShapeDtypeStruct((rows, n), F32),
        compiler_params=_cparams(("parallel", "parallel")),
        name="mm_res",
    )(a, w, res)


def _mm_glu_res_kernel(a_ref, wv_ref, wg_ref, r_ref, o_ref):
    a = a_ref[...]
    val = jnp.dot(a, wv_ref[...], preferred_element_type=F32)
    gate = jnp.dot(a, wg_ref[...], preferred_element_type=F32)
    o_ref[...] = r_ref[...] + val * jax.nn.sigmoid(gate)


def _mm_glu_res(a, w, res, tm, tn):
    rows, k = a.shape
    n = w.shape[1] // 2
    nj = n // tn
    return pl.pallas_call(
        _mm_glu_res_kernel,
        grid=(rows // tm, nj),
        in_specs=[pl.BlockSpec((tm, k), lambda i, j: (i, 0)),
                  pl.BlockSpec((k, tn), lambda i, j: (0, j)),
                  pl.BlockSpec((k, tn), lambda i, j: (0, nj + j)),
                  pl.BlockSpec((tm, tn), lambda i, j: (i, j))],
        out_specs=pl.BlockSpec((tm, tn), lambda i, j: (i, j)),
        out_shape=jax.ShapeDtypeStruct((rows, n), F32),
        compiler_params=_cparams(("parallel", "parallel")),
        name="mm_glu_res",
    )(a, w, w, res)


def _s5_core_kernel(u_ref, h0re_ref, h0im_ref, bre_ref, bim_ref, cre_ref, cim_ref, are_ref, aim_ref,
                    d_ref, y_ref, hre_out, him_out, hre_s, him_s, *, nb, rows):
    i = pl.program_id(0)

    @pl.when(i == 0)
    def _():
        hre_s[0:nb, :] = h0re_ref[...]
        him_s[0:nb, :] = h0im_ref[...]

    u = u_ref[...]
    ub = u.astype(BF16)
    kw = S5_COLS // S5_KB
    uw = D_MODEL // S5_KB
    for kb in range(S5_KB):
        ukb = ub[:, kb * uw:(kb + 1) * uw]
        hre_s[nb:nb + rows, kb * kw:(kb + 1) * kw] = jnp.dot(ukb, bre_ref[kb], preferred_element_type=F32)
        him_s[nb:nb + rows, kb * kw:(kb + 1) * kw] = jnp.dot(ukb, bim_ref[kb], preferred_element_type=F32)

    are = are_ref[...]
    aim = aim_ref[...]

    def step(t, carry):
        r0 = pl.multiple_of(t * nb, nb)
        r1 = pl.multiple_of(t * nb + nb, nb)
        pr = hre_s[pl.ds(r0, nb), :]
        pi = him_s[pl.ds(r0, nb), :]
        br = hre_s[pl.ds(r1, nb), :]
        bi = him_s[pl.ds(r1, nb), :]
        hre_s[pl.ds(r1, nb), :] = are * pr - aim * pi + br
        him_s[pl.ds(r1, nb), :] = are * pi + aim * pr + bi
        return carry

    lax.fori_loop(0, rows // nb, step, 0)

    for kb in range(S5_KB):
        hr = hre_s[nb:nb + rows, kb * kw:(kb + 1) * kw].astype(BF16)
        hi = him_s[nb:nb + rows, kb * kw:(kb + 1) * kw].astype(BF16)
        yk = (jnp.dot(hr, cre_ref[kb], preferred_element_type=F32)
              - jnp.dot(hi, cim_ref[kb], preferred_element_type=F32))
        yk = yk + d_ref[:, kb * uw:(kb + 1) * uw] * u[:, kb * uw:(kb + 1) * uw]
        y_ref[:, kb * uw:(kb + 1) * uw] = jax.nn.gelu(yk).astype(BF16)

    last_re = hre_s[rows:rows + nb, :]
    last_im = him_s[rows:rows + nb, :]
    hre_s[0:nb, :] = last_re
    him_s[0:nb, :] = last_im
    hre_out[...] = last_re
    him_out[...] = last_im


def _s5_core(u, h0re, h0im, bre, bim, cre, cim, are, aim, d, nb, rows):
    total = u.shape[0]
    full = lambda shape: pl.BlockSpec(shape, lambda i: (0,) * len(shape))
    return pl.pallas_call(
        functools.partial(_s5_core_kernel, nb=nb, rows=rows),
        grid=(total // rows,),
        in_specs=[pl.BlockSpec((rows, D_MODEL), lambda i: (i, 0)),
                  full((nb, S5_COLS)), full((nb, S5_COLS)),
                  full(bre.shape), full(bim.shape), full(cre.shape), full(cim.shape),
                  full((1, S5_COLS)), full((1, S5_COLS)), full((1, D_MODEL))],
        out_specs=[pl.BlockSpec((rows, D_MODEL), lambda i: (i, 0)),
                   full((nb, S5_COLS)), full((nb, S5_COLS))],
        out_shape=[jax.ShapeDtypeStruct((total, D_MODEL), BF16),
                   jax.ShapeDtypeStruct((nb, S5_COLS), F32),
                   jax.ShapeDtypeStruct((nb, S5_COLS), F32)],
        scratch_shapes=[pltpu.VMEM((nb + rows, S5_COLS), F32),
                        pltpu.VMEM((nb + rows, S5_COLS), F32)],
        compiler_params=_cparams(("arbitrary",)),
        name="s5_core",
    )(u, h0re, h0im, bre, bim, cre, cim, are, aim, d)


def _lru_core_kernel(gate_ref, xbr_ref, prev_ref, h0_ref, cw_ref, cb_ref, wga_ref, bga_ref, wgx_ref,
                     bgx_ref, lam_ref, y_ref, hout_ref, cout_ref, xp_s, h_s, a_s, *, nb, rows):
    i = pl.program_id(0)
    hist = (CONV_WIDTH - 1) * nb

    @pl.when(i == 0)
    def _():
        xp_s[0:hist, :] = prev_ref[...]
        h_s[0:nb, :] = h0_ref[...]

    xp_s[hist:hist + rows, :] = xbr_ref[...]
    xc = xp_s[0:rows, :] * cw_ref[0:1, :]
    for k in range(1, CONV_WIDTH):
        xc = xc + xp_s[k * nb:k * nb + rows, :] * cw_ref[k:k + 1, :]
    xc = xc + cb_ref[...]
    c8 = -LRU_C * _softplus(-lam_ref[...])
    for n in range(LRU_BLOCKS):
        sl = slice(n * LRU_BLOCK, (n + 1) * LRU_BLOCK)
        xcn = xc[:, sl]
        xcb = xcn.astype(BF16)
        r = jax.nn.sigmoid(jnp.dot(xcb, wga_ref[n], preferred_element_type=F32) + bga_ref[:, sl])
        ig = jax.nn.sigmoid(jnp.dot(xcb, wgx_ref[n], preferred_element_type=F32) + bgx_ref[:, sl])
        log_a = c8[:, sl] * r
        a_s[:, sl] = jnp.exp(log_a)
        h_s[nb:nb + rows, sl] = jnp.sqrt(-_expm1(2.0 * log_a)) * ig * xcn

    def step(t, carry):
        r0 = pl.multiple_of(t * nb, nb)
        r1 = pl.multiple_of(t * nb + nb, nb)
        h_s[pl.ds(r1, nb), :] = a_s[pl.ds(r0, nb), :] * h_s[pl.ds(r0, nb), :] + h_s[pl.ds(r1, nb), :]
        return carry

    lax.fori_loop(0, rows // nb, step, 0)

    y_ref[...] = (jax.nn.gelu(gate_ref[...]) * h_s[nb:nb + rows, :]).astype(BF16)
    tail = xp_s[rows:rows + hist, :]
    last = h_s[rows:rows + nb, :]
    xp_s[0:hist, :] = tail
    h_s[0:nb, :] = last
    cout_ref[...] = tail
    hout_ref[...] = last


def _lru_core(proj, prev, h0, cw, cb, wga, bga, wgx, bgx, lam, nb, rows):
    total = proj.shape[0]
    hist = (CONV_WIDTH - 1) * nb
    full = lambda shape: pl.BlockSpec(shape, lambda i: (0,) * len(shape))
    return pl.pallas_call(
        functools.partial(_lru_core_kernel, nb=nb, rows=rows),
        grid=(total // rows,),
        in_specs=[pl.BlockSpec((rows, LRU_WIDTH), lambda i: (i, 0)),
                  pl.BlockSpec((rows, LRU_WIDTH), lambda i: (i, 1)),
                  full((hist, LRU_WIDTH)), full((nb, LRU_WIDTH)),
                  full((CONV_WIDTH, LRU_WIDTH)), full((1, LRU_WIDTH)),
                  full(wga.shape), full((1, LRU_WIDTH)), full(wgx.shape), full((1, LRU_WIDTH)),
                  full((1, LRU_WIDTH))],
        out_specs=[pl.BlockSpec((rows, LRU_WIDTH), lambda i: (i, 0)),
                   full((nb, LRU_WIDTH)), full((hist, LRU_WIDTH))],
        out_shape=[jax.ShapeDtypeStruct((total, LRU_WIDTH), BF16),
                   jax.ShapeDtypeStruct((nb, LRU_WIDTH), F32),
                   jax.ShapeDtypeStruct((hist, LRU_WIDTH), F32)],
        scratch_shapes=[pltpu.VMEM((hist + rows, LRU_WIDTH), F32),
                        pltpu.VMEM((nb + rows, LRU_WIDTH), F32),
                        pltpu.VMEM((rows, LRU_WIDTH), F32)],
        compiler_params=_cparams(("arbitrary",)),
        name="lru_core",
    )(proj, proj, prev, h0, cw, cb, wga, bga, wgx, bgx, lam)


def _gdn_prep_kernel(qkv_ref, ab_ref, prev_ref, cw_ref, alog_ref, dtb_ref,
                     q_ref, k_ref, v_ref, g_ref, beta_ref, cout_ref, xp_s, *, nb, rows):
    i = pl.program_id(0)
    hist = (CONV_WIDTH - 1) * nb

    @pl.when(i == 0)
    def _():
        xp_s[0:hist, :] = prev_ref[...]

    xp_s[hist:hist + rows, :] = qkv_ref[...]
    for part, out in enumerate((q_ref, k_ref, v_ref)):
        for h in range(GDN_HEADS):
            lo = part * GDN_KEY_DIM + h * GDN_DK
            sl = slice(lo, lo + GDN_DK)
            acc = xp_s[0:rows, sl] * cw_ref[0:1, sl]
            for k in range(1, CONV_WIDTH):
                acc = acc + xp_s[k * nb:k * nb + rows, sl] * cw_ref[k:k + 1, sl]
            s = acc * jax.nn.sigmoid(acc)
            if part < 2:
                s = s * lax.rsqrt(jnp.sum(s * s, axis=-1, keepdims=True) + L2_EPS)
            if part == 0:
                s = s * (GDN_DK ** -0.5)
            out[:, h * GDN_DK:(h + 1) * GDN_DK] = s
    ab = ab_ref[...]
    g_ref[...] = -jnp.exp(alog_ref[...]) * _softplus(ab + dtb_ref[...])
    beta_ref[...] = jax.nn.sigmoid(ab)
    tail = xp_s[rows:rows + hist, :]
    xp_s[0:hist, :] = tail
    cout_ref[...] = tail


def _gdn_prep(proj, prev, cw, alog, dtb, nb, rows):
    total = proj.shape[0]
    hist = (CONV_WIDTH - 1) * nb
    full = lambda shape: pl.BlockSpec(shape, lambda i: (0,) * len(shape))
    tile = lambda n: pl.BlockSpec((rows, n), lambda i: (i, 0))
    return pl.pallas_call(
        functools.partial(_gdn_prep_kernel, nb=nb, rows=rows),
        grid=(total // rows,),
        in_specs=[tile(GDN_CONV_DIM),
                  pl.BlockSpec((rows, 128), lambda i: (i, (GDN_CONV_DIM + GDN_KEY_DIM) // 128)),
                  full((hist, GDN_CONV_DIM)), full((CONV_WIDTH, GDN_CONV_DIM)),
                  full((1, 128)), full((1, 128))],
        out_specs=[tile(GDN_KEY_DIM), tile(GDN_KEY_DIM), tile(GDN_KEY_DIM), tile(128), tile(128),
                   full((hist, GDN_CONV_DIM))],
        out_shape=[jax.ShapeDtypeStruct((total, GDN_KEY_DIM), F32)] * 3
                  + [jax.ShapeDtypeStruct((total, 128), F32)] * 2
                  + [jax.ShapeDtypeStruct((hist, GDN_CONV_DIM), F32)],
        scratch_shapes=[pltpu.VMEM((hist + rows, GDN_CONV_DIM), F32)],
        compiler_params=_cparams(("arbitrary",)),
        name="gdn_prep",
    )(proj, proj, prev, cw, alog, dtb)


def _unit_lower_inverses(ms, ri, ci, chunk):
    eye = (ri == ci).astype(F32)
    blk = (ri >> 3) == (ci >> 3)
    n1 = [jnp.where(blk, -m, 0.0) for m in ms]
    n2 = [_dot(a, a) for a in n1]
    n4 = [_dot(a, a) for a in n2]
    ts = [_dot(eye + a, eye + b) for a, b in zip(n1, n2)]
    ts = [_dot(t, eye + a) for t, a in zip(ts, n4)]
    shift = 3
    while (1 << shift) < chunk:
        pair = ((ri >> (shift + 1)) == (ci >> (shift + 1))) & ((ri >> shift) != (ci >> shift))
        left = [_dot(t, jnp.where(pair, m, 0.0)) for t, m in zip(ts, ms)]
        ts = [t - _dot(a, t) for t, a in zip(ts, left)]
        shift += 1
    return ts


def _gdn_core_kernel(q_ref, k_ref, v_ref, z_ref, g_ref, beta_ref, s0_ref, nw_ref, o_ref, s_ref, *, chunk, bb):
    c = pl.program_id(1)

    @pl.when(c == 0)
    def _():
        s_ref[...] = s0_ref[...]

    ri = lax.broadcasted_iota(jnp.int32, (chunk, chunk), 0)
    ci = lax.broadcasted_iota(jnp.int32, (chunk, chunk), 1)
    causal = ri >= ci
    strict = ri > ci
    tril = causal.astype(BF16)
    e_r = lax.broadcasted_iota(jnp.int32, (128, 128), 0)
    e_c = lax.broadcasted_iota(jnp.int32, (128, 128), 1)
    eye128 = (e_r == e_c).astype(BF16)
    dotf = functools.partial(jnp.dot, preferred_element_type=F32)
    nt = lambda a, b: lax.dot_general(a, b, (((1,), (1,)), ((), ())), preferred_element_type=F32)
    nw = nw_ref[...]

    cums, cum_ts, ecums, e_lasts, e_rests, betas = [], [], [], [], [], []
    for bi in range(bb):
        g3 = _split3(g_ref[bi])
        cum = dotf(tril, g3[0]) + dotf(tril, g3[1]) + dotf(tril, g3[2])
        c3 = _split3(cum)
        cums.append(cum)
        cum_ts.append(nt(eye128, c3[0]) + nt(eye128, c3[1]) + nt(eye128, c3[2]))
        ecums.append(jnp.exp(cum))
        g_last = cum[chunk - 1:chunk, :]
        e_lasts.append(jnp.exp(g_last))
        e_rests.append(jnp.exp(g_last - cum))
        betas.append(beta_ref[bi])

    units = [(bi, h) for bi in range(bb) for h in range(GDN_HEADS)]
    col = lambda a, h: a[:, h:h + 1]
    sl = lambda h: slice(h * GDN_DK, (h + 1) * GDN_DK)
    q = [q_ref[bi, :, sl(h)] for bi, h in units]
    k = [k_ref[bi, :, sl(h)] for bi, h in units]
    decay = [jnp.exp(jnp.where(causal, col(cums[bi], h) - cum_ts[bi][h:h + 1, :], -jnp.inf)) for bi, h in units]
    k_beta = [kk * col(betas[bi], GDN_HEADS + h) for kk, (bi, h) in zip(k, units)]
    ak = [_dot_nt(jnp.concatenate([kb, qq], axis=0), kk) for kb, qq, kk in zip(k_beta, q, k)]
    ms = [jnp.where(strict, a[:chunk] * d, 0.0) for a, d in zip(ak, decay)]
    ts = _unit_lower_inverses(ms, ri, ci, chunk)
    rhs = [jnp.concatenate([v_ref[bi, :, sl(h)] * col(betas[bi], GDN_HEADS + h), kb * col(ecums[bi], h)], axis=1)
           for kb, (bi, h) in zip(k_beta, units)]
    sol = [_dot(t, r) for t, r in zip(ts, rhs)]
    s_old = [s_ref[bi, h] for bi, h in units]
    ws = [_dot(jnp.concatenate([so[:, GDN_DV:], qq * col(ecums[bi], h)], axis=0), s)
          for so, qq, s, (bi, h) in zip(sol, q, s_old, units)]
    v_new = [so[:, :GDN_DV] - w[:chunk] for so, w in zip(sol, ws)]
    o = [w[chunk:] + _dot(a[chunk:] * d, vn) for w, a, d, vn in zip(ws, ak, decay, v_new)]
    k_dec_t = [nt(eye128, (kk * col(e_rests[bi], h)).astype(BF16)) for kk, (bi, h) in zip(k, units)]
    for (bi, h), s, kt, vn, oo in zip(units, s_old, k_dec_t, v_new, o):
        s_ref[bi, h] = s * col(e_lasts[bi], h) + _dot(kt, vn)
        on = oo * lax.rsqrt(jnp.mean(oo * oo, axis=-1, keepdims=True) + RMS_EPS) * nw
        zh = z_ref[bi, :, sl(h)]
        o_ref[bi, :, sl(h)] = (on * (zh * jax.nn.sigmoid(zh))).astype(BF16)


def _gdn_core(q, k, v, z, g, beta, s0, nw, chunk, bb):
    nb, lp = q.shape[0], q.shape[1]
    seq = lambda n: pl.BlockSpec((bb, chunk, n), lambda b, c: (b, c, 0))
    st = pl.BlockSpec((bb, GDN_HEADS, GDN_DK, GDN_DV), lambda b, c: (b, 0, 0, 0))
    return pl.pallas_call(
        functools.partial(_gdn_core_kernel, chunk=chunk, bb=bb),
        grid=(nb // bb, lp // chunk),
        in_specs=[seq(GDN_KEY_DIM), seq(GDN_KEY_DIM), seq(GDN_KEY_DIM), seq(GDN_KEY_DIM),
                  seq(128), seq(128), st, pl.BlockSpec((1, GDN_DV), lambda b, c: (0, 0))],
        out_specs=[seq(GDN_KEY_DIM), st],
        out_shape=[jax.ShapeDtypeStruct((nb, lp, GDN_KEY_DIM), BF16),
                   jax.ShapeDtypeStruct(s0.shape, F32)],
        compiler_params=_cparams(("parallel", "arbitrary")),
        name="gdn_core",
    )(q, k, v, z, g, beta, s0, nw)


def _ffn_kernel(x_ref, g_ref, wa_ref, wb_ref, cwa_ref, cwb_ref, cba_ref, cbb_ref, pa_ref, pb_ref, wd_ref,
                gf_ref, o_ref, ca_out, cb_out, xn_s, acc_s, hpa_s, hpb_s, cara_s, carb_s,
                *, nb, tm, final_norm):
    i = pl.program_id(0)
    j = pl.program_id(1)
    hist = (FFN_CONV_WIDTH - 1) * nb

    @pl.when(j == 0)
    def _():
        xn_s[...] = _rms(x_ref[...], g_ref[...]).astype(BF16)
        acc_s[...] = jnp.zeros_like(acc_s)

    @pl.when(i == 0)
    def _():
        hpa_s[0:hist, :] = pa_ref[...]
        hpb_s[0:hist, :] = pb_ref[...]

    @pl.when(i > 0)
    def _():
        hpa_s[0:hist, :] = cara_s[j]
        hpb_s[0:hist, :] = carb_s[j]

    xn = xn_s[...]
    hpa_s[hist:hist + tm, :] = jnp.dot(xn, wa_ref[...], preferred_element_type=F32)
    hpb_s[hist:hist + tm, :] = jnp.dot(xn, wb_ref[...], preferred_element_type=F32)
    tail_a = hpa_s[tm:tm + hist, :]
    tail_b = hpb_s[tm:tm + hist, :]
    cara_s[j] = tail_a
    carb_s[j] = tail_b
    ca_out[...] = tail_a
    cb_out[...] = tail_b

    def conv(hp_s, cw_ref, cb_ref):
        y = hp_s[0:tm, :] * cw_ref[0:1, :]
        for k in range(1, FFN_CONV_WIDTH):
            y = y + hp_s[k * nb:k * nb + tm, :] * cw_ref[k:k + 1, :]
        return y + cb_ref[...]

    act = (jax.nn.gelu(conv(hpa_s, cwa_ref, cba_ref)) * conv(hpb_s, cwb_ref, cbb_ref)).astype(BF16)
    acc_s[...] += jnp.dot(act, wd_ref[...], preferred_element_type=F32)

    @pl.when(j == pl.num_programs(1) - 1)
    def _():
        y = x_ref[...] + acc_s[...]
        if final_norm:
            y = _rms(y, gf_ref[...])
        o_ref[...] = y


def _ffn(x, g, w_up, cw, cb, prev, w_down, g_final, nb, tm, final_norm):
    rows = x.shape[0]
    tn = FFN_TN
    nj = FFN_HIDDEN // tn
    hist = (FFN_CONV_WIDTH - 1) * nb
    col_a = lambda r: pl.BlockSpec((r, tn), lambda i, j: (0, j))
    col_b = lambda r: pl.BlockSpec((r, tn), lambda i, j: (0, nj + j))
    vec = pl.BlockSpec((1, D_MODEL), lambda i, j: (0, 0))
    return pl.pallas_call(
        functools.partial(_ffn_kernel, nb=nb, tm=tm, final_norm=final_norm),
        grid=(rows // tm, nj),
        in_specs=[pl.BlockSpec((tm, D_MODEL), lambda i, j: (i, 0)), vec,
                  col_a(D_MODEL), col_b(D_MODEL),
                  col_a(FFN_CONV_WIDTH), col_b(FFN_CONV_WIDTH), col_a(1), col_b(1),
                  col_a(hist), col_b(hist),
                  pl.BlockSpec((tn, D_MODEL), lambda i, j: (j, 0)), vec],
        out_specs=[pl.BlockSpec((tm, D_MODEL), lambda i, j: (i, 0)),
                   pl.BlockSpec((hist, tn), lambda i, j: (i, j)),
                   pl.BlockSpec((hist, tn), lambda i, j: (i, j))],
        out_shape=[jax.ShapeDtypeStruct((rows, D_MODEL), F32),
                   jax.ShapeDtypeStruct((rows // tm * hist, FFN_HIDDEN), F32),
                   jax.ShapeDtypeStruct((rows // tm * hist, FFN_HIDDEN), F32)],
        scratch_shapes=[pltpu.VMEM((tm, D_MODEL), BF16), pltpu.VMEM((tm, D_MODEL), F32),
                        pltpu.VMEM((hist + tm, tn), F32), pltpu.VMEM((hist + tm, tn), F32),
                        pltpu.VMEM((nj, hist, tn), F32), pltpu.VMEM((nj, hist, tn), F32)],
        compiler_params=_cparams(("arbitrary", "arbitrary")),
        name="conv_ffn",
    )(x, g, w_up, w_up, cw, cw, cb, cb, prev, prev, w_down, g_final)


def _s5_disc_kernel(are_ref, aim_ref, ldt_ref, bre_ref, bim_ref, abr_ref, abi_ref, bbr_ref, bbi_ref):
    a_re, a_im = are_ref[...], aim_ref[...]
    dt = jnp.exp(ldt_ref[...])
    mag = jnp.exp(a_re * dt)
    ar = mag * jnp.cos(a_im * dt)
    ai = mag * jnp.sin(a_im * dt)
    den = a_re * a_re + a_im * a_im
    nr = ar - 1.0
    cr = (nr * a_re + ai * a_im) / den
    ci = (ai * a_re - nr * a_im) / den
    b_re, b_im = bre_ref[...], bim_ref[...]
    abr_ref[...] = ar
    abi_ref[...] = ai
    bbr_ref[...] = cr * b_re - ci * b_im
    bbi_ref[...] = cr * b_im + ci * b_re


def _s5_params(a_re, a_im, log_dt, b_re, b_im, c_re, c_im):
    rep = lambda a: jnp.repeat(a.astype(F32), S5_GROUP_CH, axis=0)
    rows_gc = lambda b: b.astype(F32).transpose(0, 2, 1).reshape(D_MODEL, S5_STATE)
    ldt = jnp.broadcast_to(log_dt.astype(F32)[:, None], (S5_GROUPS, S5_STATE))
    sds = jax.ShapeDtypeStruct((D_MODEL, S5_STATE), F32)
    abr, abi, bbr, bbi = pl.pallas_call(_s5_disc_kernel, out_shape=[sds] * 4, name="s5_discretize")(
        rep(a_re), rep(a_im), rep(ldt), rows_gc(b_re), rows_gc(b_im))
    eye = jnp.eye(S5_GROUPS // S5_KB, dtype=F32)

    def b_blocks(b):
        b = b.reshape(S5_KB, S5_GROUPS // S5_KB, S5_GROUP_CH, S5_STATE)
        return jnp.einsum('kgcp,gh->kgchp', b, eye).reshape(S5_KB, D_MODEL // S5_KB, S5_COLS // S5_KB).astype(BF16)

    def c_blocks(c):
        c = c.astype(F32).reshape(S5_KB, S5_GROUPS // S5_KB, S5_GROUP_CH, S5_STATE)
        return jnp.einsum('kgcp,gh->kgphc', c, eye).reshape(S5_KB, S5_COLS // S5_KB, D_MODEL // S5_KB).astype(BF16)

    return (b_blocks(bbr), b_blocks(bbi), c_blocks(c_re), c_blocks(c_im),
            abr[::S5_GROUP_CH].reshape(1, S5_COLS), abi[::S5_GROUP_CH].reshape(1, S5_COLS))


def _to_time_major(a):
    return a.transpose(1, 0, 2).reshape(a.shape[0] * a.shape[1], a.shape[2])


def _from_time_major(a, nb):
    return a.reshape(a.shape[0] // nb, nb, a.shape[1]).transpose(1, 0, 2)


def _trunk(x, nb, seq, s5_re, s5_im, lru_h, lru_conv, gdn_s, gdn_conv, ffn_conv, p):
    total = seq * nb
    tm = min(total, 1024)
    rows = 512
    o_s5_re, o_s5_im, o_lru, o_lru_conv, o_gdn, o_gdn_conv, o_ffn_conv = [], [], [], [], [], [], []
    depth = p['norm_mix'].shape[0]
    for i in range(depth):
        kind, j = i % 3, i // 3
        g_mix = p['norm_mix'][i].reshape(1, D_MODEL)
        if kind == 0:
            bre, bim, cre, cim, are, aim = p['s5_disc'][j]
            u = _norm_mm(x, g_mix, p['s5_w_in'][j].astype(BF16), tm, 512)
            y, hre, him = _s5_core(u, s5_re[j].reshape(nb, S5_COLS), s5_im[j].reshape(nb, S5_COLS),
                                   bre, bim, cre, cim, are, aim, p['s5_d'][j].reshape(1, D_MODEL), nb, rows)
            x = _mm_glu_res(y, p['s5_w_glu'][j].astype(BF16), x, tm, 512)
            o_s5_re.append(hre.reshape(nb, S5_GROUPS, S5_STATE))
            o_s5_im.append(him.reshape(nb, S5_GROUPS, S5_STATE))
        elif kind == 1:
            proj = _norm_mm(x, g_mix, p['lru_w_in'][j].astype(BF16), tm, 512)
            y, h_new, conv_new = _lru_core(
                proj, _to_time_major(lru_conv[j]), lru_h[j],
                p['lru_conv_w'][j], p['lru_conv_b'][j].reshape(1, LRU_WIDTH),
                p['lru_w_gate_a'][j].astype(BF16), p['lru_b_gate_a'][j].reshape(1, LRU_WIDTH),
                p['lru_w_gate_x'][j].astype(BF16), p['lru_b_gate_x'][j].reshape(1, LRU_WIDTH),
                p['lru_lambda'][j].reshape(1, LRU_WIDTH), nb, rows)
            x = _mm_res(y, p['lru_w_out'][j].astype(BF16), x, tm, 512)
            o_lru.append(h_new)
            o_lru_conv.append(_from_time_major(conv_new, nb))
        else:
            w_in = p['gdn_w_in'][j]
            w_pad = jnp.pad(w_in, ((0, 0), (0, GDN_PROJ_PAD - w_in.shape[1]))).astype(BF16)
            proj = _norm_mm(x, g_mix, w_pad, tm, 384)
            pad8 = lambda a: jnp.pad(a.reshape(1, GDN_HEADS), ((0, 0), (0, 128 - GDN_HEADS)))
            q, k, v, g, beta, conv_new = _gdn_prep(proj, _to_time_major(gdn_conv[j]), p['gdn_conv_w'][j],
                                                   pad8(p['gdn_a_log'][j]), pad8(p['gdn_dt_bias'][j]), nb, rows)
            chunk = GDN_CHUNK if seq >= GDN_CHUNK else 8
            lp = -(-seq // chunk) * chunk
            z = proj[:, GDN_CONV_DIM:GDN_CONV_DIM + GDN_KEY_DIM]

            def bm(a):
                a = a.reshape(seq, nb, a.shape[1]).transpose(1, 0, 2)
                return jnp.pad(a, ((0, 0), (0, lp - seq), (0, 0)))

            o, s_new = _gdn_core(bm(q), bm(k), bm(v), bm(z), bm(g), bm(beta), gdn_s[j],
                                 p['gdn_norm'][j].reshape(1, GDN_DV), chunk, 1 if chunk == GDN_CHUNK else 4)
            o = o[:, :seq].transpose(1, 0, 2).reshape(total, GDN_KEY_DIM)
            x = _mm_res(o, p['gdn_w_out'][j].astype(BF16), x, tm, 512)
            o_gdn.append(s_new)
            o_gdn_conv.append(_from_time_major(conv_new, nb))
        x, ca, cb = _ffn(x, p['norm_ffn'][i].reshape(1, D_MODEL), p['ffn_w_up'][i].astype(BF16),
                         p['ffn_conv_w'][i], p['ffn_conv_b'][i].reshape(1, 2 * FFN_HIDDEN),
                         _to_time_major(ffn_conv[i]), p['ffn_w_down'][i].astype(BF16),
                         p['norm_final'].reshape(1, D_MODEL), nb, tm, i == depth - 1)
        hist = (FFN_CONV_WIDTH - 1) * nb
        o_ffn_conv.append(_from_time_major(jnp.concatenate([ca[-hist:], cb[-hist:]], axis=1), nb))
    return (x, jnp.stack(o_s5_re), jnp.stack(o_s5_im), jnp.stack(o_lru), jnp.stack(o_lru_conv),
            jnp.stack(o_gdn), jnp.stack(o_gdn_conv), jnp.stack(o_ffn_conv))


def kernel(x_prompt, x_sample, state_s5_re, state_s5_im, state_lru, state_lru_conv, state_gdn, state_gdn_conv, state_ffn_conv, norm_mix, norm_ffn, norm_final, s5_w_in, s5_a_re, s5_a_im, s5_log_dt, s5_b_re, s5_b_im, s5_c_re, s5_c_im, s5_d, s5_w_glu, lru_w_in, lru_conv_w, lru_conv_b, lru_w_gate_a, lru_b_gate_a, lru_w_gate_x, lru_b_gate_x, lru_lambda, lru_w_out, gdn_w_in, gdn_conv_w, gdn_a_log, gdn_dt_bias, gdn_norm, gdn_w_out, ffn_w_up, ffn_conv_w, ffn_conv_b, ffn_w_down):
    p = dict(norm_mix=norm_mix, norm_ffn=norm_ffn, norm_final=norm_final, s5_w_in=s5_w_in, s5_a_re=s5_a_re,
             s5_a_im=s5_a_im, s5_log_dt=s5_log_dt, s5_b_re=s5_b_re, s5_b_im=s5_b_im, s5_c_re=s5_c_re,
             s5_c_im=s5_c_im, s5_d=s5_d, s5_w_glu=s5_w_glu, lru_w_in=lru_w_in, lru_conv_w=lru_conv_w,
             lru_conv_b=lru_conv_b, lru_w_gate_a=lru_w_gate_a, lru_b_gate_a=lru_b_gate_a,
             lru_w_gate_x=lru_w_gate_x, lru_b_gate_x=lru_b_gate_x, lru_lambda=lru_lambda, lru_w_out=lru_w_out,
             gdn_w_in=gdn_w_in, gdn_conv_w=gdn_conv_w, gdn_a_log=gdn_a_log, gdn_dt_bias=gdn_dt_bias,
             gdn_norm=gdn_norm, gdn_w_out=gdn_w_out, ffn_w_up=ffn_w_up, ffn_conv_w=ffn_conv_w,
             ffn_conv_b=ffn_conv_b, ffn_w_down=ffn_w_down)
    p['s5_disc'] = [_s5_params(s5_a_re[j], s5_a_im[j], s5_log_dt[j], s5_b_re[j], s5_b_im[j], s5_c_re[j],
                               s5_c_im[j]) for j in range(s5_a_re.shape[0])]
    outs = []
    for x, states in (
            (x_prompt, None),
            (x_sample, (state_s5_re, state_s5_im, state_lru, state_lru_conv, state_gdn, state_gdn_conv,
                        state_ffn_conv))):
        nb, seq, _ = x.shape
        if states is None:
            states = tuple(jnp.zeros((s.shape[0], nb) + s.shape[2:], F32) for s in (
                state_s5_re, state_s5_im, state_lru, state_lru_conv, state_gdn, state_gdn_conv, state_ffn_conv))
        res = _trunk(_to_time_major(x), nb, seq, *states, p)
        outs.append((_from_time_major(res[0], nb),) + tuple(res[1:]))
    (y_p, *st_p), (y_s, *st_s) = outs
    return (y_p, y_s, *st_p, *st_s)
```

```python
import functools
import math

import jax
import jax.numpy as jnp
from jax import lax
from jax.experimental import pallas as pl
from jax.experimental.pallas import tpu as pltpu

F32 = jnp.float32
BF16 = jnp.bfloat16

D_MODEL = 1024
RMS_EPS = 1e-6
L2_EPS = 1e-6
S5_GROUPS = 64
S5_STATE = 64
S5_GROUP_CH = 16
S5_COLS = S5_GROUPS * S5_STATE
S5_KB = 8
LRU_WIDTH = 1280
LRU_BLOCK = 128
LRU_BLOCKS = LRU_WIDTH // LRU_BLOCK
LRU_C = 8.0
CONV_WIDTH = 4
GDN_HEADS = 8
GDN_DK = 128
GDN_DV = 128
GDN_KEY_DIM = GDN_HEADS * GDN_DK
GDN_CONV_DIM = 3 * GDN_KEY_DIM
GDN_CHUNK = 64
GDN_PROJ_PAD = 4224
FFN_HIDDEN = 2816
FFN_CONV_WIDTH = 3
FFN_TN = 256
SUB_ROWS = 256
VMEM_LIMIT_BYTES = 56 * 1024 * 1024


def _cparams(sem):
    return pltpu.CompilerParams(dimension_semantics=sem, vmem_limit_bytes=VMEM_LIMIT_BYTES)


def _rms(x, g):
    ms = jnp.mean(x * x, axis=-1, keepdims=True)
    return x * lax.rsqrt(ms + RMS_EPS) * g


def _softplus(x):
    return jnp.maximum(x, 0.0) + jnp.log1p(jnp.exp(-jnp.abs(x)))


def _expm1(x):
    u = jnp.exp(x)
    small = jnp.abs(x) < 0.5
    usable = small & (u != 1.0)
    ratio = (u - 1.0) * x / jnp.log(jnp.where(usable, u, 2.0))
    return jnp.where(small, jnp.where(usable, ratio, x), u - 1.0)


def _dot(a, b):
    return jnp.dot(a.astype(BF16), b.astype(BF16), preferred_element_type=F32)


def _dot_nt(a, b):
    return lax.dot_general(a.astype(BF16), b.astype(BF16), (((1,), (1,)), ((), ())),
                           preferred_element_type=F32)


def _split2(a):
    hi = a.astype(BF16)
    lo = (a - hi.astype(F32)).astype(BF16)
    return hi, lo


def _split3(a):
    hi = a.astype(BF16)
    r = a - hi.astype(F32)
    mid = r.astype(BF16)
    lo = (r - mid.astype(F32)).astype(BF16)
    return hi, mid, lo


def _dot3(a, b):
    ah, al = _split2(a)
    bh, bl = _split2(b)
    d = functools.partial(jnp.dot, preferred_element_type=F32)
    return d(ah, bh) + d(al, bh) + d(ah, bl)


def _norm_mm_kernel(x_ref, g_ref, w_ref, o_ref, xn_ref, *, tm):
    sub = lambda r: slice(r * SUB_ROWS, (r + 1) * SUB_ROWS)

    def norm(r):
        xn_ref[sub(r), :] = _rms(x_ref[sub(r), :], g_ref[...]).astype(BF16)

    nsub = tm // SUB_ROWS
    norm(0)
    for r in range(nsub):
        if r + 1 < nsub:
            norm(r + 1)
        o_ref[sub(r), :] = jnp.dot(xn_ref[sub(r), :], w_ref[...], preferred_element_type=F32)


def _norm_mm(x, g, w, tm):
    rows, n = x.shape[0], w.shape[1]
    return pl.pallas_call(
        functools.partial(_norm_mm_kernel, tm=tm),
        grid=(rows // tm,),
        in_specs=[pl.BlockSpec((tm, D_MODEL), lambda i: (i, 0)),
                  pl.BlockSpec((1, D_MODEL), lambda i: (0, 0)),
                  pl.BlockSpec((D_MODEL, n), lambda i: (0, 0))],
        out_specs=pl.BlockSpec((tm, n), lambda i: (i, 0)),
        out_shape=jax.ShapeDtypeStruct((rows, n), F32),
        scratch_shapes=[pltpu.VMEM((tm, D_MODEL), BF16)],
        compiler_params=_cparams(("parallel",)),
        name="norm_mm",
    )(x, g, w)


def _mm_res_kernel(a_ref, w_ref, r_ref, o_ref):
    o_ref[...] = r_ref[...] + jnp.dot(a_ref[...], w_ref[...], preferred_element_type=F32)


def _mm_res(a, w, res, tm, tn):
    rows, k = a.shape
    n = w.shape[1]
    return pl.pallas_call(
        _mm_res_kernel,
        grid=(rows // tm, n // tn),
        in_specs=[pl.BlockSpec((tm, k), lambda i, j: (i, 0)),
                  pl.BlockSpec((k, tn), lambda i, j: (0, j)),
                  pl.BlockSpec((tm, tn), lambda i, j: (i, j))],
        out_specs=pl.BlockSpec((tm, tn), lambda i, j: (i, j)),
        out_shape=jax.ShapeDtypeStruct((rows, n), F32),
        compiler_params=_cparams(("parallel", "parallel")),
        name="mm_res",
    )(a, w, res)


def _mm_glu_res_kernel(a_ref, wv_ref, wg_ref, r_ref, o_ref):
    a = a_ref[...]
    val = jnp.dot(a, wv_ref[...], preferred_element_type=F32)
    gate = jnp.dot(a, wg_ref[...], preferred_element_type=F32)
    o_ref[...] = r_ref[...] + val * jax.nn.sigmoid(gate)


def _mm_glu_res(a, w, res, tm, tn):
    rows, k = a.shape
    n = w.shape[1] // 2
    nj = n // tn
    return pl.pallas_call(
        _mm_glu_res_kernel,
        grid=(rows // tm, nj),
        in_specs=[pl.BlockSpec((tm, k), lambda i, j: (i, 0)),
                  pl.BlockSpec((k, tn), lambda i, j: (0, j)),
                  pl.BlockSpec((k, tn), lambda i, j: (0, nj + j)),
                  pl.BlockSpec((tm, tn), lambda i, j: (i, j))],
        out_specs=pl.BlockSpec((tm, tn), lambda i, j: (i, j)),
        out_shape=jax.ShapeDtypeStruct((rows, n), F32),
        compiler_params=_cparams(("parallel", "parallel")),
        name="mm_glu_res",
    )(a, w, w, res)


def _s5_core_kernel(u_ref, h0re_ref, h0im_ref, bre_ref, bim_ref, cre_ref, cim_ref, are_ref, aim_ref,
                    d_ref, y_ref, hre_out, him_out, hre_s, him_s, *, nb, rows):
    i = pl.program_id(0)

    @pl.when(i == 0)
    def _():
        hre_s[0:nb, :] = h0re_ref[...]
        him_s[0:nb, :] = h0im_ref[...]

    u = u_ref[...]
    ub = u.astype(BF16)
    kw = S5_COLS // S5_KB
    uw = D_MODEL // S5_KB
    for kb in range(S5_KB):
        ukb = ub[:, kb * uw:(kb + 1) * uw]
        hre_s[nb:nb + rows, kb * kw:(kb + 1) * kw] = jnp.dot(ukb, bre_ref[kb], preferred_element_type=F32)
        him_s[nb:nb + rows, kb * kw:(kb + 1) * kw] = jnp.dot(ukb, bim_ref[kb], preferred_element_type=F32)

    are = are_ref[...]
    aim = aim_ref[...]

    def step(t, carry):
        r0 = pl.multiple_of(t * nb, nb)
        r1 = pl.multiple_of(t * nb + nb, nb)
        pr = hre_s[pl.ds(r0, nb), :]
        pi = him_s[pl.ds(r0, nb), :]
        br = hre_s[pl.ds(r1, nb), :]
        bi = him_s[pl.ds(r1, nb), :]
        hre_s[pl.ds(r1, nb), :] = are * pr - aim * pi + br
        him_s[pl.ds(r1, nb), :] = are * pi + aim * pr + bi
        return carry

    lax.fori_loop(0, rows // nb, step, 0)

    for kb in range(S5_KB):
        hr = hre_s[nb:nb + rows, kb * kw:(kb + 1) * kw].astype(BF16)
        hi = him_s[nb:nb + rows, kb * kw:(kb + 1) * kw].astype(BF16)
        yk = (jnp.dot(hr, cre_ref[kb], preferred_element_type=F32)
              - jnp.dot(hi, cim_ref[kb], preferred_element_type=F32))
        yk = yk + d_ref[:, kb * uw:(kb + 1) * uw] * u[:, kb * uw:(kb + 1) * uw]
        y_ref[:, kb * uw:(kb + 1) * uw] = jax.nn.gelu(yk).astype(BF16)

    last_re = hre_s[rows:rows + nb, :]
    last_im = him_s[rows:rows + nb, :]
    hre_s[0:nb, :] = last_re
    him_s[0:nb, :] = last_im
    hre_out[...] = last_re
    him_out[...] = last_im


def _s5_core(u, h0re, h0im, bre, bim, cre, cim, are, aim, d, nb, rows):
    total = u.shape[0]
    full = lambda shape: pl.BlockSpec(shape, lambda i: (0,) * len(shape))
    return pl.pallas_call(
        functools.partial(_s5_core_kernel, nb=nb, rows=rows),
        grid=(total // rows,),
        in_specs=[pl.BlockSpec((rows, D_MODEL), lambda i: (i, 0)),
                  full((nb, S5_COLS)), full((nb, S5_COLS)),
                  full(bre.shape), full(bim.shape), full(cre.shape), full(cim.shape),
                  full((1, S5_COLS)), full((1, S5_COLS)), full((1, D_MODEL))],
        out_specs=[pl.BlockSpec((rows, D_MODEL), lambda i: (i, 0)),
                   full((nb, S5_COLS)), full((nb, S5_COLS))],
        out_shape=[jax.ShapeDtypeStruct((total, D_MODEL), BF16),
                   jax.ShapeDtypeStruct((nb, S5_COLS), F32),
                   jax.ShapeDtypeStruct((nb, S5_COLS), F32)],
        scratch_shapes=[pltpu.VMEM((nb + rows, S5_COLS), F32),
                        pltpu.VMEM((nb + rows, S5_COLS), F32)],
        compiler_params=_cparams(("arbitrary",)),
        name="s5_core",
    )(u, h0re, h0im, bre, bim, cre, cim, are, aim, d)


def _lru_core_kernel(gate_ref, xbr_ref, prev_ref, h0_ref, cw_ref, cb_ref, wga_ref, bga_ref, wgx_ref,
                     bgx_ref, lam_ref, y_ref, hout_ref, cout_ref, xp_s, h_s, a_s, *, nb, rows):
    i = pl.program_id(0)
    hist = (CONV_WIDTH - 1) * nb

    @pl.when(i == 0)
    def _():
        xp_s[0:hist, :] = prev_ref[...]
        h_s[0:nb, :] = h0_ref[...]

    xp_s[hist:hist + rows, :] = xbr_ref[...]
    xc = xp_s[0:rows, :] * cw_ref[0:1, :]
    for k in range(1, CONV_WIDTH):
        xc = xc + xp_s[k * nb:k * nb + rows, :] * cw_ref[k:k + 1, :]
    xc = xc + cb_ref[...]
    c8 = -LRU_C * _softplus(-lam_ref[...])
    for n in range(LRU_BLOCKS):
        sl = slice(n * LRU_BLOCK, (n + 1) * LRU_BLOCK)
        xcn = xc[:, sl]
        xcb = xcn.astype(BF16)
        r = jax.nn.sigmoid(jnp.dot(xcb, wga_ref[n], preferred_element_type=F32) + bga_ref[:, sl])
        ig = jax.nn.sigmoid(jnp.dot(xcb, wgx_ref[n], preferred_element_type=F32) + bgx_ref[:, sl])
        log_a = c8[:, sl] * r
        a_s[:, sl] = jnp.exp(log_a)
        h_s[nb:nb + rows, sl] = jnp.sqrt(-_expm1(2.0 * log_a)) * ig * xcn

    def step(t, carry):
        r0 = pl.multiple_of(t * nb, nb)
        r1 = pl.multiple_of(t * nb + nb, nb)
        h_s[pl.ds(r1, nb), :] = a_s[pl.ds(r0, nb), :] * h_s[pl.ds(r0, nb), :] + h_s[pl.ds(r1, nb), :]
        return carry

    lax.fori_loop(0, rows // nb, step, 0)

    y_ref[...] = (jax.nn.gelu(gate_ref[...]) * h_s[nb:nb + rows, :]).astype(BF16)
    tail = xp_s[rows:rows + hist, :]
    last = h_s[rows:rows + nb, :]
    xp_s[0:hist, :] = tail
    h_s[0:nb, :] = last
    cout_ref[...] = tail
    hout_ref[...] = last


def _lru_core(proj, prev, h0, cw, cb, wga, bga, wgx, bgx, lam, nb, rows):
    total = proj.shape[0]
    hist = (CONV_WIDTH - 1) * nb
    full = lambda shape: pl.BlockSpec(shape, lambda i: (0,) * len(shape))
    return pl.pallas_call(
        functools.partial(_lru_core_kernel, nb=nb, rows=rows),
        grid=(total // rows,),
        in_specs=[pl.BlockSpec((rows, LRU_WIDTH), lambda i: (i, 0)),
                  pl.BlockSpec((rows, LRU_WIDTH), lambda i: (i, 1)),
                  full((hist, LRU_WIDTH)), full((nb, LRU_WIDTH)),
                  full((CONV_WIDTH, LRU_WIDTH)), full((1, LRU_WIDTH)),
                  full(wga.shape), full((1, LRU_WIDTH)), full(wgx.shape), full((1, LRU_WIDTH)),
                  full((1, LRU_WIDTH))],
        out_specs=[pl.BlockSpec((rows, LRU_WIDTH), lambda i: (i, 0)),
                   full((nb, LRU_WIDTH)), full((hist, LRU_WIDTH))],
        out_shape=[jax.ShapeDtypeStruct((total, LRU_WIDTH), BF16),
                   jax.ShapeDtypeStruct((nb, LRU_WIDTH), F32),
                   jax.ShapeDtypeStruct((hist, LRU_WIDTH), F32)],
        scratch_shapes=[pltpu.VMEM((hist + rows, LRU_WIDTH), F32),
                        pltpu.VMEM((nb + rows, LRU_WIDTH), F32),
                        pltpu.VMEM((rows, LRU_WIDTH), F32)],
        compiler_params=_cparams(("arbitrary",)),
        name="lru_core",
    )(proj, proj, prev, h0, cw, cb, wga, bga, wgx, bgx, lam)


def _gdn_prep_kernel(qkv_ref, ab_ref, prev_ref, cw_ref, alog_ref, dtb_ref,
                     q_ref, k_ref, v_ref, g_ref, beta_ref, cout_ref, xp_s, *, nb, rows):
    i = pl.program_id(0)
    hist = (CONV_WIDTH - 1) * nb

    @pl.when(i == 0)
    def _():
        xp_s[0:hist, :] = prev_ref[...]

    xp_s[hist:hist + rows, :] = qkv_ref[...]
    for part, out in enumerate((q_ref, k_ref, v_ref)):
        for h in range(GDN_HEADS):
            lo = part * GDN_KEY_DIM + h * GDN_DK
            sl = slice(lo, lo + GDN_DK)
            acc = xp_s[0:rows, sl] * cw_ref[0:1, sl]
            for k in range(1, CONV_WIDTH):
                acc = acc + xp_s[k * nb:k * nb + rows, sl] * cw_ref[k:k + 1, sl]
            s = acc * jax.nn.sigmoid(acc)
            if part < 2:
                s = s * lax.rsqrt(jnp.sum(s * s, axis=-1, keepdims=True) + L2_EPS)
            if part == 0:
                s = s * (GDN_DK ** -0.5)
            out[:, h * GDN_DK:(h + 1) * GDN_DK] = s
    ab = ab_ref[...]
    g_ref[...] = -jnp.exp(alog_ref[...]) * _softplus(ab + dtb_ref[...])
    beta_ref[...] = jax.nn.sigmoid(ab)
    tail = xp_s[rows:rows + hist, :]
    xp_s[0:hist, :] = tail
    cout_ref[...] = tail


def _gdn_prep(proj, prev, cw, alog, dtb, nb, rows):
    total = proj.shape[0]
    hist = (CONV_WIDTH - 1) * nb
    full = lambda shape: pl.BlockSpec(shape, lambda i: (0,) * len(shape))
    tile = lambda n: pl.BlockSpec((rows, n), lambda i: (i, 0))
    return pl.pallas_call(
        functools.partial(_gdn_prep_kernel, nb=nb, rows=rows),
        grid=(total // rows,),
        in_specs=[tile(GDN_CONV_DIM),
                  pl.BlockSpec((rows, 128), lambda i: (i, (GDN_CONV_DIM + GDN_KEY_DIM) // 128)),
                  full((hist, GDN_CONV_DIM)), full((CONV_WIDTH, GDN_CONV_DIM)),
                  full((1, 128)), full((1, 128))],
        out_specs=[tile(GDN_KEY_DIM), tile(GDN_KEY_DIM), tile(GDN_KEY_DIM), tile(128), tile(128),
                   full((hist, GDN_CONV_DIM))],
        out_shape=[jax.ShapeDtypeStruct((total, GDN_KEY_DIM), F32)] * 3
                  + [jax.ShapeDtypeStruct((total, 128), F32)] * 2
                  + [jax.ShapeDtypeStruct((hist, GDN_CONV_DIM), F32)],
        scratch_shapes=[pltpu.VMEM((hist + rows, GDN_CONV_DIM), F32)],
        compiler_params=_cparams(("arbitrary",)),
        name="gdn_prep",
    )(proj, proj, prev, cw, alog, dtb)


def _unit_lower_inverses(ms, ri, ci, chunk):
    eye = (ri == ci).astype(F32)
    blk = (ri >> 3) == (ci >> 3)
    n1 = [jnp.where(blk, -m, 0.0) for m in ms]
    n2 = [_dot(a, a) for a in n1]
    n4 = [_dot(a, a) for a in n2]
    ts = [_dot(eye + a, eye + b) for a, b in zip(n1, n2)]
    ts = [_dot(t, eye + a) for t, a in zip(ts, n4)]
    shift = 3
    while (1 << shift) < chunk:
        pair = ((ri >> (shift + 1)) == (ci >> (shift + 1))) & ((ri >> shift) != (ci >> shift))
        left = [_dot(t, jnp.where(pair, m, 0.0)) for t, m in zip(ts, ms)]
        ts = [t - _dot(a, t) for t, a in zip(ts, left)]
        shift += 1
    return ts


def _gdn_core_kernel(q_ref, k_ref, v_ref, z_ref, g_ref, beta_ref, s0_ref, nw_ref, o_ref, s_ref, *, chunk, bb):
    c = pl.program_id(1)

    @pl.when(c == 0)
    def _():
        s_ref[...] = s0_ref[...]

    ri = lax.broadcasted_iota(jnp.int32, (chunk, chunk), 0)
    ci = lax.broadcasted_iota(jnp.int32, (chunk, chunk), 1)
    causal = ri >= ci
    strict = ri > ci
    tril = causal.astype(BF16)
    e_r = lax.broadcasted_iota(jnp.int32, (128, 128), 0)
    e_c = lax.broadcasted_iota(jnp.int32, (128, 128), 1)
    eye128 = (e_r == e_c).astype(BF16)
    dotf = functools.partial(jnp.dot, preferred_element_type=F32)
    nt = lambda a, b: lax.dot_general(a, b, (((1,), (1,)), ((), ())), preferred_element_type=F32)
    nw = nw_ref[...]

    cums, cum_ts, ecums, e_lasts, e_rests, betas = [], [], [], [], [], []
    for bi in range(bb):
        g3 = _split3(g_ref[bi])
        cum = dotf(tril, g3[0]) + dotf(tril, g3[1]) + dotf(tril, g3[2])
        c3 = _split3(cum)
        cums.append(cum)
        cum_ts.append(nt(eye128, c3[0]) + nt(eye128, c3[1]) + nt(eye128, c3[2]))
        ecums.append(jnp.exp(cum))
        g_last = cum[chunk - 1:chunk, :]
        e_lasts.append(jnp.exp(g_last))
        e_rests.append(jnp.exp(g_last - cum))
        betas.append(beta_ref[bi])

    units = [(bi, h) for bi in range(bb) for h in range(GDN_HEADS)]
    col = lambda a, h: a[:, h:h + 1]
    sl = lambda h: slice(h * GDN_DK, (h + 1) * GDN_DK)
    q = [q_ref[bi, :, sl(h)] for bi, h in units]
    k = [k_ref[bi, :, sl(h)] for bi, h in units]
    decay = [jnp.exp(jnp.where(causal, col(cums[bi], h) - cum_ts[bi][h:h + 1, :], -jnp.inf)) for bi, h in units]
    k_beta = [kk * col(betas[bi], GDN_HEADS + h) for kk, (bi, h) in zip(k, units)]
    ak = [_dot_nt(jnp.concatenate([kb, qq], axis=0), kk) for kb, qq, kk in zip(k_beta, q, k)]
    ms = [jnp.where(strict, a[:chunk] * d, 0.0) for a, d in zip(ak, decay)]
    ts = _unit_lower_inverses(ms, ri, ci, chunk)
    rhs = [jnp.concatenate([v_ref[bi, :, sl(h)] * col(betas[bi], GDN_HEADS + h), kb * col(ecums[bi], h)], axis=1)
           for kb, (bi, h) in zip(k_beta, units)]
    sol = [_dot(t, r) for t, r in zip(ts, rhs)]
    s_old = [s_ref[bi, h] for bi, h in units]
    ws = [_dot(jnp.concatenate([so[:, GDN_DV:], qq * col(ecums[bi], h)], axis=0), s)
          for so, qq, s, (bi, h) in zip(sol, q, s_old, units)]
    v_new = [so[:, :GDN_DV] - w[:chunk] for so, w in zip(sol, ws)]
    o = [w[chunk:] + _dot(a[chunk:] * d, vn) for w, a, d, vn in zip(ws, ak, decay, v_new)]
    k_dec_t = [nt(eye128, (kk * col(e_rests[bi], h)).astype(BF16)) for kk, (bi, h) in zip(k, units)]
    for (bi, h), s, kt, vn, oo in zip(units, s_old, k_dec_t, v_new, o):
        s_ref[bi, h] = s * col(e_lasts[bi], h) + _dot(kt, vn)
        on = oo * lax.rsqrt(jnp.mean(oo * oo, axis=-1, keepdims=True) + RMS_EPS) * nw
        zh = z_ref[bi, :, sl(h)]
        o_ref[bi, :, sl(h)] = (on * (zh * jax.nn.sigmoid(zh))).astype(BF16)


def _gdn_core(q, k, v, z, g, beta, s0, nw, chunk, bb):
    nb, lp = q.shape[0], q.shape[1]
    seq = lambda n: pl.BlockSpec((bb, chunk, n), lambda b, c: (b, c, 0))
    st = pl.BlockSpec((bb, GDN_HEADS, GDN_DK, GDN_DV), lambda b, c: (b, 0, 0, 0))
    return pl.pallas_call(
        functools.partial(_gdn_core_kernel, chunk=chunk, bb=bb),
        grid=(nb // bb, lp // chunk),
        in_specs=[seq(GDN_KEY_DIM), seq(GDN_KEY_DIM), seq(GDN_KEY_DIM), seq(GDN_KEY_DIM),
                  seq(128), seq(128), st, pl.BlockSpec((1, GDN_DV), lambda b, c: (0, 0))],
        out_specs=[seq(GDN_KEY_DIM), st],
        out_shape=[jax.ShapeDtypeStruct((nb, lp, GDN_KEY_DIM), BF16),
                   jax.ShapeDtypeStruct(s0.shape, F32)],
        compiler_params=_cparams(("parallel", "arbitrary")),
        name="gdn_core",
    )(q, k, v, z, g, beta, s0, nw)


def _ffn_kernel(x_ref, g_ref, wa_ref, wb_ref, cwa_ref, cwb_ref, cba_ref, cbb_ref, pa_ref, pb_ref, wd_ref,
                gf_ref, o_ref, ca_out, cb_out, xn_s, acc_s, hpa_s, hpb_s, cara_s, carb_s,
                *, nb, tm, final_norm):
    i = pl.program_id(0)
    j = pl.program_id(1)
    hist = (FFN_CONV_WIDTH - 1) * nb

    @pl.when(j == 0)
    def _():
        xn_s[...] = _rms(x_ref[...], g_ref[...]).astype(BF16)
        acc_s[...] = jnp.zeros_like(acc_s)

    @pl.when(i == 0)
    def _():
        hpa_s[0:hist, :] = pa_ref[...]
        hpb_s[0:hist, :] = pb_ref[...]

    @pl.when(i > 0)
    def _():
        hpa_s[0:hist, :] = cara_s[j]
        hpb_s[0:hist, :] = carb_s[j]

    rs = min(SUB_ROWS, tm)
    nsub = tm // rs

    def up(r):
        xr = xn_s[r * rs:(r + 1) * rs, :]
        hpa_s[hist + r * rs:hist + (r + 1) * rs, :] = jnp.dot(xr, wa_ref[...], preferred_element_type=F32)
        hpb_s[hist + r * rs:hist + (r + 1) * rs, :] = jnp.dot(xr, wb_ref[...], preferred_element_type=F32)

    def conv(hp_s, cw_ref, cb_ref, r):
        y = hp_s[r * rs:(r + 1) * rs, :] * cw_ref[0:1, :]
        for k in range(1, FFN_CONV_WIDTH):
            y = y + hp_s[k * nb + r * rs:k * nb + (r + 1) * rs, :] * cw_ref[k:k + 1, :]
        return y + cb_ref[...]

    def down(r):
        act = (jax.nn.gelu(conv(hpa_s, cwa_ref, cba_ref, r)) * conv(hpb_s, cwb_ref, cbb_ref, r)).astype(BF16)
        acc_s[r * rs:(r + 1) * rs, :] += jnp.dot(act, wd_ref[...], preferred_element_type=F32)

    up(0)
    for r in range(nsub):
        if r + 1 < nsub:
            up(r + 1)
        down(r)

    tail_a = hpa_s[tm:tm + hist, :]
    tail_b = hpb_s[tm:tm + hist, :]
    cara_s[j] = tail_a
    carb_s[j] = tail_b
    ca_out[...] = tail_a
    cb_out[...] = tail_b

    @pl.when(j == pl.num_programs(1) - 1)
    def _():
        y = x_ref[...] + acc_s[...]
        if final_norm:
            y = _rms(y, gf_ref[...])
        o_ref[...] = y


def _ffn(x, g, w_up, cw, cb, prev, w_down, g_final, nb, tm, final_norm):
    rows = x.shape[0]
    tn = FFN_TN
    nj = FFN_HIDDEN // tn
    hist = (FFN_CONV_WIDTH - 1) * nb
    col_a = lambda r: pl.BlockSpec((r, tn), lambda i, j: (0, j))
    col_b = lambda r: pl.BlockSpec((r, tn), lambda i, j: (0, nj + j))
    vec = pl.BlockSpec((1, D_MODEL), lambda i, j: (0, 0))
    return pl.pallas_call(
        functools.partial(_ffn_kernel, nb=nb, tm=tm, final_norm=final_norm),
        grid=(rows // tm, nj),
        in_specs=[pl.BlockSpec((tm, D_MODEL), lambda i, j: (i, 0)), vec,
                  col_a(D_MODEL), col_b(D_MODEL),
                  col_a(FFN_CONV_WIDTH), col_b(FFN_CONV_WIDTH), col_a(1), col_b(1),
                  col_a(hist), col_b(hist),
                  pl.BlockSpec((tn, D_MODEL), lambda i, j: (j, 0)), vec],
        out_specs=[pl.BlockSpec((tm, D_MODEL), lambda i, j: (i, 0)),
                   pl.BlockSpec((hist, tn), lambda i, j: (i, j)),
                   pl.BlockSpec((hist, tn), lambda i, j: (i, j))],
        out_shape=[jax.ShapeDtypeStruct((rows, D_MODEL), F32),
                   jax.ShapeDtypeStruct((rows // tm * hist, FFN_HIDDEN), F32),
                   jax.ShapeDtypeStruct((rows // tm * hist, FFN_HIDDEN), F32)],
        scratch_shapes=[pltpu.VMEM((tm, D_MODEL), BF16), pltpu.VMEM((tm, D_MODEL), F32),
                        pltpu.VMEM((hist + tm, tn), F32), pltpu.VMEM((hist + tm, tn), F32),
                        pltpu.VMEM((nj, hist, tn), F32), pltpu.VMEM((nj, hist, tn), F32)],
        compiler_params=_cparams(("arbitrary", "arbitrary")),
        name="conv_ffn",
    )(x, g, w_up, w_up, cw, cw, cb, cb, prev, prev, w_down, g_final)


def _s5_disc_kernel(are_ref, aim_ref, ldt_ref, bre_ref, bim_ref, abr_ref, abi_ref, bbr_ref, bbi_ref):
    a_re, a_im = are_ref[...], aim_ref[...]
    dt = jnp.exp(ldt_ref[...])
    mag = jnp.exp(a_re * dt)
    ar = mag * jnp.cos(a_im * dt)
    ai = mag * jnp.sin(a_im * dt)
    den = a_re * a_re + a_im * a_im
    nr = ar - 1.0
    cr = (nr * a_re + ai * a_im) / den
    ci = (ai * a_re - nr * a_im) / den
    b_re, b_im = bre_ref[...], bim_ref[...]
    abr_ref[...] = ar
    abi_ref[...] = ai
    bbr_ref[...] = cr * b_re - ci * b_im
    bbi_ref[...] = cr * b_im + ci * b_re


def _s5_params(a_re, a_im, log_dt, b_re, b_im, c_re, c_im):
    rep = lambda a: jnp.repeat(a.astype(F32), S5_GROUP_CH, axis=0)
    rows_gc = lambda b: b.astype(F32).transpose(0, 2, 1).reshape(D_MODEL, S5_STATE)
    ldt = jnp.broadcast_to(log_dt.astype(F32)[:, None], (S5_GROUPS, S5_STATE))
    sds = jax.ShapeDtypeStruct((D_MODEL, S5_STATE), F32)
    abr, abi, bbr, bbi = pl.pallas_call(_s5_disc_kernel, out_shape=[sds] * 4, name="s5_discretize")(
        rep(a_re), rep(a_im), rep(ldt), rows_gc(b_re), rows_gc(b_im))
    eye = jnp.eye(S5_GROUPS // S5_KB, dtype=F32)

    def b_blocks(b):
        b = b.reshape(S5_KB, S5_GROUPS // S5_KB, S5_GROUP_CH, S5_STATE)
        return jnp.einsum('kgcp,gh->kgchp', b, eye).reshape(S5_KB, D_MODEL // S5_KB, S5_COLS // S5_KB).astype(BF16)

    def c_blocks(c):
        c = c.astype(F32).reshape(S5_KB, S5_GROUPS // S5_KB, S5_GROUP_CH, S5_STATE)
        return jnp.einsum('kgcp,gh->kgphc', c, eye).reshape(S5_KB, S5_COLS // S5_KB, D_MODEL // S5_KB).astype(BF16)

    return (b_blocks(bbr), b_blocks(bbi), c_blocks(c_re), c_blocks(c_im),
            abr[::S5_GROUP_CH].reshape(1, S5_COLS), abi[::S5_GROUP_CH].reshape(1, S5_COLS))


def _to_time_major(a):
    return a.transpose(1, 0, 2).reshape(a.shape[0] * a.shape[1], a.shape[2])


def _from_time_major(a, nb):
    return a.reshape(a.shape[0] // nb, nb, a.shape[1]).transpose(1, 0, 2)


def _trunk(x, nb, seq, s5_re, s5_im, lru_h, lru_conv, gdn_s, gdn_conv, ffn_conv, p):
    total = seq * nb
    tm = min(total, 1024)
    rows = 512
    o_s5_re, o_s5_im, o_lru, o_lru_conv, o_gdn, o_gdn_conv, o_ffn_conv = [], [], [], [], [], [], []
    depth = p['norm_mix'].shape[0]
    for i in range(depth):
        kind, j = i % 3, i // 3
        g_mix = p['norm_mix'][i].reshape(1, D_MODEL)
        if kind == 0:
            bre, bim, cre, cim, are, aim = p['s5_disc'][j]
            u = _norm_mm(x, g_mix, p['s5_w_in'][j].astype(BF16), rows)
            y, hre, him = _s5_core(u, s5_re[j].reshape(nb, S5_COLS), s5_im[j].reshape(nb, S5_COLS),
                                   bre, bim, cre, cim, are, aim, p['s5_d'][j].reshape(1, D_MODEL), nb, rows)
            x = _mm_glu_res(y, p['s5_w_glu'][j].astype(BF16), x, tm, 512)
            o_s5_re.append(hre.reshape(nb, S5_GROUPS, S5_STATE))
            o_s5_im.append(him.reshape(nb, S5_GROUPS, S5_STATE))
        elif kind == 1:
            proj = _norm_mm(x, g_mix, p['lru_w_in'][j].astype(BF16), rows)
            y, h_new, conv_new = _lru_core(
                proj, _to_time_major(lru_conv[j]), lru_h[j],
                p['lru_conv_w'][j], p['lru_conv_b'][j].reshape(1, LRU_WIDTH),
                p['lru_w_gate_a'][j].astype(BF16), p['lru_b_gate_a'][j].reshape(1, LRU_WIDTH),
                p['lru_w_gate_x'][j].astype(BF16), p['lru_b_gate_x'][j].reshape(1, LRU_WIDTH),
                p['lru_lambda'][j].reshape(1, LRU_WIDTH), nb, rows)
            x = _mm_res(y, p['lru_w_out'][j].astype(BF16), x, tm, 512)
            o_lru.append(h_new)
            o_lru_conv.append(_from_time_major(conv_new, nb))
        else:
            w_in = p['gdn_w_in'][j]
            w_pad = jnp.pad(w_in, ((0, 0), (0, GDN_PROJ_PAD - w_in.shape[1]))).astype(BF16)
            proj = _norm_mm(x, g_mix, w_pad, rows)
            pad8 = lambda a: jnp.pad(a.reshape(1, GDN_HEADS), ((0, 0), (0, 128 - GDN_HEADS)))
            q, k, v, g, beta, conv_new = _gdn_prep(proj, _to_time_major(gdn_conv[j]), p['gdn_conv_w'][j],
                                                   pad8(p['gdn_a_log'][j]), pad8(p['gdn_dt_bias'][j]), nb, rows)
            chunk = GDN_CHUNK if seq >= GDN_CHUNK else 8
            lp = -(-seq // chunk) * chunk
            z = proj[:, GDN_CONV_DIM:GDN_CONV_DIM + GDN_KEY_DIM]

            def bm(a):
                a = a.reshape(seq, nb, a.shape[1]).transpose(1, 0, 2)
                return jnp.pad(a, ((0, 0), (0, lp - seq), (0, 0)))

            o, s_new = _gdn_core(bm(q), bm(k), bm(v), bm(z), bm(g), bm(beta), gdn_s[j],
                                 p['gdn_norm'][j].reshape(1, GDN_DV), chunk, 1 if chunk == GDN_CHUNK else 4)
            o = o[:, :seq].transpose(1, 0, 2).reshape(total, GDN_KEY_DIM)
            x = _mm_res(o, p['gdn_w_out'][j].astype(BF16), x, tm, 512)
            o_gdn.append(s_new)
            o_gdn_conv.append(_from_time_major(conv_new, nb))
        x, ca, cb = _ffn(x, p['norm_ffn'][i].reshape(1, D_MODEL), p['ffn_w_up'][i].astype(BF16),
                         p['ffn_conv_w'][i], p['ffn_conv_b'][i].reshape(1, 2 * FFN_HIDDEN),
                         _to_time_major(ffn_conv[i]), p['ffn_w_down'][i].astype(BF16),
                         p['norm_final'].reshape(1, D_MODEL), nb, tm, i == depth - 1)
        hist = (FFN_CONV_WIDTH - 1) * nb
        o_ffn_conv.append(_from_time_major(jnp.concatenate([ca[-hist:], cb[-hist:]], axis=1), nb))
    return (x, jnp.stack(o_s5_re), jnp.stack(o_s5_im), jnp.stack(o_lru), jnp.stack(o_lru_conv),
            jnp.stack(o_gdn), jnp.stack(o_gdn_conv), jnp.stack(o_ffn_conv))


def kernel(x_prompt, x_sample, state_s5_re, state_s5_im, state_lru, state_lru_conv, state_gdn, state_gdn_conv, state_ffn_conv, norm_mix, norm_ffn, norm_final, s5_w_in, s5_a_re, s5_a_im, s5_log_dt, s5_b_re, s5_b_im, s5_c_re, s5_c_im, s5_d, s5_w_glu, lru_w_in, lru_conv_w, lru_conv_b, lru_w_gate_a, lru_b_gate_a, lru_w_gate_x, lru_b_gate_x, lru_lambda, lru_w_out, gdn_w_in, gdn_conv_w, gdn_a_log, gdn_dt_bias, gdn_norm, gdn_w_out, ffn_w_up, ffn_conv_w, ffn_conv_b, ffn_w_down):
    p = dict(norm_mix=norm_mix, norm_ffn=norm_ffn, norm_final=norm_final, s5_w_in=s5_w_in, s5_a_re=s5_a_re,
             s5_a_im=s5_a_im, s5_log_dt=s5_log_dt, s5_b_re=s5_b_re, s5_b_im=s5_b_im, s5_c_re=s5_c_re,
             s5_c_im=s5_c_im, s5_d=s5_d, s5_w_glu=s5_w_glu, lru_w_in=lru_w_in, lru_conv_w=lru_conv_w,
             lru_conv_b=lru_conv_b, lru_w_gate_a=lru_w_gate_a, lru_b_gate_a=lru_b_gate_a,
             lru_w_gate_x=lru_w_gate_x, lru_b_gate_x=lru_b_gate_x, lru_lambda=lru_lambda, lru_w_out=lru_w_out,
             gdn_w_in=gdn_w_in, gdn_conv_w=gdn_conv_w, gdn_a_log=gdn_a_log, gdn_dt_bias=gdn_dt_bias,
             gdn_norm=gdn_norm, gdn_w_out=gdn_w_out, ffn_w_up=ffn_w_up, ffn_conv_w=ffn_conv_w,
             ffn_conv_b=ffn_conv_b, ffn_w_down=ffn_w_down)
    p['s5_disc'] = [_s5_params(s5_a_re[j], s5_a_im[j], s5_log_dt[j], s5_b_re[j], s5_b_im[j], s5_c_re[j],
                               s5_c_im[j]) for j in range(s5_a_re.shape[0])]
    outs = []
    for x, states in (
            (x_prompt, None),
            (x_sample, (state_s5_re, state_s5_im, state_lru, state_lru_conv, state_gdn, state_gdn_conv,
                        state_ffn_conv))):
        nb, seq, _ = x.shape
        if states is None:
            states = tuple(jnp.zeros((s.shape[0], nb) + s.shape[2:], F32) for s in (
                state_s5_re, state_s5_im, state_lru, state_lru_conv, state_gdn, state_gdn_conv, state_ffn_conv))
        res = _trunk(_to_time_major(x), nb, seq, *states, p)
        outs.append((_from_time_major(res[0], nb),) + tuple(res[1:]))
    (y_p, *st_p), (y_s, *st_s) = outs
    return (y_p, y_s, *st_p, *st_s)
```

```python
import functools
import math

import jax
import jax.numpy as jnp
from jax import lax
from jax.experimental import pallas as pl
from jax.experimental.pallas import tpu as pltpu

F32 = jnp.float32
BF16 = jnp.bfloat16

D_MODEL = 1024
RMS_EPS = 1e-6
L2_EPS = 1e-6
S5_GROUPS = 64
S5_STATE = 64
S5_GROUP_CH = 16
S5_COLS = S5_GROUPS * S5_STATE
S5_KB = 8
S5_SCAN_LANES = 1024
LRU_WIDTH = 1280
LRU_BLOCK = 128
LRU_BLOCKS = LRU_WIDTH // LRU_BLOCK
LRU_C = 8.0
CONV_WIDTH = 4
GDN_HEADS = 8
GDN_DK = 128
GDN_DV = 128
GDN_KEY_DIM = GDN_HEADS * GDN_DK
GDN_CONV_DIM = 3 * GDN_KEY_DIM
GDN_CHUNK = 64
GDN_PROJ_PAD = 4224
FFN_HIDDEN = 2816
FFN_CONV_WIDTH = 3
FFN_TN = 256
SUB_ROWS = 256
VMEM_LIMIT_BYTES = 56 * 1024 * 1024


def _cparams(sem):
    return pltpu.CompilerParams(dimension_semantics=sem, vmem_limit_bytes=VMEM_LIMIT_BYTES)


def _rms(x, g):
    ms = jnp.mean(x * x, axis=-1, keepdims=True)
    return x * lax.rsqrt(ms + RMS_EPS) * g


def _softplus(x):
    return jnp.maximum(x, 0.0) + jnp.log1p(jnp.exp(-jnp.abs(x)))


def _expm1(x):
    u = jnp.exp(x)
    small = jnp.abs(x) < 0.5
    usable = small & (u != 1.0)
    ratio = (u - 1.0) * x / jnp.log(jnp.where(usable, u, 2.0))
    return jnp.where(small, jnp.where(usable, ratio, x), u - 1.0)


def _dot(a, b):
    return jnp.dot(a.astype(BF16), b.astype(BF16), preferred_element_type=F32)


def _dot_nt(a, b):
    return lax.dot_general(a.astype(BF16), b.astype(BF16), (((1,), (1,)), ((), ())),
                           preferred_element_type=F32)


def _split2(a):
    hi = a.astype(BF16)
    lo = (a - hi.astype(F32)).astype(BF16)
    return hi, lo


def _split3(a):
    hi = a.astype(BF16)
    r = a - hi.astype(F32)
    mid = r.astype(BF16)
    lo = (r - mid.astype(F32)).astype(BF16)
    return hi, mid, lo


def _dot3(a, b):
    ah, al = _split2(a)
    bh, bl = _split2(b)
    d = functools.partial(jnp.dot, preferred_element_type=F32)
    return d(ah, bh) + d(al, bh) + d(ah, bl)


def _norm_mm_kernel(x_ref, g_ref, w_ref, o_ref, xn_ref, *, tm):
    sub = lambda r: slice(r * SUB_ROWS, (r + 1) * SUB_ROWS)

    def norm(r):
        xn_ref[sub(r), :] = _rms(x_ref[sub(r), :], g_ref[...]).astype(BF16)

    nsub = tm // SUB_ROWS
    norm(0)
    for r in range(nsub):
        if r + 1 < nsub:
            norm(r + 1)
        o_ref[sub(r), :] = jnp.dot(xn_ref[sub(r), :], w_ref[...], preferred_element_type=F32)


def _norm_mm(x, g, w, tm):
    rows, n = x.shape[0], w.shape[1]
    return pl.pallas_call(
        functools.partial(_norm_mm_kernel, tm=tm),
        grid=(rows // tm,),
        in_specs=[pl.BlockSpec((tm, D_MODEL), lambda i: (i, 0)),
                  pl.BlockSpec((1, D_MODEL), lambda i: (0, 0)),
                  pl.BlockSpec((D_MODEL, n), lambda i: (0, 0))],
        out_specs=pl.BlockSpec((tm, n), lambda i: (i, 0)),
        out_shape=jax.ShapeDtypeStruct((rows, n), F32),
        scratch_shapes=[pltpu.VMEM((tm, D_MODEL), BF16)],
        compiler_params=_cparams(("parallel",)),
        name="norm_mm",
    )(x, g, w)


def _mm_res_kernel(a_ref, w_ref, r_ref, o_ref):
    o_ref[...] = r_ref[...] + jnp.dot(a_ref[...], w_ref[...], preferred_element_type=F32)


def _mm_res(a, w, res, tm, tn):
    rows, k = a.shape
    n = w.shape[1]
    return pl.pallas_call(
        _mm_res_kernel,
        grid=(rows // tm, n // tn),
        in_specs=[pl.BlockSpec((tm, k), lambda i, j: (i, 0)),
                  pl.BlockSpec((k, tn), lambda i, j: (0, j)),
                  pl.BlockSpec((tm, tn), lambda i, j: (i, j))],
        out_specs=pl.BlockSpec((tm, tn), lambda i, j: (i, j)),
        out_shape=jax.ShapeDtypeStruct((rows, n), F32),
        compiler_params=_cparams(("parallel", "parallel")),
        name="mm_res",
    )(a, w, res)


def _mm_res_seq_major_kernel(a_ref, w_ref, r_ref, o_ref, a_s, *, nb, tm):
    nk = a_ref.shape[2] // 128
    for b in range(nb):
        for kt in range(nk):
            a_s[kt, pl.ds(b, tm // nb, stride=nb), :] = a_ref[b, :, kt * 128:(kt + 1) * 128]
    a = jnp.concatenate([a_s[kt].astype(BF16) for kt in range(nk)], axis=1)
    o_ref[...] = r_ref[...] + jnp.dot(a, w_ref[...], preferred_element_type=F32)


def _mm_res_seq_major(a, w, res, nb, tm):
    k, n = w.shape
    rows = res.shape[0]
    return pl.pallas_call(
        functools.partial(_mm_res_seq_major_kernel, nb=nb, tm=tm),
        grid=(rows // tm,),
        in_specs=[pl.BlockSpec((nb, tm // nb, k), lambda i: (0, i, 0)),
                  pl.BlockSpec((k, n), lambda i: (0, 0)),
                  pl.BlockSpec((tm, n), lambda i: (i, 0))],
        out_specs=pl.BlockSpec((tm, n), lambda i: (i, 0)),
        out_shape=jax.ShapeDtypeStruct((rows, n), F32),
        scratch_shapes=[pltpu.VMEM((k // 128, tm, 128), F32)],
        compiler_params=_cparams(("parallel",)),
        name="mm_res_seq_major",
    )(a, w, res)


def _mm_glu_res_kernel(a_ref, wv_ref, wg_ref, r_ref, o_ref):
    a = a_ref[...]
    val = jnp.dot(a, wv_ref[...], preferred_element_type=F32)
    gate = jnp.dot(a, wg_ref[...], preferred_element_type=F32)
    o_ref[...] = r_ref[...] + val * jax.nn.sigmoid(gate)


def _mm_glu_res(a, w, res, tm, tn):
    rows, k = a.shape
    n = w.shape[1] // 2
    nj = n // tn
    return pl.pallas_call(
        _mm_glu_res_kernel,
        grid=(rows // tm, nj),
        in_specs=[pl.BlockSpec((tm, k), lambda i, j: (i, 0)),
                  pl.BlockSpec((k, tn), lambda i, j: (0, j)),
                  pl.BlockSpec((k, tn), lambda i, j: (0, nj + j)),
                  pl.BlockSpec((tm, tn), lambda i, j: (i, j))],
        out_specs=pl.BlockSpec((tm, tn), lambda i, j: (i, j)),
        out_shape=jax.ShapeDtypeStruct((rows, n), F32),
        compiler_params=_cparams(("parallel", "parallel")),
        name="mm_glu_res",
    )(a, w, w, res)


def _s5_core_kernel(u_ref, h0re_ref, h0im_ref, bre_ref, bim_ref, cre_ref, cim_ref, are_ref, aim_ref,
                    d_ref, y_ref, hre_out, him_out, hre_s, him_s, *, nb, rows):
    i = pl.program_id(0)

    @pl.when(i == 0)
    def _():
        hre_s[0:nb, :] = h0re_ref[...]
        him_s[0:nb, :] = h0im_ref[...]

    kw = S5_COLS // S5_KB
    uw = D_MODEL // S5_KB
    per_grp = S5_SCAN_LANES // kw
    n_grp = S5_COLS // S5_SCAN_LANES

    def project_in(grp):
        for kb in range(grp * per_grp, (grp + 1) * per_grp):
            ukb = u_ref[:, kb * uw:(kb + 1) * uw].astype(BF16)
            hre_s[nb:nb + rows, kb * kw:(kb + 1) * kw] = jnp.dot(ukb, bre_ref[kb], preferred_element_type=F32)
            him_s[nb:nb + rows, kb * kw:(kb + 1) * kw] = jnp.dot(ukb, bim_ref[kb], preferred_element_type=F32)

    def scan(grp):
        cols = slice(grp * S5_SCAN_LANES, (grp + 1) * S5_SCAN_LANES)
        are = jnp.broadcast_to(are_ref[:, cols], (nb, S5_SCAN_LANES))
        aim = jnp.broadcast_to(aim_ref[:, cols], (nb, S5_SCAN_LANES))
        hr, hi = hre_s[0:nb, cols], him_s[0:nb, cols]
        for t in range(rows // nb):
            r = slice(nb + t * nb, 2 * nb + t * nb)
            hr, hi = (are * hr - aim * hi + hre_s[r, cols],
                      are * hi + aim * hr + him_s[r, cols])
            hre_s[r, cols] = hr
            him_s[r, cols] = hi

    def project_out(grp):
        for kb in range(grp * per_grp, (grp + 1) * per_grp):
            hr = hre_s[nb:nb + rows, kb * kw:(kb + 1) * kw].astype(BF16)
            hi = him_s[nb:nb + rows, kb * kw:(kb + 1) * kw].astype(BF16)
            yk = (jnp.dot(hr, cre_ref[kb], preferred_element_type=F32)
                  - jnp.dot(hi, cim_ref[kb], preferred_element_type=F32))
            yk = yk + d_ref[:, kb * uw:(kb + 1) * uw] * u_ref[:, kb * uw:(kb + 1) * uw]
            y_ref[:, kb * uw:(kb + 1) * uw] = jax.nn.gelu(yk).astype(BF16)

    project_in(0)
    for grp in range(n_grp):
        if grp + 1 < n_grp:
            project_in(grp + 1)
        scan(grp)
        project_out(grp)

    last_re = hre_s[rows:rows + nb, :]
    last_im = him_s[rows:rows + nb, :]
    hre_s[0:nb, :] = last_re
    him_s[0:nb, :] = last_im
    hre_out[...] = last_re
    him_out[...] = last_im


def _s5_core(u, h0re, h0im, bre, bim, cre, cim, are, aim, d, nb, rows):
    total = u.shape[0]
    full = lambda shape: pl.BlockSpec(shape, lambda i: (0,) * len(shape))
    return pl.pallas_call(
        functools.partial(_s5_core_kernel, nb=nb, rows=rows),
        grid=(total // rows,),
        in_specs=[pl.BlockSpec((rows, D_MODEL), lambda i: (i, 0)),
                  full((nb, S5_COLS)), full((nb, S5_COLS)),
                  full(bre.shape), full(bim.shape), full(cre.shape), full(cim.shape),
                  full((1, S5_COLS)), full((1, S5_COLS)), full((1, D_MODEL))],
        out_specs=[pl.BlockSpec((rows, D_MODEL), lambda i: (i, 0)),
                   full((nb, S5_COLS)), full((nb, S5_COLS))],
        out_shape=[jax.ShapeDtypeStruct((total, D_MODEL), BF16),
                   jax.ShapeDtypeStruct((nb, S5_COLS), F32),
                   jax.ShapeDtypeStruct((nb, S5_COLS), F32)],
        scratch_shapes=[pltpu.VMEM((nb + rows, S5_COLS), F32),
                        pltpu.VMEM((nb + rows, S5_COLS), F32)],
        compiler_params=_cparams(("arbitrary",)),
        name="s5_core",
    )(u, h0re, h0im, bre, bim, cre, cim, are, aim, d)


def _lru_core_kernel(gate_ref, xbr_ref, prev_ref, h0_ref, cw_ref, cb_ref, wga_ref, bga_ref, wgx_ref,
                     bgx_ref, lam_ref, y_ref, hout_ref, cout_ref, xp_s, h_s, a_s, *, nb, rows):
    i = pl.program_id(0)
    hist = (CONV_WIDTH - 1) * nb

    @pl.when(i == 0)
    def _():
        xp_s[0:hist, :] = prev_ref[...]
        h_s[0:nb, :] = h0_ref[...]

    xp_s[hist:hist + rows, :] = xbr_ref[...]
    xc = xp_s[0:rows, :] * cw_ref[0:1, :]
    for k in range(1, CONV_WIDTH):
        xc = xc + xp_s[k * nb:k * nb + rows, :] * cw_ref[k:k + 1, :]
    xc = xc + cb_ref[...]
    c8 = -LRU_C * _softplus(-lam_ref[...])
    for n in range(LRU_BLOCKS):
        sl = slice(n * LRU_BLOCK, (n + 1) * LRU_BLOCK)
        xcn = xc[:, sl]
        xcb = xcn.astype(BF16)
        r = jax.nn.sigmoid(jnp.dot(xcb, wga_ref[n], preferred_element_type=F32) + bga_ref[:, sl])
        ig = jax.nn.sigmoid(jnp.dot(xcb, wgx_ref[n], preferred_element_type=F32) + bgx_ref[:, sl])
        log_a = c8[:, sl] * r
        a_s[:, sl] = jnp.exp(log_a)
        h_s[nb:nb + rows, sl] = jnp.sqrt(-_expm1(2.0 * log_a)) * ig * xcn

    def step(t, carry):
        r0 = pl.multiple_of(t * nb, nb)
        r1 = pl.multiple_of(t * nb + nb, nb)
        h_s[pl.ds(r1, nb), :] = a_s[pl.ds(r0, nb), :] * h_s[pl.ds(r0, nb), :] + h_s[pl.ds(r1, nb), :]
        return carry

    lax.fori_loop(0, rows // nb, step, 0)

    y_ref[...] = (jax.nn.gelu(gate_ref[...]) * h_s[nb:nb + rows, :]).astype(BF16)
    tail = xp_s[rows:rows + hist, :]
    last = h_s[rows:rows + nb, :]
    xp_s[0:hist, :] = tail
    h_s[0:nb, :] = last
    cout_ref[...] = tail
    hout_ref[...] = last


def _lru_core(proj, prev, h0, cw, cb, wga, bga, wgx, bgx, lam, nb, rows):
    total = proj.shape[0]
    hist = (CONV_WIDTH - 1) * nb
    full = lambda shape: pl.BlockSpec(shape, lambda i: (0,) * len(shape))
    return pl.pallas_call(
        functools.partial(_lru_core_kernel, nb=nb, rows=rows),
        grid=(total // rows,),
        in_specs=[pl.BlockSpec((rows, LRU_WIDTH), lambda i: (i, 0)),
                  pl.BlockSpec((rows, LRU_WIDTH), lambda i: (i, 1)),
                  full((hist, LRU_WIDTH)), full((nb, LRU_WIDTH)),
                  full((CONV_WIDTH, LRU_WIDTH)), full((1, LRU_WIDTH)),
                  full(wga.shape), full((1, LRU_WIDTH)), full(wgx.shape), full((1, LRU_WIDTH)),
                  full((1, LRU_WIDTH))],
        out_specs=[pl.BlockSpec((rows, LRU_WIDTH), lambda i: (i, 0)),
                   full((nb, LRU_WIDTH)), full((hist, LRU_WIDTH))],
        out_shape=[jax.ShapeDtypeStruct((total, LRU_WIDTH), BF16),
                   jax.ShapeDtypeStruct((nb, LRU_WIDTH), F32),
                   jax.ShapeDtypeStruct((hist, LRU_WIDTH), F32)],
        scratch_shapes=[pltpu.VMEM((hist + rows, LRU_WIDTH), F32),
                        pltpu.VMEM((nb + rows, LRU_WIDTH), F32),
                        pltpu.VMEM((rows, LRU_WIDTH), F32)],
        compiler_params=_cparams(("arbitrary",)),
        name="lru_core",
    )(proj, proj, prev, h0, cw, cb, wga, bga, wgx, bgx, lam)


def _gdn_prep_kernel(qkv_ref, z_ref, ab_ref, prev_ref, cw_ref, alog_ref, dtb_ref,
                     q_ref, k_ref, v_ref, zo_ref, g_ref, beta_ref, cout_ref, xp_s, st_s,
                     *, nb, rows, batch_major):
    i = pl.program_id(0)
    hist = (CONV_WIDTH - 1) * nb

    @pl.when(i == 0)
    def _():
        xp_s[0:hist, :] = prev_ref[...]

    xp_s[hist:hist + rows, :] = qkv_ref[...]
    for part in range(3):
        for h in range(GDN_HEADS):
            lo = part * GDN_KEY_DIM + h * GDN_DK
            sl = slice(lo, lo + GDN_DK)
            acc = xp_s[0:rows, sl] * cw_ref[0:1, sl]
            for k in range(1, CONV_WIDTH):
                acc = acc + xp_s[k * nb:k * nb + rows, sl] * cw_ref[k:k + 1, sl]
            s = acc * jax.nn.sigmoid(acc)
            if part < 2:
                s = s * lax.rsqrt(jnp.sum(s * s, axis=-1, keepdims=True) + L2_EPS)
            if part == 0:
                s = s * (GDN_DK ** -0.5)
            if batch_major:
                st_s[part * GDN_HEADS + h] = s
            else:
                (q_ref, k_ref, v_ref)[part][:, h * GDN_DK:(h + 1) * GDN_DK] = s
    ab = ab_ref[...]
    g = -jnp.exp(alog_ref[...]) * _softplus(ab + dtb_ref[...])
    beta = jax.nn.sigmoid(ab)
    if batch_major:
        for h in range(GDN_HEADS):
            st_s[3 * GDN_HEADS + h] = z_ref[:, h * GDN_DK:(h + 1) * GDN_DK]
        st_s[4 * GDN_HEADS] = g
        st_s[4 * GDN_HEADS + 1] = beta
        steps = rows // nb
        for b in range(nb):
            pick = pl.ds(b, steps, stride=nb)
            for part, out in enumerate((q_ref, k_ref, v_ref, zo_ref)):
                for h in range(GDN_HEADS):
                    out[b, :, h * GDN_DK:(h + 1) * GDN_DK] = st_s[part * GDN_HEADS + h, pick, :]
            g_ref[b] = st_s[4 * GDN_HEADS, pick, :]
            beta_ref[b] = st_s[4 * GDN_HEADS + 1, pick, :]
    else:
        zo_ref[...] = z_ref[...]
        g_ref[...] = g
        beta_ref[...] = beta
    tail = xp_s[rows:rows + hist, :]
    xp_s[0:hist, :] = tail
    cout_ref[...] = tail


def _gdn_prep(proj, prev, cw, alog, dtb, nb, rows, batch_major):
    total = proj.shape[0]
    hist = (CONV_WIDTH - 1) * nb
    full = lambda shape: pl.BlockSpec(shape, lambda i: (0,) * len(shape))
    tile = lambda n: pl.BlockSpec((rows, n), lambda i: (i, 0))
    if batch_major:
        out_tile = lambda n: pl.BlockSpec((nb, rows // nb, n), lambda i: (0, i, 0))
        out_sds = lambda n: jax.ShapeDtypeStruct((nb, total // nb, n), F32)
    else:
        out_tile = tile
        out_sds = lambda n: jax.ShapeDtypeStruct((total, n), F32)
    widths = (GDN_KEY_DIM,) * 4 + (128, 128)
    return pl.pallas_call(
        functools.partial(_gdn_prep_kernel, nb=nb, rows=rows, batch_major=batch_major),
        grid=(total // rows,),
        in_specs=[tile(GDN_CONV_DIM),
                  pl.BlockSpec((rows, GDN_KEY_DIM), lambda i: (i, GDN_CONV_DIM // GDN_KEY_DIM)),
                  pl.BlockSpec((rows, 128), lambda i: (i, (GDN_CONV_DIM + GDN_KEY_DIM) // 128)),
                  full((hist, GDN_CONV_DIM)), full((CONV_WIDTH, GDN_CONV_DIM)),
                  full((1, 128)), full((1, 128))],
        out_specs=[out_tile(n) for n in widths] + [full((hist, GDN_CONV_DIM))],
        out_shape=[out_sds(n) for n in widths] + [jax.ShapeDtypeStruct((hist, GDN_CONV_DIM), F32)],
        scratch_shapes=[pltpu.VMEM((hist + rows, GDN_CONV_DIM), F32),
                        pltpu.VMEM((4 * GDN_HEADS + 2, rows if batch_major else 8, 128), F32)],
        compiler_params=_cparams(("arbitrary",)),
        name="gdn_prep",
    )(proj, proj, proj, prev, cw, alog, dtb)


def _unit_lower_inverses(ms, ri, ci, chunk):
    eye = (ri == ci).astype(F32)
    blk = (ri >> 3) == (ci >> 3)
    n1 = [jnp.where(blk, -m, 0.0) for m in ms]
    n2 = [_dot(a, a) for a in n1]
    n4 = [_dot(a, a) for a in n2]
    ts = [_dot(eye + a, eye + b) for a, b in zip(n1, n2)]
    ts = [_dot(t, eye + a) for t, a in zip(ts, n4)]
    shift = 3
    while (1 << shift) < chunk:
        pair = ((ri >> (shift + 1)) == (ci >> (shift + 1))) & ((ri >> shift) != (ci >> shift))
        left = [_dot(t, jnp.where(pair, m, 0.0)) for t, m in zip(ts, ms)]
        ts = [t - _dot(a, t) for t, a in zip(ts, left)]
        shift += 1
    return ts


def _gdn_core_kernel(q_ref, k_ref, v_ref, z_ref, g_ref, beta_ref, s0_ref, nw_ref, o_ref, s_ref, *, chunk, bb):
    c = pl.program_id(1)

    @pl.when(c == 0)
    def _():
        s_ref[...] = s0_ref[...]

    ri = lax.broadcasted_iota(jnp.int32, (chunk, chunk), 0)
    ci = lax.broadcasted_iota(jnp.int32, (chunk, chunk), 1)
    causal = ri >= ci
    strict = ri > ci
    tril = causal.astype(BF16)
    e_r = lax.broadcasted_iota(jnp.int32, (128, 128), 0)
    e_c = lax.broadcasted_iota(jnp.int32, (128, 128), 1)
    eye128 = (e_r == e_c).astype(BF16)
    dotf = functools.partial(jnp.dot, preferred_element_type=F32)
    nt = lambda a, b: lax.dot_general(a, b, (((1,), (1,)), ((), ())), preferred_element_type=F32)
    nw = nw_ref[...]

    cums, cum_ts, ecums, e_lasts, e_rests, betas = [], [], [], [], [], []
    for bi in range(bb):
        g3 = _split3(g_ref[bi])
        cum = dotf(tril, g3[0]) + dotf(tril, g3[1]) + dotf(tril, g3[2])
        c3 = _split3(cum)
        cums.append(cum)
        cum_ts.append(nt(eye128, c3[0]) + nt(eye128, c3[1]) + nt(eye128, c3[2]))
        ecums.append(jnp.exp(cum))
        g_last = cum[chunk - 1:chunk, :]
        e_lasts.append(jnp.exp(g_last))
        e_rests.append(jnp.exp(g_last - cum))
        betas.append(beta_ref[bi])

    units = [(bi, h) for bi in range(bb) for h in range(GDN_HEADS)]
    col = lambda a, h: a[:, h:h + 1]
    sl = lambda h: slice(h * GDN_DK, (h + 1) * GDN_DK)
    q = [q_ref[bi, :, sl(h)] for bi, h in units]
    k = [k_ref[bi, :, sl(h)] for bi, h in units]
    decay = [jnp.exp(jnp.where(causal, col(cums[bi], h) - cum_ts[bi][h:h + 1, :], -jnp.inf)) for bi, h in units]
    k_beta = [kk * col(betas[bi], GDN_HEADS + h) for kk, (bi, h) in zip(k, units)]
    ak = [_dot_nt(jnp.concatenate([kb, qq], axis=0), kk) for kb, qq, kk in zip(k_beta, q, k)]
    ms = [jnp.where(strict, a[:chunk] * d, 0.0) for a, d in zip(ak, decay)]
    ts = _unit_lower_inverses(ms, ri, ci, chunk)
    rhs = [jnp.concatenate([v_ref[bi, :, sl(h)] * col(betas[bi], GDN_HEADS + h), kb * col(ecums[bi], h)], axis=1)
           for kb, (bi, h) in zip(k_beta, units)]
    sol = [_dot(t, r) for t, r in zip(ts, rhs)]
    s_old = [s_ref[bi, h] for bi, h in units]
    ws = [_dot(jnp.concatenate([so[:, GDN_DV:], qq * col(ecums[bi], h)], axis=0), s)
          for so, qq, s, (bi, h) in zip(sol, q, s_old, units)]
    v_new = [so[:, :GDN_DV] - w[:chunk] for so, w in zip(sol, ws)]
    o = [w[chunk:] + _dot(a[chunk:] * d, vn) for w, a, d, vn in zip(ws, ak, decay, v_new)]
    k_dec_t = [nt(eye128, (kk * col(e_rests[bi], h)).astype(BF16)) for kk, (bi, h) in zip(k, units)]
    for (bi, h), s, kt, vn, oo in zip(units, s_old, k_dec_t, v_new, o):
        s_ref[bi, h] = s * col(e_lasts[bi], h) + _dot(kt, vn)
        on = oo * lax.rsqrt(jnp.mean(oo * oo, axis=-1, keepdims=True) + RMS_EPS) * nw
        zh = z_ref[bi, :, sl(h)]
        o_ref[bi, :, sl(h)] = on * (zh * jax.nn.sigmoid(zh))


def _gdn_core(q, k, v, z, g, beta, s0, nw, chunk, bb):
    nb, lp = q.shape[0], q.shape[1]
    seq = lambda n: pl.BlockSpec((bb, chunk, n), lambda b, c: (b, c, 0))
    st = pl.BlockSpec((bb, GDN_HEADS, GDN_DK, GDN_DV), lambda b, c: (b, 0, 0, 0))
    return pl.pallas_call(
        functools.partial(_gdn_core_kernel, chunk=chunk, bb=bb),
        grid=(nb // bb, lp // chunk),
        in_specs=[seq(GDN_KEY_DIM), seq(GDN_KEY_DIM), seq(GDN_KEY_DIM), seq(GDN_KEY_DIM),
                  seq(128), seq(128), st, pl.BlockSpec((1, GDN_DV), lambda b, c: (0, 0))],
        out_specs=[seq(GDN_KEY_DIM), st],
        out_shape=[jax.ShapeDtypeStruct((nb, lp, GDN_KEY_DIM), F32),
                   jax.ShapeDtypeStruct(s0.shape, F32)],
        compiler_params=_cparams(("parallel", "arbitrary")),
        name="gdn_core",
    )(q, k, v, z, g, beta, s0, nw)


def _ffn_kernel(x_ref, g_ref, wa_ref, wb_ref, cwa_ref, cwb_ref, cba_ref, cbb_ref, pa_ref, pb_ref, wd_ref,
                gf_ref, o_ref, ca_out, cb_out, xn_s, acc_s, hpa_s, hpb_s, cara_s, carb_s,
                *, nb, tm, final_norm):
    i = pl.program_id(0)
    j = pl.program_id(1)
    hist = (FFN_CONV_WIDTH - 1) * nb

    @pl.when(j == 0)
    def _():
        xn_s[...] = _rms(x_ref[...], g_ref[...]).astype(BF16)
        acc_s[...] = jnp.zeros_like(acc_s)

    @pl.when(i == 0)
    def _():
        hpa_s[0:hist, :] = pa_ref[...]
        hpb_s[0:hist, :] = pb_ref[...]

    @pl.when(i > 0)
    def _():
        hpa_s[0:hist, :] = cara_s[j]
        hpb_s[0:hist, :] = carb_s[j]

    rs = min(SUB_ROWS, tm)
    nsub = tm // rs

    def up(r):
        xr = xn_s[r * rs:(r + 1) * rs, :]
        hpa_s[hist + r * rs:hist + (r + 1) * rs, :] = jnp.dot(xr, wa_ref[...], preferred_element_type=F32)
        hpb_s[hist + r * rs:hist + (r + 1) * rs, :] = jnp.dot(xr, wb_ref[...], preferred_element_type=F32)

    def conv(hp_s, cw_ref, cb_ref, r):
        y = hp_s[r * rs:(r + 1) * rs, :] * cw_ref[0:1, :]
        for k in range(1, FFN_CONV_WIDTH):
            y = y + hp_s[k * nb + r * rs:k * nb + (r + 1) * rs, :] * cw_ref[k:k + 1, :]
        return y + cb_ref[...]

    def down(r):
        act = (jax.nn.gelu(conv(hpa_s, cwa_ref, cba_ref, r)) * conv(hpb_s, cwb_ref, cbb_ref, r)).astype(BF16)
        acc_s[r * rs:(r + 1) * rs, :] += jnp.dot(act, wd_ref[...], preferred_element_type=F32)

    up(0)
    for r in range(nsub):
        if r + 1 < nsub:
            up(r + 1)
        down(r)

    tail_a = hpa_s[tm:tm + hist, :]
    tail_b = hpb_s[tm:tm + hist, :]
    cara_s[j] = tail_a
    carb_s[j] = tail_b
    ca_out[...] = tail_a
    cb_out[...] = tail_b

    @pl.when(j == pl.num_programs(1) - 1)
    def _():
        y = x_ref[...] + acc_s[...]
        if final_norm:
            y = _rms(y, gf_ref[...])
        o_ref[...] = y


def _ffn(x, g, w_up, cw, cb, prev, w_down, g_final, nb, tm, final_norm):
    rows = x.shape[0]
    tn = FFN_TN
    nj = FFN_HIDDEN // tn
    hist = (FFN_CONV_WIDTH - 1) * nb
    col_a = lambda r: pl.BlockSpec((r, tn), lambda i, j: (0, j))
    col_b = lambda r: pl.BlockSpec((r, tn), lambda i, j: (0, nj + j))
    vec = pl.BlockSpec((1, D_MODEL), lambda i, j: (0, 0))
    return pl.pallas_call(
        functools.partial(_ffn_kernel, nb=nb, tm=tm, final_norm=final_norm),
        grid=(rows // tm, nj),
        in_specs=[pl.BlockSpec((tm, D_MODEL), lambda i, j: (i, 0)), vec,
                  col_a(D_MODEL), col_b(D_MODEL),
                  col_a(FFN_CONV_WIDTH), col_b(FFN_CONV_WIDTH), col_a(1), col_b(1),
                  col_a(hist), col_b(hist),
                  pl.BlockSpec((tn, D_MODEL), lambda i, j: (j, 0)), vec],
        out_specs=[pl.BlockSpec((tm, D_MODEL), lambda i, j: (i, 0)),
                   pl.BlockSpec((hist, tn), lambda i, j: (i, j)),
                   pl.BlockSpec((hist, tn), lambda i, j: (i, j))],
        out_shape=[jax.ShapeDtypeStruct((rows, D_MODEL), F32),
                   jax.ShapeDtypeStruct((rows // tm * hist, FFN_HIDDEN), F32),
                   jax.ShapeDtypeStruct((rows // tm * hist, FFN_HIDDEN), F32)],
        scratch_shapes=[pltpu.VMEM((tm, D_MODEL), BF16), pltpu.VMEM((tm, D_MODEL), F32),
                        pltpu.VMEM((hist + tm, tn), F32), pltpu.VMEM((hist + tm, tn), F32),
                        pltpu.VMEM((nj, hist, tn), F32), pltpu.VMEM((nj, hist, tn), F32)],
        compiler_params=_cparams(("arbitrary", "arbitrary")),
        name="conv_ffn",
    )(x, g, w_up, w_up, cw, cw, cb, cb, prev, prev, w_down, g_final)


def _s5_disc_kernel(are_ref, aim_ref, ldt_ref, bre_ref, bim_ref, abr_ref, abi_ref, bbr_ref, bbi_ref):
    a_re, a_im = are_ref[...], aim_ref[...]
    dt = jnp.exp(ldt_ref[...])
    mag = jnp.exp(a_re * dt)
    ar = mag * jnp.cos(a_im * dt)
    ai = mag * jnp.sin(a_im * dt)
    den = a_re * a_re + a_im * a_im
    nr = ar - 1.0
    cr = (nr * a_re + ai * a_im) / den
    ci = (ai * a_re - nr * a_im) / den
    b_re, b_im = bre_ref[...], bim_ref[...]
    abr_ref[...] = ar
    abi_ref[...] = ai
    bbr_ref[...] = cr * b_re - ci * b_im
    bbi_ref[...] = cr * b_im + ci * b_re


def _s5_params(a_re, a_im, log_dt, b_re, b_im, c_re, c_im):
    rep = lambda a: jnp.repeat(a.astype(F32), S5_GROUP_CH, axis=0)
    rows_gc = lambda b: b.astype(F32).transpose(0, 2, 1).reshape(D_MODEL, S5_STATE)
    ldt = jnp.broadcast_to(log_dt.astype(F32)[:, None], (S5_GROUPS, S5_STATE))
    sds = jax.ShapeDtypeStruct((D_MODEL, S5_STATE), F32)
    abr, abi, bbr, bbi = pl.pallas_call(_s5_disc_kernel, out_shape=[sds] * 4, name="s5_discretize")(
        rep(a_re), rep(a_im), rep(ldt), rows_gc(b_re), rows_gc(b_im))
    eye = jnp.eye(S5_GROUPS // S5_KB, dtype=F32)

    def b_blocks(b):
        b = b.reshape(S5_KB, S5_GROUPS // S5_KB, S5_GROUP_CH, S5_STATE)
        return jnp.einsum('kgcp,gh->kgchp', b, eye).reshape(S5_KB, D_MODEL // S5_KB, S5_COLS // S5_KB).astype(BF16)

    def c_blocks(c):
        c = c.astype(F32).reshape(S5_KB, S5_GROUPS // S5_KB, S5_GROUP_CH, S5_STATE)
        return jnp.einsum('kgcp,gh->kgphc', c, eye).reshape(S5_KB, S5_COLS // S5_KB, D_MODEL // S5_KB).astype(BF16)

    return (b_blocks(bbr), b_blocks(bbi), c_blocks(c_re), c_blocks(c_im),
            abr[::S5_GROUP_CH].reshape(1, S5_COLS), abi[::S5_GROUP_CH].reshape(1, S5_COLS))


def _to_time_major(a):
    return a.transpose(1, 0, 2).reshape(a.shape[0] * a.shape[1], a.shape[2])


def _from_time_major(a, nb):
    return a.reshape(a.shape[0] // nb, nb, a.shape[1]).transpose(1, 0, 2)


def _trunk(x, nb, seq, s5_re, s5_im, lru_h, lru_conv, gdn_s, gdn_conv, ffn_conv, p):
    total = seq * nb
    tm = min(total, 1024)
    rows = 512
    o_s5_re, o_s5_im, o_lru, o_lru_conv, o_gdn, o_gdn_conv, o_ffn_conv = [], [], [], [], [], [], []
    depth = p['norm_mix'].shape[0]
    for i in range(depth):
        kind, j = i % 3, i // 3
        g_mix = p['norm_mix'][i].reshape(1, D_MODEL)
        if kind == 0:
            bre, bim, cre, cim, are, aim = p['s5_disc'][j]
            u = _norm_mm(x, g_mix, p['s5_w_in'][j].astype(BF16), rows)
            y, hre, him = _s5_core(u, s5_re[j].reshape(nb, S5_COLS), s5_im[j].reshape(nb, S5_COLS),
                                   bre, bim, cre, cim, are, aim, p['s5_d'][j].reshape(1, D_MODEL), nb, rows)
            x = _mm_glu_res(y, p['s5_w_glu'][j].astype(BF16), x, tm, 512)
            o_s5_re.append(hre.reshape(nb, S5_GROUPS, S5_STATE))
            o_s5_im.append(him.reshape(nb, S5_GROUPS, S5_STATE))
        elif kind == 1:
            proj = _norm_mm(x, g_mix, p['lru_w_in'][j].astype(BF16), rows)
            y, h_new, conv_new = _lru_core(
                proj, _to_time_major(lru_conv[j]), lru_h[j],
                p['lru_conv_w'][j], p['lru_conv_b'][j].reshape(1, LRU_WIDTH),
                p['lru_w_gate_a'][j].astype(BF16), p['lru_b_gate_a'][j].reshape(1, LRU_WIDTH),
                p['lru_w_gate_x'][j].astype(BF16), p['lru_b_gate_x'][j].reshape(1, LRU_WIDTH),
                p['lru_lambda'][j].reshape(1, LRU_WIDTH), nb, rows)
            x = _mm_res(y, p['lru_w_out'][j].astype(BF16), x, tm, 512)
            o_lru.append(h_new)
            o_lru_conv.append(_from_time_major(conv_new, nb))
        else:
            w_in = p['gdn_w_in'][j]
            w_pad = jnp.pad(w_in, ((0, 0), (0, GDN_PROJ_PAD - w_in.shape[1]))).astype(BF16)
            proj = _norm_mm(x, g_mix, w_pad, rows)
            pad8 = lambda a: jnp.pad(a.reshape(1, GDN_HEADS), ((0, 0), (0, 128 - GDN_HEADS)))
            chunk = GDN_CHUNK if seq >= GDN_CHUNK else 8
            batch_major = nb == 8 and seq % chunk == 0
            *qkvzgb, conv_new = _gdn_prep(proj, _to_time_major(gdn_conv[j]), p['gdn_conv_w'][j],
                                          pad8(p['gdn_a_log'][j]), pad8(p['gdn_dt_bias'][j]), nb, rows,
                                          batch_major)
            nw = p['gdn_norm'][j].reshape(1, GDN_DV)
            w_out = p['gdn_w_out'][j].astype(BF16)
            if batch_major:
                o, s_new = _gdn_core(*qkvzgb, gdn_s[j], nw, chunk, 4)
                x = _mm_res_seq_major(o, w_out, x, nb, rows)
            else:
                lp = -(-seq // chunk) * chunk

                def bm(a):
                    a = a.reshape(seq, nb, a.shape[1]).transpose(1, 0, 2)
                    return jnp.pad(a, ((0, 0), (0, lp - seq), (0, 0)))

                o, s_new = _gdn_core(*[bm(a) for a in qkvzgb], gdn_s[j], nw, chunk, 4)
                o = o[:, :seq].transpose(1, 0, 2).reshape(total, GDN_KEY_DIM).astype(BF16)
                x = _mm_res(o, w_out, x, tm, 512)
            o_gdn.append(s_new)
            o_gdn_conv.append(_from_time_major(conv_new, nb))
        x, ca, cb = _ffn(x, p['norm_ffn'][i].reshape(1, D_MODEL), p['ffn_w_up'][i].astype(BF16),
                         p['ffn_conv_w'][i], p['ffn_conv_b'][i].reshape(1, 2 * FFN_HIDDEN),
                         _to_time_major(ffn_conv[i]), p['ffn_w_down'][i].astype(BF16),
                         p['norm_final'].reshape(1, D_MODEL), nb, tm, i == depth - 1)
        hist = (FFN_CONV_WIDTH - 1) * nb
        o_ffn_conv.append(_from_time_major(jnp.concatenate([ca[-hist:], cb[-hist:]], axis=1), nb))
    return (x, jnp.stack(o_s5_re), jnp.stack(o_s5_im), jnp.stack(o_lru), jnp.stack(o_lru_conv),
            jnp.stack(o_gdn), jnp.stack(o_gdn_conv), jnp.stack(o_ffn_conv))


def kernel(x_prompt, x_sample, state_s5_re, state_s5_im, state_lru, state_lru_conv, state_gdn, state_gdn_conv, state_ffn_conv, norm_mix, norm_ffn, norm_final, s5_w_in, s5_a_re, s5_a_im, s5_log_dt, s5_b_re, s5_b_im, s5_c_re, s5_c_im, s5_d, s5_w_glu, lru_w_in, lru_conv_w, lru_conv_b, lru_w_gate_a, lru_b_gate_a, lru_w_gate_x, lru_b_gate_x, lru_lambda, lru_w_out, gdn_w_in, gdn_conv_w, gdn_a_log, gdn_dt_bias, gdn_norm, gdn_w_out, ffn_w_up, ffn_conv_w, ffn_conv_b, ffn_w_down):
    p = dict(norm_mix=norm_mix, norm_ffn=norm_ffn, norm_final=norm_final, s5_w_in=s5_w_in, s5_a_re=s5_a_re,
             s5_a_im=s5_a_im, s5_log_dt=s5_log_dt, s5_b_re=s5_b_re, s5_b_im=s5_b_im, s5_c_re=s5_c_re,
             s5_c_im=s5_c_im, s5_d=s5_d, s5_w_glu=s5_w_glu, lru_w_in=lru_w_in, lru_conv_w=lru_conv_w,
             lru_conv_b=lru_conv_b, lru_w_gate_a=lru_w_gate_a, lru_b_gate_a=lru_b_gate_a,
             lru_w_gate_x=lru_w_gate_x, lru_b_gate_x=lru_b_gate_x, lru_lambda=lru_lambda, lru_w_out=lru_w_out,
             gdn_w_in=gdn_w_in, gdn_conv_w=gdn_conv_w, gdn_a_log=gdn_a_log, gdn_dt_bias=gdn_dt_bias,
             gdn_norm=gdn_norm, gdn_w_out=gdn_w_out, ffn_w_up=ffn_w_up, ffn_conv_w=ffn_conv_w,
             ffn_conv_b=ffn_conv_b, ffn_w_down=ffn_w_down)
    p['s5_disc'] = [_s5_params(s5_a_re[j], s5_a_im[j], s5_log_dt[j], s5_b_re[j], s5_b_im[j], s5_c_re[j],
                               s5_c_im[j]) for j in range(s5_a_re.shape[0])]
    outs = []
    for x, states in (
            (x_prompt, None),
            (x_sample, (state_s5_re, state_s5_im, state_lru, state_lru_conv, state_gdn, state_gdn_conv,
                        state_ffn_conv))):
        nb, seq, _ = x.shape
        if states is None:
            states = tuple(jnp.zeros((s.shape[0], nb) + s.shape[2:], F32) for s in (
                state_s5_re, state_s5_im, state_lru, state_lru_conv, state_gdn, state_gdn_conv, state_ffn_conv))
        res = _trunk(_to_time_major(x), nb, seq, *states, p)
        outs.append((_from_time_major(res[0], nb),) + tuple(res[1:]))
    (y_p, *st_p), (y_s, *st_s) = outs
    return (y_p, y_s, *st_p, *st_s)
```

```python
import functools
import math

import jax
import jax.numpy as jnp
from jax import lax
from jax.experimental import pallas as pl
from jax.experimental.pallas import tpu as pltpu

F32 = jnp.float32
BF16 = jnp.bfloat16

D_MODEL = 1024
RMS_EPS = 1e-6
L2_EPS = 1e-6
S5_GROUPS = 64
S5_STATE = 64
S5_GROUP_CH = 16
S5_COLS = S5_GROUPS * S5_STATE
S5_KB = 8
S5_SCAN_LANES = 1024
LRU_WIDTH = 1280
LRU_BLOCK = 128
LRU_BLOCKS = LRU_WIDTH // LRU_BLOCK
LRU_C = 8.0
CONV_WIDTH = 4
GDN_HEADS = 8
GDN_DK = 128
GDN_DV = 128
GDN_KEY_DIM = GDN_HEADS * GDN_DK
GDN_CONV_DIM = 3 * GDN_KEY_DIM
GDN_CHUNK = 64
GDN_PROJ_PAD = 4224
FFN_HIDDEN = 2816
FFN_CONV_WIDTH = 3
FFN_TN = 256
SUB_ROWS = 256
VMEM_LIMIT_BYTES = 56 * 1024 * 1024


def _cparams(sem):
    return pltpu.CompilerParams(dimension_semantics=sem, vmem_limit_bytes=VMEM_LIMIT_BYTES)


def _rms(x, g):
    ms = jnp.mean(x * x, axis=-1, keepdims=True)
    return x * lax.rsqrt(ms + RMS_EPS) * g


def _softplus(x):
    return jnp.maximum(x, 0.0) + jnp.log1p(jnp.exp(-jnp.abs(x)))


def _expm1(x):
    u = jnp.exp(x)
    small = jnp.abs(x) < 0.5
    usable = small & (u != 1.0)
    ratio = (u - 1.0) * x / jnp.log(jnp.where(usable, u, 2.0))
    return jnp.where(small, jnp.where(usable, ratio, x), u - 1.0)


def _dot(a, b):
    return jnp.dot(a.astype(BF16), b.astype(BF16), preferred_element_type=F32)


def _dot_nt(a, b):
    return lax.dot_general(a.astype(BF16), b.astype(BF16), (((1,), (1,)), ((), ())),
                           preferred_element_type=F32)


def _split2(a):
    hi = a.astype(BF16)
    lo = (a - hi.astype(F32)).astype(BF16)
    return hi, lo


def _split3(a):
    hi = a.astype(BF16)
    r = a - hi.astype(F32)
    mid = r.astype(BF16)
    lo = (r - mid.astype(F32)).astype(BF16)
    return hi, mid, lo


def _dot3(a, b):
    ah, al = _split2(a)
    bh, bl = _split2(b)
    d = functools.partial(jnp.dot, preferred_element_type=F32)
    return d(ah, bh) + d(al, bh) + d(ah, bl)


def _norm_mm_kernel(x_ref, g_ref, w_ref, o_ref, xn_ref, *, tm):
    sub = lambda r: slice(r * SUB_ROWS, (r + 1) * SUB_ROWS)

    def norm(r):
        xn_ref[sub(r), :] = _rms(x_ref[sub(r), :], g_ref[...]).astype(BF16)

    nsub = tm // SUB_ROWS
    norm(0)
    for r in range(nsub):
        if r + 1 < nsub:
            norm(r + 1)
        o_ref[sub(r), :] = jnp.dot(xn_ref[sub(r), :], w_ref[...], preferred_element_type=F32)


def _norm_mm(x, g, w, tm):
    rows, n = x.shape[0], w.shape[1]
    return pl.pallas_call(
        functools.partial(_norm_mm_kernel, tm=tm),
        grid=(rows // tm,),
        in_specs=[pl.BlockSpec((tm, D_MODEL), lambda i: (i, 0)),
                  pl.BlockSpec((1, D_MODEL), lambda i: (0, 0)),
                  pl.BlockSpec((D_MODEL, n), lambda i: (0, 0))],
        out_specs=pl.BlockSpec((tm, n), lambda i: (i, 0)),
        out_shape=jax.ShapeDtypeStruct((rows, n), F32),
        scratch_shapes=[pltpu.VMEM((tm, D_MODEL), BF16)],
        compiler_params=_cparams(("parallel",)),
        name="norm_mm",
    )(x, g, w)


def _s5_core_kernel(u_ref, h0re_ref, h0im_ref, bre_ref, bim_ref, cre_ref, cim_ref, are_ref, aim_ref,
                    d_ref, y_ref, hre_out, him_out, hre_s, him_s, *, nb, rows):
    i = pl.program_id(0)

    @pl.when(i == 0)
    def _():
        hre_s[0:nb, :] = h0re_ref[...]
        him_s[0:nb, :] = h0im_ref[...]

    kw = S5_COLS // S5_KB
    uw = D_MODEL // S5_KB
    per_grp = S5_SCAN_LANES // kw
    n_grp = S5_COLS // S5_SCAN_LANES

    def project_in(grp):
        for kb in range(grp * per_grp, (grp + 1) * per_grp):
            ukb = u_ref[:, kb * uw:(kb + 1) * uw].astype(BF16)
            hre_s[nb:nb + rows, kb * kw:(kb + 1) * kw] = jnp.dot(ukb, bre_ref[kb], preferred_element_type=F32)
            him_s[nb:nb + rows, kb * kw:(kb + 1) * kw] = jnp.dot(ukb, bim_ref[kb], preferred_element_type=F32)

    def scan(grp):
        cols = slice(grp * S5_SCAN_LANES, (grp + 1) * S5_SCAN_LANES)
        are = jnp.broadcast_to(are_ref[:, cols], (nb, S5_SCAN_LANES))
        aim = jnp.broadcast_to(aim_ref[:, cols], (nb, S5_SCAN_LANES))
        hr, hi = hre_s[0:nb, cols], him_s[0:nb, cols]
        for t in range(rows // nb):
            r = slice(nb + t * nb, 2 * nb + t * nb)
            hr, hi = (are * hr - aim * hi + hre_s[r, cols],
                      are * hi + aim * hr + him_s[r, cols])
            hre_s[r, cols] = hr
            him_s[r, cols] = hi

    def project_out(grp):
        for kb in range(grp * per_grp, (grp + 1) * per_grp):
            hr = hre_s[nb:nb + rows, kb * kw:(kb + 1) * kw].astype(BF16)
            hi = him_s[nb:nb + rows, kb * kw:(kb + 1) * kw].astype(BF16)
            yk = (jnp.dot(hr, cre_ref[kb], preferred_element_type=F32)
                  - jnp.dot(hi, cim_ref[kb], preferred_element_type=F32))
            yk = yk + d_ref[:, kb * uw:(kb + 1) * uw] * u_ref[:, kb * uw:(kb + 1) * uw]
            y_ref[:, kb * uw:(kb + 1) * uw] = jax.nn.gelu(yk).astype(BF16)

    project_in(0)
    for grp in range(n_grp):
        if grp + 1 < n_grp:
            project_in(grp + 1)
        scan(grp)
        project_out(grp)

    last_re = hre_s[rows:rows + nb, :]
    last_im = him_s[rows:rows + nb, :]
    hre_s[0:nb, :] = last_re
    him_s[0:nb, :] = last_im
    hre_out[...] = last_re
    him_out[...] = last_im


def _s5_core(u, h0re, h0im, bre, bim, cre, cim, are, aim, d, nb, rows):
    total = u.shape[0]
    full = lambda shape: pl.BlockSpec(shape, lambda i: (0,) * len(shape))
    return pl.pallas_call(
        functools.partial(_s5_core_kernel, nb=nb, rows=rows),
        grid=(total // rows,),
        in_specs=[pl.BlockSpec((rows, D_MODEL), lambda i: (i, 0)),
                  full((nb, S5_COLS)), full((nb, S5_COLS)),
                  full(bre.shape), full(bim.shape), full(cre.shape), full(cim.shape),
                  full((1, S5_COLS)), full((1, S5_COLS)), full((1, D_MODEL))],
        out_specs=[pl.BlockSpec((rows, D_MODEL), lambda i: (i, 0)),
                   full((nb, S5_COLS)), full((nb, S5_COLS))],
        out_shape=[jax.ShapeDtypeStruct((total, D_MODEL), BF16),
                   jax.ShapeDtypeStruct((nb, S5_COLS), F32),
                   jax.ShapeDtypeStruct((nb, S5_COLS), F32)],
        scratch_shapes=[pltpu.VMEM((nb + rows, S5_COLS), F32),
                        pltpu.VMEM((nb + rows, S5_COLS), F32)],
        compiler_params=_cparams(("arbitrary",)),
        name="s5_core",
    )(u, h0re, h0im, bre, bim, cre, cim, are, aim, d)


def _lru_core_kernel(gate_ref, xbr_ref, prev_ref, h0_ref, cw_ref, cb_ref, wga_ref, bga_ref, wgx_ref,
                     bgx_ref, lam_ref, y_ref, hout_ref, cout_ref, xp_s, h_s, a_s, *, nb, rows):
    i = pl.program_id(0)
    hist = (CONV_WIDTH - 1) * nb

    @pl.when(i == 0)
    def _():
        xp_s[0:hist, :] = prev_ref[...]
        h_s[0:nb, :] = h0_ref[...]

    xp_s[hist:hist + rows, :] = xbr_ref[...]
    xc = xp_s[0:rows, :] * cw_ref[0:1, :]
    for k in range(1, CONV_WIDTH):
        xc = xc + xp_s[k * nb:k * nb + rows, :] * cw_ref[k:k + 1, :]
    xc = xc + cb_ref[...]
    c8 = -LRU_C * _softplus(-lam_ref[...])
    for n in range(LRU_BLOCKS):
        sl = slice(n * LRU_BLOCK, (n + 1) * LRU_BLOCK)
        xcn = xc[:, sl]
        xcb = xcn.astype(BF16)
        r = jax.nn.sigmoid(jnp.dot(xcb, wga_ref[n], preferred_element_type=F32) + bga_ref[:, sl])
        ig = jax.nn.sigmoid(jnp.dot(xcb, wgx_ref[n], preferred_element_type=F32) + bgx_ref[:, sl])
        log_a = c8[:, sl] * r
        a_s[:, sl] = jnp.exp(log_a)
        h_s[nb:nb + rows, sl] = jnp.sqrt(-_expm1(2.0 * log_a)) * ig * xcn

    def step(t, carry):
        r0 = pl.multiple_of(t * nb, nb)
        r1 = pl.multiple_of(t * nb + nb, nb)
        h_s[pl.ds(r1, nb), :] = a_s[pl.ds(r0, nb), :] * h_s[pl.ds(r0, nb), :] + h_s[pl.ds(r1, nb), :]
        return carry

    lax.fori_loop(0, rows // nb, step, 0)

    y_ref[...] = (jax.nn.gelu(gate_ref[...]) * h_s[nb:nb + rows, :]).astype(BF16)
    tail = xp_s[rows:rows + hist, :]
    last = h_s[rows:rows + nb, :]
    xp_s[0:hist, :] = tail
    h_s[0:nb, :] = last
    cout_ref[...] = tail
    hout_ref[...] = last


def _lru_core(proj, prev, h0, cw, cb, wga, bga, wgx, bgx, lam, nb, rows):
    total = proj.shape[0]
    hist = (CONV_WIDTH - 1) * nb
    full = lambda shape: pl.BlockSpec(shape, lambda i: (0,) * len(shape))
    return pl.pallas_call(
        functools.partial(_lru_core_kernel, nb=nb, rows=rows),
        grid=(total // rows,),
        in_specs=[pl.BlockSpec((rows, LRU_WIDTH), lambda i: (i, 0)),
                  pl.BlockSpec((rows, LRU_WIDTH), lambda i: (i, 1)),
                  full((hist, LRU_WIDTH)), full((nb, LRU_WIDTH)),
                  full((CONV_WIDTH, LRU_WIDTH)), full((1, LRU_WIDTH)),
                  full(wga.shape), full((1, LRU_WIDTH)), full(wgx.shape), full((1, LRU_WIDTH)),
                  full((1, LRU_WIDTH))],
        out_specs=[pl.BlockSpec((rows, LRU_WIDTH), lambda i: (i, 0)),
                   full((nb, LRU_WIDTH)), full((hist, LRU_WIDTH))],
        out_shape=[jax.ShapeDtypeStruct((total, LRU_WIDTH), BF16),
                   jax.ShapeDtypeStruct((nb, LRU_WIDTH), F32),
                   jax.ShapeDtypeStruct((hist, LRU_WIDTH), F32)],
        scratch_shapes=[pltpu.VMEM((hist + rows, LRU_WIDTH), F32),
                        pltpu.VMEM((nb + rows, LRU_WIDTH), F32),
                        pltpu.VMEM((rows, LRU_WIDTH), F32)],
        compiler_params=_cparams(("arbitrary",)),
        name="lru_core",
    )(proj, proj, prev, h0, cw, cb, wga, bga, wgx, bgx, lam)


def _gdn_prep_kernel(qkv_ref, z_ref, ab_ref, prev_ref, cw_ref, alog_ref, dtb_ref,
                     q_ref, k_ref, v_ref, zo_ref, g_ref, beta_ref, cout_ref, xp_s, st_s,
                     *, nb, rows, batch_major):
    i = pl.program_id(0)
    hist = (CONV_WIDTH - 1) * nb

    @pl.when(i == 0)
    def _():
        xp_s[0:hist, :] = prev_ref[...]

    xp_s[hist:hist + rows, :] = qkv_ref[...]
    for part in range(3):
        for h in range(GDN_HEADS):
            lo = part * GDN_KEY_DIM + h * GDN_DK
            sl = slice(lo, lo + GDN_DK)
            acc = xp_s[0:rows, sl] * cw_ref[0:1, sl]
            for k in range(1, CONV_WIDTH):
                acc = acc + xp_s[k * nb:k * nb + rows, sl] * cw_ref[k:k + 1, sl]
            s = acc * jax.nn.sigmoid(acc)
            if part < 2:
                s = s * lax.rsqrt(jnp.sum(s * s, axis=-1, keepdims=True) + L2_EPS)
            if part == 0:
                s = s * (GDN_DK ** -0.5)
            if batch_major:
                st_s[part * GDN_HEADS + h] = s
            else:
                (q_ref, k_ref, v_ref)[part][:, h * GDN_DK:(h + 1) * GDN_DK] = s
    ab = ab_ref[...]
    g = -jnp.exp(alog_ref[...]) * _softplus(ab + dtb_ref[...])
    beta = jax.nn.sigmoid(ab)
    if batch_major:
        for h in range(GDN_HEADS):
            st_s[3 * GDN_HEADS + h] = z_ref[:, h * GDN_DK:(h + 1) * GDN_DK]
        st_s[4 * GDN_HEADS] = g
        st_s[4 * GDN_HEADS + 1] = beta
        steps = rows // nb
        for b in range(nb):
            pick = pl.ds(b, steps, stride=nb)
            for part, out in enumerate((q_ref, k_ref, v_ref, zo_ref)):
                for h in range(GDN_HEADS):
                    out[b, :, h * GDN_DK:(h + 1) * GDN_DK] = st_s[part * GDN_HEADS + h, pick, :]
            g_ref[b] = st_s[4 * GDN_HEADS, pick, :]
            beta_ref[b] = st_s[4 * GDN_HEADS + 1, pick, :]
    else:
        zo_ref[...] = z_ref[...]
        g_ref[...] = g
        beta_ref[...] = beta
    tail = xp_s[rows:rows + hist, :]
    xp_s[0:hist, :] = tail
    cout_ref[...] = tail


def _gdn_prep(proj, prev, cw, alog, dtb, nb, rows, batch_major):
    total = proj.shape[0]
    hist = (CONV_WIDTH - 1) * nb
    full = lambda shape: pl.BlockSpec(shape, lambda i: (0,) * len(shape))
    tile = lambda n: pl.BlockSpec((rows, n), lambda i: (i, 0))
    if batch_major:
        out_tile = lambda n: pl.BlockSpec((nb, rows // nb, n), lambda i: (0, i, 0))
        out_sds = lambda n: jax.ShapeDtypeStruct((nb, total // nb, n), F32)
    else:
        out_tile = tile
        out_sds = lambda n: jax.ShapeDtypeStruct((total, n), F32)
    widths = (GDN_KEY_DIM,) * 4 + (128, 128)
    return pl.pallas_call(
        functools.partial(_gdn_prep_kernel, nb=nb, rows=rows, batch_major=batch_major),
        grid=(total // rows,),
        in_specs=[tile(GDN_CONV_DIM),
                  pl.BlockSpec((rows, GDN_KEY_DIM), lambda i: (i, GDN_CONV_DIM // GDN_KEY_DIM)),
                  pl.BlockSpec((rows, 128), lambda i: (i, (GDN_CONV_DIM + GDN_KEY_DIM) // 128)),
                  full((hist, GDN_CONV_DIM)), full((CONV_WIDTH, GDN_CONV_DIM)),
                  full((1, 128)), full((1, 128))],
        out_specs=[out_tile(n) for n in widths] + [full((hist, GDN_CONV_DIM))],
        out_shape=[out_sds(n) for n in widths] + [jax.ShapeDtypeStruct((hist, GDN_CONV_DIM), F32)],
        scratch_shapes=[pltpu.VMEM((hist + rows, GDN_CONV_DIM), F32),
                        pltpu.VMEM((4 * GDN_HEADS + 2, rows if batch_major else 8, 128), F32)],
        compiler_params=_cparams(("arbitrary",)),
        name="gdn_prep",
    )(proj, proj, proj, prev, cw, alog, dtb)


def _unit_lower_inverses(ms, ri, ci, chunk):
    eye = (ri == ci).astype(F32)
    blk = (ri >> 3) == (ci >> 3)
    n1 = [jnp.where(blk, -m, 0.0) for m in ms]
    n2 = [_dot(a, a) for a in n1]
    n4 = [_dot(a, a) for a in n2]
    ts = [_dot(eye + a, eye + b) for a, b in zip(n1, n2)]
    ts = [_dot(t, eye + a) for t, a in zip(ts, n4)]
    shift = 3
    while (1 << shift) < chunk:
        pair = ((ri >> (shift + 1)) == (ci >> (shift + 1))) & ((ri >> shift) != (ci >> shift))
        left = [_dot(t, jnp.where(pair, m, 0.0)) for t, m in zip(ts, ms)]
        ts = [t - _dot(a, t) for t, a in zip(ts, left)]
        shift += 1
    return ts


def _gdn_core_kernel(q_ref, k_ref, v_ref, z_ref, g_ref, beta_ref, s0_ref, nw_ref, o_ref, s_ref, *, chunk, bb):
    c = pl.program_id(1)

    @pl.when(c == 0)
    def _():
        s_ref[...] = s0_ref[...]

    ri = lax.broadcasted_iota(jnp.int32, (chunk, chunk), 0)
    ci = lax.broadcasted_iota(jnp.int32, (chunk, chunk), 1)
    causal = ri >= ci
    strict = ri > ci
    tril = causal.astype(BF16)
    e_r = lax.broadcasted_iota(jnp.int32, (128, 128), 0)
    e_c = lax.broadcasted_iota(jnp.int32, (128, 128), 1)
    eye128 = (e_r == e_c).astype(BF16)
    dotf = functools.partial(jnp.dot, preferred_element_type=F32)
    nt = lambda a, b: lax.dot_general(a, b, (((1,), (1,)), ((), ())), preferred_element_type=F32)
    nw = nw_ref[...]

    cums, cum_ts, ecums, e_lasts, e_rests, betas = [], [], [], [], [], []
    for bi in range(bb):
        g3 = _split3(g_ref[bi])
        cum = dotf(tril, g3[0]) + dotf(tril, g3[1]) + dotf(tril, g3[2])
        c3 = _split3(cum)
        cums.append(cum)
        cum_ts.append(nt(eye128, c3[0]) + nt(eye128, c3[1]) + nt(eye128, c3[2]))
        ecums.append(jnp.exp(cum))
        g_last = cum[chunk - 1:chunk, :]
        e_lasts.append(jnp.exp(g_last))
        e_rests.append(jnp.exp(g_last - cum))
        betas.append(beta_ref[bi])

    units = [(bi, h) for bi in range(bb) for h in range(GDN_HEADS)]
    col = lambda a, h: a[:, h:h + 1]
    sl = lambda h: slice(h * GDN_DK, (h + 1) * GDN_DK)
    q = [q_ref[bi, :, sl(h)] for bi, h in units]
    k = [k_ref[bi, :, sl(h)] for bi, h in units]
    decay = [jnp.exp(jnp.where(causal, col(cums[bi], h) - cum_ts[bi][h:h + 1, :], -jnp.inf)) for bi, h in units]
    k_beta = [kk * col(betas[bi], GDN_HEADS + h) for kk, (bi, h) in zip(k, units)]
    ak = [_dot_nt(jnp.concatenate([kb, qq], axis=0), kk) for kb, qq, kk in zip(k_beta, q, k)]
    ms = [jnp.where(strict, a[:chunk] * d, 0.0) for a, d in zip(ak, decay)]
    ts = _unit_lower_inverses(ms, ri, ci, chunk)
    rhs = [jnp.concatenate([v_ref[bi, :, sl(h)] * col(betas[bi], GDN_HEADS + h), kb * col(ecums[bi], h)], axis=1)
           for kb, (bi, h) in zip(k_beta, units)]
    sol = [_dot(t, r) for t, r in zip(ts, rhs)]
    s_old = [s_ref[bi, h] for bi, h in units]
    ws = [_dot(jnp.concatenate([so[:, GDN_DV:], qq * col(ecums[bi], h)], axis=0), s)
          for so, qq, s, (bi, h) in zip(sol, q, s_old, units)]
    v_new = [so[:, :GDN_DV] - w[:chunk] for so, w in zip(sol, ws)]
    o = [w[chunk:] + _dot(a[chunk:] * d, vn) for w, a, d, vn in zip(ws, ak, decay, v_new)]
    k_dec_t = [nt(eye128, (kk * col(e_rests[bi], h)).astype(BF16)) for kk, (bi, h) in zip(k, units)]
    for (bi, h), s, kt, vn, oo in zip(units, s_old, k_dec_t, v_new, o):
        s_ref[bi, h] = s * col(e_lasts[bi], h) + _dot(kt, vn)
        on = oo * lax.rsqrt(jnp.mean(oo * oo, axis=-1, keepdims=True) + RMS_EPS) * nw
        zh = z_ref[bi, :, sl(h)]
        o_ref[bi, :, sl(h)] = on * (zh * jax.nn.sigmoid(zh))


def _gdn_core(q, k, v, z, g, beta, s0, nw, chunk, bb):
    nb, lp = q.shape[0], q.shape[1]
    seq = lambda n: pl.BlockSpec((bb, chunk, n), lambda b, c: (b, c, 0))
    st = pl.BlockSpec((bb, GDN_HEADS, GDN_DK, GDN_DV), lambda b, c: (b, 0, 0, 0))
    return pl.pallas_call(
        functools.partial(_gdn_core_kernel, chunk=chunk, bb=bb),
        grid=(nb // bb, lp // chunk),
        in_specs=[seq(GDN_KEY_DIM), seq(GDN_KEY_DIM), seq(GDN_KEY_DIM), seq(GDN_KEY_DIM),
                  seq(128), seq(128), st, pl.BlockSpec((1, GDN_DV), lambda b, c: (0, 0))],
        out_specs=[seq(GDN_KEY_DIM), st],
        out_shape=[jax.ShapeDtypeStruct((nb, lp, GDN_KEY_DIM), F32),
                   jax.ShapeDtypeStruct(s0.shape, F32)],
        compiler_params=_cparams(("parallel", "arbitrary")),
        name="gdn_core",
    )(q, k, v, z, g, beta, s0, nw)


def _ffn_kernel(r_ref, a_ref, wo_ref, g_ref, wa_ref, wb_ref, cwa_ref, cwb_ref, cba_ref, cbb_ref, pa_ref, pb_ref,
                wd_ref, gf_ref, o_ref, ca_out, cb_out, x_s, xn_s, acc_s, hpa_s, hpb_s, cara_s, carb_s, *stage,
                nb, tm, final_norm, mixer_out):
    i = pl.program_id(0)
    j = pl.program_id(1)
    hist = (FFN_CONV_WIDTH - 1) * nb
    rs = min(SUB_ROWS, tm)
    nsub = tm // rs
    sub = lambda r: slice(r * rs, (r + 1) * rs)

    @pl.when(j == 0)
    def _():
        if mixer_out == "seq_major":
            a_s, = stage
            nk = a_ref.shape[2] // 128
            for b in range(nb):
                for kt in range(nk):
                    a_s[kt, pl.ds(b, tm // nb, stride=nb), :] = a_ref[b, :, kt * 128:(kt + 1) * 128]

        def project(r):
            if mixer_out == "seq_major":
                a = jnp.concatenate([a_s[kt, sub(r), :].astype(BF16) for kt in range(nk)], axis=1)
            else:
                a = a_ref[sub(r), :]
            if mixer_out == "glu":
                half = wo_ref.shape[1] // 2
                val = jnp.dot(a, wo_ref[:, :half], preferred_element_type=F32)
                gate = jnp.dot(a, wo_ref[:, half:], preferred_element_type=F32)
                y = val * jax.nn.sigmoid(gate)
            else:
                y = jnp.dot(a, wo_ref[...], preferred_element_type=F32)
            x_s[sub(r), :] = r_ref[sub(r), :] + y

        project(0)
        for r in range(nsub):
            if r + 1 < nsub:
                project(r + 1)
            xn_s[sub(r), :] = _rms(x_s[sub(r), :], g_ref[...]).astype(BF16)
        acc_s[...] = jnp.zeros_like(acc_s)

    @pl.when(i == 0)
    def _():
        hpa_s[0:hist, :] = pa_ref[...]
        hpb_s[0:hist, :] = pb_ref[...]

    @pl.when(i > 0)
    def _():
        hpa_s[0:hist, :] = cara_s[j]
        hpb_s[0:hist, :] = carb_s[j]

    def up(r):
        xr = xn_s[r * rs:(r + 1) * rs, :]
        hpa_s[hist + r * rs:hist + (r + 1) * rs, :] = jnp.dot(xr, wa_ref[...], preferred_element_type=F32)
        hpb_s[hist + r * rs:hist + (r + 1) * rs, :] = jnp.dot(xr, wb_ref[...], preferred_element_type=F32)

    def conv(hp_s, cw_ref, cb_ref, r):
        y = hp_s[r * rs:(r + 1) * rs, :] * cw_ref[0:1, :]
        for k in range(1, FFN_CONV_WIDTH):
            y = y + hp_s[k * nb + r * rs:k * nb + (r + 1) * rs, :] * cw_ref[k:k + 1, :]
        return y + cb_ref[...]

    def down(r):
        act = (jax.nn.gelu(conv(hpa_s, cwa_ref, cba_ref, r)) * conv(hpb_s, cwb_ref, cbb_ref, r)).astype(BF16)
        acc_s[r * rs:(r + 1) * rs, :] += jnp.dot(act, wd_ref[...], preferred_element_type=F32)

    up(0)
    for r in range(nsub):
        if r + 1 < nsub:
            up(r + 1)
        down(r)

    tail_a = hpa_s[tm:tm + hist, :]
    tail_b = hpb_s[tm:tm + hist, :]
    cara_s[j] = tail_a
    carb_s[j] = tail_b
    ca_out[...] = tail_a
    cb_out[...] = tail_b

    @pl.when(j == pl.num_programs(1) - 1)
    def _():
        y = x_s[...] + acc_s[...]
        if final_norm:
            y = _rms(y, gf_ref[...])
        o_ref[...] = y


def _ffn(res, a, w_out, mixer_out, g, w_up, cw, cb, prev, w_down, g_final, nb, tm, final_norm):
    rows = res.shape[0]
    tn = FFN_TN
    nj = FFN_HIDDEN // tn
    hist = (FFN_CONV_WIDTH - 1) * nb
    col_a = lambda r: pl.BlockSpec((r, tn), lambda i, j: (0, j))
    col_b = lambda r: pl.BlockSpec((r, tn), lambda i, j: (0, nj + j))
    vec = pl.BlockSpec((1, D_MODEL), lambda i, j: (0, 0))
    k = w_out.shape[0]
    if mixer_out == "seq_major":
        a_spec = pl.BlockSpec((nb, tm // nb, k), lambda i, j: (0, i, 0))
        stage = [pltpu.VMEM((k // 128, tm, 128), F32)]
    else:
        a_spec = pl.BlockSpec((tm, k), lambda i, j: (i, 0))
        stage = []
    return pl.pallas_call(
        functools.partial(_ffn_kernel, nb=nb, tm=tm, final_norm=final_norm, mixer_out=mixer_out),
        grid=(rows // tm, nj),
        in_specs=[pl.BlockSpec((tm, D_MODEL), lambda i, j: (i, 0)), a_spec,
                  pl.BlockSpec(w_out.shape, lambda i, j: (0, 0)), vec,
                  col_a(D_MODEL), col_b(D_MODEL),
                  col_a(FFN_CONV_WIDTH), col_b(FFN_CONV_WIDTH), col_a(1), col_b(1),
                  col_a(hist), col_b(hist),
                  pl.BlockSpec((tn, D_MODEL), lambda i, j: (j, 0)), vec],
        out_specs=[pl.BlockSpec((tm, D_MODEL), lambda i, j: (i, 0)),
                   pl.BlockSpec((hist, tn), lambda i, j: (i, j)),
                   pl.BlockSpec((hist, tn), lambda i, j: (i, j))],
        out_shape=[jax.ShapeDtypeStruct((rows, D_MODEL), F32),
                   jax.ShapeDtypeStruct((rows // tm * hist, FFN_HIDDEN), F32),
                   jax.ShapeDtypeStruct((rows // tm * hist, FFN_HIDDEN), F32)],
        scratch_shapes=[pltpu.VMEM((tm, D_MODEL), F32), pltpu.VMEM((tm, D_MODEL), BF16),
                        pltpu.VMEM((tm, D_MODEL), F32),
                        pltpu.VMEM((hist + tm, tn), F32), pltpu.VMEM((hist + tm, tn), F32),
                        pltpu.VMEM((nj, hist, tn), F32), pltpu.VMEM((nj, hist, tn), F32)] + stage,
        compiler_params=_cparams(("arbitrary", "arbitrary")),
        name="conv_ffn",
    )(res, a, w_out, g, w_up, w_up, cw, cw, cb, cb, prev, prev, w_down, g_final)


def _s5_disc_kernel(are_ref, aim_ref, ldt_ref, bre_ref, bim_ref, abr_ref, abi_ref, bbr_ref, bbi_ref):
    a_re, a_im = are_ref[...], aim_ref[...]
    dt = jnp.exp(ldt_ref[...])
    mag = jnp.exp(a_re * dt)
    ar = mag * jnp.cos(a_im * dt)
    ai = mag * jnp.sin(a_im * dt)
    den = a_re * a_re + a_im * a_im
    nr = ar - 1.0
    cr = (nr * a_re + ai * a_im) / den
    ci = (ai * a_re - nr * a_im) / den
    b_re, b_im = bre_ref[...], bim_ref[...]
    abr_ref[...] = ar
    abi_ref[...] = ai
    bbr_ref[...] = cr * b_re - ci * b_im
    bbi_ref[...] = cr * b_im + ci * b_re


def _s5_params(a_re, a_im, log_dt, b_re, b_im, c_re, c_im):
    rep = lambda a: jnp.repeat(a.astype(F32), S5_GROUP_CH, axis=0)
    rows_gc = lambda b: b.astype(F32).transpose(0, 2, 1).reshape(D_MODEL, S5_STATE)
    ldt = jnp.broadcast_to(log_dt.astype(F32)[:, None], (S5_GROUPS, S5_STATE))
    sds = jax.ShapeDtypeStruct((D_MODEL, S5_STATE), F32)
    abr, abi, bbr, bbi = pl.pallas_call(_s5_disc_kernel, out_shape=[sds] * 4, name="s5_discretize")(
        rep(a_re), rep(a_im), rep(ldt), rows_gc(b_re), rows_gc(b_im))
    eye = jnp.eye(S5_GROUPS // S5_KB, dtype=F32)

    def b_blocks(b):
        b = b.reshape(S5_KB, S5_GROUPS // S5_KB, S5_GROUP_CH, S5_STATE)
        return jnp.einsum('kgcp,gh->kgchp', b, eye).reshape(S5_KB, D_MODEL // S5_KB, S5_COLS // S5_KB).astype(BF16)

    def c_blocks(c):
        c = c.astype(F32).reshape(S5_KB, S5_GROUPS // S5_KB, S5_GROUP_CH, S5_STATE)
        return jnp.einsum('kgcp,gh->kgphc', c, eye).reshape(S5_KB, S5_COLS // S5_KB, D_MODEL // S5_KB).astype(BF16)

    return (b_blocks(bbr), b_blocks(bbi), c_blocks(c_re), c_blocks(c_im),
            abr[::S5_GROUP_CH].reshape(1, S5_COLS), abi[::S5_GROUP_CH].reshape(1, S5_COLS))


def _to_time_major(a):
    return a.transpose(1, 0, 2).reshape(a.shape[0] * a.shape[1], a.shape[2])


def _from_time_major(a, nb):
    return a.reshape(a.shape[0] // nb, nb, a.shape[1]).transpose(1, 0, 2)


def _trunk(x, nb, seq, s5_re, s5_im, lru_h, lru_conv, gdn_s, gdn_conv, ffn_conv, p):
    total = seq * nb
    tm = min(total, 1024)
    rows = 512
    o_s5_re, o_s5_im, o_lru, o_lru_conv, o_gdn, o_gdn_conv, o_ffn_conv = [], [], [], [], [], [], []
    depth = p['norm_mix'].shape[0]
    for i in range(depth):
        kind, j = i % 3, i // 3
        g_mix = p['norm_mix'][i].reshape(1, D_MODEL)
        if kind == 0:
            bre, bim, cre, cim, are, aim = p['s5_disc'][j]
            u = _norm_mm(x, g_mix, p['s5_w_in'][j].astype(BF16), rows)
            y, hre, him = _s5_core(u, s5_re[j].reshape(nb, S5_COLS), s5_im[j].reshape(nb, S5_COLS),
                                   bre, bim, cre, cim, are, aim, p['s5_d'][j].reshape(1, D_MODEL), nb, rows)
            mix = (y, p['s5_w_glu'][j].astype(BF16), "glu")
            o_s5_re.append(hre.reshape(nb, S5_GROUPS, S5_STATE))
            o_s5_im.append(him.reshape(nb, S5_GROUPS, S5_STATE))
        elif kind == 1:
            proj = _norm_mm(x, g_mix, p['lru_w_in'][j].astype(BF16), rows)
            y, h_new, conv_new = _lru_core(
                proj, _to_time_major(lru_conv[j]), lru_h[j],
                p['lru_conv_w'][j], p['lru_conv_b'][j].reshape(1, LRU_WIDTH),
                p['lru_w_gate_a'][j].astype(BF16), p['lru_b_gate_a'][j].reshape(1, LRU_WIDTH),
                p['lru_w_gate_x'][j].astype(BF16), p['lru_b_gate_x'][j].reshape(1, LRU_WIDTH),
                p['lru_lambda'][j].reshape(1, LRU_WIDTH), nb, rows)
            mix = (y, p['lru_w_out'][j].astype(BF16), "plain")
            o_lru.append(h_new)
            o_lru_conv.append(_from_time_major(conv_new, nb))
        else:
            w_in = p['gdn_w_in'][j]
            w_pad = jnp.pad(w_in, ((0, 0), (0, GDN_PROJ_PAD - w_in.shape[1]))).astype(BF16)
            proj = _norm_mm(x, g_mix, w_pad, rows)
            pad8 = lambda a: jnp.pad(a.reshape(1, GDN_HEADS), ((0, 0), (0, 128 - GDN_HEADS)))
            chunk = GDN_CHUNK if seq >= GDN_CHUNK else 8
            batch_major = nb == 8 and seq % chunk == 0
            *qkvzgb, conv_new = _gdn_prep(proj, _to_time_major(gdn_conv[j]), p['gdn_conv_w'][j],
                                          pad8(p['gdn_a_log'][j]), pad8(p['gdn_dt_bias'][j]), nb, rows,
                                          batch_major)
            nw = p['gdn_norm'][j].reshape(1, GDN_DV)
            w_out = p['gdn_w_out'][j].astype(BF16)
            if batch_major:
                o, s_new = _gdn_core(*qkvzgb, gdn_s[j], nw, chunk, 4)
                mix = (o, w_out, "seq_major")
            else:
                lp = -(-seq // chunk) * chunk

                def bm(a):
                    a = a.reshape(seq, nb, a.shape[1]).transpose(1, 0, 2)
                    return jnp.pad(a, ((0, 0), (0, lp - seq), (0, 0)))

                o, s_new = _gdn_core(*[bm(a) for a in qkvzgb], gdn_s[j], nw, chunk, 4)
                o = o[:, :seq].transpose(1, 0, 2).reshape(total, GDN_KEY_DIM).astype(BF16)
                mix = (o, w_out, "plain")
            o_gdn.append(s_new)
            o_gdn_conv.append(_from_time_major(conv_new, nb))
        x, ca, cb = _ffn(x, *mix, p['norm_ffn'][i].reshape(1, D_MODEL), p['ffn_w_up'][i].astype(BF16),
                         p['ffn_conv_w'][i], p['ffn_conv_b'][i].reshape(1, 2 * FFN_HIDDEN),
                         _to_time_major(ffn_conv[i]), p['ffn_w_down'][i].astype(BF16),
                         p['norm_final'].reshape(1, D_MODEL), nb, tm, i == depth - 1)
        hist = (FFN_CONV_WIDTH - 1) * nb
        o_ffn_conv.append(_from_time_major(jnp.concatenate([ca[-hist:], cb[-hist:]], axis=1), nb))
    return (x, jnp.stack(o_s5_re), jnp.stack(o_s5_im), jnp.stack(o_lru), jnp.stack(o_lru_conv),
            jnp.stack(o_gdn), jnp.stack(o_gdn_conv), jnp.stack(o_ffn_conv))


def kernel(x_prompt, x_sample, state_s5_re, state_s5_im, state_lru, state_lru_conv, state_gdn, state_gdn_conv, state_ffn_conv, norm_mix, norm_ffn, norm_final, s5_w_in, s5_a_re, s5_a_im, s5_log_dt, s5_b_re, s5_b_im, s5_c_re, s5_c_im, s5_d, s5_w_glu, lru_w_in, lru_conv_w, lru_conv_b, lru_w_gate_a, lru_b_gate_a, lru_w_gate_x, lru_b_gate_x, lru_lambda, lru_w_out, gdn_w_in, gdn_conv_w, gdn_a_log, gdn_dt_bias, gdn_norm, gdn_w_out, ffn_w_up, ffn_conv_w, ffn_conv_b, ffn_w_down):
    p = dict(norm_mix=norm_mix, norm_ffn=norm_ffn, norm_final=norm_final, s5_w_in=s5_w_in, s5_a_re=s5_a_re,
             s5_a_im=s5_a_im, s5_log_dt=s5_log_dt, s5_b_re=s5_b_re, s5_b_im=s5_b_im, s5_c_re=s5_c_re,
             s5_c_im=s5_c_im, s5_d=s5_d, s5_w_glu=s5_w_glu, lru_w_in=lru_w_in, lru_conv_w=lru_conv_w,
             lru_conv_b=lru_conv_b, lru_w_gate_a=lru_w_gate_a, lru_b_gate_a=lru_b_gate_a,
             lru_w_gate_x=lru_w_gate_x, lru_b_gate_x=lru_b_gate_x, lru_lambda=lru_lambda, lru_w_out=lru_w_out,
             gdn_w_in=gdn_w_in, gdn_conv_w=gdn_conv_w, gdn_a_log=gdn_a_log, gdn_dt_bias=gdn_dt_bias,
             gdn_norm=gdn_norm, gdn_w_out=gdn_w_out, ffn_w_up=ffn_w_up, ffn_conv_w=ffn_conv_w,
             ffn_conv_b=ffn_conv_b, ffn_w_down=ffn_w_down)
    p['s5_disc'] = [_s5_params(s5_a_re[j], s5_a_im[j], s5_log_dt[j], s5_b_re[j], s5_b_im[j], s5_c_re[j],
                               s5_c_im[j]) for j in range(s5_a_re.shape[0])]
    outs = []
    for x, states in (
            (x_prompt, None),
            (x_sample, (state_s5_re, state_s5_im, state_lru, state_lru_conv, state_gdn, state_gdn_conv,
                        state_ffn_conv))):
        nb, seq, _ = x.shape
        if states is None:
            states = tuple(jnp.zeros((s.shape[0], nb) + s.shape[2:], F32) for s in (
                state_s5_re, state_s5_im, state_lru, state_lru_conv, state_gdn, state_gdn_conv, state_ffn_conv))
        res = _trunk(_to_time_major(x), nb, seq, *states, p)
        outs.append((_from_time_major(res[0], nb),) + tuple(res[1:]))
    (y_p, *st_p), (y_s, *st_s) = outs
    return (y_p, y_s, *st_p, *st_s)
```

```python
import functools
import math

import jax
import jax.numpy as jnp
from jax import lax
from jax.experimental import pallas as pl
from jax.experimental.pallas import tpu as pltpu

F32 = jnp.float32
BF16 = jnp.bfloat16

D_MODEL = 1024
RMS_EPS = 1e-6
L2_EPS = 1e-6
S5_GROUPS = 64
S5_STATE = 64
S5_GROUP_CH = 16
S5_COLS = S5_GROUPS * S5_STATE
S5_KB = 8
S5_SCAN_LANES = 1024
LRU_WIDTH = 1280
LRU_BLOCK = 128
LRU_BLOCKS = LRU_WIDTH // LRU_BLOCK
LRU_C = 8.0
CONV_WIDTH = 4
GDN_HEADS = 8
GDN_DK = 128
GDN_DV = 128
GDN_KEY_DIM = GDN_HEADS * GDN_DK
GDN_CONV_DIM = 3 * GDN_KEY_DIM
GDN_CHUNK = 64
GDN_PROJ_PAD = 4224
FFN_HIDDEN = 2816
FFN_CONV_WIDTH = 3
FFN_TN = 256
FFN_SUBS = 2
SUB_ROWS = 256
VMEM_LIMIT_BYTES = 56 * 1024 * 1024


def _cparams(sem):
    return pltpu.CompilerParams(dimension_semantics=sem, vmem_limit_bytes=VMEM_LIMIT_BYTES)


def _rms(x, g):
    ms = jnp.mean(x * x, axis=-1, keepdims=True)
    return x * lax.rsqrt(ms + RMS_EPS) * g


def _softplus(x):
    return jnp.maximum(x, 0.0) + jnp.log1p(jnp.exp(-jnp.abs(x)))


def _expm1(x):
    u = jnp.exp(x)
    small = jnp.abs(x) < 0.5
    usable = small & (u != 1.0)
    ratio = (u - 1.0) * x / jnp.log(jnp.where(usable, u, 2.0))
    return jnp.where(small, jnp.where(usable, ratio, x), u - 1.0)


def _dot(a, b):
    return jnp.dot(a.astype(BF16), b.astype(BF16), preferred_element_type=F32)


def _dot_nt(a, b):
    return lax.dot_general(a.astype(BF16), b.astype(BF16), (((1,), (1,)), ((), ())),
                           preferred_element_type=F32)


def _split2(a):
    hi = a.astype(BF16)
    lo = (a - hi.astype(F32)).astype(BF16)
    return hi, lo


def _split3(a):
    hi = a.astype(BF16)
    r = a - hi.astype(F32)
    mid = r.astype(BF16)
    lo = (r - mid.astype(F32)).astype(BF16)
    return hi, mid, lo


def _dot3(a, b):
    ah, al = _split2(a)
    bh, bl = _split2(b)
    d = functools.partial(jnp.dot, preferred_element_type=F32)
    return d(ah, bh) + d(al, bh) + d(ah, bl)


def _norm_mm_kernel(x_ref, g_ref, w_ref, o_ref, xn_ref, *, tm):
    sub = lambda r: slice(r * SUB_ROWS, (r + 1) * SUB_ROWS)

    def norm(r):
        xn_ref[sub(r), :] = _rms(x_ref[sub(r), :], g_ref[...]).astype(BF16)

    nsub = tm // SUB_ROWS
    norm(0)
    for r in range(nsub):
        if r + 1 < nsub:
            norm(r + 1)
        o_ref[sub(r), :] = jnp.dot(xn_ref[sub(r), :], w_ref[...], preferred_element_type=F32)


def _norm_mm(x, g, w, tm):
    rows, n = x.shape[0], w.shape[1]
    return pl.pallas_call(
        functools.partial(_norm_mm_kernel, tm=tm),
        grid=(rows // tm,),
        in_specs=[pl.BlockSpec((tm, D_MODEL), lambda i: (i, 0)),
                  pl.BlockSpec((1, D_MODEL), lambda i: (0, 0)),
                  pl.BlockSpec((D_MODEL, n), lambda i: (0, 0))],
        out_specs=pl.BlockSpec((tm, n), lambda i: (i, 0)),
        out_shape=jax.ShapeDtypeStruct((rows, n), F32),
        scratch_shapes=[pltpu.VMEM((tm, D_MODEL), BF16)],
        compiler_params=_cparams(("parallel",)),
        name="norm_mm",
    )(x, g, w)


def _s5_core_kernel(u_ref, h0re_ref, h0im_ref, bre_ref, bim_ref, cre_ref, cim_ref, are_ref, aim_ref,
                    d_ref, y_ref, hre_out, him_out, hre_s, him_s, *, nb, rows):
    i = pl.program_id(0)

    @pl.when(i == 0)
    def _():
        hre_s[0:nb, :] = h0re_ref[...]
        him_s[0:nb, :] = h0im_ref[...]

    kw = S5_COLS // S5_KB
    uw = D_MODEL // S5_KB
    per_grp = S5_SCAN_LANES // kw
    n_grp = S5_COLS // S5_SCAN_LANES

    def project_in(grp):
        for kb in range(grp * per_grp, (grp + 1) * per_grp):
            ukb = u_ref[:, kb * uw:(kb + 1) * uw].astype(BF16)
            hre_s[nb:nb + rows, kb * kw:(kb + 1) * kw] = jnp.dot(ukb, bre_ref[kb], preferred_element_type=F32)
            him_s[nb:nb + rows, kb * kw:(kb + 1) * kw] = jnp.dot(ukb, bim_ref[kb], preferred_element_type=F32)

    def scan(grp):
        cols = slice(grp * S5_SCAN_LANES, (grp + 1) * S5_SCAN_LANES)
        are = jnp.broadcast_to(are_ref[:, cols], (nb, S5_SCAN_LANES))
        aim = jnp.broadcast_to(aim_ref[:, cols], (nb, S5_SCAN_LANES))
        hr, hi = hre_s[0:nb, cols], him_s[0:nb, cols]
        for t in range(rows // nb):
            r = slice(nb + t * nb, 2 * nb + t * nb)
            hr, hi = (are * hr - aim * hi + hre_s[r, cols],
                      are * hi + aim * hr + him_s[r, cols])
            hre_s[r, cols] = hr
            him_s[r, cols] = hi

    def project_out(grp):
        for kb in range(grp * per_grp, (grp + 1) * per_grp):
            hr = hre_s[nb:nb + rows, kb * kw:(kb + 1) * kw].astype(BF16)
            hi = him_s[nb:nb + rows, kb * kw:(kb + 1) * kw].astype(BF16)
            yk = (jnp.dot(hr, cre_ref[kb], preferred_element_type=F32)
                  - jnp.dot(hi, cim_ref[kb], preferred_element_type=F32))
            yk = yk + d_ref[:, kb * uw:(kb + 1) * uw] * u_ref[:, kb * uw:(kb + 1) * uw]
            y_ref[:, kb * uw:(kb + 1) * uw] = jax.nn.gelu(yk).astype(BF16)

    project_in(0)
    for grp in range(n_grp):
        if grp + 1 < n_grp:
            project_in(grp + 1)
        scan(grp)
        project_out(grp)

    last_re = hre_s[rows:rows + nb, :]
    last_im = him_s[rows:rows + nb, :]
    hre_s[0:nb, :] = last_re
    him_s[0:nb, :] = last_im
    hre_out[...] = last_re
    him_out[...] = last_im


def _s5_core(u, h0re, h0im, bre, bim, cre, cim, are, aim, d, nb, rows):
    total = u.shape[0]
    full = lambda shape: pl.BlockSpec(shape, lambda i: (0,) * len(shape))
    return pl.pallas_call(
        functools.partial(_s5_core_kernel, nb=nb, rows=rows),
        grid=(total // rows,),
        in_specs=[pl.BlockSpec((rows, D_MODEL), lambda i: (i, 0)),
                  full((nb, S5_COLS)), full((nb, S5_COLS)),
                  full(bre.shape), full(bim.shape), full(cre.shape), full(cim.shape),
                  full((1, S5_COLS)), full((1, S5_COLS)), full((1, D_MODEL))],
        out_specs=[pl.BlockSpec((rows, D_MODEL), lambda i: (i, 0)),
                   full((nb, S5_COLS)), full((nb, S5_COLS))],
        out_shape=[jax.ShapeDtypeStruct((total, D_MODEL), BF16),
                   jax.ShapeDtypeStruct((nb, S5_COLS), F32),
                   jax.ShapeDtypeStruct((nb, S5_COLS), F32)],
        scratch_shapes=[pltpu.VMEM((nb + rows, S5_COLS), F32),
                        pltpu.VMEM((nb + rows, S5_COLS), F32)],
        compiler_params=_cparams(("arbitrary",)),
        name="s5_core",
    )(u, h0re, h0im, bre, bim, cre, cim, are, aim, d)


def _lru_core_kernel(gate_ref, xbr_ref, prev_ref, h0_ref, cw_ref, cb_ref, wga_ref, bga_ref, wgx_ref,
                     bgx_ref, lam_ref, y_ref, hout_ref, cout_ref, xp_s, h_s, a_s, *, nb, rows):
    i = pl.program_id(0)
    hist = (CONV_WIDTH - 1) * nb

    @pl.when(i == 0)
    def _():
        xp_s[0:hist, :] = prev_ref[...]
        h_s[0:nb, :] = h0_ref[...]

    xp_s[hist:hist + rows, :] = xbr_ref[...]
    xc = xp_s[0:rows, :] * cw_ref[0:1, :]
    for k in range(1, CONV_WIDTH):
        xc = xc + xp_s[k * nb:k * nb + rows, :] * cw_ref[k:k + 1, :]
    xc = xc + cb_ref[...]
    c8 = -LRU_C * _softplus(-lam_ref[...])
    for n in range(LRU_BLOCKS):
        sl = slice(n * LRU_BLOCK, (n + 1) * LRU_BLOCK)
        xcn = xc[:, sl]
        xcb = xcn.astype(BF16)
        r = jax.nn.sigmoid(jnp.dot(xcb, wga_ref[n], preferred_element_type=F32) + bga_ref[:, sl])
        ig = jax.nn.sigmoid(jnp.dot(xcb, wgx_ref[n], preferred_element_type=F32) + bgx_ref[:, sl])
        log_a = c8[:, sl] * r
        a_s[:, sl] = jnp.exp(log_a)
        h_s[nb:nb + rows, sl] = jnp.sqrt(-_expm1(2.0 * log_a)) * ig * xcn

    def step(t, carry):
        r0 = pl.multiple_of(t * nb, nb)
        r1 = pl.multiple_of(t * nb + nb, nb)
        h_s[pl.ds(r1, nb), :] = a_s[pl.ds(r0, nb), :] * h_s[pl.ds(r0, nb), :] + h_s[pl.ds(r1, nb), :]
        return carry

    lax.fori_loop(0, rows // nb, step, 0)

    y_ref[...] = (jax.nn.gelu(gate_ref[...]) * h_s[nb:nb + rows, :]).astype(BF16)
    tail = xp_s[rows:rows + hist, :]
    last = h_s[rows:rows + nb, :]
    xp_s[0:hist, :] = tail
    h_s[0:nb, :] = last
    cout_ref[...] = tail
    hout_ref[...] = last


def _lru_core(proj, prev, h0, cw, cb, wga, bga, wgx, bgx, lam, nb, rows):
    total = proj.shape[0]
    hist = (CONV_WIDTH - 1) * nb
    full = lambda shape: pl.BlockSpec(shape, lambda i: (0,) * len(shape))
    return pl.pallas_call(
        functools.partial(_lru_core_kernel, nb=nb, rows=rows),
        grid=(total // rows,),
        in_specs=[pl.BlockSpec((rows, LRU_WIDTH), lambda i: (i, 0)),
                  pl.BlockSpec((rows, LRU_WIDTH), lambda i: (i, 1)),
                  full((hist, LRU_WIDTH)), full((nb, LRU_WIDTH)),
                  full((CONV_WIDTH, LRU_WIDTH)), full((1, LRU_WIDTH)),
                  full(wga.shape), full((1, LRU_WIDTH)), full(wgx.shape), full((1, LRU_WIDTH)),
                  full((1, LRU_WIDTH))],
        out_specs=[pl.BlockSpec((rows, LRU_WIDTH), lambda i: (i, 0)),
                   full((nb, LRU_WIDTH)), full((hist, LRU_WIDTH))],
        out_shape=[jax.ShapeDtypeStruct((total, LRU_WIDTH), BF16),
                   jax.ShapeDtypeStruct((nb, LRU_WIDTH), F32),
                   jax.ShapeDtypeStruct((hist, LRU_WIDTH), F32)],
        scratch_shapes=[pltpu.VMEM((hist + rows, LRU_WIDTH), F32),
                        pltpu.VMEM((nb + rows, LRU_WIDTH), F32),
                        pltpu.VMEM((rows, LRU_WIDTH), F32)],
        compiler_params=_cparams(("arbitrary",)),
        name="lru_core",
    )(proj, proj, prev, h0, cw, cb, wga, bga, wgx, bgx, lam)


def _gdn_prep_kernel(qkv_ref, z_ref, ab_ref, prev_ref, cw_ref, alog_ref, dtb_ref,
                     q_ref, k_ref, v_ref, zo_ref, g_ref, beta_ref, cout_ref, xp_s, st_s,
                     *, nb, rows, batch_major):
    i = pl.program_id(0)
    hist = (CONV_WIDTH - 1) * nb

    @pl.when(i == 0)
    def _():
        xp_s[0:hist, :] = prev_ref[...]

    xp_s[hist:hist + rows, :] = qkv_ref[...]
    for part in range(3):
        for h in range(GDN_HEADS):
            lo = part * GDN_KEY_DIM + h * GDN_DK
            sl = slice(lo, lo + GDN_DK)
            acc = xp_s[0:rows, sl] * cw_ref[0:1, sl]
            for k in range(1, CONV_WIDTH):
                acc = acc + xp_s[k * nb:k * nb + rows, sl] * cw_ref[k:k + 1, sl]
            s = acc * jax.nn.sigmoid(acc)
            if part < 2:
                s = s * lax.rsqrt(jnp.sum(s * s, axis=-1, keepdims=True) + L2_EPS)
            if part == 0:
                s = s * (GDN_DK ** -0.5)
            if batch_major:
                st_s[part * GDN_HEADS + h] = s
            else:
                (q_ref, k_ref, v_ref)[part][:, h * GDN_DK:(h + 1) * GDN_DK] = s
    ab = ab_ref[...]
    g = -jnp.exp(alog_ref[...]) * _softplus(ab + dtb_ref[...])
    beta = jax.nn.sigmoid(ab)
    if batch_major:
        for h in range(GDN_HEADS):
            st_s[3 * GDN_HEADS + h] = z_ref[:, h * GDN_DK:(h + 1) * GDN_DK]
        st_s[4 * GDN_HEADS] = g
        st_s[4 * GDN_HEADS + 1] = beta
        steps = rows // nb
        for b in range(nb):
            pick = pl.ds(b, steps, stride=nb)
            for part, out in enumerate((q_ref, k_ref, v_ref, zo_ref)):
                for h in range(GDN_HEADS):
                    out[b, :, h * GDN_DK:(h + 1) * GDN_DK] = st_s[part * GDN_HEADS + h, pick, :]
            g_ref[b] = st_s[4 * GDN_HEADS, pick, :]
            beta_ref[b] = st_s[4 * GDN_HEADS + 1, pick, :]
    else:
        zo_ref[...] = z_ref[...]
        g_ref[...] = g
        beta_ref[...] = beta
    tail = xp_s[rows:rows + hist, :]
    xp_s[0:hist, :] = tail
    cout_ref[...] = tail


def _gdn_prep(proj, prev, cw, alog, dtb, nb, rows, batch_major):
    total = proj.shape[0]
    hist = (CONV_WIDTH - 1) * nb
    full = lambda shape: pl.BlockSpec(shape, lambda i: (0,) * len(shape))
    tile = lambda n: pl.BlockSpec((rows, n), lambda i: (i, 0))
    if batch_major:
        out_tile = lambda n: pl.BlockSpec((nb, rows // nb, n), lambda i: (0, i, 0))
        out_sds = lambda n: jax.ShapeDtypeStruct((nb, total // nb, n), F32)
    else:
        out_tile = tile
        out_sds = lambda n: jax.ShapeDtypeStruct((total, n), F32)
    widths = (GDN_KEY_DIM,) * 4 + (128, 128)
    return pl.pallas_call(
        functools.partial(_gdn_prep_kernel, nb=nb, rows=rows, batch_major=batch_major),
        grid=(total // rows,),
        in_specs=[tile(GDN_CONV_DIM),
                  pl.BlockSpec((rows, GDN_KEY_DIM), lambda i: (i, GDN_CONV_DIM // GDN_KEY_DIM)),
                  pl.BlockSpec((rows, 128), lambda i: (i, (GDN_CONV_DIM + GDN_KEY_DIM) // 128)),
                  full((hist, GDN_CONV_DIM)), full((CONV_WIDTH, GDN_CONV_DIM)),
                  full((1, 128)), full((1, 128))],
        out_specs=[out_tile(n) for n in widths] + [full((hist, GDN_CONV_DIM))],
        out_shape=[out_sds(n) for n in widths] + [jax.ShapeDtypeStruct((hist, GDN_CONV_DIM), F32)],
        scratch_shapes=[pltpu.VMEM((hist + rows, GDN_CONV_DIM), F32),
                        pltpu.VMEM((4 * GDN_HEADS + 2, rows if batch_major else 8, 128), F32)],
        compiler_params=_cparams(("arbitrary",)),
        name="gdn_prep",
    )(proj, proj, proj, prev, cw, alog, dtb)


def _unit_lower_inverses(ms, ri, ci, chunk):
    eye = (ri == ci).astype(F32)
    blk = (ri >> 3) == (ci >> 3)
    n1 = [jnp.where(blk, -m, 0.0) for m in ms]
    n2 = [_dot(a, a) for a in n1]
    n4 = [_dot(a, a) for a in n2]
    ts = [_dot(eye + a, eye + b) for a, b in zip(n1, n2)]
    ts = [_dot(t, eye + a) for t, a in zip(ts, n4)]
    shift = 3
    while (1 << shift) < chunk:
        pair = ((ri >> (shift + 1)) == (ci >> (shift + 1))) & ((ri >> shift) != (ci >> shift))
        left = [_dot(t, jnp.where(pair, m, 0.0)) for t, m in zip(ts, ms)]
        ts = [t - _dot(a, t) for t, a in zip(ts, left)]
        shift += 1
    return ts


def _gdn_core_kernel(q_ref, k_ref, v_ref, z_ref, g_ref, beta_ref, s0_ref, nw_ref, o_ref, s_ref, *, chunk, bb):
    c = pl.program_id(1)

    @pl.when(c == 0)
    def _():
        s_ref[...] = s0_ref[...]

    ri = lax.broadcasted_iota(jnp.int32, (chunk, chunk), 0)
    ci = lax.broadcasted_iota(jnp.int32, (chunk, chunk), 1)
    causal = ri >= ci
    strict = ri > ci
    tril = causal.astype(BF16)
    e_r = lax.broadcasted_iota(jnp.int32, (128, 128), 0)
    e_c = lax.broadcasted_iota(jnp.int32, (128, 128), 1)
    eye128 = (e_r == e_c).astype(BF16)
    dotf = functools.partial(jnp.dot, preferred_element_type=F32)
    nt = lambda a, b: lax.dot_general(a, b, (((1,), (1,)), ((), ())), preferred_element_type=F32)
    nw = nw_ref[...]

    cums, cum_ts, ecums, e_lasts, e_rests, betas = [], [], [], [], [], []
    for bi in range(bb):
        g3 = _split3(g_ref[bi])
        cum = dotf(tril, g3[0]) + dotf(tril, g3[1]) + dotf(tril, g3[2])
        c3 = _split3(cum)
        cums.append(cum)
        cum_ts.append(nt(eye128, c3[0]) + nt(eye128, c3[1]) + nt(eye128, c3[2]))
        ecums.append(jnp.exp(cum))
        g_last = cum[chunk - 1:chunk, :]
        e_lasts.append(jnp.exp(g_last))
        e_rests.append(jnp.exp(g_last - cum))
        betas.append(beta_ref[bi])

    units = [(bi, h) for bi in range(bb) for h in range(GDN_HEADS)]
    col = lambda a, h: a[:, h:h + 1]
    sl = lambda h: slice(h * GDN_DK, (h + 1) * GDN_DK)
    q = [q_ref[bi, :, sl(h)] for bi, h in units]
    k = [k_ref[bi, :, sl(h)] for bi, h in units]
    decay = [jnp.exp(jnp.where(causal, col(cums[bi], h) - cum_ts[bi][h:h + 1, :], -jnp.inf)) for bi, h in units]
    k_beta = [kk * col(betas[bi], GDN_HEADS + h) for kk, (bi, h) in zip(k, units)]
    ak = [_dot_nt(jnp.concatenate([kb, qq], axis=0), kk) for kb, qq, kk in zip(k_beta, q, k)]
    ms = [jnp.where(strict, a[:chunk] * d, 0.0) for a, d in zip(ak, decay)]
    ts = _unit_lower_inverses(ms, ri, ci, chunk)
    rhs = [jnp.concatenate([v_ref[bi, :, sl(h)] * col(betas[bi], GDN_HEADS + h), kb * col(ecums[bi], h)], axis=1)
           for kb, (bi, h) in zip(k_beta, units)]
    sol = [_dot(t, r) for t, r in zip(ts, rhs)]
    s_old = [s_ref[bi, h] for bi, h in units]
    ws = [_dot(jnp.concatenate([so[:, GDN_DV:], qq * col(ecums[bi], h)], axis=0), s)
          for so, qq, s, (bi, h) in zip(sol, q, s_old, units)]
    v_new = [so[:, :GDN_DV] - w[:chunk] for so, w in zip(sol, ws)]
    o = [w[chunk:] + _dot(a[chunk:] * d, vn) for w, a, d, vn in zip(ws, ak, decay, v_new)]
    k_dec_t = [nt(eye128, (kk * col(e_rests[bi], h)).astype(BF16)) for kk, (bi, h) in zip(k, units)]
    for (bi, h), s, kt, vn, oo in zip(units, s_old, k_dec_t, v_new, o):
        s_ref[bi, h] = s * col(e_lasts[bi], h) + _dot(kt, vn)
        on = oo * lax.rsqrt(jnp.mean(oo * oo, axis=-1, keepdims=True) + RMS_EPS) * nw
        zh = z_ref[bi, :, sl(h)]
        o_ref[bi, :, sl(h)] = on * (zh * jax.nn.sigmoid(zh))


def _gdn_core(q, k, v, z, g, beta, s0, nw, chunk, bb):
    nb, lp = q.shape[0], q.shape[1]
    seq = lambda n: pl.BlockSpec((bb, chunk, n), lambda b, c: (b, c, 0))
    st = pl.BlockSpec((bb, GDN_HEADS, GDN_DK, GDN_DV), lambda b, c: (b, 0, 0, 0))
    return pl.pallas_call(
        functools.partial(_gdn_core_kernel, chunk=chunk, bb=bb),
        grid=(nb // bb, lp // chunk),
        in_specs=[seq(GDN_KEY_DIM), seq(GDN_KEY_DIM), seq(GDN_KEY_DIM), seq(GDN_KEY_DIM),
                  seq(128), seq(128), st, pl.BlockSpec((1, GDN_DV), lambda b, c: (0, 0))],
        out_specs=[seq(GDN_KEY_DIM), st],
        out_shape=[jax.ShapeDtypeStruct((nb, lp, GDN_KEY_DIM), F32),
                   jax.ShapeDtypeStruct(s0.shape, F32)],
        compiler_params=_cparams(("parallel", "arbitrary")),
        name="gdn_core",
    )(q, k, v, z, g, beta, s0, nw)


def _ffn_kernel(r_ref, a_ref, wo_ref, g_ref, wa_ref, wb_ref, cwa_ref, cwb_ref, cba_ref, cbb_ref, pa_ref, pb_ref,
                wd_ref, gf_ref, o_ref, ca_out, cb_out, x_s, xn_s, acc_s, hpa_s, hpb_s, cara_s, carb_s, *stage,
                nb, tm, final_norm, mixer_out):
    i = pl.program_id(0)
    j = pl.program_id(1)
    hist = (FFN_CONV_WIDTH - 1) * nb
    rs = min(SUB_ROWS, tm)
    nsub = tm // rs
    sub = lambda r: slice(r * rs, (r + 1) * rs)

    @pl.when(j == 0)
    def _():
        if mixer_out == "seq_major":
            a_s, = stage
            nk = a_ref.shape[2] // 128
            for b in range(nb):
                for kt in range(nk):
                    a_s[kt, pl.ds(b, tm // nb, stride=nb), :] = a_ref[b, :, kt * 128:(kt + 1) * 128]

        def project(r):
            if mixer_out == "seq_major":
                a = jnp.concatenate([a_s[kt, sub(r), :].astype(BF16) for kt in range(nk)], axis=1)
            else:
                a = a_ref[sub(r), :]
            if mixer_out == "glu":
                half = wo_ref.shape[1] // 2
                val = jnp.dot(a, wo_ref[:, :half], preferred_element_type=F32)
                gate = jnp.dot(a, wo_ref[:, half:], preferred_element_type=F32)
                y = val * jax.nn.sigmoid(gate)
            else:
                y = jnp.dot(a, wo_ref[...], preferred_element_type=F32)
            x_s[sub(r), :] = r_ref[sub(r), :] + y

        project(0)
        for r in range(nsub):
            if r + 1 < nsub:
                project(r + 1)
            xn_s[sub(r), :] = _rms(x_s[sub(r), :], g_ref[...]).astype(BF16)
        acc_s[...] = jnp.zeros_like(acc_s)

    @pl.when(i == 0)
    def _():
        hpa_s[0:hist, :] = pa_ref[...]
        hpb_s[0:hist, :] = pb_ref[...]

    @pl.when(i > 0)
    def _():
        hpa_s[0:hist, :] = cara_s[j]
        hpb_s[0:hist, :] = carb_s[j]

    def up(r):
        xr = xn_s[r * rs:(r + 1) * rs, :]
        hpa_s[hist + r * rs:hist + (r + 1) * rs, :] = jnp.dot(xr, wa_ref[...], preferred_element_type=F32)
        hpb_s[hist + r * rs:hist + (r + 1) * rs, :] = jnp.dot(xr, wb_ref[...], preferred_element_type=F32)

    def conv(hp_s, cw_ref, cb_ref, r):
        y = hp_s[r * rs:(r + 1) * rs, :] * cw_ref[0:1, :]
        for k in range(1, FFN_CONV_WIDTH):
            y = y + hp_s[k * nb + r * rs:k * nb + (r + 1) * rs, :] * cw_ref[k:k + 1, :]
        return y + cb_ref[...]

    def down(r):
        act = (jax.nn.gelu(conv(hpa_s, cwa_ref, cba_ref, r)) * conv(hpb_s, cwb_ref, cbb_ref, r)).astype(BF16)
        acc_s[r * rs:(r + 1) * rs, :] += jnp.dot(act, wd_ref[...], preferred_element_type=F32)

    up(0)
    for r in range(nsub):
        if r + 1 < nsub:
            up(r + 1)
        down(r)

    tail_a = hpa_s[tm:tm + hist, :]
    tail_b = hpb_s[tm:tm + hist, :]
    cara_s[j] = tail_a
    carb_s[j] = tail_b
    ca_out[...] = tail_a
    cb_out[...] = tail_b

    @pl.when(j == pl.num_programs(1) - 1)
    def _():
        y = x_s[...] + acc_s[...]
        if final_norm:
            y = _rms(y, gf_ref[...])
        o_ref[...] = y


def _ffn(res, a, w_out, mixer_out, g, w_up, cw, cb, prev, w_down, g_final, nb, tm, final_norm):
    rows = res.shape[0]
    tn = FFN_TN
    nj = FFN_HIDDEN // tn
    hist = (FFN_CONV_WIDTH - 1) * nb
    col_a = lambda r: pl.BlockSpec((r, tn), lambda i, j: (0, j))
    col_b = lambda r: pl.BlockSpec((r, tn), lambda i, j: (0, nj + j))
    vec = pl.BlockSpec((1, D_MODEL), lambda i, j: (0, 0))
    k = w_out.shape[0]
    if mixer_out == "seq_major":
        a_spec = pl.BlockSpec((nb, tm // nb, k), lambda i, j: (0, i, 0))
        stage = [pltpu.VMEM((k // 128, tm, 128), F32)]
    else:
        a_spec = pl.BlockSpec((tm, k), lambda i, j: (i, 0))
        stage = []
    return pl.pallas_call(
        functools.partial(_ffn_kernel, nb=nb, tm=tm, final_norm=final_norm, mixer_out=mixer_out),
        grid=(rows // tm, nj),
        in_specs=[pl.BlockSpec((tm, D_MODEL), lambda i, j: (i, 0)), a_spec,
                  pl.BlockSpec(w_out.shape, lambda i, j: (0, 0)), vec,
                  col_a(D_MODEL), col_b(D_MODEL),
                  col_a(FFN_CONV_WIDTH), col_b(FFN_CONV_WIDTH), col_a(1), col_b(1),
                  col_a(hist), col_b(hist),
                  pl.BlockSpec((tn, D_MODEL), lambda i, j: (j, 0)), vec],
        out_specs=[pl.BlockSpec((tm, D_MODEL), lambda i, j: (i, 0)),
                   pl.BlockSpec((hist, tn), lambda i, j: (i, j)),
                   pl.BlockSpec((hist, tn), lambda i, j: (i, j))],
        out_shape=[jax.ShapeDtypeStruct((rows, D_MODEL), F32),
                   jax.ShapeDtypeStruct((rows // tm * hist, FFN_HIDDEN), F32),
                   jax.ShapeDtypeStruct((rows // tm * hist, FFN_HIDDEN), F32)],
        scratch_shapes=[pltpu.VMEM((tm, D_MODEL), F32), pltpu.VMEM((tm, D_MODEL), BF16),
                        pltpu.VMEM((tm, D_MODEL), F32),
                        pltpu.VMEM((hist + tm, tn), F32), pltpu.VMEM((hist + tm, tn), F32),
                        pltpu.VMEM((nj, hist, tn), F32), pltpu.VMEM((nj, hist, tn), F32)] + stage,
        compiler_params=_cparams(("arbitrary", "arbitrary")),
        name="conv_ffn",
    )(res, a, w_out, g, w_up, w_up, cw, cw, cb, cb, prev, prev, w_down, g_final)


def _ffn_cols_kernel(r_ref, a_ref, wo_ref, g_ref, wab_ref, cw_ref, cb_ref, prev_ref, wd_ref, gf_ref,
                     o_ref, hist_ref, x_s, xn_s, acc_s, hpa_s, hpb_s, *stage, nb, tm, final_norm, mixer_out):
    i = pl.program_id(0)
    hist = (FFN_CONV_WIDTH - 1) * nb
    nj = wd_ref.shape[0]
    rs = tm // FFN_SUBS
    sub = lambda r: slice(r * rs, (r + 1) * rs)

    if mixer_out == "seq_major":
        a_s, = stage
        nk = a_ref.shape[2] // 128
        for b in range(nb):
            for kt in range(nk):
                a_s[kt, pl.ds(b, tm // nb, stride=nb), :] = a_ref[b, :, kt * 128:(kt + 1) * 128]

    def project(r):
        if mixer_out == "seq_major":
            a = jnp.concatenate([a_s[kt, sub(r), :].astype(BF16) for kt in range(nk)], axis=1)
        else:
            a = a_ref[sub(r), :]
        if mixer_out == "glu":
            half = wo_ref.shape[1] // 2
            val = jnp.dot(a, wo_ref[:, :half], preferred_element_type=F32)
            gate = jnp.dot(a, wo_ref[:, half:], preferred_element_type=F32)
            y = val * jax.nn.sigmoid(gate)
        else:
            y = jnp.dot(a, wo_ref[...], preferred_element_type=F32)
        x_s[sub(r), :] = r_ref[sub(r), :] + y

    project(0)
    for r in range(FFN_SUBS):
        if r + 1 < FFN_SUBS:
            project(r + 1)
        xn_s[sub(r), :] = _rms(x_s[sub(r), :], g_ref[...]).astype(BF16)
    acc_s[...] = jnp.zeros_like(acc_s)

    @pl.when(i == 0)
    def _():
        hist_ref[...] = prev_ref[...]

    def up(c, slot):
        hpa_s[slot, 0:hist, :] = hist_ref[c]
        hpb_s[slot, 0:hist, :] = hist_ref[nj + c]
        for r in range(FFN_SUBS):
            up_rows(c, slot, r)

    def up_rows(c, slot, r):
        xr = xn_s[sub(r), :]
        hpa_s[slot, hist + r * rs:hist + (r + 1) * rs, :] = jnp.dot(xr, wab_ref[c], preferred_element_type=F32)
        hpb_s[slot, hist + r * rs:hist + (r + 1) * rs, :] = jnp.dot(xr, wab_ref[nj + c], preferred_element_type=F32)

    def down_rows(c, slot, r):
        def conv(hp_s, t):
            cw = cw_ref[t]
            y = hp_s[slot, r * rs:(r + 1) * rs, :] * cw[0:1, :]
            for k in range(1, FFN_CONV_WIDTH):
                y = y + hp_s[slot, k * nb + r * rs:k * nb + (r + 1) * rs, :] * cw[k:k + 1, :]
            return y + cb_ref[t]

        act = (jax.nn.gelu(conv(hpa_s, c)) * conv(hpb_s, nj + c)).astype(BF16)
        acc_s[sub(r), :] += jnp.dot(act, wd_ref[c], preferred_element_type=F32)

    def keep_history(c, slot):
        hist_ref[c] = hpa_s[slot, tm:tm + hist, :]
        hist_ref[nj + c] = hpb_s[slot, tm:tm + hist, :]

    def step(c, slot):
        nxt = 1 - slot
        hpa_s[nxt, 0:hist, :] = hist_ref[c + 1]
        hpb_s[nxt, 0:hist, :] = hist_ref[nj + c + 1]
        for r in range(FFN_SUBS):
            up_rows(c + 1, nxt, r)
            down_rows(c, slot, r)
        keep_history(c, slot)

    def tile_pair(t, carry):
        step(2 * t, 0)
        step(2 * t + 1, 1)
        return carry

    up(0, 0)
    lax.fori_loop(0, (nj - 1) // 2, tile_pair, 0)
    for c in range((nj - 1) // 2 * 2, nj - 1):
        step(c, c % 2)
    last_slot = (nj - 1) % 2
    for r in range(FFN_SUBS):
        down_rows(nj - 1, last_slot, r)
    keep_history(nj - 1, last_slot)

    y = x_s[...] + acc_s[...]
    if final_norm:
        y = _rms(y, gf_ref[...])
    o_ref[...] = y


def _ffn_cols(res, a, w_out, mixer_out, g, w_up, cw, cb, prev, w_down, g_final, nb, tm, final_norm):
    rows = res.shape[0]
    tn = FFN_TN
    nj = FFN_HIDDEN // tn
    hist = (FFN_CONV_WIDTH - 1) * nb
    tiles = lambda m: m.reshape(m.shape[0], 2 * nj, tn).transpose(1, 0, 2)
    once = lambda shape: pl.BlockSpec(shape, lambda i: (0,) * len(shape), pipeline_mode=pl.Buffered(1))
    k = w_out.shape[0]
    if mixer_out == "seq_major":
        a_spec = pl.BlockSpec((nb, tm // nb, k), lambda i: (0, i, 0))
        stage = [pltpu.VMEM((k // 128, tm, 128), F32)]
    else:
        a_spec = pl.BlockSpec((tm, k), lambda i: (i, 0))
        stage = []
    return pl.pallas_call(
        functools.partial(_ffn_cols_kernel, nb=nb, tm=tm, final_norm=final_norm, mixer_out=mixer_out),
        grid=(rows // tm,),
        in_specs=[pl.BlockSpec((tm, D_MODEL), lambda i: (i, 0)), a_spec, once(w_out.shape), once((1, D_MODEL)),
                  once((2 * nj, D_MODEL, tn)), once((2 * nj, FFN_CONV_WIDTH, tn)), once((2 * nj, 1, tn)),
                  once((2 * nj, hist, tn)), once((nj, tn, D_MODEL)), once((1, D_MODEL))],
        out_specs=[pl.BlockSpec((tm, D_MODEL), lambda i: (i, 0)),
                   pl.BlockSpec((2 * nj, hist, tn), lambda i: (0, 0, 0))],
        out_shape=[jax.ShapeDtypeStruct((rows, D_MODEL), F32),
                   jax.ShapeDtypeStruct((2 * nj, hist, tn), F32)],
        scratch_shapes=[pltpu.VMEM((tm, D_MODEL), F32), pltpu.VMEM((tm, D_MODEL), BF16),
                        pltpu.VMEM((tm, D_MODEL), F32),
                        pltpu.VMEM((2, hist + tm, tn), F32), pltpu.VMEM((2, hist + tm, tn), F32)] + stage,
        compiler_params=_cparams(("arbitrary",)),
        name="conv_ffn",
    )(res, a, w_out, g, tiles(w_up), tiles(cw), tiles(cb), tiles(prev), w_down.reshape(nj, tn, D_MODEL), g_final)


def _s5_disc_kernel(are_ref, aim_ref, ldt_ref, bre_ref, bim_ref, abr_ref, abi_ref, bbr_ref, bbi_ref):
    a_re, a_im = are_ref[...], aim_ref[...]
    dt = jnp.exp(ldt_ref[...])
    mag = jnp.exp(a_re * dt)
    ar = mag * jnp.cos(a_im * dt)
    ai = mag * jnp.sin(a_im * dt)
    den = a_re * a_re + a_im * a_im
    nr = ar - 1.0
    cr = (nr * a_re + ai * a_im) / den
    ci = (ai * a_re - nr * a_im) / den
    b_re, b_im = bre_ref[...], bim_ref[...]
    abr_ref[...] = ar
    abi_ref[...] = ai
    bbr_ref[...] = cr * b_re - ci * b_im
    bbi_ref[...] = cr * b_im + ci * b_re


def _s5_params(a_re, a_im, log_dt, b_re, b_im, c_re, c_im):
    rep = lambda a: jnp.repeat(a.astype(F32), S5_GROUP_CH, axis=0)
    rows_gc = lambda b: b.astype(F32).transpose(0, 2, 1).reshape(D_MODEL, S5_STATE)
    ldt = jnp.broadcast_to(log_dt.astype(F32)[:, None], (S5_GROUPS, S5_STATE))
    sds = jax.ShapeDtypeStruct((D_MODEL, S5_STATE), F32)
    abr, abi, bbr, bbi = pl.pallas_call(_s5_disc_kernel, out_shape=[sds] * 4, name="s5_discretize")(
        rep(a_re), rep(a_im), rep(ldt), rows_gc(b_re), rows_gc(b_im))
    eye = jnp.eye(S5_GROUPS // S5_KB, dtype=F32)

    def b_blocks(b):
        b = b.reshape(S5_KB, S5_GROUPS // S5_KB, S5_GROUP_CH, S5_STATE)
        return jnp.einsum('kgcp,gh->kgchp', b, eye).reshape(S5_KB, D_MODEL // S5_KB, S5_COLS // S5_KB).astype(BF16)

    def c_blocks(c):
        c = c.astype(F32).reshape(S5_KB, S5_GROUPS // S5_KB, S5_GROUP_CH, S5_STATE)
        return jnp.einsum('kgcp,gh->kgphc', c, eye).reshape(S5_KB, S5_COLS // S5_KB, D_MODEL // S5_KB).astype(BF16)

    return (b_blocks(bbr), b_blocks(bbi), c_blocks(c_re), c_blocks(c_im),
            abr[::S5_GROUP_CH].reshape(1, S5_COLS), abi[::S5_GROUP_CH].reshape(1, S5_COLS))


def _to_time_major(a):
    return a.transpose(1, 0, 2).reshape(a.shape[0] * a.shape[1], a.shape[2])


def _from_time_major(a, nb):
    return a.reshape(a.shape[0] // nb, nb, a.shape[1]).transpose(1, 0, 2)


def _trunk(x, nb, seq, s5_re, s5_im, lru_h, lru_conv, gdn_s, gdn_conv, ffn_conv, p):
    total = seq * nb
    tm = 512 if nb * (FFN_CONV_WIDTH - 1) <= 64 else 256
    rows = 512
    o_s5_re, o_s5_im, o_lru, o_lru_conv, o_gdn, o_gdn_conv, o_ffn_conv = [], [], [], [], [], [], []
    depth = p['norm_mix'].shape[0]
    for i in range(depth):
        kind, j = i % 3, i // 3
        g_mix = p['norm_mix'][i].reshape(1, D_MODEL)
        if kind == 0:
            bre, bim, cre, cim, are, aim = p['s5_disc'][j]
            u = _norm_mm(x, g_mix, p['s5_w_in'][j].astype(BF16), rows)
            y, hre, him = _s5_core(u, s5_re[j].reshape(nb, S5_COLS), s5_im[j].reshape(nb, S5_COLS),
                                   bre, bim, cre, cim, are, aim, p['s5_d'][j].reshape(1, D_MODEL), nb, rows)
            mix = (y, p['s5_w_glu'][j].astype(BF16), "glu")
            o_s5_re.append(hre.reshape(nb, S5_GROUPS, S5_STATE))
            o_s5_im.append(him.reshape(nb, S5_GROUPS, S5_STATE))
        elif kind == 1:
            proj = _norm_mm(x, g_mix, p['lru_w_in'][j].astype(BF16), rows)
            y, h_new, conv_new = _lru_core(
                proj, _to_time_major(lru_conv[j]), lru_h[j],
                p['lru_conv_w'][j], p['lru_conv_b'][j].reshape(1, LRU_WIDTH),
                p['lru_w_gate_a'][j].astype(BF16), p['lru_b_gate_a'][j].reshape(1, LRU_WIDTH),
                p['lru_w_gate_x'][j].astype(BF16), p['lru_b_gate_x'][j].reshape(1, LRU_WIDTH),
                p['lru_lambda'][j].reshape(1, LRU_WIDTH), nb, rows)
            mix = (y, p['lru_w_out'][j].astype(BF16), "plain")
            o_lru.append(h_new)
            o_lru_conv.append(_from_time_major(conv_new, nb))
        else:
            w_in = p['gdn_w_in'][j]
            w_pad = jnp.pad(w_in, ((0, 0), (0, GDN_PROJ_PAD - w_in.shape[1]))).astype(BF16)
            proj = _norm_mm(x, g_mix, w_pad, rows)
            pad8 = lambda a: jnp.pad(a.reshape(1, GDN_HEADS), ((0, 0), (0, 128 - GDN_HEADS)))
            chunk = GDN_CHUNK if seq >= GDN_CHUNK else 8
            batch_major = nb == 8 and seq % chunk == 0
            *qkvzgb, conv_new = _gdn_prep(proj, _to_time_major(gdn_conv[j]), p['gdn_conv_w'][j],
                                          pad8(p['gdn_a_log'][j]), pad8(p['gdn_dt_bias'][j]), nb, rows,
                                          batch_major)
            nw = p['gdn_norm'][j].reshape(1, GDN_DV)
            w_out = p['gdn_w_out'][j].astype(BF16)
            if batch_major:
                o, s_new = _gdn_core(*qkvzgb, gdn_s[j], nw, chunk, 4)
                mix = (o, w_out, "seq_major")
            else:
                lp = -(-seq // chunk) * chunk

                def bm(a):
                    a = a.reshape(seq, nb, a.shape[1]).transpose(1, 0, 2)
                    return jnp.pad(a, ((0, 0), (0, lp - seq), (0, 0)))

                o, s_new = _gdn_core(*[bm(a) for a in qkvzgb], gdn_s[j], nw, chunk, 4)
                o = o[:, :seq].transpose(1, 0, 2).reshape(total, GDN_KEY_DIM).astype(BF16)
                mix = (o, w_out, "plain")
            o_gdn.append(s_new)
            o_gdn_conv.append(_from_time_major(conv_new, nb))
        x, hist_new = _ffn_cols(x, *mix, p['norm_ffn'][i].reshape(1, D_MODEL), p['ffn_w_up'][i].astype(BF16),
                                p['ffn_conv_w'][i], p['ffn_conv_b'][i].reshape(1, 2 * FFN_HIDDEN),
                                _to_time_major(ffn_conv[i]), p['ffn_w_down'][i].astype(BF16),
                                p['norm_final'].reshape(1, D_MODEL), nb, tm, i == depth - 1)
        hist_new = hist_new.transpose(1, 0, 2).reshape(hist_new.shape[1], 2 * FFN_HIDDEN)
        o_ffn_conv.append(_from_time_major(hist_new, nb))
    return (x, jnp.stack(o_s5_re), jnp.stack(o_s5_im), jnp.stack(o_lru), jnp.stack(o_lru_conv),
            jnp.stack(o_gdn), jnp.stack(o_gdn_conv), jnp.stack(o_ffn_conv))


def kernel(x_prompt, x_sample, state_s5_re, state_s5_im, state_lru, state_lru_conv, state_gdn, state_gdn_conv, state_ffn_conv, norm_mix, norm_ffn, norm_final, s5_w_in, s5_a_re, s5_a_im, s5_log_dt, s5_b_re, s5_b_im, s5_c_re, s5_c_im, s5_d, s5_w_glu, lru_w_in, lru_conv_w, lru_conv_b, lru_w_gate_a, lru_b_gate_a, lru_w_gate_x, lru_b_gate_x, lru_lambda, lru_w_out, gdn_w_in, gdn_conv_w, gdn_a_log, gdn_dt_bias, gdn_norm, gdn_w_out, ffn_w_up, ffn_conv_w, ffn_conv_b, ffn_w_down):
    p = dict(norm_mix=norm_mix, norm_ffn=norm_ffn, norm_final=norm_final, s5_w_in=s5_w_in, s5_a_re=s5_a_re,
             s5_a_im=s5_a_im, s5_log_dt=s5_log_dt, s5_b_re=s5_b_re, s5_b_im=s5_b_im, s5_c_re=s5_c_re,
             s5_c_im=s5_c_im, s5_d=s5_d, s5_w_glu=s5_w_glu, lru_w_in=lru_w_in, lru_conv_w=lru_conv_w,
             lru_conv_b=lru_conv_b, lru_w_gate_a=lru_w_gate_a, lru_b_gate_a=lru_b_gate_a,
             lru_w_gate_x=lru_w_gate_x, lru_b_gate_x=lru_b_gate_x, lru_lambda=lru_lambda, lru_w_out=lru_w_out,
             gdn_w_in=gdn_w_in, gdn_conv_w=gdn_conv_w, gdn_a_log=gdn_a_log, gdn_dt_bias=gdn_dt_bias,
             gdn_norm=gdn_norm, gdn_w_out=gdn_w_out, ffn_w_up=ffn_w_up, ffn_conv_w=ffn_conv_w,
             ffn_conv_b=ffn_conv_b, ffn_w_down=ffn_w_down)
    p['s5_disc'] = [_s5_params(s5_a_re[j], s5_a_im[j], s5_log_dt[j], s5_b_re[j], s5_b_im[j], s5_c_re[j],
                               s5_c_im[j]) for j in range(s5_a_re.shape[0])]
    outs = []
    for x, states in (
            (x_prompt, None),
            (x_sample, (state_s5_re, state_s5_im, state_lru, state_lru_conv, state_gdn, state_gdn_conv,
                        state_ffn_conv))):
        nb, seq, _ = x.shape
        if states is None:
            states = tuple(jnp.zeros((s.shape[0], nb) + s.shape[2:], F32) for s in (
                state_s5_re, state_s5_im, state_lru, state_lru_conv, state_gdn, state_gdn_conv, state_ffn_conv))
        res = _trunk(_to_time_major(x), nb, seq, *states, p)
        outs.append((_from_time_major(res[0], nb),) + tuple(res[1:]))
    (y_p, *st_p), (y_s, *st_s) = outs
    return (y_p, y_s, *st_p, *st_s)
```

```python
import functools
import math

import jax
import jax.numpy as jnp
from jax import lax
from jax.experimental import pallas as pl
from jax.experimental.pallas import tpu as pltpu

F32 = jnp.float32
BF16 = jnp.bfloat16

D_MODEL = 1024
RMS_EPS = 1e-6
L2_EPS = 1e-6
S5_GROUPS = 64
S5_STATE = 64
S5_GROUP_CH = 16
S5_COLS = S5_GROUPS * S5_STATE
S5_KB = 8
S5_SCAN_LANES = 1024
LRU_WIDTH = 1280
LRU_BLOCK = 128
LRU_BLOCKS = LRU_WIDTH // LRU_BLOCK
LRU_C = 8.0
CONV_WIDTH = 4
GDN_HEADS = 8
GDN_DK = 128
GDN_DV = 128
GDN_KEY_DIM = GDN_HEADS * GDN_DK
GDN_CONV_DIM = 3 * GDN_KEY_DIM
GDN_CHUNK = 64
GDN_PROJ_PAD = 4352
FFN_HIDDEN = 2816
FFN_CONV_WIDTH = 3
FFN_TN = 256
FFN_SUBS = 2
SUB_ROWS = 256
VMEM_LIMIT_BYTES = 56 * 1024 * 1024


def _cparams(sem):
    return pltpu.CompilerParams(dimension_semantics=sem, vmem_limit_bytes=VMEM_LIMIT_BYTES)


def _rms(x, g):
    ms = jnp.mean(x * x, axis=-1, keepdims=True)
    return x * lax.rsqrt(ms + RMS_EPS) * g


def _softplus(x):
    return jnp.maximum(x, 0.0) + jnp.log1p(jnp.exp(-jnp.abs(x)))


def _expm1(x):
    u = jnp.exp(x)
    small = jnp.abs(x) < 0.5
    usable = small & (u != 1.0)
    ratio = (u - 1.0) * x / jnp.log(jnp.where(usable, u, 2.0))
    return jnp.where(small, jnp.where(usable, ratio, x), u - 1.0)


def _dot(a, b):
    return jnp.dot(a.astype(BF16), b.astype(BF16), preferred_element_type=F32)


def _dot_nt(a, b):
    return lax.dot_general(a.astype(BF16), b.astype(BF16), (((1,), (1,)), ((), ())),
                           preferred_element_type=F32)


def _split2(a):
    hi = a.astype(BF16)
    lo = (a - hi.astype(F32)).astype(BF16)
    return hi, lo


def _split3(a):
    hi = a.astype(BF16)
    r = a - hi.astype(F32)
    mid = r.astype(BF16)
    lo = (r - mid.astype(F32)).astype(BF16)
    return hi, mid, lo


def _dot3(a, b):
    ah, al = _split2(a)
    bh, bl = _split2(b)
    d = functools.partial(jnp.dot, preferred_element_type=F32)
    return d(ah, bh) + d(al, bh) + d(ah, bl)


def _norm_mm_kernel(x_ref, g_ref, w_ref, o_ref, xn_ref, *, tm):
    sub = lambda r: slice(r * SUB_ROWS, (r + 1) * SUB_ROWS)

    def norm(r):
        xn_ref[sub(r), :] = _rms(x_ref[sub(r), :], g_ref[...]).astype(BF16)

    nsub = tm // SUB_ROWS
    norm(0)
    for r in range(nsub):
        if r + 1 < nsub:
            norm(r + 1)
        o_ref[sub(r), :] = jnp.dot(xn_ref[sub(r), :], w_ref[...], preferred_element_type=F32)


def _norm_mm(x, g, w, tm):
    rows, n = x.shape[0], w.shape[1]
    return pl.pallas_call(
        functools.partial(_norm_mm_kernel, tm=tm),
        grid=(rows // tm,),
        in_specs=[pl.BlockSpec((tm, D_MODEL), lambda i: (i, 0)),
                  pl.BlockSpec((1, D_MODEL), lambda i: (0, 0)),
                  pl.BlockSpec((D_MODEL, n), lambda i: (0, 0))],
        out_specs=pl.BlockSpec((tm, n), lambda i: (i, 0)),
        out_shape=jax.ShapeDtypeStruct((rows, n), F32),
        scratch_shapes=[pltpu.VMEM((tm, D_MODEL), BF16)],
        compiler_params=_cparams(("parallel",)),
        name="norm_mm",
    )(x, g, w)


def _s5_core_kernel(u_ref, h0re_ref, h0im_ref, bre_ref, bim_ref, cre_ref, cim_ref, are_ref, aim_ref,
                    d_ref, y_ref, hre_out, him_out, hre_s, him_s, *, nb, rows):
    i = pl.program_id(0)

    @pl.when(i == 0)
    def _():
        hre_s[0:nb, :] = h0re_ref[...]
        him_s[0:nb, :] = h0im_ref[...]

    kw = S5_COLS // S5_KB
    uw = D_MODEL // S5_KB
    per_grp = S5_SCAN_LANES // kw
    n_grp = S5_COLS // S5_SCAN_LANES

    def project_in(grp):
        for kb in range(grp * per_grp, (grp + 1) * per_grp):
            ukb = u_ref[:, kb * uw:(kb + 1) * uw].astype(BF16)
            hre_s[nb:nb + rows, kb * kw:(kb + 1) * kw] = jnp.dot(ukb, bre_ref[kb], preferred_element_type=F32)
            him_s[nb:nb + rows, kb * kw:(kb + 1) * kw] = jnp.dot(ukb, bim_ref[kb], preferred_element_type=F32)

    def scan(grp):
        cols = slice(grp * S5_SCAN_LANES, (grp + 1) * S5_SCAN_LANES)
        are = jnp.broadcast_to(are_ref[:, cols], (nb, S5_SCAN_LANES))
        aim = jnp.broadcast_to(aim_ref[:, cols], (nb, S5_SCAN_LANES))
        hr, hi = hre_s[0:nb, cols], him_s[0:nb, cols]
        for t in range(rows // nb):
            r = slice(nb + t * nb, 2 * nb + t * nb)
            hr, hi = (are * hr - aim * hi + hre_s[r, cols],
                      are * hi + aim * hr + him_s[r, cols])
            hre_s[r, cols] = hr
            him_s[r, cols] = hi

    def project_out(grp):
        for kb in range(grp * per_grp, (grp + 1) * per_grp):
            hr = hre_s[nb:nb + rows, kb * kw:(kb + 1) * kw].astype(BF16)
            hi = him_s[nb:nb + rows, kb * kw:(kb + 1) * kw].astype(BF16)
            yk = (jnp.dot(hr, cre_ref[kb], preferred_element_type=F32)
                  - jnp.dot(hi, cim_ref[kb], preferred_element_type=F32))
            yk = yk + d_ref[:, kb * uw:(kb + 1) * uw] * u_ref[:, kb * uw:(kb + 1) * uw]
            y_ref[:, kb * uw:(kb + 1) * uw] = jax.nn.gelu(yk).astype(BF16)

    project_in(0)
    for grp in range(n_grp):
        if grp + 1 < n_grp:
            project_in(grp + 1)
        scan(grp)
        project_out(grp)

    last_re = hre_s[rows:rows + nb, :]
    last_im = him_s[rows:rows + nb, :]
    hre_s[0:nb, :] = last_re
    him_s[0:nb, :] = last_im
    hre_out[...] = last_re
    him_out[...] = last_im


def _s5_core(u, h0re, h0im, bre, bim, cre, cim, are, aim, d, nb, rows):
    total = u.shape[0]
    full = lambda shape: pl.BlockSpec(shape, lambda i: (0,) * len(shape))
    return pl.pallas_call(
        functools.partial(_s5_core_kernel, nb=nb, rows=rows),
        grid=(total // rows,),
        in_specs=[pl.BlockSpec((rows, D_MODEL), lambda i: (i, 0)),
                  full((nb, S5_COLS)), full((nb, S5_COLS)),
                  full(bre.shape), full(bim.shape), full(cre.shape), full(cim.shape),
                  full((1, S5_COLS)), full((1, S5_COLS)), full((1, D_MODEL))],
        out_specs=[pl.BlockSpec((rows, D_MODEL), lambda i: (i, 0)),
                   full((nb, S5_COLS)), full((nb, S5_COLS))],
        out_shape=[jax.ShapeDtypeStruct((total, D_MODEL), BF16),
                   jax.ShapeDtypeStruct((nb, S5_COLS), F32),
                   jax.ShapeDtypeStruct((nb, S5_COLS), F32)],
        scratch_shapes=[pltpu.VMEM((nb + rows, S5_COLS), F32),
                        pltpu.VMEM((nb + rows, S5_COLS), F32)],
        compiler_params=_cparams(("arbitrary",)),
        name="s5_core",
    )(u, h0re, h0im, bre, bim, cre, cim, are, aim, d)


def _lru_core_kernel(x_ref, g_ref, win_ref, prev_ref, h0_ref, cw_ref, cb_ref, wga_ref, bga_ref, wgx_ref,
                     bgx_ref, lam_ref, y_ref, hout_ref, cout_ref, xn_s, gate_s, xp_s, h_s, a_s, *, nb, rows):
    i = pl.program_id(0)
    hist = (CONV_WIDTH - 1) * nb

    @pl.when(i == 0)
    def _():
        xp_s[0:hist, :] = prev_ref[...]
        h_s[0:nb, :] = h0_ref[...]

    xn_s[...] = _rms(x_ref[...], g_ref[...]).astype(BF16)
    c8 = -LRU_C * _softplus(-lam_ref[...])

    def project(n):
        pg = jnp.dot(xn_s[...], win_ref[n], preferred_element_type=F32)
        sl = slice(n * LRU_BLOCK, (n + 1) * LRU_BLOCK)
        gate_s[:, sl] = pg[:, :LRU_BLOCK]
        xp_s[hist:hist + rows, sl] = pg[:, LRU_BLOCK:]

    def gates(n):
        sl = slice(n * LRU_BLOCK, (n + 1) * LRU_BLOCK)
        xcn = xp_s[0:rows, sl] * cw_ref[0:1, sl]
        for k in range(1, CONV_WIDTH):
            xcn = xcn + xp_s[k * nb:k * nb + rows, sl] * cw_ref[k:k + 1, sl]
        xcn = xcn + cb_ref[:, sl]
        xcb = xcn.astype(BF16)
        r = jax.nn.sigmoid(jnp.dot(xcb, wga_ref[n], preferred_element_type=F32) + bga_ref[:, sl])
        ig = jax.nn.sigmoid(jnp.dot(xcb, wgx_ref[n], preferred_element_type=F32) + bgx_ref[:, sl])
        log_a = c8[:, sl] * r
        a_s[:, sl] = jnp.exp(log_a)
        h_s[nb:nb + rows, sl] = jnp.sqrt(-_expm1(2.0 * log_a)) * ig * xcn

    project(0)
    for n in range(LRU_BLOCKS):
        if n + 1 < LRU_BLOCKS:
            project(n + 1)
        gates(n)

    def step(t, carry):
        r0 = pl.multiple_of(t * nb, nb)
        r1 = pl.multiple_of(t * nb + nb, nb)
        h_s[pl.ds(r1, nb), :] = a_s[pl.ds(r0, nb), :] * h_s[pl.ds(r0, nb), :] + h_s[pl.ds(r1, nb), :]
        return carry

    lax.fori_loop(0, rows // nb, step, 0)

    y_ref[...] = (jax.nn.gelu(gate_s[...]) * h_s[nb:nb + rows, :]).astype(BF16)
    tail = xp_s[rows:rows + hist, :]
    last = h_s[rows:rows + nb, :]
    xp_s[0:hist, :] = tail
    h_s[0:nb, :] = last
    cout_ref[...] = tail
    hout_ref[...] = last


def _lru_core(x, g, w_in, prev, h0, cw, cb, wga, bga, wgx, bgx, lam, nb, rows):
    total = x.shape[0]
    hist = (CONV_WIDTH - 1) * nb
    full = lambda shape: pl.BlockSpec(shape, lambda i: (0,) * len(shape))
    return pl.pallas_call(
        functools.partial(_lru_core_kernel, nb=nb, rows=rows),
        grid=(total // rows,),
        in_specs=[pl.BlockSpec((rows, D_MODEL), lambda i: (i, 0)), full((1, D_MODEL)),
                  pl.BlockSpec(w_in.shape, lambda i: (0, 0, 0), pipeline_mode=pl.Buffered(1)),
                  full((hist, LRU_WIDTH)), full((nb, LRU_WIDTH)),
                  full((CONV_WIDTH, LRU_WIDTH)), full((1, LRU_WIDTH)),
                  full(wga.shape), full((1, LRU_WIDTH)), full(wgx.shape), full((1, LRU_WIDTH)),
                  full((1, LRU_WIDTH))],
        out_specs=[pl.BlockSpec((rows, LRU_WIDTH), lambda i: (i, 0)),
                   full((nb, LRU_WIDTH)), full((hist, LRU_WIDTH))],
        out_shape=[jax.ShapeDtypeStruct((total, LRU_WIDTH), BF16),
                   jax.ShapeDtypeStruct((nb, LRU_WIDTH), F32),
                   jax.ShapeDtypeStruct((hist, LRU_WIDTH), F32)],
        scratch_shapes=[pltpu.VMEM((rows, D_MODEL), BF16),
                        pltpu.VMEM((rows, LRU_WIDTH), F32),
                        pltpu.VMEM((hist + rows, LRU_WIDTH), F32),
                        pltpu.VMEM((nb + rows, LRU_WIDTH), F32),
                        pltpu.VMEM((rows, LRU_WIDTH), F32)],
        compiler_params=_cparams(("arbitrary",)),
        name="lru_core",
    )(x, g, w_in, prev, h0, cw, cb, wga, bga, wgx, bgx, lam)


def _gdn_prep_kernel(x_ref, gn_ref, win_ref, prev_ref, cw_ref, alog_ref, dtb_ref,
                     q_ref, k_ref, v_ref, zo_ref, g_ref, beta_ref, cout_ref, xn_s, xp_s, ab_s, st_s,
                     *, nb, rows, batch_major):
    i = pl.program_id(0)
    hist = (CONV_WIDTH - 1) * nb
    n_qkv = 3 * GDN_HEADS
    n_z = GDN_HEADS

    @pl.when(i == 0)
    def _():
        xp_s[0:hist, :] = prev_ref[...]

    xn_s[...] = _rms(x_ref[...], gn_ref[...]).astype(BF16)

    def project(p):
        pg = jnp.dot(xn_s[...], win_ref[p], preferred_element_type=F32)
        for half in range(2):
            s = 2 * p + half
            col = pg[:, half * 128:(half + 1) * 128]
            if s < n_qkv:
                xp_s[hist:hist + rows, s * 128:(s + 1) * 128] = col
            elif s < n_qkv + n_z:
                if batch_major:
                    st_s[s] = col
                else:
                    zo_ref[:, (s - n_qkv) * 128:(s - n_qkv + 1) * 128] = col
            elif s == n_qkv + n_z:
                ab_s[...] = col

    def activate(s):
        part, h = divmod(s, GDN_HEADS)
        sl = slice(s * 128, (s + 1) * 128)
        acc = xp_s[0:rows, sl] * cw_ref[0:1, sl]
        for k in range(1, CONV_WIDTH):
            acc = acc + xp_s[k * nb:k * nb + rows, sl] * cw_ref[k:k + 1, sl]
        y = acc * jax.nn.sigmoid(acc)
        if part < 2:
            y = y * lax.rsqrt(jnp.sum(y * y, axis=-1, keepdims=True) + L2_EPS)
        if part == 0:
            y = y * (GDN_DK ** -0.5)
        if batch_major:
            st_s[s] = y
        else:
            (q_ref, k_ref, v_ref)[part][:, h * GDN_DK:(h + 1) * GDN_DK] = y

    n_pairs = win_ref.shape[0]
    project(0)
    for p in range(n_pairs):
        if p + 1 < n_pairs:
            project(p + 1)
        for s in (2 * p, 2 * p + 1):
            if s < n_qkv:
                activate(s)

    ab = ab_s[...]
    g = -jnp.exp(alog_ref[...]) * _softplus(ab + dtb_ref[...])
    beta = jax.nn.sigmoid(ab)
    if batch_major:
        st_s[4 * GDN_HEADS] = g
        st_s[4 * GDN_HEADS + 1] = beta
        steps = rows // nb
        for b in range(nb):
            pick = pl.ds(b, steps, stride=nb)
            for part, out in enumerate((q_ref, k_ref, v_ref, zo_ref)):
                for h in range(GDN_HEADS):
                    out[b, :, h * GDN_DK:(h + 1) * GDN_DK] = st_s[part * GDN_HEADS + h, pick, :]
            g_ref[b] = st_s[4 * GDN_HEADS, pick, :]
            beta_ref[b] = st_s[4 * GDN_HEADS + 1, pick, :]
    else:
        g_ref[...] = g
        beta_ref[...] = beta
    tail = xp_s[rows:rows + hist, :]
    xp_s[0:hist, :] = tail
    cout_ref[...] = tail


def _gdn_prep(x, gn, w_in, prev, cw, alog, dtb, nb, rows, batch_major):
    total = x.shape[0]
    hist = (CONV_WIDTH - 1) * nb
    full = lambda shape: pl.BlockSpec(shape, lambda i: (0,) * len(shape))
    tile = lambda n: pl.BlockSpec((rows, n), lambda i: (i, 0))
    if batch_major:
        out_tile = lambda n: pl.BlockSpec((nb, rows // nb, n), lambda i: (0, i, 0))
        out_sds = lambda n: jax.ShapeDtypeStruct((nb, total // nb, n), F32)
    else:
        out_tile = tile
        out_sds = lambda n: jax.ShapeDtypeStruct((total, n), F32)
    widths = (GDN_KEY_DIM,) * 4 + (128, 128)
    return pl.pallas_call(
        functools.partial(_gdn_prep_kernel, nb=nb, rows=rows, batch_major=batch_major),
        grid=(total // rows,),
        in_specs=[tile(D_MODEL), full((1, D_MODEL)),
                  pl.BlockSpec(w_in.shape, lambda i: (0, 0, 0), pipeline_mode=pl.Buffered(1)),
                  full((hist, GDN_CONV_DIM)), full((CONV_WIDTH, GDN_CONV_DIM)),
                  full((1, 128)), full((1, 128))],
        out_specs=[out_tile(n) for n in widths] + [full((hist, GDN_CONV_DIM))],
        out_shape=[out_sds(n) for n in widths] + [jax.ShapeDtypeStruct((hist, GDN_CONV_DIM), F32)],
        scratch_shapes=[pltpu.VMEM((rows, D_MODEL), BF16),
                        pltpu.VMEM((hist + rows, GDN_CONV_DIM), F32),
                        pltpu.VMEM((rows, 128), F32),
                        pltpu.VMEM((4 * GDN_HEADS + 2, rows if batch_major else 8, 128), F32)],
        compiler_params=_cparams(("arbitrary",)),
        name="gdn_prep",
    )(x, gn, w_in, prev, cw, alog, dtb)


def _unit_lower_inverses(ms, ri, ci, chunk):
    eye = (ri == ci).astype(F32)
    blk = (ri >> 3) == (ci >> 3)
    n1 = [jnp.where(blk, -m, 0.0) for m in ms]
    n2 = [_dot(a, a) for a in n1]
    n4 = [_dot(a, a) for a in n2]
    ts = [_dot(eye + a, eye + b) for a, b in zip(n1, n2)]
    ts = [_dot(t, eye + a) for t, a in zip(ts, n4)]
    shift = 3
    while (1 << shift) < chunk:
        pair = ((ri >> (shift + 1)) == (ci >> (shift + 1))) & ((ri >> shift) != (ci >> shift))
        left = [_dot(t, jnp.where(pair, m, 0.0)) for t, m in zip(ts, ms)]
        ts = [t - _dot(a, t) for t, a in zip(ts, left)]
        shift += 1
    return ts


def _gdn_core_kernel(q_ref, k_ref, v_ref, z_ref, g_ref, beta_ref, s0_ref, nw_ref, o_ref, s_ref, *, chunk, bb):
    c = pl.program_id(1)

    @pl.when(c == 0)
    def _():
        s_ref[...] = s0_ref[...]

    ri = lax.broadcasted_iota(jnp.int32, (chunk, chunk), 0)
    ci = lax.broadcasted_iota(jnp.int32, (chunk, chunk), 1)
    causal = ri >= ci
    strict = ri > ci
    tril = causal.astype(BF16)
    e_r = lax.broadcasted_iota(jnp.int32, (128, 128), 0)
    e_c = lax.broadcasted_iota(jnp.int32, (128, 128), 1)
    eye128 = (e_r == e_c).astype(BF16)
    dotf = functools.partial(jnp.dot, preferred_element_type=F32)
    nt = lambda a, b: lax.dot_general(a, b, (((1,), (1,)), ((), ())), preferred_element_type=F32)
    nw = nw_ref[...]

    cums, cum_ts, ecums, e_lasts, e_rests, betas = [], [], [], [], [], []
    for bi in range(bb):
        g3 = _split3(g_ref[bi])
        cum = dotf(tril, g3[0]) + dotf(tril, g3[1]) + dotf(tril, g3[2])
        c3 = _split3(cum)
        cums.append(cum)
        cum_ts.append(nt(eye128, c3[0]) + nt(eye128, c3[1]) + nt(eye128, c3[2]))
        ecums.append(jnp.exp(cum))
        g_last = cum[chunk - 1:chunk, :]
        e_lasts.append(jnp.exp(g_last))
        e_rests.append(jnp.exp(g_last - cum))
        betas.append(beta_ref[bi])

    units = [(bi, h) for bi in range(bb) for h in range(GDN_HEADS)]
    col = lambda a, h: a[:, h:h + 1]
    sl = lambda h: slice(h * GDN_DK, (h + 1) * GDN_DK)
    q = [q_ref[bi, :, sl(h)] for bi, h in units]
    k = [k_ref[bi, :, sl(h)] for bi, h in units]
    decay = [jnp.exp(jnp.where(causal, col(cums[bi], h) - cum_ts[bi][h:h + 1, :], -jnp.inf)) for bi, h in units]
    k_beta = [kk * col(betas[bi], GDN_HEADS + h) for kk, (bi, h) in zip(k, units)]
    ak = [_dot_nt(jnp.concatenate([kb, qq], axis=0), kk) for kb, qq, kk in zip(k_beta, q, k)]
    ms = [jnp.where(strict, a[:chunk] * d, 0.0) for a, d in zip(ak, decay)]
    ts = _unit_lower_inverses(ms, ri, ci, chunk)
    rhs = [jnp.concatenate([v_ref[bi, :, sl(h)] * col(betas[bi], GDN_HEADS + h), kb * col(ecums[bi], h)], axis=1)
           for kb, (bi, h) in zip(k_beta, units)]
    sol = [_dot(t, r) for t, r in zip(ts, rhs)]
    s_old = [s_ref[bi, h] for bi, h in units]
    ws = [_dot(jnp.concatenate([so[:, GDN_DV:], qq * col(ecums[bi], h)], axis=0), s)
          for so, qq, s, (bi, h) in zip(sol, q, s_old, units)]
    v_new = [so[:, :GDN_DV] - w[:chunk] for so, w in zip(sol, ws)]
    o = [w[chunk:] + _dot(a[chunk:] * d, vn) for w, a, d, vn in zip(ws, ak, decay, v_new)]
    k_dec_t = [nt(eye128, (kk * col(e_rests[bi], h)).astype(BF16)) for kk, (bi, h) in zip(k, units)]
    for (bi, h), s, kt, vn, oo in zip(units, s_old, k_dec_t, v_new, o):
        s_ref[bi, h] = s * col(e_lasts[bi], h) + _dot(kt, vn)
        on = oo * lax.rsqrt(jnp.mean(oo * oo, axis=-1, keepdims=True) + RMS_EPS) * nw
        zh = z_ref[bi, :, sl(h)]
        o_ref[bi, :, sl(h)] = on * (zh * jax.nn.sigmoid(zh))


def _gdn_core(q, k, v, z, g, beta, s0, nw, chunk, bb):
    nb, lp = q.shape[0], q.shape[1]
    seq = lambda n: pl.BlockSpec((bb, chunk, n), lambda b, c: (b, c, 0))
    st = pl.BlockSpec((bb, GDN_HEADS, GDN_DK, GDN_DV), lambda b, c: (b, 0, 0, 0))
    return pl.pallas_call(
        functools.partial(_gdn_core_kernel, chunk=chunk, bb=bb),
        grid=(nb // bb, lp // chunk),
        in_specs=[seq(GDN_KEY_DIM), seq(GDN_KEY_DIM), seq(GDN_KEY_DIM), seq(GDN_KEY_DIM),
                  seq(128), seq(128), st, pl.BlockSpec((1, GDN_DV), lambda b, c: (0, 0))],
        out_specs=[seq(GDN_KEY_DIM), st],
        out_shape=[jax.ShapeDtypeStruct((nb, lp, GDN_KEY_DIM), F32),
                   jax.ShapeDtypeStruct(s0.shape, F32)],
        compiler_params=_cparams(("parallel", "arbitrary")),
        name="gdn_core",
    )(q, k, v, z, g, beta, s0, nw)


def _ffn_kernel(r_ref, a_ref, wo_ref, g_ref, wa_ref, wb_ref, cwa_ref, cwb_ref, cba_ref, cbb_ref, pa_ref, pb_ref,
                wd_ref, gf_ref, o_ref, ca_out, cb_out, x_s, xn_s, acc_s, hpa_s, hpb_s, cara_s, carb_s, *stage,
                nb, tm, final_norm, mixer_out):
    i = pl.program_id(0)
    j = pl.program_id(1)
    hist = (FFN_CONV_WIDTH - 1) * nb
    rs = min(SUB_ROWS, tm)
    nsub = tm // rs
    sub = lambda r: slice(r * rs, (r + 1) * rs)

    @pl.when(j == 0)
    def _():
        if mixer_out == "seq_major":
            a_s, = stage
            nk = a_ref.shape[2] // 128
            for b in range(nb):
                for kt in range(nk):
                    a_s[kt, pl.ds(b, tm // nb, stride=nb), :] = a_ref[b, :, kt * 128:(kt + 1) * 128]

        def project(r):
            if mixer_out == "seq_major":
                a = jnp.concatenate([a_s[kt, sub(r), :].astype(BF16) for kt in range(nk)], axis=1)
            else:
                a = a_ref[sub(r), :]
            if mixer_out == "glu":
                half = wo_ref.shape[1] // 2
                val = jnp.dot(a, wo_ref[:, :half], preferred_element_type=F32)
                gate = jnp.dot(a, wo_ref[:, half:], preferred_element_type=F32)
                y = val * jax.nn.sigmoid(gate)
            else:
                y = jnp.dot(a, wo_ref[...], preferred_element_type=F32)
            x_s[sub(r), :] = r_ref[sub(r), :] + y

        project(0)
        for r in range(nsub):
            if r + 1 < nsub:
                project(r + 1)
            xn_s[sub(r), :] = _rms(x_s[sub(r), :], g_ref[...]).astype(BF16)
        acc_s[...] = jnp.zeros_like(acc_s)

    @pl.when(i == 0)
    def _():
        hpa_s[0:hist, :] = pa_ref[...]
        hpb_s[0:hist, :] = pb_ref[...]

    @pl.when(i > 0)
    def _():
        hpa_s[0:hist, :] = cara_s[j]
        hpb_s[0:hist, :] = carb_s[j]

    def up(r):
        xr = xn_s[r * rs:(r + 1) * rs, :]
        hpa_s[hist + r * rs:hist + (r + 1) * rs, :] = jnp.dot(xr, wa_ref[...], preferred_element_type=F32)
        hpb_s[hist + r * rs:hist + (r + 1) * rs, :] = jnp.dot(xr, wb_ref[...], preferred_element_type=F32)

    def conv(hp_s, cw_ref, cb_ref, r):
        y = hp_s[r * rs:(r + 1) * rs, :] * cw_ref[0:1, :]
        for k in range(1, FFN_CONV_WIDTH):
            y = y + hp_s[k * nb + r * rs:k * nb + (r + 1) * rs, :] * cw_ref[k:k + 1, :]
        return y + cb_ref[...]

    def down(r):
        act = (jax.nn.gelu(conv(hpa_s, cwa_ref, cba_ref, r)) * conv(hpb_s, cwb_ref, cbb_ref, r)).astype(BF16)
        acc_s[r * rs:(r + 1) * rs, :] += jnp.dot(act, wd_ref[...], preferred_element_type=F32)

    up(0)
    for r in range(nsub):
        if r + 1 < nsub:
            up(r + 1)
        down(r)

    tail_a = hpa_s[tm:tm + hist, :]
    tail_b = hpb_s[tm:tm + hist, :]
    cara_s[j] = tail_a
    carb_s[j] = tail_b
    ca_out[...] = tail_a
    cb_out[...] = tail_b

    @pl.when(j == pl.num_programs(1) - 1)
    def _():
        y = x_s[...] + acc_s[...]
        if final_norm:
            y = _rms(y, gf_ref[...])
        o_ref[...] = y


def _ffn(res, a, w_out, mixer_out, g, w_up, cw, cb, prev, w_down, g_final, nb, tm, final_norm):
    rows = res.shape[0]
    tn = FFN_TN
    nj = FFN_HIDDEN // tn
    hist = (FFN_CONV_WIDTH - 1) * nb
    col_a = lambda r: pl.BlockSpec((r, tn), lambda i, j: (0, j))
    col_b = lambda r: pl.BlockSpec((r, tn), lambda i, j: (0, nj + j))
    vec = pl.BlockSpec((1, D_MODEL), lambda i, j: (0, 0))
    k = w_out.shape[0]
    if mixer_out == "seq_major":
        a_spec = pl.BlockSpec((nb, tm // nb, k), lambda i, j: (0, i, 0))
        stage = [pltpu.VMEM((k // 128, tm, 128), F32)]
    else:
        a_spec = pl.BlockSpec((tm, k), lambda i, j: (i, 0))
        stage = []
    return pl.pallas_call(
        functools.partial(_ffn_kernel, nb=nb, tm=tm, final_norm=final_norm, mixer_out=mixer_out),
        grid=(rows // tm, nj),
        in_specs=[pl.BlockSpec((tm, D_MODEL), lambda i, j: (i, 0)), a_spec,
                  pl.BlockSpec(w_out.shape, lambda i, j: (0, 0)), vec,
                  col_a(D_MODEL), col_b(D_MODEL),
                  col_a(FFN_CONV_WIDTH), col_b(FFN_CONV_WIDTH), col_a(1), col_b(1),
                  col_a(hist), col_b(hist),
                  pl.BlockSpec((tn, D_MODEL), lambda i, j: (j, 0)), vec],
        out_specs=[pl.BlockSpec((tm, D_MODEL), lambda i, j: (i, 0)),
                   pl.BlockSpec((hist, tn), lambda i, j: (i, j)),
                   pl.BlockSpec((hist, tn), lambda i, j: (i, j))],
        out_shape=[jax.ShapeDtypeStruct((rows, D_MODEL), F32),
                   jax.ShapeDtypeStruct((rows // tm * hist, FFN_HIDDEN), F32),
                   jax.ShapeDtypeStruct((rows // tm * hist, FFN_HIDDEN), F32)],
        scratch_shapes=[pltpu.VMEM((tm, D_MODEL), F32), pltpu.VMEM((tm, D_MODEL), BF16),
                        pltpu.VMEM((tm, D_MODEL), F32),
                        pltpu.VMEM((hist + tm, tn), F32), pltpu.VMEM((hist + tm, tn), F32),
                        pltpu.VMEM((nj, hist, tn), F32), pltpu.VMEM((nj, hist, tn), F32)] + stage,
        compiler_params=_cparams(("arbitrary", "arbitrary")),
        name="conv_ffn",
    )(res, a, w_out, g, w_up, w_up, cw, cw, cb, cb, prev, prev, w_down, g_final)


def _ffn_cols_kernel(r_ref, a_ref, wo_ref, g_ref, wab_ref, cw_ref, cb_ref, prev_ref, wd_ref, gf_ref,
                     o_ref, hist_ref, x_s, xn_s, acc_s, hpa_s, hpb_s, *stage, nb, tm, final_norm, mixer_out):
    i = pl.program_id(0)
    hist = (FFN_CONV_WIDTH - 1) * nb
    nj = wd_ref.shape[0]
    rs = tm // FFN_SUBS
    sub = lambda r: slice(r * rs, (r + 1) * rs)

    if mixer_out == "seq_major":
        a_s, = stage
        nk = a_ref.shape[2] // 128
        for b in range(nb):
            for kt in range(nk):
                a_s[kt, pl.ds(b, tm // nb, stride=nb), :] = a_ref[b, :, kt * 128:(kt + 1) * 128]

    def project(r):
        if mixer_out == "seq_major":
            a = jnp.concatenate([a_s[kt, sub(r), :].astype(BF16) for kt in range(nk)], axis=1)
        else:
            a = a_ref[sub(r), :]
        if mixer_out == "glu":
            half = wo_ref.shape[1] // 2
            val = jnp.dot(a, wo_ref[:, :half], preferred_element_type=F32)
            gate = jnp.dot(a, wo_ref[:, half:], preferred_element_type=F32)
            y = val * jax.nn.sigmoid(gate)
        else:
            y = jnp.dot(a, wo_ref[...], preferred_element_type=F32)
        x_s[sub(r), :] = r_ref[sub(r), :] + y

    project(0)
    for r in range(FFN_SUBS):
        if r + 1 < FFN_SUBS:
            project(r + 1)
        xn_s[sub(r), :] = _rms(x_s[sub(r), :], g_ref[...]).astype(BF16)
    acc_s[...] = jnp.zeros_like(acc_s)

    @pl.when(i == 0)
    def _():
        hist_ref[...] = prev_ref[...]

    def up(c, slot):
        hpa_s[slot, 0:hist, :] = hist_ref[c]
        hpb_s[slot, 0:hist, :] = hist_ref[nj + c]
        for r in range(FFN_SUBS):
            up_rows(c, slot, r)

    def up_rows(c, slot, r):
        xr = xn_s[sub(r), :]
        hpa_s[slot, hist + r * rs:hist + (r + 1) * rs, :] = jnp.dot(xr, wab_ref[c], preferred_element_type=F32)
        hpb_s[slot, hist + r * rs:hist + (r + 1) * rs, :] = jnp.dot(xr, wab_ref[nj + c], preferred_element_type=F32)

    def down_rows(c, slot, r):
        def conv(hp_s, t):
            cw = cw_ref[t]
            y = hp_s[slot, r * rs:(r + 1) * rs, :] * cw[0:1, :]
            for k in range(1, FFN_CONV_WIDTH):
                y = y + hp_s[slot, k * nb + r * rs:k * nb + (r + 1) * rs, :] * cw[k:k + 1, :]
            return y + cb_ref[t]

        act = (jax.nn.gelu(conv(hpa_s, c)) * conv(hpb_s, nj + c)).astype(BF16)
        acc_s[sub(r), :] += jnp.dot(act, wd_ref[c], preferred_element_type=F32)

    def keep_history(c, slot):
        hist_ref[c] = hpa_s[slot, tm:tm + hist, :]
        hist_ref[nj + c] = hpb_s[slot, tm:tm + hist, :]

    def step(c, slot):
        nxt = 1 - slot
        hpa_s[nxt, 0:hist, :] = hist_ref[c + 1]
        hpb_s[nxt, 0:hist, :] = hist_ref[nj + c + 1]
        for r in range(FFN_SUBS):
            up_rows(c + 1, nxt, r)
            down_rows(c, slot, r)
        keep_history(c, slot)

    def tile_pair(t, carry):
        step(2 * t, 0)
        step(2 * t + 1, 1)
        return carry

    up(0, 0)
    lax.fori_loop(0, (nj - 1) // 2, tile_pair, 0)
    for c in range((nj - 1) // 2 * 2, nj - 1):
        step(c, c % 2)
    last_slot = (nj - 1) % 2
    for r in range(FFN_SUBS):
        down_rows(nj - 1, last_slot, r)
    keep_history(nj - 1, last_slot)

    y = x_s[...] + acc_s[...]
    if final_norm:
        y = _rms(y, gf_ref[...])
    o_ref[...] = y


def _ffn_cols(res, a, w_out, mixer_out, g, w_up, cw, cb, prev, w_down, g_final, nb, tm, final_norm):
    rows = res.shape[0]
    tn = FFN_TN
    nj = FFN_HIDDEN // tn
    hist = (FFN_CONV_WIDTH - 1) * nb
    tiles = lambda m: m.reshape(m.shape[0], 2 * nj, tn).transpose(1, 0, 2)
    once = lambda shape: pl.BlockSpec(shape, lambda i: (0,) * len(shape), pipeline_mode=pl.Buffered(1))
    k = w_out.shape[0]
    if mixer_out == "seq_major":
        a_spec = pl.BlockSpec((nb, tm // nb, k), lambda i: (0, i, 0))
        stage = [pltpu.VMEM((k // 128, tm, 128), F32)]
    else:
        a_spec = pl.BlockSpec((tm, k), lambda i: (i, 0))
        stage = []
    return pl.pallas_call(
        functools.partial(_ffn_cols_kernel, nb=nb, tm=tm, final_norm=final_norm, mixer_out=mixer_out),
        grid=(rows // tm,),
        in_specs=[pl.BlockSpec((tm, D_MODEL), lambda i: (i, 0)), a_spec, once(w_out.shape), once((1, D_MODEL)),
                  once((2 * nj, D_MODEL, tn)), once((2 * nj, FFN_CONV_WIDTH, tn)), once((2 * nj, 1, tn)),
                  once((2 * nj, hist, tn)), once((nj, tn, D_MODEL)), once((1, D_MODEL))],
        out_specs=[pl.BlockSpec((tm, D_MODEL), lambda i: (i, 0)),
                   pl.BlockSpec((2 * nj, hist, tn), lambda i: (0, 0, 0))],
        out_shape=[jax.ShapeDtypeStruct((rows, D_MODEL), F32),
                   jax.ShapeDtypeStruct((2 * nj, hist, tn), F32)],
        scratch_shapes=[pltpu.VMEM((tm, D_MODEL), F32), pltpu.VMEM((tm, D_MODEL), BF16),
                        pltpu.VMEM((tm, D_MODEL), F32),
                        pltpu.VMEM((2, hist + tm, tn), F32), pltpu.VMEM((2, hist + tm, tn), F32)] + stage,
        compiler_params=_cparams(("arbitrary",)),
        name="conv_ffn",
    )(res, a, w_out, g, tiles(w_up), tiles(cw), tiles(cb), tiles(prev), w_down.reshape(nj, tn, D_MODEL), g_final)


def _s5_disc_kernel(are_ref, aim_ref, ldt_ref, bre_ref, bim_ref, abr_ref, abi_ref, bbr_ref, bbi_ref):
    a_re, a_im = are_ref[...], aim_ref[...]
    dt = jnp.exp(ldt_ref[...])
    mag = jnp.exp(a_re * dt)
    ar = mag * jnp.cos(a_im * dt)
    ai = mag * jnp.sin(a_im * dt)
    den = a_re * a_re + a_im * a_im
    nr = ar - 1.0
    cr = (nr * a_re + ai * a_im) / den
    ci = (ai * a_re - nr * a_im) / den
    b_re, b_im = bre_ref[...], bim_ref[...]
    abr_ref[...] = ar
    abi_ref[...] = ai
    bbr_ref[...] = cr * b_re - ci * b_im
    bbi_ref[...] = cr * b_im + ci * b_re


def _s5_params(a_re, a_im, log_dt, b_re, b_im, c_re, c_im):
    rep = lambda a: jnp.repeat(a.astype(F32), S5_GROUP_CH, axis=0)
    rows_gc = lambda b: b.astype(F32).transpose(0, 2, 1).reshape(D_MODEL, S5_STATE)
    ldt = jnp.broadcast_to(log_dt.astype(F32)[:, None], (S5_GROUPS, S5_STATE))
    sds = jax.ShapeDtypeStruct((D_MODEL, S5_STATE), F32)
    abr, abi, bbr, bbi = pl.pallas_call(_s5_disc_kernel, out_shape=[sds] * 4, name="s5_discretize")(
        rep(a_re), rep(a_im), rep(ldt), rows_gc(b_re), rows_gc(b_im))
    eye = jnp.eye(S5_GROUPS // S5_KB, dtype=F32)

    def b_blocks(b):
        b = b.reshape(S5_KB, S5_GROUPS // S5_KB, S5_GROUP_CH, S5_STATE)
        return jnp.einsum('kgcp,gh->kgchp', b, eye).reshape(S5_KB, D_MODEL // S5_KB, S5_COLS // S5_KB).astype(BF16)

    def c_blocks(c):
        c = c.astype(F32).reshape(S5_KB, S5_GROUPS // S5_KB, S5_GROUP_CH, S5_STATE)
        return jnp.einsum('kgcp,gh->kgphc', c, eye).reshape(S5_KB, S5_COLS // S5_KB, D_MODEL // S5_KB).astype(BF16)

    return (b_blocks(bbr), b_blocks(bbi), c_blocks(c_re), c_blocks(c_im),
            abr[::S5_GROUP_CH].reshape(1, S5_COLS), abi[::S5_GROUP_CH].reshape(1, S5_COLS))


def _to_time_major(a):
    return a.transpose(1, 0, 2).reshape(a.shape[0] * a.shape[1], a.shape[2])


def _from_time_major(a, nb):
    return a.reshape(a.shape[0] // nb, nb, a.shape[1]).transpose(1, 0, 2)


def _trunk(x, nb, seq, s5_re, s5_im, lru_h, lru_conv, gdn_s, gdn_conv, ffn_conv, p):
    total = seq * nb
    tm = min(total, 1024)
    rows = 512
    o_s5_re, o_s5_im, o_lru, o_lru_conv, o_gdn, o_gdn_conv, o_ffn_conv = [], [], [], [], [], [], []
    depth = p['norm_mix'].shape[0]
    for i in range(depth):
        kind, j = i % 3, i // 3
        g_mix = p['norm_mix'][i].reshape(1, D_MODEL)
        if kind == 0:
            bre, bim, cre, cim, are, aim = p['s5_disc'][j]
            u = _norm_mm(x, g_mix, p['s5_w_in'][j].astype(BF16), rows)
            y, hre, him = _s5_core(u, s5_re[j].reshape(nb, S5_COLS), s5_im[j].reshape(nb, S5_COLS),
                                   bre, bim, cre, cim, are, aim, p['s5_d'][j].reshape(1, D_MODEL), nb, rows)
            mix = (y, p['s5_w_glu'][j].astype(BF16), "glu")
            o_s5_re.append(hre.reshape(nb, S5_GROUPS, S5_STATE))
            o_s5_im.append(him.reshape(nb, S5_GROUPS, S5_STATE))
        elif kind == 1:
            w_in = p['lru_w_in'][j].astype(BF16).reshape(D_MODEL, 2, LRU_BLOCKS, LRU_BLOCK)
            w_in = w_in.transpose(2, 0, 1, 3).reshape(LRU_BLOCKS, D_MODEL, 2 * LRU_BLOCK)
            y, h_new, conv_new = _lru_core(
                x, g_mix, w_in, _to_time_major(lru_conv[j]), lru_h[j],
                p['lru_conv_w'][j], p['lru_conv_b'][j].reshape(1, LRU_WIDTH),
                p['lru_w_gate_a'][j].astype(BF16), p['lru_b_gate_a'][j].reshape(1, LRU_WIDTH),
                p['lru_w_gate_x'][j].astype(BF16), p['lru_b_gate_x'][j].reshape(1, LRU_WIDTH),
                p['lru_lambda'][j].reshape(1, LRU_WIDTH), nb, rows)
            mix = (y, p['lru_w_out'][j].astype(BF16), "plain")
            o_lru.append(h_new)
            o_lru_conv.append(_from_time_major(conv_new, nb))
        else:
            w_in = p['gdn_w_in'][j]
            w_pad = jnp.pad(w_in, ((0, 0), (0, GDN_PROJ_PAD - w_in.shape[1]))).astype(BF16)
            w_pad = w_pad.reshape(D_MODEL, GDN_PROJ_PAD // 256, 256).transpose(1, 0, 2)
            pad8 = lambda a: jnp.pad(a.reshape(1, GDN_HEADS), ((0, 0), (0, 128 - GDN_HEADS)))
            chunk = GDN_CHUNK if seq >= GDN_CHUNK else 8
            batch_major = nb == 8 and seq % chunk == 0
            *qkvzgb, conv_new = _gdn_prep(x, g_mix, w_pad, _to_time_major(gdn_conv[j]), p['gdn_conv_w'][j],
                                          pad8(p['gdn_a_log'][j]), pad8(p['gdn_dt_bias'][j]), nb, rows,
                                          batch_major)
            nw = p['gdn_norm'][j].reshape(1, GDN_DV)
            w_out = p['gdn_w_out'][j].astype(BF16)
            if batch_major:
                o, s_new = _gdn_core(*qkvzgb, gdn_s[j], nw, chunk, 4)
                mix = (o, w_out, "seq_major")
            else:
                lp = -(-seq // chunk) * chunk

                def bm(a):
                    a = a.reshape(seq, nb, a.shape[1]).transpose(1, 0, 2)
                    return jnp.pad(a, ((0, 0), (0, lp - seq), (0, 0)))

                o, s_new = _gdn_core(*[bm(a) for a in qkvzgb], gdn_s[j], nw, chunk, 4)
                o = o[:, :seq].transpose(1, 0, 2).reshape(total, GDN_KEY_DIM).astype(BF16)
                mix = (o, w_out, "plain")
            o_gdn.append(s_new)
            o_gdn_conv.append(_from_time_major(conv_new, nb))
        x, ca, cb = _ffn(x, *mix, p['norm_ffn'][i].reshape(1, D_MODEL), p['ffn_w_up'][i].astype(BF16),
                         p['ffn_conv_w'][i], p['ffn_conv_b'][i].reshape(1, 2 * FFN_HIDDEN),
                         _to_time_major(ffn_conv[i]), p['ffn_w_down'][i].astype(BF16),
                         p['norm_final'].reshape(1, D_MODEL), nb, tm, i == depth - 1)
        hist = (FFN_CONV_WIDTH - 1) * nb
        o_ffn_conv.append(_from_time_major(jnp.concatenate([ca[-hist:], cb[-hist:]], axis=1), nb))
    return (x, jnp.stack(o_s5_re), jnp.stack(o_s5_im), jnp.stack(o_lru), jnp.stack(o_lru_conv),
            jnp.stack(o_gdn), jnp.stack(o_gdn_conv), jnp.stack(o_ffn_conv))


def kernel(x_prompt, x_sample, state_s5_re, state_s5_im, state_lru, state_lru_conv, state_gdn, state_gdn_conv, state_ffn_conv, norm_mix, norm_ffn, norm_final, s5_w_in, s5_a_re, s5_a_im, s5_log_dt, s5_b_re, s5_b_im, s5_c_re, s5_c_im, s5_d, s5_w_glu, lru_w_in, lru_conv_w, lru_conv_b, lru_w_gate_a, lru_b_gate_a, lru_w_gate_x, lru_b_gate_x, lru_lambda, lru_w_out, gdn_w_in, gdn_conv_w, gdn_a_log, gdn_dt_bias, gdn_norm, gdn_w_out, ffn_w_up, ffn_conv_w, ffn_conv_b, ffn_w_down):
    p = dict(norm_mix=norm_mix, norm_ffn=norm_ffn, norm_final=norm_final, s5_w_in=s5_w_in, s5_a_re=s5_a_re,
             s5_a_im=s5_a_im, s5_log_dt=s5_log_dt, s5_b_re=s5_b_re, s5_b_im=s5_b_im, s5_c_re=s5_c_re,
             s5_c_im=s5_c_im, s5_d=s5_d, s5_w_glu=s5_w_glu, lru_w_in=lru_w_in, lru_conv_w=lru_conv_w,
             lru_conv_b=lru_conv_b, lru_w_gate_a=lru_w_gate_a, lru_b_gate_a=lru_b_gate_a,
             lru_w_gate_x=lru_w_gate_x, lru_b_gate_x=lru_b_gate_x, lru_lambda=lru_lambda, lru_w_out=lru_w_out,
             gdn_w_in=gdn_w_in, gdn_conv_w=gdn_conv_w, gdn_a_log=gdn_a_log, gdn_dt_bias=gdn_dt_bias,
             gdn_norm=gdn_norm, gdn_w_out=gdn_w_out, ffn_w_up=ffn_w_up, ffn_conv_w=ffn_conv_w,
             ffn_conv_b=ffn_conv_b, ffn_w_down=ffn_w_down)
    p['s5_disc'] = [_s5_params(s5_a_re[j], s5_a_im[j], s5_log_dt[j], s5_b_re[j], s5_b_im[j], s5_c_re[j],
                               s5_c_im[j]) for j in range(s5_a_re.shape[0])]
    outs = []
    for x, states in (
            (x_prompt, None),
            (x_sample, (state_s5_re, state_s5_im, state_lru, state_lru_conv, state_gdn, state_gdn_conv,
                        state_ffn_conv))):
        nb, seq, _ = x.shape
        if states is None:
            states = tuple(jnp.zeros((s.shape[0], nb) + s.shape[2:], F32) for s in (
                state_s5_re, state_s5_im, state_lru, state_lru_conv, state_gdn, state_gdn_conv, state_ffn_conv))
        res = _trunk(_to_time_major(x), nb, seq, *states, p)
        outs.append((_from_time_major(res[0], nb),) + tuple(res[1:]))
    (y_p, *st_p), (y_s, *st_s) = outs
    return (y_p, y_s, *st_p, *st_s)
```

```python
import functools
import math

import jax
import jax.numpy as jnp
from jax import lax
from jax.experimental import pallas as pl
from jax.experimental.pallas import tpu as pltpu

F32 = jnp.float32
BF16 = jnp.bfloat16

D_MODEL = 1024
RMS_EPS = 1e-6
L2_EPS = 1e-6
S5_GROUPS = 64
S5_STATE = 64
S5_GROUP_CH = 16
S5_COLS = S5_GROUPS * S5_STATE
S5_KB = 8
S5_SCAN_LANES = 1024
LRU_WIDTH = 1280
LRU_BLOCK = 128
LRU_BLOCKS = LRU_WIDTH // LRU_BLOCK
LRU_C = 8.0
CONV_WIDTH = 4
GDN_HEADS = 8
GDN_DK = 128
GDN_DV = 128
GDN_KEY_DIM = GDN_HEADS * GDN_DK
GDN_CONV_DIM = 3 * GDN_KEY_DIM
GDN_CHUNK = 64
GDN_PROJ_PAD = 4352
FFN_HIDDEN = 2816
FFN_CONV_WIDTH = 3
FFN_TN = 256
FFN_SUBS = 2
SUB_ROWS = 256
VMEM_LIMIT_BYTES = 56 * 1024 * 1024


def _cparams(sem):
    return pltpu.CompilerParams(dimension_semantics=sem, vmem_limit_bytes=VMEM_LIMIT_BYTES)


def _rms(x, g):
    ms = jnp.mean(x * x, axis=-1, keepdims=True)
    return x * lax.rsqrt(ms + RMS_EPS) * g


def _softplus(x):
    return jnp.maximum(x, 0.0) + jnp.log1p(jnp.exp(-jnp.abs(x)))


def _gelu(x):
    c = math.sqrt(2.0 / math.pi)
    half = 0.5 * x
    return half + half * jnp.tanh(x * (c + (c * 0.044715) * (x * x)))


def _expm1(x):
    u = jnp.exp(x)
    small = jnp.abs(x) < 0.5
    usable = small & (u != 1.0)
    ratio = (u - 1.0) * x / jnp.log(jnp.where(usable, u, 2.0))
    return jnp.where(small, jnp.where(usable, ratio, x), u - 1.0)


def _dot(a, b):
    return jnp.dot(a.astype(BF16), b.astype(BF16), preferred_element_type=F32)


def _dot_nt(a, b):
    return lax.dot_general(a.astype(BF16), b.astype(BF16), (((1,), (1,)), ((), ())),
                           preferred_element_type=F32)


def _split2(a):
    hi = a.astype(BF16)
    lo = (a - hi.astype(F32)).astype(BF16)
    return hi, lo


def _split3(a):
    hi = a.astype(BF16)
    r = a - hi.astype(F32)
    mid = r.astype(BF16)
    lo = (r - mid.astype(F32)).astype(BF16)
    return hi, mid, lo


def _dot3(a, b):
    ah, al = _split2(a)
    bh, bl = _split2(b)
    d = functools.partial(jnp.dot, preferred_element_type=F32)
    return d(ah, bh) + d(al, bh) + d(ah, bl)


def _norm_mm_kernel(x_ref, g_ref, w_ref, o_ref, xn_ref, *, tm):
    sub = lambda r: slice(r * SUB_ROWS, (r + 1) * SUB_ROWS)

    def norm(r):
        xn_ref[sub(r), :] = _rms(x_ref[sub(r), :], g_ref[...]).astype(BF16)

    nsub = tm // SUB_ROWS
    norm(0)
    for r in range(nsub):
        if r + 1 < nsub:
            norm(r + 1)
        o_ref[sub(r), :] = jnp.dot(xn_ref[sub(r), :], w_ref[...], preferred_element_type=F32)


def _norm_mm(x, g, w, tm):
    rows, n = x.shape[0], w.shape[1]
    return pl.pallas_call(
        functools.partial(_norm_mm_kernel, tm=tm),
        grid=(rows // tm,),
        in_specs=[pl.BlockSpec((tm, D_MODEL), lambda i: (i, 0)),
                  pl.BlockSpec((1, D_MODEL), lambda i: (0, 0)),
                  pl.BlockSpec((D_MODEL, n), lambda i: (0, 0))],
        out_specs=pl.BlockSpec((tm, n), lambda i: (i, 0)),
        out_shape=jax.ShapeDtypeStruct((rows, n), F32),
        scratch_shapes=[pltpu.VMEM((tm, D_MODEL), BF16)],
        compiler_params=_cparams(("parallel",)),
        name="norm_mm",
    )(x, g, w)


def _s5_core_kernel(x_ref, g_ref, win_ref, h0re_ref, h0im_ref, bre_ref, bim_ref, cre_ref, cim_ref, are_ref,
                    aim_ref, d_ref, y_ref, hre_out, him_out, xn_s, u_ref, hre_s, him_s, *, nb, rows):
    i = pl.program_id(0)

    @pl.when(i == 0)
    def _():
        hre_s[0:nb, :] = h0re_ref[...]
        him_s[0:nb, :] = h0im_ref[...]

    kw = S5_COLS // S5_KB
    uw = D_MODEL // S5_KB
    per_grp = S5_SCAN_LANES // kw
    n_grp = S5_COLS // S5_SCAN_LANES
    xn_s[...] = _rms(x_ref[...], g_ref[...]).astype(BF16)

    def project_in(grp):
        ucols = slice(grp * per_grp * uw, (grp + 1) * per_grp * uw)
        u_ref[:, ucols] = jnp.dot(xn_s[...], win_ref[:, ucols], preferred_element_type=F32)
        for kb in range(grp * per_grp, (grp + 1) * per_grp):
            ukb = u_ref[:, kb * uw:(kb + 1) * uw].astype(BF16)
            hre_s[nb:nb + rows, kb * kw:(kb + 1) * kw] = jnp.dot(ukb, bre_ref[kb], preferred_element_type=F32)
            him_s[nb:nb + rows, kb * kw:(kb + 1) * kw] = jnp.dot(ukb, bim_ref[kb], preferred_element_type=F32)

    def scan(grp):
        cols = slice(grp * S5_SCAN_LANES, (grp + 1) * S5_SCAN_LANES)
        are = jnp.broadcast_to(are_ref[:, cols], (nb, S5_SCAN_LANES))
        aim = jnp.broadcast_to(aim_ref[:, cols], (nb, S5_SCAN_LANES))
        hr, hi = hre_s[0:nb, cols], him_s[0:nb, cols]
        for t in range(rows // nb):
            r = slice(nb + t * nb, 2 * nb + t * nb)
            hr, hi = (are * hr - aim * hi + hre_s[r, cols],
                      are * hi + aim * hr + him_s[r, cols])
            hre_s[r, cols] = hr
            him_s[r, cols] = hi

    def project_out(grp):
        for kb in range(grp * per_grp, (grp + 1) * per_grp):
            hr = hre_s[nb:nb + rows, kb * kw:(kb + 1) * kw].astype(BF16)
            hi = him_s[nb:nb + rows, kb * kw:(kb + 1) * kw].astype(BF16)
            yk = (jnp.dot(hr, cre_ref[kb], preferred_element_type=F32)
                  - jnp.dot(hi, cim_ref[kb], preferred_element_type=F32))
            yk = yk + d_ref[:, kb * uw:(kb + 1) * uw] * u_ref[:, kb * uw:(kb + 1) * uw]
            y_ref[:, kb * uw:(kb + 1) * uw] = _gelu(yk).astype(BF16)

    project_in(0)
    for grp in range(n_grp):
        if grp + 1 < n_grp:
            project_in(grp + 1)
        scan(grp)
        project_out(grp)

    last_re = hre_s[rows:rows + nb, :]
    last_im = him_s[rows:rows + nb, :]
    hre_s[0:nb, :] = last_re
    him_s[0:nb, :] = last_im
    hre_out[...] = last_re
    him_out[...] = last_im


def _s5_core(x, g, w_in, h0re, h0im, bre, bim, cre, cim, are, aim, d, nb, rows):
    total = x.shape[0]
    full = lambda shape: pl.BlockSpec(shape, lambda i: (0,) * len(shape))
    return pl.pallas_call(
        functools.partial(_s5_core_kernel, nb=nb, rows=rows),
        grid=(total // rows,),
        in_specs=[pl.BlockSpec((rows, D_MODEL), lambda i: (i, 0)), full((1, D_MODEL)), full(w_in.shape),
                  full((nb, S5_COLS)), full((nb, S5_COLS)),
                  full(bre.shape), full(bim.shape), full(cre.shape), full(cim.shape),
                  full((1, S5_COLS)), full((1, S5_COLS)), full((1, D_MODEL))],
        out_specs=[pl.BlockSpec((rows, D_MODEL), lambda i: (i, 0)),
                   full((nb, S5_COLS)), full((nb, S5_COLS))],
        out_shape=[jax.ShapeDtypeStruct((total, D_MODEL), BF16),
                   jax.ShapeDtypeStruct((nb, S5_COLS), F32),
                   jax.ShapeDtypeStruct((nb, S5_COLS), F32)],
        scratch_shapes=[pltpu.VMEM((rows, D_MODEL), BF16), pltpu.VMEM((rows, D_MODEL), F32),
                        pltpu.VMEM((nb + rows, S5_COLS), F32),
                        pltpu.VMEM((nb + rows, S5_COLS), F32)],
        compiler_params=_cparams(("arbitrary",)),
        name="s5_core",
    )(x, g, w_in, h0re, h0im, bre, bim, cre, cim, are, aim, d)


def _lru_core_kernel(x_ref, g_ref, win_ref, prev_ref, h0_ref, cw_ref, cb_ref, wga_ref, bga_ref, wgx_ref,
                     bgx_ref, lam_ref, y_ref, hout_ref, cout_ref, xn_s, gate_s, xp_s, h_s, a_s, *, nb, rows):
    i = pl.program_id(0)
    hist = (CONV_WIDTH - 1) * nb

    @pl.when(i == 0)
    def _():
        xp_s[0:hist, :] = prev_ref[...]
        h_s[0:nb, :] = h0_ref[...]

    xn_s[...] = _rms(x_ref[...], g_ref[...]).astype(BF16)
    c8 = -LRU_C * _softplus(-lam_ref[...])

    def project(n):
        pg = jnp.dot(xn_s[...], win_ref[n], preferred_element_type=F32)
        sl = slice(n * LRU_BLOCK, (n + 1) * LRU_BLOCK)
        gate_s[:, sl] = pg[:, :LRU_BLOCK]
        xp_s[hist:hist + rows, sl] = pg[:, LRU_BLOCK:]

    def gates(n):
        sl = slice(n * LRU_BLOCK, (n + 1) * LRU_BLOCK)
        xcn = xp_s[0:rows, sl] * cw_ref[0:1, sl]
        for k in range(1, CONV_WIDTH):
            xcn = xcn + xp_s[k * nb:k * nb + rows, sl] * cw_ref[k:k + 1, sl]
        xcn = xcn + cb_ref[:, sl]
        xcb = xcn.astype(BF16)
        r = jax.nn.sigmoid(jnp.dot(xcb, wga_ref[n], preferred_element_type=F32) + bga_ref[:, sl])
        ig = jax.nn.sigmoid(jnp.dot(xcb, wgx_ref[n], preferred_element_type=F32) + bgx_ref[:, sl])
        log_a = c8[:, sl] * r
        a_s[:, sl] = jnp.exp(log_a)
        h_s[nb:nb + rows, sl] = jnp.sqrt(-_expm1(2.0 * log_a)) * ig * xcn

    project(0)
    for n in range(LRU_BLOCKS):
        if n + 1 < LRU_BLOCKS:
            project(n + 1)
        gates(n)

    def step(t, carry):
        r0 = pl.multiple_of(t * nb, nb)
        r1 = pl.multiple_of(t * nb + nb, nb)
        h_s[pl.ds(r1, nb), :] = a_s[pl.ds(r0, nb), :] * h_s[pl.ds(r0, nb), :] + h_s[pl.ds(r1, nb), :]
        return carry

    lax.fori_loop(0, rows // nb, step, 0)

    y_ref[...] = (_gelu(gate_s[...]) * h_s[nb:nb + rows, :]).astype(BF16)
    tail = xp_s[rows:rows + hist, :]
    last = h_s[rows:rows + nb, :]
    xp_s[0:hist, :] = tail
    h_s[0:nb, :] = last
    cout_ref[...] = tail
    hout_ref[...] = last


def _lru_core(x, g, w_in, prev, h0, cw, cb, wga, bga, wgx, bgx, lam, nb, rows):
    total = x.shape[0]
    hist = (CONV_WIDTH - 1) * nb
    full = lambda shape: pl.BlockSpec(shape, lambda i: (0,) * len(shape))
    return pl.pallas_call(
        functools.partial(_lru_core_kernel, nb=nb, rows=rows),
        grid=(total // rows,),
        in_specs=[pl.BlockSpec((rows, D_MODEL), lambda i: (i, 0)), full((1, D_MODEL)),
                  pl.BlockSpec(w_in.shape, lambda i: (0, 0, 0), pipeline_mode=pl.Buffered(1)),
                  full((hist, LRU_WIDTH)), full((nb, LRU_WIDTH)),
                  full((CONV_WIDTH, LRU_WIDTH)), full((1, LRU_WIDTH)),
                  full(wga.shape), full((1, LRU_WIDTH)), full(wgx.shape), full((1, LRU_WIDTH)),
                  full((1, LRU_WIDTH))],
        out_specs=[pl.BlockSpec((rows, LRU_WIDTH), lambda i: (i, 0)),
                   full((nb, LRU_WIDTH)), full((hist, LRU_WIDTH))],
        out_shape=[jax.ShapeDtypeStruct((total, LRU_WIDTH), BF16),
                   jax.ShapeDtypeStruct((nb, LRU_WIDTH), F32),
                   jax.ShapeDtypeStruct((hist, LRU_WIDTH), F32)],
        scratch_shapes=[pltpu.VMEM((rows, D_MODEL), BF16),
                        pltpu.VMEM((rows, LRU_WIDTH), F32),
                        pltpu.VMEM((hist + rows, LRU_WIDTH), F32),
                        pltpu.VMEM((nb + rows, LRU_WIDTH), F32),
                        pltpu.VMEM((rows, LRU_WIDTH), F32)],
        compiler_params=_cparams(("arbitrary",)),
        name="lru_core",
    )(x, g, w_in, prev, h0, cw, cb, wga, bga, wgx, bgx, lam)


def _gdn_prep_kernel(x_ref, gn_ref, win_ref, prev_ref, cw_ref, alog_ref, dtb_ref,
                     q_ref, k_ref, v_ref, zo_ref, g_ref, beta_ref, cout_ref, xn_s, xp_s, ab_s, st_s,
                     *, nb, rows, batch_major):
    i = pl.program_id(0)
    hist = (CONV_WIDTH - 1) * nb
    n_qkv = 3 * GDN_HEADS
    n_z = GDN_HEADS

    @pl.when(i == 0)
    def _():
        xp_s[0:hist, :] = prev_ref[...]

    xn_s[...] = _rms(x_ref[...], gn_ref[...]).astype(BF16)

    def project(p):
        pg = jnp.dot(xn_s[...], win_ref[p], preferred_element_type=F32)
        for half in range(2):
            s = 2 * p + half
            col = pg[:, half * 128:(half + 1) * 128]
            if s < n_qkv:
                xp_s[hist:hist + rows, s * 128:(s + 1) * 128] = col
            elif s < n_qkv + n_z:
                if batch_major:
                    st_s[s] = col
                else:
                    zo_ref[:, (s - n_qkv) * 128:(s - n_qkv + 1) * 128] = col
            elif s == n_qkv + n_z:
                ab_s[...] = col

    def activate(s):
        part, h = divmod(s, GDN_HEADS)
        sl = slice(s * 128, (s + 1) * 128)
        acc = xp_s[0:rows, sl] * cw_ref[0:1, sl]
        for k in range(1, CONV_WIDTH):
            acc = acc + xp_s[k * nb:k * nb + rows, sl] * cw_ref[k:k + 1, sl]
        y = acc * jax.nn.sigmoid(acc)
        if part < 2:
            y = y * lax.rsqrt(jnp.sum(y * y, axis=-1, keepdims=True) + L2_EPS)
        if part == 0:
            y = y * (GDN_DK ** -0.5)
        if batch_major:
            st_s[s] = y
        else:
            (q_ref, k_ref, v_ref)[part][:, h * GDN_DK:(h + 1) * GDN_DK] = y

    n_pairs = win_ref.shape[0]
    project(0)
    for p in range(n_pairs):
        if p + 1 < n_pairs:
            project(p + 1)
        for s in (2 * p, 2 * p + 1):
            if s < n_qkv:
                activate(s)

    ab = ab_s[...]
    g = -jnp.exp(alog_ref[...]) * _softplus(ab + dtb_ref[...])
    beta = jax.nn.sigmoid(ab)
    if batch_major:
        st_s[4 * GDN_HEADS] = g
        st_s[4 * GDN_HEADS + 1] = beta
        steps = rows // nb
        for b in range(nb):
            pick = pl.ds(b, steps, stride=nb)
            for part, out in enumerate((q_ref, k_ref, v_ref, zo_ref)):
                for h in range(GDN_HEADS):
                    out[b, :, h * GDN_DK:(h + 1) * GDN_DK] = st_s[part * GDN_HEADS + h, pick, :]
            g_ref[b] = st_s[4 * GDN_HEADS, pick, :]
            beta_ref[b] = st_s[4 * GDN_HEADS + 1, pick, :]
    else:
        g_ref[...] = g
        beta_ref[...] = beta
    tail = xp_s[rows:rows + hist, :]
    xp_s[0:hist, :] = tail
    cout_ref[...] = tail


def _gdn_prep(x, gn, w_in, prev, cw, alog, dtb, nb, rows, batch_major):
    total = x.shape[0]
    hist = (CONV_WIDTH - 1) * nb
    full = lambda shape: pl.BlockSpec(shape, lambda i: (0,) * len(shape))
    tile = lambda n: pl.BlockSpec((rows, n), lambda i: (i, 0))
    if batch_major:
        out_tile = lambda n: pl.BlockSpec((nb, rows // nb, n), lambda i: (0, i, 0))
        out_sds = lambda n: jax.ShapeDtypeStruct((nb, total // nb, n), F32)
    else:
        out_tile = tile
        out_sds = lambda n: jax.ShapeDtypeStruct((total, n), F32)
    widths = (GDN_KEY_DIM,) * 4 + (128, 128)
    return pl.pallas_call(
        functools.partial(_gdn_prep_kernel, nb=nb, rows=rows, batch_major=batch_major),
        grid=(total // rows,),
        in_specs=[tile(D_MODEL), full((1, D_MODEL)),
                  pl.BlockSpec(w_in.shape, lambda i: (0, 0, 0), pipeline_mode=pl.Buffered(1)),
                  full((hist, GDN_CONV_DIM)), full((CONV_WIDTH, GDN_CONV_DIM)),
                  full((1, 128)), full((1, 128))],
        out_specs=[out_tile(n) for n in widths] + [full((hist, GDN_CONV_DIM))],
        out_shape=[out_sds(n) for n in widths] + [jax.ShapeDtypeStruct((hist, GDN_CONV_DIM), F32)],
        scratch_shapes=[pltpu.VMEM((rows, D_MODEL), BF16),
                        pltpu.VMEM((hist + rows, GDN_CONV_DIM), F32),
                        pltpu.VMEM((rows, 128), F32),
                        pltpu.VMEM((4 * GDN_HEADS + 2, rows if batch_major else 8, 128), F32)],
        compiler_params=_cparams(("arbitrary",)),
        name="gdn_prep",
    )(x, gn, w_in, prev, cw, alog, dtb)


def _unit_lower_inverses(ms, ri, ci, chunk):
    eye = (ri == ci).astype(F32)
    blk = (ri >> 3) == (ci >> 3)
    n1 = [jnp.where(blk, -m, 0.0) for m in ms]
    n2 = [_dot(a, a) for a in n1]
    n4 = [_dot(a, a) for a in n2]
    ts = [_dot(eye + a, eye + b) for a, b in zip(n1, n2)]
    ts = [_dot(t, eye + a) for t, a in zip(ts, n4)]
    shift = 3
    while (1 << shift) < chunk:
        pair = ((ri >> (shift + 1)) == (ci >> (shift + 1))) & ((ri >> shift) != (ci >> shift))
        left = [_dot(t, jnp.where(pair, m, 0.0)) for t, m in zip(ts, ms)]
        ts = [t - _dot(a, t) for t, a in zip(ts, left)]
        shift += 1
    return ts


def _gdn_core_kernel(q_ref, k_ref, v_ref, z_ref, g_ref, beta_ref, s0_ref, nw_ref, o_ref, s_ref, *, chunk, bb):
    c = pl.program_id(1)

    @pl.when(c == 0)
    def _():
        s_ref[...] = s0_ref[...]

    ri = lax.broadcasted_iota(jnp.int32, (chunk, chunk), 0)
    ci = lax.broadcasted_iota(jnp.int32, (chunk, chunk), 1)
    causal = ri >= ci
    strict = ri > ci
    tril = causal.astype(BF16)
    e_r = lax.broadcasted_iota(jnp.int32, (128, 128), 0)
    e_c = lax.broadcasted_iota(jnp.int32, (128, 128), 1)
    eye128 = (e_r == e_c).astype(BF16)
    dotf = functools.partial(jnp.dot, preferred_element_type=F32)
    nt = lambda a, b: lax.dot_general(a, b, (((1,), (1,)), ((), ())), preferred_element_type=F32)
    nw = nw_ref[...]

    cums, cum_ts, ecums, e_lasts, e_rests, betas = [], [], [], [], [], []
    for bi in range(bb):
        g3 = _split3(g_ref[bi])
        cum = dotf(tril, g3[0]) + dotf(tril, g3[1]) + dotf(tril, g3[2])
        c3 = _split3(cum)
        cums.append(cum)
        cum_ts.append(nt(eye128, c3[0]) + nt(eye128, c3[1]) + nt(eye128, c3[2]))
        ecums.append(jnp.exp(cum))
        g_last = cum[chunk - 1:chunk, :]
        e_lasts.append(jnp.exp(g_last))
        e_rests.append(jnp.exp(g_last - cum))
        betas.append(beta_ref[bi])

    units = [(bi, h) for bi in range(bb) for h in range(GDN_HEADS)]
    col = lambda a, h: a[:, h:h + 1]
    sl = lambda h: slice(h * GDN_DK, (h + 1) * GDN_DK)
    q = [q_ref[bi, :, sl(h)] for bi, h in units]
    k = [k_ref[bi, :, sl(h)] for bi, h in units]
    decay = [jnp.exp(jnp.where(causal, col(cums[bi], h) - cum_ts[bi][h:h + 1, :], -jnp.inf)) for bi, h in units]
    k_beta = [kk * col(betas[bi], GDN_HEADS + h) for kk, (bi, h) in zip(k, units)]
    ak = [_dot_nt(jnp.concatenate([kb, qq], axis=0), kk) for kb, qq, kk in zip(k_beta, q, k)]
    ms = [jnp.where(strict, a[:chunk] * d, 0.0) for a, d in zip(ak, decay)]
    ts = _unit_lower_inverses(ms, ri, ci, chunk)
    rhs = [jnp.concatenate([v_ref[bi, :, sl(h)] * col(betas[bi], GDN_HEADS + h), kb * col(ecums[bi], h)], axis=1)
           for kb, (bi, h) in zip(k_beta, units)]
    sol = [_dot(t, r) for t, r in zip(ts, rhs)]
    s_old = [s_ref[bi, h] for bi, h in units]
    ws = [_dot(jnp.concatenate([so[:, GDN_DV:], qq * col(ecums[bi], h)], axis=0), s)
          for so, qq, s, (bi, h) in zip(sol, q, s_old, units)]
    v_new = [so[:, :GDN_DV] - w[:chunk] for so, w in zip(sol, ws)]
    o = [w[chunk:] + _dot(a[chunk:] * d, vn) for w, a, d, vn in zip(ws, ak, decay, v_new)]
    k_dec_t = [nt(eye128, (kk * col(e_rests[bi], h)).astype(BF16)) for kk, (bi, h) in zip(k, units)]
    for (bi, h), s, kt, vn, oo in zip(units, s_old, k_dec_t, v_new, o):
        s_ref[bi, h] = s * col(e_lasts[bi], h) + _dot(kt, vn)
        on = oo * lax.rsqrt(jnp.mean(oo * oo, axis=-1, keepdims=True) + RMS_EPS) * nw
        zh = z_ref[bi, :, sl(h)]
        o_ref[bi, :, sl(h)] = on * (zh * jax.nn.sigmoid(zh))


def _gdn_core(q, k, v, z, g, beta, s0, nw, chunk, bb):
    nb, lp = q.shape[0], q.shape[1]
    seq = lambda n: pl.BlockSpec((bb, chunk, n), lambda b, c: (b, c, 0))
    st = pl.BlockSpec((bb, GDN_HEADS, GDN_DK, GDN_DV), lambda b, c: (b, 0, 0, 0))
    return pl.pallas_call(
        functools.partial(_gdn_core_kernel, chunk=chunk, bb=bb),
        grid=(nb // bb, lp // chunk),
        in_specs=[seq(GDN_KEY_DIM), seq(GDN_KEY_DIM), seq(GDN_KEY_DIM), seq(GDN_KEY_DIM),
                  seq(128), seq(128), st, pl.BlockSpec((1, GDN_DV), lambda b, c: (0, 0))],
        out_specs=[seq(GDN_KEY_DIM), st],
        out_shape=[jax.ShapeDtypeStruct((nb, lp, GDN_KEY_DIM), F32),
                   jax.ShapeDtypeStruct(s0.shape, F32)],
        compiler_params=_cparams(("parallel", "arbitrary")),
        name="gdn_core",
    )(q, k, v, z, g, beta, s0, nw)


def _ffn_kernel(r_ref, a_ref, wo_ref, g_ref, wa_ref, wb_ref, cwa_ref, cwb_ref, cba_ref, cbb_ref, pa_ref, pb_ref,
                wd_ref, gf_ref, o_ref, ca_out, cb_out, x_s, xn_s, acc_s, hpa_s, hpb_s, cara_s, carb_s, *stage,
                nb, tm, final_norm, mixer_out):
    i = pl.program_id(0)
    j = pl.program_id(1)
    hist = (FFN_CONV_WIDTH - 1) * nb
    rs = min(SUB_ROWS, tm)
    nsub = tm // rs
    sub = lambda r: slice(r * rs, (r + 1) * rs)

    @pl.when(j == 0)
    def _():
        if mixer_out == "seq_major":
            a_s, = stage
            nk = a_ref.shape[2] // 128
            for b in range(nb):
                for kt in range(nk):
                    a_s[kt, pl.ds(b, tm // nb, stride=nb), :] = a_ref[b, :, kt * 128:(kt + 1) * 128]

        def project(r):
            if mixer_out == "seq_major":
                a = jnp.concatenate([a_s[kt, sub(r), :].astype(BF16) for kt in range(nk)], axis=1)
            else:
                a = a_ref[sub(r), :]
            if mixer_out == "glu":
                half = wo_ref.shape[1] // 2
                val = jnp.dot(a, wo_ref[:, :half], preferred_element_type=F32)
                gate = jnp.dot(a, wo_ref[:, half:], preferred_element_type=F32)
                y = val * jax.nn.sigmoid(gate)
            else:
                y = jnp.dot(a, wo_ref[...], preferred_element_type=F32)
            x_s[sub(r), :] = r_ref[sub(r), :] + y

        project(0)
        for r in range(nsub):
            if r + 1 < nsub:
                project(r + 1)
            xn_s[sub(r), :] = _rms(x_s[sub(r), :], g_ref[...]).astype(BF16)
        acc_s[...] = jnp.zeros_like(acc_s)

    @pl.when(i == 0)
    def _():
        hpa_s[0:hist, :] = pa_ref[...]
        hpb_s[0:hist, :] = pb_ref[...]

    @pl.when(i > 0)
    def _():
        hpa_s[0:hist, :] = cara_s[j]
        hpb_s[0:hist, :] = carb_s[j]

    def up(r):
        xr = xn_s[r * rs:(r + 1) * rs, :]
        hpa_s[hist + r * rs:hist + (r + 1) * rs, :] = jnp.dot(xr, wa_ref[...], preferred_element_type=F32)
        hpb_s[hist + r * rs:hist + (r + 1) * rs, :] = jnp.dot(xr, wb_ref[...], preferred_element_type=F32)

    def conv(hp_s, cw_ref, cb_ref, r):
        y = hp_s[r * rs:(r + 1) * rs, :] * cw_ref[0:1, :]
        for k in range(1, FFN_CONV_WIDTH):
            y = y + hp_s[k * nb + r * rs:k * nb + (r + 1) * rs, :] * cw_ref[k:k + 1, :]
        return y + cb_ref[...]

    def down(r):
        act = (_gelu(conv(hpa_s, cwa_ref, cba_ref, r)) * conv(hpb_s, cwb_ref, cbb_ref, r)).astype(BF16)
        acc_s[r * rs:(r + 1) * rs, :] += jnp.dot(act, wd_ref[...], preferred_element_type=F32)

    up(0)
    for r in range(nsub):
        if r + 1 < nsub:
            up(r + 1)
        down(r)

    tail_a = hpa_s[tm:tm + hist, :]
    tail_b = hpb_s[tm:tm + hist, :]
    cara_s[j] = tail_a
    carb_s[j] = tail_b
    ca_out[...] = tail_a
    cb_out[...] = tail_b

    @pl.when(j == pl.num_programs(1) - 1)
    def _():
        y = x_s[...] + acc_s[...]
        if final_norm:
            y = _rms(y, gf_ref[...])
        o_ref[...] = y


def _ffn(res, a, w_out, mixer_out, g, w_up, cw, cb, prev, w_down, g_final, nb, tm, final_norm):
    rows = res.shape[0]
    tn = FFN_TN
    nj = FFN_HIDDEN // tn
    hist = (FFN_CONV_WIDTH - 1) * nb
    col_a = lambda r: pl.BlockSpec((r, tn), lambda i, j: (0, j))
    col_b = lambda r: pl.BlockSpec((r, tn), lambda i, j: (0, nj + j))
    vec = pl.BlockSpec((1, D_MODEL), lambda i, j: (0, 0))
    k = w_out.shape[0]
    if mixer_out == "seq_major":
        a_spec = pl.BlockSpec((nb, tm // nb, k), lambda i, j: (0, i, 0))
        stage = [pltpu.VMEM((k // 128, tm, 128), F32)]
    else:
        a_spec = pl.BlockSpec((tm, k), lambda i, j: (i, 0))
        stage = []
    return pl.pallas_call(
        functools.partial(_ffn_kernel, nb=nb, tm=tm, final_norm=final_norm, mixer_out=mixer_out),
        grid=(rows // tm, nj),
        in_specs=[pl.BlockSpec((tm, D_MODEL), lambda i, j: (i, 0)), a_spec,
                  pl.BlockSpec(w_out.shape, lambda i, j: (0, 0)), vec,
                  col_a(D_MODEL), col_b(D_MODEL),
                  col_a(FFN_CONV_WIDTH), col_b(FFN_CONV_WIDTH), col_a(1), col_b(1),
                  col_a(hist), col_b(hist),
                  pl.BlockSpec((tn, D_MODEL), lambda i, j: (j, 0)), vec],
        out_specs=[pl.BlockSpec((tm, D_MODEL), lambda i, j: (i, 0)),
                   pl.BlockSpec((hist, tn), lambda i, j: (i, j)),
                   pl.BlockSpec((hist, tn), lambda i, j: (i, j))],
        out_shape=[jax.ShapeDtypeStruct((rows, D_MODEL), F32),
                   jax.ShapeDtypeStruct((rows // tm * hist, FFN_HIDDEN), F32),
                   jax.ShapeDtypeStruct((rows // tm * hist, FFN_HIDDEN), F32)],
        scratch_shapes=[pltpu.VMEM((tm, D_MODEL), F32), pltpu.VMEM((tm, D_MODEL), BF16),
                        pltpu.VMEM((tm, D_MODEL), F32),
                        pltpu.VMEM((hist + tm, tn), F32), pltpu.VMEM((hist + tm, tn), F32),
                        pltpu.VMEM((nj, hist, tn), F32), pltpu.VMEM((nj, hist, tn), F32)] + stage,
        compiler_params=_cparams(("arbitrary", "arbitrary")),
        name="conv_ffn",
    )(res, a, w_out, g, w_up, w_up, cw, cw, cb, cb, prev, prev, w_down, g_final)


def _ffn_cols_kernel(r_ref, a_ref, wo_ref, g_ref, wab_ref, cw_ref, cb_ref, prev_ref, wd_ref, gf_ref,
                     o_ref, hist_ref, x_s, xn_s, acc_s, hpa_s, hpb_s, *stage, nb, tm, final_norm, mixer_out):
    i = pl.program_id(0)
    hist = (FFN_CONV_WIDTH - 1) * nb
    nj = wd_ref.shape[0]
    rs = tm // FFN_SUBS
    sub = lambda r: slice(r * rs, (r + 1) * rs)

    if mixer_out == "seq_major":
        a_s, = stage
        nk = a_ref.shape[2] // 128
        for b in range(nb):
            for kt in range(nk):
                a_s[kt, pl.ds(b, tm // nb, stride=nb), :] = a_ref[b, :, kt * 128:(kt + 1) * 128]

    def project(r):
        if mixer_out == "seq_major":
            a = jnp.concatenate([a_s[kt, sub(r), :].astype(BF16) for kt in range(nk)], axis=1)
        else:
            a = a_ref[sub(r), :]
        if mixer_out == "glu":
            half = wo_ref.shape[1] // 2
            val = jnp.dot(a, wo_ref[:, :half], preferred_element_type=F32)
            gate = jnp.dot(a, wo_ref[:, half:], preferred_element_type=F32)
            y = val * jax.nn.sigmoid(gate)
        else:
            y = jnp.dot(a, wo_ref[...], preferred_element_type=F32)
        x_s[sub(r), :] = r_ref[sub(r), :] + y

    project(0)
    for r in range(FFN_SUBS):
        if r + 1 < FFN_SUBS:
            project(r + 1)
        xn_s[sub(r), :] = _rms(x_s[sub(r), :], g_ref[...]).astype(BF16)
    acc_s[...] = jnp.zeros_like(acc_s)

    @pl.when(i == 0)
    def _():
        hist_ref[...] = prev_ref[...]

    def up(c, slot):
        hpa_s[slot, 0:hist, :] = hist_ref[c]
        hpb_s[slot, 0:hist, :] = hist_ref[nj + c]
        for r in range(FFN_SUBS):
            up_rows(c, slot, r)

    def up_rows(c, slot, r):
        xr = xn_s[sub(r), :]
        hpa_s[slot, hist + r * rs:hist + (r + 1) * rs, :] = jnp.dot(xr, wab_ref[c], preferred_element_type=F32)
        hpb_s[slot, hist + r * rs:hist + (r + 1) * rs, :] = jnp.dot(xr, wab_ref[nj + c], preferred_element_type=F32)

    def down_rows(c, slot, r):
        def conv(hp_s, t):
            cw = cw_ref[t]
            y = hp_s[slot, r * rs:(r + 1) * rs, :] * cw[0:1, :]
            for k in range(1, FFN_CONV_WIDTH):
                y = y + hp_s[slot, k * nb + r * rs:k * nb + (r + 1) * rs, :] * cw[k:k + 1, :]
            return y + cb_ref[t]

        act = (_gelu(conv(hpa_s, c)) * conv(hpb_s, nj + c)).astype(BF16)
        acc_s[sub(r), :] += jnp.dot(act, wd_ref[c], preferred_element_type=F32)

    def keep_history(c, slot):
        hist_ref[c] = hpa_s[slot, tm:tm + hist, :]
        hist_ref[nj + c] = hpb_s[slot, tm:tm + hist, :]

    def step(c, slot):
        nxt = 1 - slot
        hpa_s[nxt, 0:hist, :] = hist_ref[c + 1]
        hpb_s[nxt, 0:hist, :] = hist_ref[nj + c + 1]
        for r in range(FFN_SUBS):
            up_rows(c + 1, nxt, r)
            down_rows(c, slot, r)
        keep_history(c, slot)

    def tile_pair(t, carry):
        step(2 * t, 0)
        step(2 * t + 1, 1)
        return carry

    up(0, 0)
    lax.fori_loop(0, (nj - 1) // 2, tile_pair, 0)
    for c in range((nj - 1) // 2 * 2, nj - 1):
        step(c, c % 2)
    last_slot = (nj - 1) % 2
    for r in range(FFN_SUBS):
        down_rows(nj - 1, last_slot, r)
    keep_history(nj - 1, last_slot)

    y = x_s[...] + acc_s[...]
    if final_norm:
        y = _rms(y, gf_ref[...])
    o_ref[...] = y


def _ffn_cols(res, a, w_out, mixer_out, g, w_up, cw, cb, prev, w_down, g_final, nb, tm, final_norm):
    rows = res.shape[0]
    tn = FFN_TN
    nj = FFN_HIDDEN // tn
    hist = (FFN_CONV_WIDTH - 1) * nb
    tiles = lambda m: m.reshape(m.shape[0], 2 * nj, tn).transpose(1, 0, 2)
    once = lambda shape: pl.BlockSpec(shape, lambda i: (0,) * len(shape), pipeline_mode=pl.Buffered(1))
    k = w_out.shape[0]
    if mixer_out == "seq_major":
        a_spec = pl.BlockSpec((nb, tm // nb, k), lambda i: (0, i, 0))
        stage = [pltpu.VMEM((k // 128, tm, 128), F32)]
    else:
        a_spec = pl.BlockSpec((tm, k), lambda i: (i, 0))
        stage = []
    return pl.pallas_call(
        functools.partial(_ffn_cols_kernel, nb=nb, tm=tm, final_norm=final_norm, mixer_out=mixer_out),
        grid=(rows // tm,),
        in_specs=[pl.BlockSpec((tm, D_MODEL), lambda i: (i, 0)), a_spec, once(w_out.shape), once((1, D_MODEL)),
                  once((2 * nj, D_MODEL, tn)), once((2 * nj, FFN_CONV_WIDTH, tn)), once((2 * nj, 1, tn)),
                  once((2 * nj, hist, tn)), once((nj, tn, D_MODEL)), once((1, D_MODEL))],
        out_specs=[pl.BlockSpec((tm, D_MODEL), lambda i: (i, 0)),
                   pl.BlockSpec((2 * nj, hist, tn), lambda i: (0, 0, 0))],
        out_shape=[jax.ShapeDtypeStruct((rows, D_MODEL), F32),
                   jax.ShapeDtypeStruct((2 * nj, hist, tn), F32)],
        scratch_shapes=[pltpu.VMEM((tm, D_MODEL), F32), pltpu.VMEM((tm, D_MODEL), BF16),
                        pltpu.VMEM((tm, D_MODEL), F32),
                        pltpu.VMEM((2, hist + tm, tn), F32), pltpu.VMEM((2, hist + tm, tn), F32)] + stage,
        compiler_params=_cparams(("arbitrary",)),
        name="conv_ffn",
    )(res, a, w_out, g, tiles(w_up), tiles(cw), tiles(cb), tiles(prev), w_down.reshape(nj, tn, D_MODEL), g_final)


def _s5_disc_kernel(are_ref, aim_ref, ldt_ref, bre_ref, bim_ref, abr_ref, abi_ref, bbr_ref, bbi_ref):
    a_re, a_im = are_ref[...], aim_ref[...]
    dt = jnp.exp(ldt_ref[...])
    mag = jnp.exp(a_re * dt)
    ar = mag * jnp.cos(a_im * dt)
    ai = mag * jnp.sin(a_im * dt)
    den = a_re * a_re + a_im * a_im
    nr = ar - 1.0
    cr = (nr * a_re + ai * a_im) / den
    ci = (ai * a_re - nr * a_im) / den
    b_re, b_im = bre_ref[...], bim_ref[...]
    abr_ref[...] = ar
    abi_ref[...] = ai
    bbr_ref[...] = cr * b_re - ci * b_im
    bbi_ref[...] = cr * b_im + ci * b_re


def _s5_params(a_re, a_im, log_dt, b_re, b_im, c_re, c_im):
    rep = lambda a: jnp.repeat(a.astype(F32), S5_GROUP_CH, axis=0)
    rows_gc = lambda b: b.astype(F32).transpose(0, 2, 1).reshape(D_MODEL, S5_STATE)
    ldt = jnp.broadcast_to(log_dt.astype(F32)[:, None], (S5_GROUPS, S5_STATE))
    sds = jax.ShapeDtypeStruct((D_MODEL, S5_STATE), F32)
    abr, abi, bbr, bbi = pl.pallas_call(_s5_disc_kernel, out_shape=[sds] * 4, name="s5_discretize")(
        rep(a_re), rep(a_im), rep(ldt), rows_gc(b_re), rows_gc(b_im))
    eye = jnp.eye(S5_GROUPS // S5_KB, dtype=F32)

    def b_blocks(b):
        b = b.reshape(S5_KB, S5_GROUPS // S5_KB, S5_GROUP_CH, S5_STATE)
        return jnp.einsum('kgcp,gh->kgchp', b, eye).reshape(S5_KB, D_MODEL // S5_KB, S5_COLS // S5_KB).astype(BF16)

    def c_blocks(c):
        c = c.astype(F32).reshape(S5_KB, S5_GROUPS // S5_KB, S5_GROUP_CH, S5_STATE)
        return jnp.einsum('kgcp,gh->kgphc', c, eye).reshape(S5_KB, S5_COLS // S5_KB, D_MODEL // S5_KB).astype(BF16)

    return (b_blocks(bbr), b_blocks(bbi), c_blocks(c_re), c_blocks(c_im),
            abr[::S5_GROUP_CH].reshape(1, S5_COLS), abi[::S5_GROUP_CH].reshape(1, S5_COLS))


def _to_time_major(a):
    return a.transpose(1, 0, 2).reshape(a.shape[0] * a.shape[1], a.shape[2])


def _from_time_major(a, nb):
    return a.reshape(a.shape[0] // nb, nb, a.shape[1]).transpose(1, 0, 2)


def _trunk(x, nb, seq, s5_re, s5_im, lru_h, lru_conv, gdn_s, gdn_conv, ffn_conv, p):
    total = seq * nb
    tm = min(total, 1024)
    rows = 512
    o_s5_re, o_s5_im, o_lru, o_lru_conv, o_gdn, o_gdn_conv, o_ffn_conv = [], [], [], [], [], [], []
    depth = p['norm_mix'].shape[0]
    for i in range(depth):
        kind, j = i % 3, i // 3
        g_mix = p['norm_mix'][i].reshape(1, D_MODEL)
        if kind == 0:
            bre, bim, cre, cim, are, aim = p['s5_disc'][j]
            y, hre, him = _s5_core(x, g_mix, p['s5_w_in'][j].astype(BF16),
                                   s5_re[j].reshape(nb, S5_COLS), s5_im[j].reshape(nb, S5_COLS),
                                   bre, bim, cre, cim, are, aim, p['s5_d'][j].reshape(1, D_MODEL), nb, rows)
            mix = (y, p['s5_w_glu'][j].astype(BF16), "glu")
            o_s5_re.append(hre.reshape(nb, S5_GROUPS, S5_STATE))
            o_s5_im.append(him.reshape(nb, S5_GROUPS, S5_STATE))
        elif kind == 1:
            w_in = p['lru_w_in'][j].astype(BF16).reshape(D_MODEL, 2, LRU_BLOCKS, LRU_BLOCK)
            w_in = w_in.transpose(2, 0, 1, 3).reshape(LRU_BLOCKS, D_MODEL, 2 * LRU_BLOCK)
            y, h_new, conv_new = _lru_core(
                x, g_mix, w_in, _to_time_major(lru_conv[j]), lru_h[j],
                p['lru_conv_w'][j], p['lru_conv_b'][j].reshape(1, LRU_WIDTH),
                p['lru_w_gate_a'][j].astype(BF16), p['lru_b_gate_a'][j].reshape(1, LRU_WIDTH),
                p['lru_w_gate_x'][j].astype(BF16), p['lru_b_gate_x'][j].reshape(1, LRU_WIDTH),
                p['lru_lambda'][j].reshape(1, LRU_WIDTH), nb, rows)
            mix = (y, p['lru_w_out'][j].astype(BF16), "plain")
            o_lru.append(h_new)
            o_lru_conv.append(_from_time_major(conv_new, nb))
        else:
            w_in = p['gdn_w_in'][j]
            w_pad = jnp.pad(w_in, ((0, 0), (0, GDN_PROJ_PAD - w_in.shape[1]))).astype(BF16)
            w_pad = w_pad.reshape(D_MODEL, GDN_PROJ_PAD // 256, 256).transpose(1, 0, 2)
            pad8 = lambda a: jnp.pad(a.reshape(1, GDN_HEADS), ((0, 0), (0, 128 - GDN_HEADS)))
            chunk = GDN_CHUNK if seq >= GDN_CHUNK else 8
            batch_major = nb == 8 and seq % chunk == 0
            *qkvzgb, conv_new = _gdn_prep(x, g_mix, w_pad, _to_time_major(gdn_conv[j]), p['gdn_conv_w'][j],
                                          pad8(p['gdn_a_log'][j]), pad8(p['gdn_dt_bias'][j]), nb, rows,
                                          batch_major)
            nw = p['gdn_norm'][j].reshape(1, GDN_DV)
            w_out = p['gdn_w_out'][j].astype(BF16)
            if batch_major:
                o, s_new = _gdn_core(*qkvzgb, gdn_s[j], nw, chunk, 4)
                mix = (o, w_out, "seq_major")
            else:
                lp = -(-seq // chunk) * chunk

                def bm(a):
                    a = a.reshape(seq, nb, a.shape[1]).transpose(1, 0, 2)
                    return jnp.pad(a, ((0, 0), (0, lp - seq), (0, 0)))

                o, s_new = _gdn_core(*[bm(a) for a in qkvzgb], gdn_s[j], nw, chunk, 8)
                o = o[:, :seq].transpose(1, 0, 2).reshape(total, GDN_KEY_DIM).astype(BF16)
                mix = (o, w_out, "plain")
            o_gdn.append(s_new)
            o_gdn_conv.append(_from_time_major(conv_new, nb))
        x, ca, cb = _ffn(x, *mix, p['norm_ffn'][i].reshape(1, D_MODEL), p['ffn_w_up'][i].astype(BF16),
                         p['ffn_conv_w'][i], p['ffn_conv_b'][i].reshape(1, 2 * FFN_HIDDEN),
                         _to_time_major(ffn_conv[i]), p['ffn_w_down'][i].astype(BF16),
                         p['norm_final'].reshape(1, D_MODEL), nb, tm, i == depth - 1)
        hist = (FFN_CONV_WIDTH - 1) * nb
        o_ffn_conv.append(_from_time_major(jnp.concatenate([ca[-hist:], cb[-hist:]], axis=1), nb))
    return (x, jnp.stack(o_s5_re), jnp.stack(o_s5_im), jnp.stack(o_lru), jnp.stack(o_lru_conv),
            jnp.stack(o_gdn), jnp.stack(o_gdn_conv), jnp.stack(o_ffn_conv))


def kernel(x_prompt, x_sample, state_s5_re, state_s5_im, state_lru, state_lru_conv, state_gdn, state_gdn_conv, state_ffn_conv, norm_mix, norm_ffn, norm_final, s5_w_in, s5_a_re, s5_a_im, s5_log_dt, s5_b_re, s5_b_im, s5_c_re, s5_c_im, s5_d, s5_w_glu, lru_w_in, lru_conv_w, lru_conv_b, lru_w_gate_a, lru_b_gate_a, lru_w_gate_x, lru_b_gate_x, lru_lambda, lru_w_out, gdn_w_in, gdn_conv_w, gdn_a_log, gdn_dt_bias, gdn_norm, gdn_w_out, ffn_w_up, ffn_conv_w, ffn_conv_b, ffn_w_down):
    p = dict(norm_mix=norm_mix, norm_ffn=norm_ffn, norm_final=norm_final, s5_w_in=s5_w_in, s5_a_re=s5_a_re,
             s5_a_im=s5_a_im, s5_log_dt=s5_log_dt, s5_b_re=s5_b_re, s5_b_im=s5_b_im, s5_c_re=s5_c_re,
             s5_c_im=s5_c_im, s5_d=s5_d, s5_w_glu=s5_w_glu, lru_w_in=lru_w_in, lru_conv_w=lru_conv_w,
             lru_conv_b=lru_conv_b, lru_w_gate_a=lru_w_gate_a, lru_b_gate_a=lru_b_gate_a,
             lru_w_gate_x=lru_w_gate_x, lru_b_gate_x=lru_b_gate_x, lru_lambda=lru_lambda, lru_w_out=lru_w_out,
             gdn_w_in=gdn_w_in, gdn_conv_w=gdn_conv_w, gdn_a_log=gdn_a_log, gdn_dt_bias=gdn_dt_bias,
             gdn_norm=gdn_norm, gdn_w_out=gdn_w_out, ffn_w_up=ffn_w_up, ffn_conv_w=ffn_conv_w,
             ffn_conv_b=ffn_conv_b, ffn_w_down=ffn_w_down)
    p['s5_disc'] = [_s5_params(s5_a_re[j], s5_a_im[j], s5_log_dt[j], s5_b_re[j], s5_b_im[j], s5_c_re[j],
                               s5_c_im[j]) for j in range(s5_a_re.shape[0])]
    outs = []
    for x, states in (
            (x_prompt, None),
            (x_sample, (state_s5_re, state_s5_im, state_lru, state_lru_conv, state_gdn, state_gdn_conv,
                        state_ffn_conv))):
        nb, seq, _ = x.shape
        if states is None:
            states = tuple(jnp.zeros((s.shape[0], nb) + s.shape[2:], F32) for s in (
                state_s5_re, state_s5_im, state_lru, state_lru_conv, state_gdn, state_gdn_conv, state_ffn_conv))
        res = _trunk(_to_time_major(x), nb, seq, *states, p)
        outs.append((_from_time_major(res[0], nb),) + tuple(res[1:]))
    (y_p, *st_p), (y_s, *st_s) = outs
    return (y_p, y_s, *st_p, *st_s)
```

```python
import functools
import math

import jax
import jax.numpy as jnp
from jax import lax
from jax.experimental import pallas as pl
from jax.experimental.pallas import tpu as pltpu

F32 = jnp.float32
BF16 = jnp.bfloat16

D_MODEL = 1024
RMS_EPS = 1e-6
L2_EPS = 1e-6
S5_GROUPS = 64
S5_STATE = 64
S5_GROUP_CH = 16
S5_COLS = S5_GROUPS * S5_STATE
S5_KB = 8
S5_SCAN_LANES = 1024
LRU_WIDTH = 1280
LRU_BLOCK = 128
LRU_BLOCKS = LRU_WIDTH // LRU_BLOCK
LRU_C = 8.0
CONV_WIDTH = 4
GDN_HEADS = 8
GDN_DK = 128
GDN_DV = 128
GDN_KEY_DIM = GDN_HEADS * GDN_DK
GDN_CONV_DIM = 3 * GDN_KEY_DIM
GDN_CHUNK = 64
GDN_PROJ_PAD = 4352
FFN_HIDDEN = 2816
FFN_CONV_WIDTH = 3
FFN_TN = 256
FFN_SUBS = 2
SUB_ROWS = 256
VMEM_LIMIT_BYTES = 56 * 1024 * 1024


def _cparams(sem):
    return pltpu.CompilerParams(dimension_semantics=sem, vmem_limit_bytes=VMEM_LIMIT_BYTES)


def _rms(x, g):
    ms = jnp.mean(x * x, axis=-1, keepdims=True)
    return x * lax.rsqrt(ms + RMS_EPS) * g


def _softplus(x):
    return jnp.maximum(x, 0.0) + jnp.log1p(jnp.exp(-jnp.abs(x)))


def _stage_time_major(src_ref, slab_ref, nb):
    steps = src_ref.shape[1]
    for b in range(nb):
        for kt in range(src_ref.shape[2] // 128):
            slab_ref[kt, pl.ds(b, steps, stride=nb), :] = src_ref[b, :, kt * 128:(kt + 1) * 128]


def _gelu(x):
    c = math.sqrt(2.0 / math.pi)
    half = 0.5 * x
    return half + half * jnp.tanh(x * (c + (c * 0.044715) * (x * x)))


def _expm1(x):
    u = jnp.exp(x)
    small = jnp.abs(x) < 0.5
    usable = small & (u != 1.0)
    ratio = (u - 1.0) * x / jnp.log(jnp.where(usable, u, 2.0))
    return jnp.where(small, jnp.where(usable, ratio, x), u - 1.0)


def _dot(a, b):
    return jnp.dot(a.astype(BF16), b.astype(BF16), preferred_element_type=F32)


def _dot_nt(a, b):
    return lax.dot_general(a.astype(BF16), b.astype(BF16), (((1,), (1,)), ((), ())),
                           preferred_element_type=F32)


def _split2(a):
    hi = a.astype(BF16)
    lo = (a - hi.astype(F32)).astype(BF16)
    return hi, lo


def _split3(a):
    hi = a.astype(BF16)
    r = a - hi.astype(F32)
    mid = r.astype(BF16)
    lo = (r - mid.astype(F32)).astype(BF16)
    return hi, mid, lo


def _dot3(a, b):
    ah, al = _split2(a)
    bh, bl = _split2(b)
    d = functools.partial(jnp.dot, preferred_element_type=F32)
    return d(ah, bh) + d(al, bh) + d(ah, bl)


def _norm_mm_kernel(x_ref, g_ref, w_ref, o_ref, xn_ref, *, tm):
    sub = lambda r: slice(r * SUB_ROWS, (r + 1) * SUB_ROWS)

    def norm(r):
        xn_ref[sub(r), :] = _rms(x_ref[sub(r), :], g_ref[...]).astype(BF16)

    nsub = tm // SUB_ROWS
    norm(0)
    for r in range(nsub):
        if r + 1 < nsub:
            norm(r + 1)
        o_ref[sub(r), :] = jnp.dot(xn_ref[sub(r), :], w_ref[...], preferred_element_type=F32)


def _norm_mm(x, g, w, tm):
    rows, n = x.shape[0], w.shape[1]
    return pl.pallas_call(
        functools.partial(_norm_mm_kernel, tm=tm),
        grid=(rows // tm,),
        in_specs=[pl.BlockSpec((tm, D_MODEL), lambda i: (i, 0)),
                  pl.BlockSpec((1, D_MODEL), lambda i: (0, 0)),
                  pl.BlockSpec((D_MODEL, n), lambda i: (0, 0))],
        out_specs=pl.BlockSpec((tm, n), lambda i: (i, 0)),
        out_shape=jax.ShapeDtypeStruct((rows, n), F32),
        scratch_shapes=[pltpu.VMEM((tm, D_MODEL), BF16)],
        compiler_params=_cparams(("parallel",)),
        name="norm_mm",
    )(x, g, w)


def _s5_core_kernel(x_ref, g_ref, win_ref, h0re_ref, h0im_ref, bre_ref, bim_ref, cre_ref, cim_ref, are_ref,
                    aim_ref, d_ref, y_ref, hre_out, him_out, xn_s, u_ref, hre_s, him_s, *stage,
                    nb, rows, seq_major_in):
    i = pl.program_id(0)

    @pl.when(i == 0)
    def _():
        hre_s[0:nb, :] = h0re_ref[...]
        him_s[0:nb, :] = h0im_ref[...]

    kw = S5_COLS // S5_KB
    uw = D_MODEL // S5_KB
    per_grp = S5_SCAN_LANES // kw
    n_grp = S5_COLS // S5_SCAN_LANES
    if seq_major_in:
        x_st, = stage
        _stage_time_major(x_ref, x_st, nb)
        x = jnp.concatenate([x_st[kt] for kt in range(D_MODEL // 128)], axis=1)
    else:
        x = x_ref[...]
    xn_s[...] = _rms(x, g_ref[...]).astype(BF16)

    def project_in(grp):
        ucols = slice(grp * per_grp * uw, (grp + 1) * per_grp * uw)
        u_ref[:, ucols] = jnp.dot(xn_s[...], win_ref[:, ucols], preferred_element_type=F32)
        for kb in range(grp * per_grp, (grp + 1) * per_grp):
            ukb = u_ref[:, kb * uw:(kb + 1) * uw].astype(BF16)
            hre_s[nb:nb + rows, kb * kw:(kb + 1) * kw] = jnp.dot(ukb, bre_ref[kb], preferred_element_type=F32)
            him_s[nb:nb + rows, kb * kw:(kb + 1) * kw] = jnp.dot(ukb, bim_ref[kb], preferred_element_type=F32)

    def scan(grp):
        cols = slice(grp * S5_SCAN_LANES, (grp + 1) * S5_SCAN_LANES)
        are = jnp.broadcast_to(are_ref[:, cols], (nb, S5_SCAN_LANES))
        aim = jnp.broadcast_to(aim_ref[:, cols], (nb, S5_SCAN_LANES))
        hr, hi = hre_s[0:nb, cols], him_s[0:nb, cols]
        for t in range(rows // nb):
            r = slice(nb + t * nb, 2 * nb + t * nb)
            hr, hi = (are * hr - aim * hi + hre_s[r, cols],
                      are * hi + aim * hr + him_s[r, cols])
            hre_s[r, cols] = hr
            him_s[r, cols] = hi

    def project_out(grp):
        for kb in range(grp * per_grp, (grp + 1) * per_grp):
            hr = hre_s[nb:nb + rows, kb * kw:(kb + 1) * kw].astype(BF16)
            hi = him_s[nb:nb + rows, kb * kw:(kb + 1) * kw].astype(BF16)
            yk = (jnp.dot(hr, cre_ref[kb], preferred_element_type=F32)
                  - jnp.dot(hi, cim_ref[kb], preferred_element_type=F32))
            yk = yk + d_ref[:, kb * uw:(kb + 1) * uw] * u_ref[:, kb * uw:(kb + 1) * uw]
            y_ref[:, kb * uw:(kb + 1) * uw] = _gelu(yk).astype(BF16)

    project_in(0)
    for grp in range(n_grp):
        if grp + 1 < n_grp:
            project_in(grp + 1)
        scan(grp)
        project_out(grp)

    last_re = hre_s[rows:rows + nb, :]
    last_im = him_s[rows:rows + nb, :]
    hre_s[0:nb, :] = last_re
    him_s[0:nb, :] = last_im
    hre_out[...] = last_re
    him_out[...] = last_im


def _s5_core(x, g, w_in, h0re, h0im, bre, bim, cre, cim, are, aim, d, nb, rows, seq_major_in=False):
    full = lambda shape: pl.BlockSpec(shape, lambda i: (0,) * len(shape))
    if seq_major_in:
        total = x.shape[0] * x.shape[1]
        x_spec = pl.BlockSpec((nb, rows // nb, D_MODEL), lambda i: (0, i, 0))
        stage = [pltpu.VMEM((D_MODEL // 128, rows, 128), F32)]
    else:
        total = x.shape[0]
        x_spec = pl.BlockSpec((rows, D_MODEL), lambda i: (i, 0))
        stage = []
    return pl.pallas_call(
        functools.partial(_s5_core_kernel, nb=nb, rows=rows, seq_major_in=seq_major_in),
        grid=(total // rows,),
        in_specs=[x_spec, full((1, D_MODEL)), full(w_in.shape),
                  full((nb, S5_COLS)), full((nb, S5_COLS)),
                  full(bre.shape), full(bim.shape), full(cre.shape), full(cim.shape),
                  full((1, S5_COLS)), full((1, S5_COLS)), full((1, D_MODEL))],
        out_specs=[pl.BlockSpec((rows, D_MODEL), lambda i: (i, 0)),
                   full((nb, S5_COLS)), full((nb, S5_COLS))],
        out_shape=[jax.ShapeDtypeStruct((total, D_MODEL), BF16),
                   jax.ShapeDtypeStruct((nb, S5_COLS), F32),
                   jax.ShapeDtypeStruct((nb, S5_COLS), F32)],
        scratch_shapes=[pltpu.VMEM((rows, D_MODEL), BF16), pltpu.VMEM((rows, D_MODEL), F32),
                        pltpu.VMEM((nb + rows, S5_COLS), F32),
                        pltpu.VMEM((nb + rows, S5_COLS), F32)] + stage,
        compiler_params=_cparams(("arbitrary",)),
        name="s5_core",
    )(x, g, w_in, h0re, h0im, bre, bim, cre, cim, are, aim, d)


def _lru_core_kernel(x_ref, g_ref, win_ref, prev_ref, h0_ref, cw_ref, cb_ref, wga_ref, bga_ref, wgx_ref,
                     bgx_ref, lam_ref, y_ref, hout_ref, cout_ref, xn_s, gate_s, xp_s, h_s, a_s, *, nb, rows):
    i = pl.program_id(0)
    hist = (CONV_WIDTH - 1) * nb

    @pl.when(i == 0)
    def _():
        xp_s[0:hist, :] = prev_ref[...]
        h_s[0:nb, :] = h0_ref[...]

    xn_s[...] = _rms(x_ref[...], g_ref[...]).astype(BF16)
    c8 = -LRU_C * _softplus(-lam_ref[...])

    def project(n):
        pg = jnp.dot(xn_s[...], win_ref[n], preferred_element_type=F32)
        sl = slice(n * LRU_BLOCK, (n + 1) * LRU_BLOCK)
        gate_s[:, sl] = pg[:, :LRU_BLOCK]
        xp_s[hist:hist + rows, sl] = pg[:, LRU_BLOCK:]

    def gates(n):
        sl = slice(n * LRU_BLOCK, (n + 1) * LRU_BLOCK)
        xcn = xp_s[0:rows, sl] * cw_ref[0:1, sl]
        for k in range(1, CONV_WIDTH):
            xcn = xcn + xp_s[k * nb:k * nb + rows, sl] * cw_ref[k:k + 1, sl]
        xcn = xcn + cb_ref[:, sl]
        xcb = xcn.astype(BF16)
        r = jax.nn.sigmoid(jnp.dot(xcb, wga_ref[n], preferred_element_type=F32) + bga_ref[:, sl])
        ig = jax.nn.sigmoid(jnp.dot(xcb, wgx_ref[n], preferred_element_type=F32) + bgx_ref[:, sl])
        log_a = c8[:, sl] * r
        a_s[:, sl] = jnp.exp(log_a)
        h_s[nb:nb + rows, sl] = jnp.sqrt(-_expm1(2.0 * log_a)) * ig * xcn

    project(0)
    for n in range(LRU_BLOCKS):
        if n + 1 < LRU_BLOCKS:
            project(n + 1)
        gates(n)

    def step(t, carry):
        r0 = pl.multiple_of(t * nb, nb)
        r1 = pl.multiple_of(t * nb + nb, nb)
        h_s[pl.ds(r1, nb), :] = a_s[pl.ds(r0, nb), :] * h_s[pl.ds(r0, nb), :] + h_s[pl.ds(r1, nb), :]
        return carry

    lax.fori_loop(0, rows // nb, step, 0)

    y_ref[...] = (_gelu(gate_s[...]) * h_s[nb:nb + rows, :]).astype(BF16)
    tail = xp_s[rows:rows + hist, :]
    last = h_s[rows:rows + nb, :]
    xp_s[0:hist, :] = tail
    h_s[0:nb, :] = last
    cout_ref[...] = tail
    hout_ref[...] = last


def _lru_core(x, g, w_in, prev, h0, cw, cb, wga, bga, wgx, bgx, lam, nb, rows):
    total = x.shape[0]
    hist = (CONV_WIDTH - 1) * nb
    full = lambda shape: pl.BlockSpec(shape, lambda i: (0,) * len(shape))
    return pl.pallas_call(
        functools.partial(_lru_core_kernel, nb=nb, rows=rows),
        grid=(total // rows,),
        in_specs=[pl.BlockSpec((rows, D_MODEL), lambda i: (i, 0)), full((1, D_MODEL)),
                  pl.BlockSpec(w_in.shape, lambda i: (0, 0, 0), pipeline_mode=pl.Buffered(1)),
                  full((hist, LRU_WIDTH)), full((nb, LRU_WIDTH)),
                  full((CONV_WIDTH, LRU_WIDTH)), full((1, LRU_WIDTH)),
                  full(wga.shape), full((1, LRU_WIDTH)), full(wgx.shape), full((1, LRU_WIDTH)),
                  full((1, LRU_WIDTH))],
        out_specs=[pl.BlockSpec((rows, LRU_WIDTH), lambda i: (i, 0)),
                   full((nb, LRU_WIDTH)), full((hist, LRU_WIDTH))],
        out_shape=[jax.ShapeDtypeStruct((total, LRU_WIDTH), BF16),
                   jax.ShapeDtypeStruct((nb, LRU_WIDTH), F32),
                   jax.ShapeDtypeStruct((hist, LRU_WIDTH), F32)],
        scratch_shapes=[pltpu.VMEM((rows, D_MODEL), BF16),
                        pltpu.VMEM((rows, LRU_WIDTH), F32),
                        pltpu.VMEM((hist + rows, LRU_WIDTH), F32),
                        pltpu.VMEM((nb + rows, LRU_WIDTH), F32),
                        pltpu.VMEM((rows, LRU_WIDTH), F32)],
        compiler_params=_cparams(("arbitrary",)),
        name="lru_core",
    )(x, g, w_in, prev, h0, cw, cb, wga, bga, wgx, bgx, lam)


def _gdn_prep_kernel(x_ref, gn_ref, win_ref, prev_ref, cw_ref, alog_ref, dtb_ref,
                     q_ref, k_ref, v_ref, zo_ref, g_ref, beta_ref, cout_ref, xn_s, xp_s, ab_s, st_s,
                     *, nb, rows, batch_major):
    i = pl.program_id(0)
    hist = (CONV_WIDTH - 1) * nb
    n_qkv = 3 * GDN_HEADS
    n_z = GDN_HEADS

    @pl.when(i == 0)
    def _():
        xp_s[0:hist, :] = prev_ref[...]

    xn_s[...] = _rms(x_ref[...], gn_ref[...]).astype(BF16)

    def project(p):
        pg = jnp.dot(xn_s[...], win_ref[p], preferred_element_type=F32)
        for half in range(2):
            s = 2 * p + half
            col = pg[:, half * 128:(half + 1) * 128]
            if s < n_qkv:
                xp_s[hist:hist + rows, s * 128:(s + 1) * 128] = col
            elif s < n_qkv + n_z:
                if batch_major:
                    st_s[s] = col
                else:
                    zo_ref[:, (s - n_qkv) * 128:(s - n_qkv + 1) * 128] = col
            elif s == n_qkv + n_z:
                ab_s[...] = col

    def activate(s):
        part, h = divmod(s, GDN_HEADS)
        sl = slice(s * 128, (s + 1) * 128)
        acc = xp_s[0:rows, sl] * cw_ref[0:1, sl]
        for k in range(1, CONV_WIDTH):
            acc = acc + xp_s[k * nb:k * nb + rows, sl] * cw_ref[k:k + 1, sl]
        y = acc * jax.nn.sigmoid(acc)
        if part < 2:
            y = y * lax.rsqrt(jnp.sum(y * y, axis=-1, keepdims=True) + L2_EPS)
        if part == 0:
            y = y * (GDN_DK ** -0.5)
        if batch_major:
            st_s[s] = y
        else:
            (q_ref, k_ref, v_ref)[part][:, h * GDN_DK:(h + 1) * GDN_DK] = y

    n_pairs = win_ref.shape[0]
    project(0)
    for p in range(n_pairs):
        if p + 1 < n_pairs:
            project(p + 1)
        for s in (2 * p, 2 * p + 1):
            if s < n_qkv:
                activate(s)

    ab = ab_s[...]
    g = -jnp.exp(alog_ref[...]) * _softplus(ab + dtb_ref[...])
    beta = jax.nn.sigmoid(ab)
    if batch_major:
        st_s[4 * GDN_HEADS] = g
        st_s[4 * GDN_HEADS + 1] = beta
        steps = rows // nb
        for b in range(nb):
            pick = pl.ds(b, steps, stride=nb)
            for part, out in enumerate((q_ref, k_ref, v_ref, zo_ref)):
                for h in range(GDN_HEADS):
                    out[b, :, h * GDN_DK:(h + 1) * GDN_DK] = st_s[part * GDN_HEADS + h, pick, :]
            g_ref[b] = st_s[4 * GDN_HEADS, pick, :]
            beta_ref[b] = st_s[4 * GDN_HEADS + 1, pick, :]
    else:
        g_ref[...] = g
        beta_ref[...] = beta
    tail = xp_s[rows:rows + hist, :]
    xp_s[0:hist, :] = tail
    cout_ref[...] = tail


def _gdn_prep(x, gn, w_in, prev, cw, alog, dtb, nb, rows, batch_major):
    total = x.shape[0]
    hist = (CONV_WIDTH - 1) * nb
    full = lambda shape: pl.BlockSpec(shape, lambda i: (0,) * len(shape))
    tile = lambda n: pl.BlockSpec((rows, n), lambda i: (i, 0))
    if batch_major:
        out_tile = lambda n: pl.BlockSpec((nb, rows // nb, n), lambda i: (0, i, 0))
        out_sds = lambda n: jax.ShapeDtypeStruct((nb, total // nb, n), F32)
    else:
        out_tile = tile
        out_sds = lambda n: jax.ShapeDtypeStruct((total, n), F32)
    widths = (GDN_KEY_DIM,) * 4 + (128, 128)
    return pl.pallas_call(
        functools.partial(_gdn_prep_kernel, nb=nb, rows=rows, batch_major=batch_major),
        grid=(total // rows,),
        in_specs=[tile(D_MODEL), full((1, D_MODEL)),
                  pl.BlockSpec(w_in.shape, lambda i: (0, 0, 0), pipeline_mode=pl.Buffered(1)),
                  full((hist, GDN_CONV_DIM)), full((CONV_WIDTH, GDN_CONV_DIM)),
                  full((1, 128)), full((1, 128))],
        out_specs=[out_tile(n) for n in widths] + [full((hist, GDN_CONV_DIM))],
        out_shape=[out_sds(n) for n in widths] + [jax.ShapeDtypeStruct((hist, GDN_CONV_DIM), F32)],
        scratch_shapes=[pltpu.VMEM((rows, D_MODEL), BF16),
                        pltpu.VMEM((hist + rows, GDN_CONV_DIM), F32),
                        pltpu.VMEM((rows, 128), F32),
                        pltpu.VMEM((4 * GDN_HEADS + 2, rows if batch_major else 8, 128), F32)],
        compiler_params=_cparams(("arbitrary",)),
        name="gdn_prep",
    )(x, gn, w_in, prev, cw, alog, dtb)


def _unit_lower_inverses(ms, ri, ci, chunk):
    eye = (ri == ci).astype(F32)
    blk = (ri >> 3) == (ci >> 3)
    n1 = [jnp.where(blk, -m, 0.0) for m in ms]
    n2 = [_dot(a, a) for a in n1]
    n4 = [_dot(a, a) for a in n2]
    ts = [_dot(eye + a, eye + b) for a, b in zip(n1, n2)]
    ts = [_dot(t, eye + a) for t, a in zip(ts, n4)]
    shift = 3
    while (1 << shift) < chunk:
        pair = ((ri >> (shift + 1)) == (ci >> (shift + 1))) & ((ri >> shift) != (ci >> shift))
        left = [_dot(t, jnp.where(pair, m, 0.0)) for t, m in zip(ts, ms)]
        ts = [t - _dot(a, t) for t, a in zip(ts, left)]
        shift += 1
    return ts


def _gdn_core_kernel(q_ref, k_ref, v_ref, z_ref, g_ref, beta_ref, s0_ref, nw_ref, o_ref, s_ref, *, chunk, bb):
    c = pl.program_id(1)

    @pl.when(c == 0)
    def _():
        s_ref[...] = s0_ref[...]

    ri = lax.broadcasted_iota(jnp.int32, (chunk, chunk), 0)
    ci = lax.broadcasted_iota(jnp.int32, (chunk, chunk), 1)
    causal = ri >= ci
    strict = ri > ci
    tril = causal.astype(BF16)
    e_r = lax.broadcasted_iota(jnp.int32, (128, 128), 0)
    e_c = lax.broadcasted_iota(jnp.int32, (128, 128), 1)
    eye128 = (e_r == e_c).astype(BF16)
    dotf = functools.partial(jnp.dot, preferred_element_type=F32)
    nt = lambda a, b: lax.dot_general(a, b, (((1,), (1,)), ((), ())), preferred_element_type=F32)
    nw = nw_ref[...]

    cums, cum_ts, ecums, e_lasts, e_rests, betas = [], [], [], [], [], []
    for bi in range(bb):
        g3 = _split3(g_ref[bi])
        cum = dotf(tril, g3[0]) + dotf(tril, g3[1]) + dotf(tril, g3[2])
        c3 = _split3(cum)
        cums.append(cum)
        cum_ts.append(nt(eye128, c3[0]) + nt(eye128, c3[1]) + nt(eye128, c3[2]))
        ecums.append(jnp.exp(cum))
        g_last = cum[chunk - 1:chunk, :]
        e_lasts.append(jnp.exp(g_last))
        e_rests.append(jnp.exp(g_last - cum))
        betas.append(beta_ref[bi])

    units = [(bi, h) for bi in range(bb) for h in range(GDN_HEADS)]
    col = lambda a, h: a[:, h:h + 1]
    sl = lambda h: slice(h * GDN_DK, (h + 1) * GDN_DK)
    q = [q_ref[bi, :, sl(h)] for bi, h in units]
    k = [k_ref[bi, :, sl(h)] for bi, h in units]
    decay = [jnp.exp(jnp.where(causal, col(cums[bi], h) - cum_ts[bi][h:h + 1, :], -jnp.inf)) for bi, h in units]
    k_beta = [kk * col(betas[bi], GDN_HEADS + h) for kk, (bi, h) in zip(k, units)]
    ak = [_dot_nt(jnp.concatenate([kb, qq], axis=0), kk) for kb, qq, kk in zip(k_beta, q, k)]
    ms = [jnp.where(strict, a[:chunk] * d, 0.0) for a, d in zip(ak, decay)]
    ts = _unit_lower_inverses(ms, ri, ci, chunk)
    rhs = [jnp.concatenate([v_ref[bi, :, sl(h)] * col(betas[bi], GDN_HEADS + h), kb * col(ecums[bi], h)], axis=1)
           for kb, (bi, h) in zip(k_beta, units)]
    sol = [_dot(t, r) for t, r in zip(ts, rhs)]
    s_old = [s_ref[bi, h] for bi, h in units]
    ws = [_dot(jnp.concatenate([so[:, GDN_DV:], qq * col(ecums[bi], h)], axis=0), s)
          for so, qq, s, (bi, h) in zip(sol, q, s_old, units)]
    v_new = [so[:, :GDN_DV] - w[:chunk] for so, w in zip(sol, ws)]
    o = [w[chunk:] + _dot(a[chunk:] * d, vn) for w, a, d, vn in zip(ws, ak, decay, v_new)]
    k_dec_t = [nt(eye128, (kk * col(e_rests[bi], h)).astype(BF16)) for kk, (bi, h) in zip(k, units)]
    for (bi, h), s, kt, vn, oo in zip(units, s_old, k_dec_t, v_new, o):
        s_ref[bi, h] = s * col(e_lasts[bi], h) + _dot(kt, vn)
        on = oo * lax.rsqrt(jnp.mean(oo * oo, axis=-1, keepdims=True) + RMS_EPS) * nw
        zh = z_ref[bi, :, sl(h)]
        o_ref[bi, :, sl(h)] = on * (zh * jax.nn.sigmoid(zh))


def _gdn_core(q, k, v, z, g, beta, s0, nw, chunk, bb):
    nb, lp = q.shape[0], q.shape[1]
    seq = lambda n: pl.BlockSpec((bb, chunk, n), lambda b, c: (b, c, 0))
    st = pl.BlockSpec((bb, GDN_HEADS, GDN_DK, GDN_DV), lambda b, c: (b, 0, 0, 0))
    return pl.pallas_call(
        functools.partial(_gdn_core_kernel, chunk=chunk, bb=bb),
        grid=(nb // bb, lp // chunk),
        in_specs=[seq(GDN_KEY_DIM), seq(GDN_KEY_DIM), seq(GDN_KEY_DIM), seq(GDN_KEY_DIM),
                  seq(128), seq(128), st, pl.BlockSpec((1, GDN_DV), lambda b, c: (0, 0))],
        out_specs=[seq(GDN_KEY_DIM), st],
        out_shape=[jax.ShapeDtypeStruct((nb, lp, GDN_KEY_DIM), F32),
                   jax.ShapeDtypeStruct(s0.shape, F32)],
        compiler_params=_cparams(("parallel", "arbitrary")),
        name="gdn_core",
    )(q, k, v, z, g, beta, s0, nw)


def _ffn_kernel(r_ref, a_ref, wo_ref, g_ref, wa_ref, wb_ref, cwa_ref, cwb_ref, cba_ref, cbb_ref, pa_ref, pb_ref,
                wd_ref, gf_ref, o_ref, ca_out, cb_out, x_s, xn_s, acc_s, hpa_s, hpb_s, cara_s, carb_s, *stage,
                nb, tm, final_norm, mixer_out, res_seq_major, out_seq_major):
    i = pl.program_id(0)
    j = pl.program_id(1)
    hist = (FFN_CONV_WIDTH - 1) * nb
    rs = min(SUB_ROWS, tm)
    nsub = tm // rs
    sub = lambda r: slice(r * rs, (r + 1) * rs)
    stage = list(stage)
    a_s = stage.pop(0) if mixer_out == "seq_major" else None
    io_s = stage.pop(0) if (res_seq_major or out_seq_major) else None
    nk = D_MODEL // 128

    @pl.when(j == 0)
    def _():
        if mixer_out == "seq_major":
            _stage_time_major(a_ref, a_s, nb)
        if res_seq_major:
            _stage_time_major(r_ref, io_s, nb)

        def project(r):
            if mixer_out == "seq_major":
                a = jnp.concatenate([a_s[kt, sub(r), :].astype(BF16) for kt in range(nk)], axis=1)
            else:
                a = a_ref[sub(r), :]
            if res_seq_major:
                res = jnp.concatenate([io_s[kt, sub(r), :] for kt in range(nk)], axis=1)
            else:
                res = r_ref[sub(r), :]
            if mixer_out == "glu":
                half = wo_ref.shape[1] // 2
                val = jnp.dot(a, wo_ref[:, :half], preferred_element_type=F32)
                gate = jnp.dot(a, wo_ref[:, half:], preferred_element_type=F32)
                y = val * jax.nn.sigmoid(gate)
            else:
                y = jnp.dot(a, wo_ref[...], preferred_element_type=F32)
            x_s[sub(r), :] = res + y

        project(0)
        for r in range(nsub):
            if r + 1 < nsub:
                project(r + 1)
            xn_s[sub(r), :] = _rms(x_s[sub(r), :], g_ref[...]).astype(BF16)
        acc_s[...] = jnp.zeros_like(acc_s)

    @pl.when(i == 0)
    def _():
        hpa_s[0:hist, :] = pa_ref[...]
        hpb_s[0:hist, :] = pb_ref[...]

    @pl.when(i > 0)
    def _():
        hpa_s[0:hist, :] = cara_s[j]
        hpb_s[0:hist, :] = carb_s[j]

    def up(r):
        xr = xn_s[r * rs:(r + 1) * rs, :]
        hpa_s[hist + r * rs:hist + (r + 1) * rs, :] = jnp.dot(xr, wa_ref[...], preferred_element_type=F32)
        hpb_s[hist + r * rs:hist + (r + 1) * rs, :] = jnp.dot(xr, wb_ref[...], preferred_element_type=F32)

    def conv(hp_s, cw_ref, cb_ref, r):
        y = hp_s[r * rs:(r + 1) * rs, :] * cw_ref[0:1, :]
        for k in range(1, FFN_CONV_WIDTH):
            y = y + hp_s[k * nb + r * rs:k * nb + (r + 1) * rs, :] * cw_ref[k:k + 1, :]
        return y + cb_ref[...]

    def down(r):
        act = (_gelu(conv(hpa_s, cwa_ref, cba_ref, r)) * conv(hpb_s, cwb_ref, cbb_ref, r)).astype(BF16)
        acc_s[r * rs:(r + 1) * rs, :] += jnp.dot(act, wd_ref[...], preferred_element_type=F32)

    up(0)
    for r in range(nsub):
        if r + 1 < nsub:
            up(r + 1)
        down(r)

    tail_a = hpa_s[tm:tm + hist, :]
    tail_b = hpb_s[tm:tm + hist, :]
    cara_s[j] = tail_a
    carb_s[j] = tail_b
    ca_out[...] = tail_a
    cb_out[...] = tail_b

    @pl.when(j == pl.num_programs(1) - 1)
    def _():
        y = x_s[...] + acc_s[...]
        if final_norm:
            y = _rms(y, gf_ref[...])
        if out_seq_major:
            for kt in range(nk):
                io_s[kt] = y[:, kt * 128:(kt + 1) * 128]
            for b in range(nb):
                for kt in range(nk):
                    o_ref[b, :, kt * 128:(kt + 1) * 128] = io_s[kt, pl.ds(b, tm // nb, stride=nb), :]
        else:
            o_ref[...] = y


def _ffn(res, a, w_out, mixer_out, g, w_up, cw, cb, prev, w_down, layer, g_final, nb, tm, final_norm,
         res_seq_major=False, out_seq_major=False):
    rows = res.shape[0] * res.shape[1] if res_seq_major else res.shape[0]
    row_tile = pl.BlockSpec((tm, D_MODEL), lambda i, j: (i, 0))
    seq_tile = pl.BlockSpec((nb, tm // nb, D_MODEL), lambda i, j: (0, i, 0))
    tn = FFN_TN
    nj = FFN_HIDDEN // tn
    hist = (FFN_CONV_WIDTH - 1) * nb
    col_a = lambda r: pl.BlockSpec((r, tn), lambda i, j: (0, j))
    col_b = lambda r: pl.BlockSpec((r, tn), lambda i, j: (0, nj + j))
    vec = pl.BlockSpec((1, D_MODEL), lambda i, j: (0, 0))
    k = w_out.shape[0]
    if mixer_out == "seq_major":
        a_spec = pl.BlockSpec((nb, tm // nb, k), lambda i, j: (0, i, 0))
        stage = [pltpu.VMEM((k // 128, tm, 128), F32)]
    else:
        a_spec = pl.BlockSpec((tm, k), lambda i, j: (i, 0))
        stage = []
    if res_seq_major or out_seq_major:
        stage = stage + [pltpu.VMEM((D_MODEL // 128, tm, 128), F32)]
    out_sds = (jax.ShapeDtypeStruct((nb, rows // nb, D_MODEL), F32) if out_seq_major
               else jax.ShapeDtypeStruct((rows, D_MODEL), F32))
    return pl.pallas_call(
        functools.partial(_ffn_kernel, nb=nb, tm=tm, final_norm=final_norm, mixer_out=mixer_out,
                          res_seq_major=res_seq_major, out_seq_major=out_seq_major),
        grid=(rows // tm, nj),
        in_specs=[seq_tile if res_seq_major else row_tile, a_spec,
                  pl.BlockSpec(w_out.shape, lambda i, j: (0, 0)), vec,
                  pl.BlockSpec((None, D_MODEL, tn), lambda i, j: (layer, 0, j)),
                  pl.BlockSpec((None, D_MODEL, tn), lambda i, j: (layer, 0, nj + j)),
                  col_a(FFN_CONV_WIDTH), col_b(FFN_CONV_WIDTH), col_a(1), col_b(1),
                  col_a(hist), col_b(hist),
                  pl.BlockSpec((None, tn, D_MODEL), lambda i, j: (layer, j, 0)), vec],
        out_specs=[seq_tile if out_seq_major else row_tile,
                   pl.BlockSpec((hist, tn), lambda i, j: (i, j)),
                   pl.BlockSpec((hist, tn), lambda i, j: (i, j))],
        out_shape=[out_sds,
                   jax.ShapeDtypeStruct((rows // tm * hist, FFN_HIDDEN), F32),
                   jax.ShapeDtypeStruct((rows // tm * hist, FFN_HIDDEN), F32)],
        scratch_shapes=[pltpu.VMEM((tm, D_MODEL), F32), pltpu.VMEM((tm, D_MODEL), BF16),
                        pltpu.VMEM((tm, D_MODEL), F32),
                        pltpu.VMEM((hist + tm, tn), F32), pltpu.VMEM((hist + tm, tn), F32),
                        pltpu.VMEM((nj, hist, tn), F32), pltpu.VMEM((nj, hist, tn), F32)] + stage,
        compiler_params=_cparams(("arbitrary", "arbitrary")),
        name="conv_ffn",
    )(res, a, w_out, g, w_up, w_up, cw, cw, cb, cb, prev, prev, w_down, g_final)


def _ffn_cols_kernel(r_ref, a_ref, wo_ref, g_ref, wab_ref, cw_ref, cb_ref, prev_ref, wd_ref, gf_ref,
                     o_ref, hist_ref, x_s, xn_s, acc_s, hpa_s, hpb_s, *stage, nb, tm, final_norm, mixer_out):
    i = pl.program_id(0)
    hist = (FFN_CONV_WIDTH - 1) * nb
    nj = wd_ref.shape[0]
    rs = tm // FFN_SUBS
    sub = lambda r: slice(r * rs, (r + 1) * rs)

    if mixer_out == "seq_major":
        a_s, = stage
        nk = a_ref.shape[2] // 128
        for b in range(nb):
            for kt in range(nk):
                a_s[kt, pl.ds(b, tm // nb, stride=nb), :] = a_ref[b, :, kt * 128:(kt + 1) * 128]

    def project(r):
        if mixer_out == "seq_major":
            a = jnp.concatenate([a_s[kt, sub(r), :].astype(BF16) for kt in range(nk)], axis=1)
        else:
            a = a_ref[sub(r), :]
        if mixer_out == "glu":
            half = wo_ref.shape[1] // 2
            val = jnp.dot(a, wo_ref[:, :half], preferred_element_type=F32)
            gate = jnp.dot(a, wo_ref[:, half:], preferred_element_type=F32)
            y = val * jax.nn.sigmoid(gate)
        else:
            y = jnp.dot(a, wo_ref[...], preferred_element_type=F32)
        x_s[sub(r), :] = r_ref[sub(r), :] + y

    project(0)
    for r in range(FFN_SUBS):
        if r + 1 < FFN_SUBS:
            project(r + 1)
        xn_s[sub(r), :] = _rms(x_s[sub(r), :], g_ref[...]).astype(BF16)
    acc_s[...] = jnp.zeros_like(acc_s)

    @pl.when(i == 0)
    def _():
        hist_ref[...] = prev_ref[...]

    def up(c, slot):
        hpa_s[slot, 0:hist, :] = hist_ref[c]
        hpb_s[slot, 0:hist, :] = hist_ref[nj + c]
        for r in range(FFN_SUBS):
            up_rows(c, slot, r)

    def up_rows(c, slot, r):
        xr = xn_s[sub(r), :]
        hpa_s[slot, hist + r * rs:hist + (r + 1) * rs, :] = jnp.dot(xr, wab_ref[c], preferred_element_type=F32)
        hpb_s[slot, hist + r * rs:hist + (r + 1) * rs, :] = jnp.dot(xr, wab_ref[nj + c], preferred_element_type=F32)

    def down_rows(c, slot, r):
        def conv(hp_s, t):
            cw = cw_ref[t]
            y = hp_s[slot, r * rs:(r + 1) * rs, :] * cw[0:1, :]
            for k in range(1, FFN_CONV_WIDTH):
                y = y + hp_s[slot, k * nb + r * rs:k * nb + (r + 1) * rs, :] * cw[k:k + 1, :]
            return y + cb_ref[t]

        act = (_gelu(conv(hpa_s, c)) * conv(hpb_s, nj + c)).astype(BF16)
        acc_s[sub(r), :] += jnp.dot(act, wd_ref[c], preferred_element_type=F32)

    def keep_history(c, slot):
        hist_ref[c] = hpa_s[slot, tm:tm + hist, :]
        hist_ref[nj + c] = hpb_s[slot, tm:tm + hist, :]

    def step(c, slot):
        nxt = 1 - slot
        hpa_s[nxt, 0:hist, :] = hist_ref[c + 1]
        hpb_s[nxt, 0:hist, :] = hist_ref[nj + c + 1]
        for r in range(FFN_SUBS):
            up_rows(c + 1, nxt, r)
            down_rows(c, slot, r)
        keep_history(c, slot)

    def tile_pair(t, carry):
        step(2 * t, 0)
        step(2 * t + 1, 1)
        return carry

    up(0, 0)
    lax.fori_loop(0, (nj - 1) // 2, tile_pair, 0)
    for c in range((nj - 1) // 2 * 2, nj - 1):
        step(c, c % 2)
    last_slot = (nj - 1) % 2
    for r in range(FFN_SUBS):
        down_rows(nj - 1, last_slot, r)
    keep_history(nj - 1, last_slot)

    y = x_s[...] + acc_s[...]
    if final_norm:
        y = _rms(y, gf_ref[...])
    o_ref[...] = y


def _ffn_cols(res, a, w_out, mixer_out, g, w_up, cw, cb, prev, w_down, g_final, nb, tm, final_norm):
    rows = res.shape[0]
    tn = FFN_TN
    nj = FFN_HIDDEN // tn
    hist = (FFN_CONV_WIDTH - 1) * nb
    tiles = lambda m: m.reshape(m.shape[0], 2 * nj, tn).transpose(1, 0, 2)
    once = lambda shape: pl.BlockSpec(shape, lambda i: (0,) * len(shape), pipeline_mode=pl.Buffered(1))
    k = w_out.shape[0]
    if mixer_out == "seq_major":
        a_spec = pl.BlockSpec((nb, tm // nb, k), lambda i: (0, i, 0))
        stage = [pltpu.VMEM((k // 128, tm, 128), F32)]
    else:
        a_spec = pl.BlockSpec((tm, k), lambda i: (i, 0))
        stage = []
    return pl.pallas_call(
        functools.partial(_ffn_cols_kernel, nb=nb, tm=tm, final_norm=final_norm, mixer_out=mixer_out),
        grid=(rows // tm,),
        in_specs=[pl.BlockSpec((tm, D_MODEL), lambda i: (i, 0)), a_spec, once(w_out.shape), once((1, D_MODEL)),
                  once((2 * nj, D_MODEL, tn)), once((2 * nj, FFN_CONV_WIDTH, tn)), once((2 * nj, 1, tn)),
                  once((2 * nj, hist, tn)), once((nj, tn, D_MODEL)), once((1, D_MODEL))],
        out_specs=[pl.BlockSpec((tm, D_MODEL), lambda i: (i, 0)),
                   pl.BlockSpec((2 * nj, hist, tn), lambda i: (0, 0, 0))],
        out_shape=[jax.ShapeDtypeStruct((rows, D_MODEL), F32),
                   jax.ShapeDtypeStruct((2 * nj, hist, tn), F32)],
        scratch_shapes=[pltpu.VMEM((tm, D_MODEL), F32), pltpu.VMEM((tm, D_MODEL), BF16),
                        pltpu.VMEM((tm, D_MODEL), F32),
                        pltpu.VMEM((2, hist + tm, tn), F32), pltpu.VMEM((2, hist + tm, tn), F32)] + stage,
        compiler_params=_cparams(("arbitrary",)),
        name="conv_ffn",
    )(res, a, w_out, g, tiles(w_up), tiles(cw), tiles(cb), tiles(prev), w_down.reshape(nj, tn, D_MODEL), g_final)


def _s5_disc_kernel(are_ref, aim_ref, ldt_ref, bre_ref, bim_ref, abr_ref, abi_ref, bbr_ref, bbi_ref):
    a_re, a_im = are_ref[...], aim_ref[...]
    dt = jnp.exp(ldt_ref[...])
    mag = jnp.exp(a_re * dt)
    ar = mag * jnp.cos(a_im * dt)
    ai = mag * jnp.sin(a_im * dt)
    den = a_re * a_re + a_im * a_im
    nr = ar - 1.0
    cr = (nr * a_re + ai * a_im) / den
    ci = (ai * a_re - nr * a_im) / den
    b_re, b_im = bre_ref[...], bim_ref[...]
    abr_ref[...] = ar
    abi_ref[...] = ai
    bbr_ref[...] = cr * b_re - ci * b_im
    bbi_ref[...] = cr * b_im + ci * b_re


def _s5_params(a_re, a_im, log_dt, b_re, b_im, c_re, c_im):
    rep = lambda a: jnp.repeat(a.astype(F32), S5_GROUP_CH, axis=0)
    rows_gc = lambda b: b.astype(F32).transpose(0, 2, 1).reshape(D_MODEL, S5_STATE)
    ldt = jnp.broadcast_to(log_dt.astype(F32)[:, None], (S5_GROUPS, S5_STATE))
    sds = jax.ShapeDtypeStruct((D_MODEL, S5_STATE), F32)
    abr, abi, bbr, bbi = pl.pallas_call(_s5_disc_kernel, out_shape=[sds] * 4, name="s5_discretize")(
        rep(a_re), rep(a_im), rep(ldt), rows_gc(b_re), rows_gc(b_im))
    eye = jnp.eye(S5_GROUPS // S5_KB, dtype=F32)

    def b_blocks(b):
        b = b.reshape(S5_KB, S5_GROUPS // S5_KB, S5_GROUP_CH, S5_STATE)
        return jnp.einsum('kgcp,gh->kgchp', b, eye).reshape(S5_KB, D_MODEL // S5_KB, S5_COLS // S5_KB).astype(BF16)

    def c_blocks(c):
        c = c.astype(F32).reshape(S5_KB, S5_GROUPS // S5_KB, S5_GROUP_CH, S5_STATE)
        return jnp.einsum('kgcp,gh->kgphc', c, eye).reshape(S5_KB, S5_COLS // S5_KB, D_MODEL // S5_KB).astype(BF16)

    return (b_blocks(bbr), b_blocks(bbi), c_blocks(c_re), c_blocks(c_im),
            abr[::S5_GROUP_CH].reshape(1, S5_COLS), abi[::S5_GROUP_CH].reshape(1, S5_COLS))


def _to_time_major(a):
    return a.transpose(1, 0, 2).reshape(a.shape[0] * a.shape[1], a.shape[2])


def _from_time_major(a, nb):
    return a.reshape(a.shape[0] // nb, nb, a.shape[1]).transpose(1, 0, 2)


def _trunk(x, nb, seq, s5_re, s5_im, lru_h, lru_conv, gdn_s, gdn_conv, ffn_conv, p, seq_major_io):
    total = seq * nb
    tm = min(total, 1024)
    rows = 512
    o_s5_re, o_s5_im, o_lru, o_lru_conv, o_gdn, o_gdn_conv, o_ffn_conv = [], [], [], [], [], [], []
    depth = p['norm_mix'].shape[0]
    for i in range(depth):
        kind, j = i % 3, i // 3
        g_mix = p['norm_mix'][i].reshape(1, D_MODEL)
        if kind == 0:
            bre, bim, cre, cim, are, aim = p['s5_disc'][j]
            y, hre, him = _s5_core(x, g_mix, p['s5_w_in'][j].astype(BF16),
                                   s5_re[j].reshape(nb, S5_COLS), s5_im[j].reshape(nb, S5_COLS),
                                   bre, bim, cre, cim, are, aim, p['s5_d'][j].reshape(1, D_MODEL), nb, rows,
                                   seq_major_io and i == 0)
            mix = (y, p['s5_w_glu'][j].astype(BF16), "glu")
            o_s5_re.append(hre.reshape(nb, S5_GROUPS, S5_STATE))
            o_s5_im.append(him.reshape(nb, S5_GROUPS, S5_STATE))
        elif kind == 1:
            w_in = p['lru_w_in'][j].astype(BF16).reshape(D_MODEL, 2, LRU_BLOCKS, LRU_BLOCK)
            w_in = w_in.transpose(2, 0, 1, 3).reshape(LRU_BLOCKS, D_MODEL, 2 * LRU_BLOCK)
            y, h_new, conv_new = _lru_core(
                x, g_mix, w_in, _to_time_major(lru_conv[j]), lru_h[j],
                p['lru_conv_w'][j], p['lru_conv_b'][j].reshape(1, LRU_WIDTH),
                p['lru_w_gate_a'][j].astype(BF16), p['lru_b_gate_a'][j].reshape(1, LRU_WIDTH),
                p['lru_w_gate_x'][j].astype(BF16), p['lru_b_gate_x'][j].reshape(1, LRU_WIDTH),
                p['lru_lambda'][j].reshape(1, LRU_WIDTH), nb, rows)
            mix = (y, p['lru_w_out'][j].astype(BF16), "plain")
            o_lru.append(h_new)
            o_lru_conv.append(_from_time_major(conv_new, nb))
        else:
            w_in = p['gdn_w_in'][j]
            w_pad = jnp.pad(w_in, ((0, 0), (0, GDN_PROJ_PAD - w_in.shape[1]))).astype(BF16)
            w_pad = w_pad.reshape(D_MODEL, GDN_PROJ_PAD // 256, 256).transpose(1, 0, 2)
            pad8 = lambda a: jnp.pad(a.reshape(1, GDN_HEADS), ((0, 0), (0, 128 - GDN_HEADS)))
            chunk = GDN_CHUNK if seq >= GDN_CHUNK else 8
            batch_major = nb == 8 and seq % chunk == 0
            *qkvzgb, conv_new = _gdn_prep(x, g_mix, w_pad, _to_time_major(gdn_conv[j]), p['gdn_conv_w'][j],
                                          pad8(p['gdn_a_log'][j]), pad8(p['gdn_dt_bias'][j]), nb, rows,
                                          batch_major)
            nw = p['gdn_norm'][j].reshape(1, GDN_DV)
            w_out = p['gdn_w_out'][j].astype(BF16)
            if batch_major:
                o, s_new = _gdn_core(*qkvzgb, gdn_s[j], nw, chunk, 4)
                mix = (o, w_out, "seq_major")
            else:
                lp = -(-seq // chunk) * chunk

                def bm(a):
                    a = a.reshape(seq, nb, a.shape[1]).transpose(1, 0, 2)
                    return jnp.pad(a, ((0, 0), (0, lp - seq), (0, 0)))

                o, s_new = _gdn_core(*[bm(a) for a in qkvzgb], gdn_s[j], nw, chunk, 8)
                o = o[:, :seq].transpose(1, 0, 2).reshape(total, GDN_KEY_DIM).astype(BF16)
                mix = (o, w_out, "plain")
            o_gdn.append(s_new)
            o_gdn_conv.append(_from_time_major(conv_new, nb))
        x, ca, cb = _ffn(x, *mix, p['norm_ffn'][i].reshape(1, D_MODEL), p['ffn_w_up'].astype(BF16),
                         p['ffn_conv_w'][i], p['ffn_conv_b'][i].reshape(1, 2 * FFN_HIDDEN),
                         _to_time_major(ffn_conv[i]), p['ffn_w_down'].astype(BF16), i,
                         p['norm_final'].reshape(1, D_MODEL), nb, tm, i == depth - 1,
                         seq_major_io and i == 0, seq_major_io and i == depth - 1)
        hist = (FFN_CONV_WIDTH - 1) * nb
        o_ffn_conv.append(_from_time_major(jnp.concatenate([ca[-hist:], cb[-hist:]], axis=1), nb))
    return (x, jnp.stack(o_s5_re), jnp.stack(o_s5_im), jnp.stack(o_lru), jnp.stack(o_lru_conv),
            jnp.stack(o_gdn), jnp.stack(o_gdn_conv), jnp.stack(o_ffn_conv))


def kernel(x_prompt, x_sample, state_s5_re, state_s5_im, state_lru, state_lru_conv, state_gdn, state_gdn_conv, state_ffn_conv, norm_mix, norm_ffn, norm_final, s5_w_in, s5_a_re, s5_a_im, s5_log_dt, s5_b_re, s5_b_im, s5_c_re, s5_c_im, s5_d, s5_w_glu, lru_w_in, lru_conv_w, lru_conv_b, lru_w_gate_a, lru_b_gate_a, lru_w_gate_x, lru_b_gate_x, lru_lambda, lru_w_out, gdn_w_in, gdn_conv_w, gdn_a_log, gdn_dt_bias, gdn_norm, gdn_w_out, ffn_w_up, ffn_conv_w, ffn_conv_b, ffn_w_down):
    p = dict(norm_mix=norm_mix, norm_ffn=norm_ffn, norm_final=norm_final, s5_w_in=s5_w_in, s5_a_re=s5_a_re,
             s5_a_im=s5_a_im, s5_log_dt=s5_log_dt, s5_b_re=s5_b_re, s5_b_im=s5_b_im, s5_c_re=s5_c_re,
             s5_c_im=s5_c_im, s5_d=s5_d, s5_w_glu=s5_w_glu, lru_w_in=lru_w_in, lru_conv_w=lru_conv_w,
             lru_conv_b=lru_conv_b, lru_w_gate_a=lru_w_gate_a, lru_b_gate_a=lru_b_gate_a,
             lru_w_gate_x=lru_w_gate_x, lru_b_gate_x=lru_b_gate_x, lru_lambda=lru_lambda, lru_w_out=lru_w_out,
             gdn_w_in=gdn_w_in, gdn_conv_w=gdn_conv_w, gdn_a_log=gdn_a_log, gdn_dt_bias=gdn_dt_bias,
             gdn_norm=gdn_norm, gdn_w_out=gdn_w_out, ffn_w_up=ffn_w_up, ffn_conv_w=ffn_conv_w,
             ffn_conv_b=ffn_conv_b, ffn_w_down=ffn_w_down)
    p['s5_disc'] = [_s5_params(s5_a_re[j], s5_a_im[j], s5_log_dt[j], s5_b_re[j], s5_b_im[j], s5_c_re[j],
                               s5_c_im[j]) for j in range(s5_a_re.shape[0])]
    outs = []
    for x, states in (
            (x_prompt, None),
            (x_sample, (state_s5_re, state_s5_im, state_lru, state_lru_conv, state_gdn, state_gdn_conv,
                        state_ffn_conv))):
        nb, seq, _ = x.shape
        if states is None:
            states = tuple(jnp.zeros((s.shape[0], nb) + s.shape[2:], F32) for s in (
                state_s5_re, state_s5_im, state_lru, state_lru_conv, state_gdn, state_gdn_conv, state_ffn_conv))
        seq_major_io = nb == 8 and seq % 128 == 0
        if seq_major_io:
            res = _trunk(x, nb, seq, *states, p, True)
            outs.append(tuple(res))
        else:
            res = _trunk(_to_time_major(x), nb, seq, *states, p, False)
            outs.append((_from_time_major(res[0], nb),) + tuple(res[1:]))
    (y_p, *st_p), (y_s, *st_s) = outs
    return (y_p, y_s, *st_p, *st_s)
```

```python
import functools
import math

import jax
import jax.numpy as jnp
from jax import lax
from jax.experimental import pallas as pl
from jax.experimental.pallas import tpu as pltpu

F32 = jnp.float32
BF16 = jnp.bfloat16

D_MODEL = 1024
RMS_EPS = 1e-6
L2_EPS = 1e-6
S5_GROUPS = 64
S5_STATE = 64
S5_GROUP_CH = 16
S5_COLS = S5_GROUPS * S5_STATE
S5_KB = 8
S5_SCAN_LANES = 1024
LRU_WIDTH = 1280
LRU_BLOCK = 128
LRU_BLOCKS = LRU_WIDTH // LRU_BLOCK
LRU_C = 8.0
CONV_WIDTH = 4
GDN_HEADS = 8
GDN_DK = 128
GDN_DV = 128
GDN_KEY_DIM = GDN_HEADS * GDN_DK
GDN_CONV_DIM = 3 * GDN_KEY_DIM
GDN_CHUNK = 64
GDN_PROJ_PAD = 4352
FFN_HIDDEN = 2816
FFN_CONV_WIDTH = 3
FFN_TN = 256
SUB_ROWS = 256
VMEM_LIMIT_BYTES = 56 * 1024 * 1024


def _cparams(sem):
    return pltpu.CompilerParams(dimension_semantics=sem, vmem_limit_bytes=VMEM_LIMIT_BYTES)


def _rms(x, g):
    ms = jnp.mean(x * x, axis=-1, keepdims=True)
    return x * lax.rsqrt(ms + RMS_EPS) * g


def _softplus(x):
    return jnp.maximum(x, 0.0) + jnp.log1p(jnp.exp(-jnp.abs(x)))


def _stage_time_major(src_ref, slab_ref, nb):
    steps = src_ref.shape[1]
    for b in range(nb):
        for kt in range(src_ref.shape[2] // 128):
            slab_ref[kt, pl.ds(b, steps, stride=nb), :] = src_ref[b, :, kt * 128:(kt + 1) * 128]


def _gelu(x):
    c = math.sqrt(2.0 / math.pi)
    half = 0.5 * x
    return half + half * jnp.tanh(x * (c + (c * 0.044715) * (x * x)))


def _dot(a, b):
    return jnp.dot(a.astype(BF16), b.astype(BF16), preferred_element_type=F32)


def _dot_nt(a, b):
    return lax.dot_general(a.astype(BF16), b.astype(BF16), (((1,), (1,)), ((), ())),
                           preferred_element_type=F32)


def _split3(a):
    hi = a.astype(BF16)
    r = a - hi.astype(F32)
    mid = r.astype(BF16)
    lo = (r - mid.astype(F32)).astype(BF16)
    return hi, mid, lo


def _s5_core_kernel(x_ref, g_ref, win_ref, h0re_ref, h0im_ref, bre_ref, bim_ref, cre_ref, cim_ref, are_ref,
                    aim_ref, d_ref, y_ref, hre_out, him_out, xn_s, u_ref, hre_s, him_s, *stage,
                    nb, rows, seq_major_in):
    i = pl.program_id(0)

    @pl.when(i == 0)
    def _():
        hre_s[0:nb, :] = h0re_ref[...]
        him_s[0:nb, :] = h0im_ref[...]

    kw = S5_COLS // S5_KB
    uw = D_MODEL // S5_KB
    per_grp = S5_SCAN_LANES // kw
    n_grp = S5_COLS // S5_SCAN_LANES
    if seq_major_in:
        x_st, = stage
        _stage_time_major(x_ref, x_st, nb)
        x = jnp.concatenate([x_st[kt] for kt in range(D_MODEL // 128)], axis=1)
    else:
        x = x_ref[...]
    xn_s[...] = _rms(x, g_ref[...]).astype(BF16)

    def project_in(grp):
        ucols = slice(grp * per_grp * uw, (grp + 1) * per_grp * uw)
        u_ref[:, ucols] = jnp.dot(xn_s[...], win_ref[:, ucols], preferred_element_type=F32)
        for kb in range(grp * per_grp, (grp + 1) * per_grp):
            ukb = u_ref[:, kb * uw:(kb + 1) * uw].astype(BF16)
            hre_s[nb:nb + rows, kb * kw:(kb + 1) * kw] = jnp.dot(ukb, bre_ref[kb], preferred_element_type=F32)
            him_s[nb:nb + rows, kb * kw:(kb + 1) * kw] = jnp.dot(ukb, bim_ref[kb], preferred_element_type=F32)

    def scan(grp):
        cols = slice(grp * S5_SCAN_LANES, (grp + 1) * S5_SCAN_LANES)
        are = jnp.broadcast_to(are_ref[:, cols], (nb, S5_SCAN_LANES))
        aim = jnp.broadcast_to(aim_ref[:, cols], (nb, S5_SCAN_LANES))
        hr, hi = hre_s[0:nb, cols], him_s[0:nb, cols]
        for t in range(rows // nb):
            r = slice(nb + t * nb, 2 * nb + t * nb)
            hr, hi = (are * hr - aim * hi + hre_s[r, cols],
                      are * hi + aim * hr + him_s[r, cols])
            hre_s[r, cols] = hr
            him_s[r, cols] = hi

    def project_out(grp):
        for kb in range(grp * per_grp, (grp + 1) * per_grp):
            hr = hre_s[nb:nb + rows, kb * kw:(kb + 1) * kw].astype(BF16)
            hi = him_s[nb:nb + rows, kb * kw:(kb + 1) * kw].astype(BF16)
            yk = (jnp.dot(hr, cre_ref[kb], preferred_element_type=F32)
                  - jnp.dot(hi, cim_ref[kb], preferred_element_type=F32))
            yk = yk + d_ref[:, kb * uw:(kb + 1) * uw] * u_ref[:, kb * uw:(kb + 1) * uw]
            y_ref[:, kb * uw:(kb + 1) * uw] = _gelu(yk).astype(BF16)

    project_in(0)
    for grp in range(n_grp):
        if grp + 1 < n_grp:
            project_in(grp + 1)
        scan(grp)
        project_out(grp)

    last_re = hre_s[rows:rows + nb, :]
    last_im = him_s[rows:rows + nb, :]
    hre_s[0:nb, :] = last_re
    him_s[0:nb, :] = last_im
    hre_out[...] = last_re
    him_out[...] = last_im


def _s5_core(x, g, w_in, h0re, h0im, bre, bim, cre, cim, are, aim, d, nb, rows, seq_major_in=False):
    full = lambda shape: pl.BlockSpec(shape, lambda i: (0,) * len(shape))
    if seq_major_in:
        total = x.shape[0] * x.shape[1]
        x_spec = pl.BlockSpec((nb, rows // nb, D_MODEL), lambda i: (0, i, 0))
        stage = [pltpu.VMEM((D_MODEL // 128, rows, 128), F32)]
    else:
        total = x.shape[0]
        x_spec = pl.BlockSpec((rows, D_MODEL), lambda i: (i, 0))
        stage = []
    return pl.pallas_call(
        functools.partial(_s5_core_kernel, nb=nb, rows=rows, seq_major_in=seq_major_in),
        grid=(total // rows,),
        in_specs=[x_spec, full((1, D_MODEL)), full(w_in.shape),
                  full((nb, S5_COLS)), full((nb, S5_COLS)),
                  full(bre.shape), full(bim.shape), full(cre.shape), full(cim.shape),
                  full((1, S5_COLS)), full((1, S5_COLS)), full((1, D_MODEL))],
        out_specs=[pl.BlockSpec((rows, D_MODEL), lambda i: (i, 0)),
                   full((nb, S5_COLS)), full((nb, S5_COLS))],
        out_shape=[jax.ShapeDtypeStruct((total, D_MODEL), BF16),
                   jax.ShapeDtypeStruct((nb, S5_COLS), F32),
                   jax.ShapeDtypeStruct((nb, S5_COLS), F32)],
        scratch_shapes=[pltpu.VMEM((rows, D_MODEL), BF16), pltpu.VMEM((rows, D_MODEL), F32),
                        pltpu.VMEM((nb + rows, S5_COLS), F32),
                        pltpu.VMEM((nb + rows, S5_COLS), F32)] + stage,
        compiler_params=_cparams(("arbitrary",)),
        name="s5_core",
    )(x, g, w_in, h0re, h0im, bre, bim, cre, cim, are, aim, d)


def _lru_core_kernel(x_ref, g_ref, win_ref, prev_ref, h0_ref, cw_ref, cb_ref, wga_ref, bga_ref, wgx_ref,
                     bgx_ref, lam_ref, y_ref, hout_ref, cout_ref, xn_s, gate_s, xp_s, h_s, a_s, *, nb, rows):
    i = pl.program_id(0)
    hist = (CONV_WIDTH - 1) * nb

    @pl.when(i == 0)
    def _():
        xp_s[0:hist, :] = prev_ref[...]
        h_s[0:nb, :] = h0_ref[...]

    xn_s[...] = _rms(x_ref[...], g_ref[...]).astype(BF16)
    c8 = -LRU_C * _softplus(-lam_ref[...])

    def project(n):
        pg = jnp.dot(xn_s[...], win_ref[n], preferred_element_type=F32)
        sl = slice(n * LRU_BLOCK, (n + 1) * LRU_BLOCK)
        gate_s[:, sl] = pg[:, :LRU_BLOCK]
        xp_s[hist:hist + rows, sl] = pg[:, LRU_BLOCK:]

    def gates(n):
        sl = slice(n * LRU_BLOCK, (n + 1) * LRU_BLOCK)
        xcn = xp_s[0:rows, sl] * cw_ref[0:1, sl]
        for k in range(1, CONV_WIDTH):
            xcn = xcn + xp_s[k * nb:k * nb + rows, sl] * cw_ref[k:k + 1, sl]
        xcn = xcn + cb_ref[:, sl]
        xcb = xcn.astype(BF16)
        r = jax.nn.sigmoid(jnp.dot(xcb, wga_ref[n], preferred_element_type=F32) + bga_ref[:, sl])
        ig = jax.nn.sigmoid(jnp.dot(xcb, wgx_ref[n], preferred_element_type=F32) + bgx_ref[:, sl])
        log_a = c8[:, sl] * r
        a_s[:, sl] = jnp.exp(log_a)
        t = jnp.tanh(log_a)
        h_s[nb:nb + rows, sl] = jnp.sqrt(-2.0 * t / (1.0 - t)) * ig * xcn

    project(0)
    for n in range(LRU_BLOCKS):
        if n + 1 < LRU_BLOCKS:
            project(n + 1)
        gates(n)

    def step(t, carry):
        r0 = pl.multiple_of(t * nb, nb)
        r1 = pl.multiple_of(t * nb + nb, nb)
        h_s[pl.ds(r1, nb), :] = a_s[pl.ds(r0, nb), :] * h_s[pl.ds(r0, nb), :] + h_s[pl.ds(r1, nb), :]
        return carry

    lax.fori_loop(0, rows // nb, step, 0)

    y_ref[...] = (_gelu(gate_s[...]) * h_s[nb:nb + rows, :]).astype(BF16)
    tail = xp_s[rows:rows + hist, :]
    last = h_s[rows:rows + nb, :]
    xp_s[0:hist, :] = tail
    h_s[0:nb, :] = last
    cout_ref[...] = tail
    hout_ref[...] = last


def _lru_core(x, g, w_in, prev, h0, cw, cb, wga, bga, wgx, bgx, lam, nb, rows):
    total = x.shape[0]
    hist = (CONV_WIDTH - 1) * nb
    full = lambda shape: pl.BlockSpec(shape, lambda i: (0,) * len(shape))
    return pl.pallas_call(
        functools.partial(_lru_core_kernel, nb=nb, rows=rows),
        grid=(total // rows,),
        in_specs=[pl.BlockSpec((rows, D_MODEL), lambda i: (i, 0)), full((1, D_MODEL)),
                  pl.BlockSpec(w_in.shape, lambda i: (0, 0, 0), pipeline_mode=pl.Buffered(1)),
                  full((hist, LRU_WIDTH)), full((nb, LRU_WIDTH)),
                  full((CONV_WIDTH, LRU_WIDTH)), full((1, LRU_WIDTH)),
                  full(wga.shape), full((1, LRU_WIDTH)), full(wgx.shape), full((1, LRU_WIDTH)),
                  full((1, LRU_WIDTH))],
        out_specs=[pl.BlockSpec((rows, LRU_WIDTH), lambda i: (i, 0)),
                   full((nb, LRU_WIDTH)), full((hist, LRU_WIDTH))],
        out_shape=[jax.ShapeDtypeStruct((total, LRU_WIDTH), BF16),
                   jax.ShapeDtypeStruct((nb, LRU_WIDTH), F32),
                   jax.ShapeDtypeStruct((hist, LRU_WIDTH), F32)],
        scratch_shapes=[pltpu.VMEM((rows, D_MODEL), BF16),
                        pltpu.VMEM((rows, LRU_WIDTH), F32),
                        pltpu.VMEM((hist + rows, LRU_WIDTH), F32),
                        pltpu.VMEM((nb + rows, LRU_WIDTH), F32),
                        pltpu.VMEM((rows, LRU_WIDTH), F32)],
        compiler_params=_cparams(("arbitrary",)),
        name="lru_core",
    )(x, g, w_in, prev, h0, cw, cb, wga, bga, wgx, bgx, lam)


def _gdn_prep_kernel(x_ref, gn_ref, win_ref, prev_ref, cw_ref, alog_ref, dtb_ref,
                     q_ref, k_ref, v_ref, zo_ref, g_ref, beta_ref, cout_ref, xn_s, xp_s, ab_s, st_s,
                     *, nb, rows, batch_major):
    i = pl.program_id(0)
    hist = (CONV_WIDTH - 1) * nb
    n_qkv = 3 * GDN_HEADS
    n_z = GDN_HEADS

    @pl.when(i == 0)
    def _():
        xp_s[0:hist, :] = prev_ref[...]

    xn_s[...] = _rms(x_ref[...], gn_ref[...]).astype(BF16)

    def project(p):
        pg = jnp.dot(xn_s[...], win_ref[p], preferred_element_type=F32)
        for half in range(2):
            s = 2 * p + half
            col = pg[:, half * 128:(half + 1) * 128]
            if s < n_qkv:
                xp_s[hist:hist + rows, s * 128:(s + 1) * 128] = col
            elif s < n_qkv + n_z:
                if batch_major:
                    st_s[s] = col
                else:
                    zo_ref[:, (s - n_qkv) * 128:(s - n_qkv + 1) * 128] = col
            elif s == n_qkv + n_z:
                ab_s[...] = col

    def activate(s):
        part, h = divmod(s, GDN_HEADS)
        sl = slice(s * 128, (s + 1) * 128)
        acc = xp_s[0:rows, sl] * cw_ref[0:1, sl]
        for k in range(1, CONV_WIDTH):
            acc = acc + xp_s[k * nb:k * nb + rows, sl] * cw_ref[k:k + 1, sl]
        y = acc * jax.nn.sigmoid(acc)
        if part < 2:
            y = y * lax.rsqrt(jnp.sum(y * y, axis=-1, keepdims=True) + L2_EPS)
        if part == 0:
            y = y * (GDN_DK ** -0.5)
        if batch_major:
            st_s[s] = y
        else:
            (q_ref, k_ref, v_ref)[part][:, h * GDN_DK:(h + 1) * GDN_DK] = y

    n_pairs = win_ref.shape[0]
    project(0)
    for p in range(n_pairs):
        if p + 1 < n_pairs:
            project(p + 1)
        for s in (2 * p, 2 * p + 1):
            if s < n_qkv:
                activate(s)

    ab = ab_s[...]
    g = -jnp.exp(alog_ref[...]) * _softplus(ab + dtb_ref[...])
    beta = jax.nn.sigmoid(ab)
    if batch_major:
        st_s[4 * GDN_HEADS] = g
        st_s[4 * GDN_HEADS + 1] = beta
        steps = rows // nb
        for b in range(nb):
            pick = pl.ds(b, steps, stride=nb)
            for part, out in enumerate((q_ref, k_ref, v_ref, zo_ref)):
                for h in range(GDN_HEADS):
                    out[b, :, h * GDN_DK:(h + 1) * GDN_DK] = st_s[part * GDN_HEADS + h, pick, :]
            g_ref[b] = st_s[4 * GDN_HEADS, pick, :]
            beta_ref[b] = st_s[4 * GDN_HEADS + 1, pick, :]
    else:
        g_ref[...] = g
        beta_ref[...] = beta
    tail = xp_s[rows:rows + hist, :]
    xp_s[0:hist, :] = tail
    cout_ref[...] = tail


def _gdn_prep(x, gn, w_in, prev, cw, alog, dtb, nb, rows, batch_major):
    total = x.shape[0]
    hist = (CONV_WIDTH - 1) * nb
    full = lambda shape: pl.BlockSpec(shape, lambda i: (0,) * len(shape))
    tile = lambda n: pl.BlockSpec((rows, n), lambda i: (i, 0))
    if batch_major:
        out_tile = lambda n: pl.BlockSpec((nb, rows // nb, n), lambda i: (0, i, 0))
        out_sds = lambda n: jax.ShapeDtypeStruct((nb, total // nb, n), F32)
    else:
        out_tile = tile
        out_sds = lambda n: jax.ShapeDtypeStruct((total, n), F32)
    widths = (GDN_KEY_DIM,) * 4 + (128, 128)
    return pl.pallas_call(
        functools.partial(_gdn_prep_kernel, nb=nb, rows=rows, batch_major=batch_major),
        grid=(total // rows,),
        in_specs=[tile(D_MODEL), full((1, D_MODEL)),
                  pl.BlockSpec(w_in.shape, lambda i: (0, 0, 0), pipeline_mode=pl.Buffered(1)),
                  full((hist, GDN_CONV_DIM)), full((CONV_WIDTH, GDN_CONV_DIM)),
                  full((1, 128)), full((1, 128))],
        out_specs=[out_tile(n) for n in widths] + [full((hist, GDN_CONV_DIM))],
        out_shape=[out_sds(n) for n in widths] + [jax.ShapeDtypeStruct((hist, GDN_CONV_DIM), F32)],
        scratch_shapes=[pltpu.VMEM((rows, D_MODEL), BF16),
                        pltpu.VMEM((hist + rows, GDN_CONV_DIM), F32),
                        pltpu.VMEM((rows, 128), F32),
                        pltpu.VMEM((4 * GDN_HEADS + 2, rows if batch_major else 8, 128), F32)],
        compiler_params=_cparams(("arbitrary",)),
        name="gdn_prep",
    )(x, gn, w_in, prev, cw, alog, dtb)


def _unit_lower_inverses(ms, ri, ci, chunk):
    eye = (ri == ci).astype(F32)
    blk = (ri >> 3) == (ci >> 3)
    n1 = [jnp.where(blk, -m, 0.0) for m in ms]
    n2 = [_dot(a, a) for a in n1]
    n4 = [_dot(a, a) for a in n2]
    ts = [_dot(eye + a, eye + b) for a, b in zip(n1, n2)]
    ts = [_dot(t, eye + a) for t, a in zip(ts, n4)]
    shift = 3
    while (1 << shift) < chunk:
        pair = ((ri >> (shift + 1)) == (ci >> (shift + 1))) & ((ri >> shift) != (ci >> shift))
        left = [_dot(t, jnp.where(pair, m, 0.0)) for t, m in zip(ts, ms)]
        ts = [t - _dot(a, t) for t, a in zip(ts, left)]
        shift += 1
    return ts


def _gdn_core_kernel(q_ref, k_ref, v_ref, z_ref, g_ref, beta_ref, s0_ref, nw_ref, o_ref, s_ref, *, chunk, bb):
    c = pl.program_id(1)

    @pl.when(c == 0)
    def _():
        s_ref[...] = s0_ref[...]

    ri = lax.broadcasted_iota(jnp.int32, (chunk, chunk), 0)
    ci = lax.broadcasted_iota(jnp.int32, (chunk, chunk), 1)
    causal = ri >= ci
    strict = ri > ci
    tril = causal.astype(BF16)
    e_r = lax.broadcasted_iota(jnp.int32, (128, 128), 0)
    e_c = lax.broadcasted_iota(jnp.int32, (128, 128), 1)
    eye128 = (e_r == e_c).astype(BF16)
    dotf = functools.partial(jnp.dot, preferred_element_type=F32)
    nt = lambda a, b: lax.dot_general(a, b, (((1,), (1,)), ((), ())), preferred_element_type=F32)
    nw = nw_ref[...]

    cums, cum_ts, ecums, e_lasts, e_rests, betas = [], [], [], [], [], []
    for bi in range(bb):
        g3 = _split3(g_ref[bi])
        cum = dotf(tril, g3[0]) + dotf(tril, g3[1]) + dotf(tril, g3[2])
        c3 = _split3(cum)
        cums.append(cum)
        cum_ts.append(nt(eye128, c3[0]) + nt(eye128, c3[1]) + nt(eye128, c3[2]))
        ecums.append(jnp.exp(cum))
        g_last = cum[chunk - 1:chunk, :]
        e_lasts.append(jnp.exp(g_last))
        e_rests.append(jnp.exp(g_last - cum))
        betas.append(beta_ref[bi])

    units = [(bi, h) for bi in range(bb) for h in range(GDN_HEADS)]
    col = lambda a, h: a[:, h:h + 1]
    sl = lambda h: slice(h * GDN_DK, (h + 1) * GDN_DK)
    q = [q_ref[bi, :, sl(h)] for bi, h in units]
    k = [k_ref[bi, :, sl(h)] for bi, h in units]
    decay = [jnp.exp(jnp.where(causal, col(cums[bi], h) - cum_ts[bi][h:h + 1, :], -jnp.inf)) for bi, h in units]
    k_beta = [kk * col(betas[bi], GDN_HEADS + h) for kk, (bi, h) in zip(k, units)]
    ak = [_dot_nt(jnp.concatenate([kb, qq], axis=0), kk) for kb, qq, kk in zip(k_beta, q, k)]
    ms = [jnp.where(strict, a[:chunk] * d, 0.0) for a, d in zip(ak, decay)]
    ts = _unit_lower_inverses(ms, ri, ci, chunk)
    rhs = [jnp.concatenate([v_ref[bi, :, sl(h)] * col(betas[bi], GDN_HEADS + h), kb * col(ecums[bi], h)], axis=1)
           for kb, (bi, h) in zip(k_beta, units)]
    sol = [_dot(t, r) for t, r in zip(ts, rhs)]
    s_old = [s_ref[bi, h] for bi, h in units]
    ws = [_dot(jnp.concatenate([so[:, GDN_DV:], qq * col(ecums[bi], h)], axis=0), s)
          for so, qq, s, (bi, h) in zip(sol, q, s_old, units)]
    v_new = [so[:, :GDN_DV] - w[:chunk] for so, w in zip(sol, ws)]
    o = [w[chunk:] + _dot(a[chunk:] * d, vn) for w, a, d, vn in zip(ws, ak, decay, v_new)]
    k_dec_t = [nt(eye128, (kk * col(e_rests[bi], h)).astype(BF16)) for kk, (bi, h) in zip(k, units)]
    for (bi, h), s, kt, vn, oo in zip(units, s_old, k_dec_t, v_new, o):
        s_ref[bi, h] = s * col(e_lasts[bi], h) + _dot(kt, vn)
        on = oo * lax.rsqrt(jnp.mean(oo * oo, axis=-1, keepdims=True) + RMS_EPS) * nw
        zh = z_ref[bi, :, sl(h)]
        o_ref[bi, :, sl(h)] = on * (zh * jax.nn.sigmoid(zh))


def _gdn_core(q, k, v, z, g, beta, s0, nw, chunk, bb):
    nb, lp = q.shape[0], q.shape[1]
    seq = lambda n: pl.BlockSpec((bb, chunk, n), lambda b, c: (b, c, 0))
    st = pl.BlockSpec((bb, GDN_HEADS, GDN_DK, GDN_DV), lambda b, c: (b, 0, 0, 0))
    return pl.pallas_call(
        functools.partial(_gdn_core_kernel, chunk=chunk, bb=bb),
        grid=(nb // bb, lp // chunk),
        in_specs=[seq(GDN_KEY_DIM), seq(GDN_KEY_DIM), seq(GDN_KEY_DIM), seq(GDN_KEY_DIM),
                  seq(128), seq(128), st, pl.BlockSpec((1, GDN_DV), lambda b, c: (0, 0))],
        out_specs=[seq(GDN_KEY_DIM), st],
        out_shape=[jax.ShapeDtypeStruct((nb, lp, GDN_KEY_DIM), F32),
                   jax.ShapeDtypeStruct(s0.shape, F32)],
        compiler_params=_cparams(("parallel", "arbitrary")),
        name="gdn_core",
    )(q, k, v, z, g, beta, s0, nw)


def _ffn_kernel(r_ref, a_ref, wo_ref, g_ref, wa_ref, wb_ref, cwa_ref, cwb_ref, cba_ref, cbb_ref, pa_ref, pb_ref,
                wd_ref, gf_ref, o_ref, ca_out, cb_out, x_s, xn_s, acc_s, hpa_s, hpb_s, cara_s, carb_s, *stage,
                nb, tm, final_norm, mixer_out, res_seq_major, out_seq_major):
    i = pl.program_id(0)
    j = pl.program_id(1)
    hist = (FFN_CONV_WIDTH - 1) * nb
    rs = min(SUB_ROWS, tm)
    nsub = tm // rs
    sub = lambda r: slice(r * rs, (r + 1) * rs)
    stage = list(stage)
    a_s = stage.pop(0) if mixer_out == "seq_major" else None
    io_s = stage.pop(0) if (res_seq_major or out_seq_major) else None
    nk = D_MODEL // 128

    @pl.when(j == 0)
    def _():
        if mixer_out == "seq_major":
            _stage_time_major(a_ref, a_s, nb)
        if res_seq_major:
            _stage_time_major(r_ref, io_s, nb)

        def project(r):
            if mixer_out == "seq_major":
                a = jnp.concatenate([a_s[kt, sub(r), :].astype(BF16) for kt in range(nk)], axis=1)
            else:
                a = a_ref[sub(r), :]
            if res_seq_major:
                res = jnp.concatenate([io_s[kt, sub(r), :] for kt in range(nk)], axis=1)
            else:
                res = r_ref[sub(r), :]
            if mixer_out == "glu":
                half = wo_ref.shape[1] // 2
                val = jnp.dot(a, wo_ref[:, :half], preferred_element_type=F32)
                gate = jnp.dot(a, wo_ref[:, half:], preferred_element_type=F32)
                y = val * jax.nn.sigmoid(gate)
            else:
                y = jnp.dot(a, wo_ref[...], preferred_element_type=F32)
            x_s[sub(r), :] = res + y

        project(0)
        for r in range(nsub):
            if r + 1 < nsub:
                project(r + 1)
            xn_s[sub(r), :] = _rms(x_s[sub(r), :], g_ref[...]).astype(BF16)
        acc_s[...] = jnp.zeros_like(acc_s)

    @pl.when(i == 0)
    def _():
        hpa_s[0:hist, :] = pa_ref[...]
        hpb_s[0:hist, :] = pb_ref[...]

    @pl.when(i > 0)
    def _():
        hpa_s[0:hist, :] = cara_s[j]
        hpb_s[0:hist, :] = carb_s[j]

    def up(r):
        xr = xn_s[r * rs:(r + 1) * rs, :]
        hpa_s[hist + r * rs:hist + (r + 1) * rs, :] = jnp.dot(xr, wa_ref[...], preferred_element_type=F32)
        hpb_s[hist + r * rs:hist + (r + 1) * rs, :] = jnp.dot(xr, wb_ref[...], preferred_element_type=F32)

    def conv(hp_s, cw_ref, cb_ref, r):
        y = hp_s[r * rs:(r + 1) * rs, :] * cw_ref[0:1, :]
        for k in range(1, FFN_CONV_WIDTH):
            y = y + hp_s[k * nb + r * rs:k * nb + (r + 1) * rs, :] * cw_ref[k:k + 1, :]
        return y + cb_ref[...]

    def down(r):
        act = (_gelu(conv(hpa_s, cwa_ref, cba_ref, r)) * conv(hpb_s, cwb_ref, cbb_ref, r)).astype(BF16)
        acc_s[r * rs:(r + 1) * rs, :] += jnp.dot(act, wd_ref[...], preferred_element_type=F32)

    up(0)
    for r in range(nsub):
        if r + 1 < nsub:
            up(r + 1)
        down(r)

    tail_a = hpa_s[tm:tm + hist, :]
    tail_b = hpb_s[tm:tm + hist, :]
    cara_s[j] = tail_a
    carb_s[j] = tail_b
    ca_out[...] = tail_a
    cb_out[...] = tail_b

    @pl.when(j == pl.num_programs(1) - 1)
    def _():
        y = x_s[...] + acc_s[...]
        if final_norm:
            y = _rms(y, gf_ref[...])
        if out_seq_major:
            for kt in range(nk):
                io_s[kt] = y[:, kt * 128:(kt + 1) * 128]
            for b in range(nb):
                for kt in range(nk):
                    o_ref[b, :, kt * 128:(kt + 1) * 128] = io_s[kt, pl.ds(b, tm // nb, stride=nb), :]
        else:
            o_ref[...] = y


def _ffn(res, a, w_out, mixer_out, g, w_up, cw, cb, prev, w_down, layer, g_final, nb, tm, final_norm,
         res_seq_major=False, out_seq_major=False):
    rows = res.shape[0] * res.shape[1] if res_seq_major else res.shape[0]
    row_tile = pl.BlockSpec((tm, D_MODEL), lambda i, j: (i, 0))
    seq_tile = pl.BlockSpec((nb, tm // nb, D_MODEL), lambda i, j: (0, i, 0))
    tn = FFN_TN
    nj = FFN_HIDDEN // tn
    hist = (FFN_CONV_WIDTH - 1) * nb
    col_a = lambda r: pl.BlockSpec((r, tn), lambda i, j: (0, j))
    col_b = lambda r: pl.BlockSpec((r, tn), lambda i, j: (0, nj + j))
    vec = pl.BlockSpec((1, D_MODEL), lambda i, j: (0, 0))
    k = w_out.shape[0]
    if mixer_out == "seq_major":
        a_spec = pl.BlockSpec((nb, tm // nb, k), lambda i, j: (0, i, 0))
        stage = [pltpu.VMEM((k // 128, tm, 128), F32)]
    else:
        a_spec = pl.BlockSpec((tm, k), lambda i, j: (i, 0))
        stage = []
    if res_seq_major or out_seq_major:
        stage = stage + [pltpu.VMEM((D_MODEL // 128, tm, 128), F32)]
    out_sds = (jax.ShapeDtypeStruct((nb, rows // nb, D_MODEL), F32) if out_seq_major
               else jax.ShapeDtypeStruct((rows, D_MODEL), F32))
    return pl.pallas_call(
        functools.partial(_ffn_kernel, nb=nb, tm=tm, final_norm=final_norm, mixer_out=mixer_out,
                          res_seq_major=res_seq_major, out_seq_major=out_seq_major),
        grid=(rows // tm, nj),
        in_specs=[seq_tile if res_seq_major else row_tile, a_spec,
                  pl.BlockSpec(w_out.shape, lambda i, j: (0, 0)), vec,
                  pl.BlockSpec((None, D_MODEL, tn), lambda i, j: (layer, 0, j)),
                  pl.BlockSpec((None, D_MODEL, tn), lambda i, j: (layer, 0, nj + j)),
                  col_a(FFN_CONV_WIDTH), col_b(FFN_CONV_WIDTH), col_a(1), col_b(1),
                  col_a(hist), col_b(hist),
                  pl.BlockSpec((None, tn, D_MODEL), lambda i, j: (layer, j, 0)), vec],
        out_specs=[seq_tile if out_seq_major else row_tile,
                   pl.BlockSpec((hist, tn), lambda i, j: (i, j)),
                   pl.BlockSpec((hist, tn), lambda i, j: (i, j))],
        out_shape=[out_sds,
                   jax.ShapeDtypeStruct((rows // tm * hist, FFN_HIDDEN), F32),
                   jax.ShapeDtypeStruct((rows // tm * hist, FFN_HIDDEN), F32)],
        scratch_shapes=[pltpu.VMEM((tm, D_MODEL), F32), pltpu.VMEM((tm, D_MODEL), BF16),
                        pltpu.VMEM((tm, D_MODEL), F32),
                        pltpu.VMEM((hist + tm, tn), F32), pltpu.VMEM((hist + tm, tn), F32),
                        pltpu.VMEM((nj, hist, tn), F32), pltpu.VMEM((nj, hist, tn), F32)] + stage,
        compiler_params=_cparams(("arbitrary", "arbitrary")),
        name="conv_ffn",
    )(res, a, w_out, g, w_up, w_up, cw, cw, cb, cb, prev, prev, w_down, g_final)


def _s5_disc_kernel(are_ref, aim_ref, ldt_ref, bre_ref, bim_ref, abr_ref, abi_ref, bbr_ref, bbi_ref):
    a_re, a_im = are_ref[...], aim_ref[...]
    dt = jnp.exp(ldt_ref[...])
    mag = jnp.exp(a_re * dt)
    ar = mag * jnp.cos(a_im * dt)
    ai = mag * jnp.sin(a_im * dt)
    den = a_re * a_re + a_im * a_im
    nr = ar - 1.0
    cr = (nr * a_re + ai * a_im) / den
    ci = (ai * a_re - nr * a_im) / den
    b_re, b_im = bre_ref[...], bim_ref[...]
    abr_ref[...] = ar
    abi_ref[...] = ai
    bbr_ref[...] = cr * b_re - ci * b_im
    bbi_ref[...] = cr * b_im + ci * b_re


def _s5_params(a_re, a_im, log_dt, b_re, b_im, c_re, c_im):
    rep = lambda a: jnp.repeat(a.astype(F32), S5_GROUP_CH, axis=0)
    rows_gc = lambda b: b.astype(F32).transpose(0, 2, 1).reshape(D_MODEL, S5_STATE)
    ldt = jnp.broadcast_to(log_dt.astype(F32)[:, None], (S5_GROUPS, S5_STATE))
    sds = jax.ShapeDtypeStruct((D_MODEL, S5_STATE), F32)
    abr, abi, bbr, bbi = pl.pallas_call(_s5_disc_kernel, out_shape=[sds] * 4, name="s5_discretize")(
        rep(a_re), rep(a_im), rep(ldt), rows_gc(b_re), rows_gc(b_im))
    eye = jnp.eye(S5_GROUPS // S5_KB, dtype=F32)

    def b_blocks(b):
        b = b.reshape(S5_KB, S5_GROUPS // S5_KB, S5_GROUP_CH, S5_STATE)
        return jnp.einsum('kgcp,gh->kgchp', b, eye).reshape(S5_KB, D_MODEL // S5_KB, S5_COLS // S5_KB).astype(BF16)

    def c_blocks(c):
        c = c.astype(F32).reshape(S5_KB, S5_GROUPS // S5_KB, S5_GROUP_CH, S5_STATE)
        return jnp.einsum('kgcp,gh->kgphc', c, eye).reshape(S5_KB, S5_COLS // S5_KB, D_MODEL // S5_KB).astype(BF16)

    return (b_blocks(bbr), b_blocks(bbi), c_blocks(c_re), c_blocks(c_im),
            abr[::S5_GROUP_CH].reshape(1, S5_COLS), abi[::S5_GROUP_CH].reshape(1, S5_COLS))


def _to_time_major(a):
    return a.transpose(1, 0, 2).reshape(a.shape[0] * a.shape[1], a.shape[2])


def _from_time_major(a, nb):
    return a.reshape(a.shape[0] // nb, nb, a.shape[1]).transpose(1, 0, 2)


def _trunk(x, nb, seq, s5_re, s5_im, lru_h, lru_conv, gdn_s, gdn_conv, ffn_conv, p, seq_major_io):
    total = seq * nb
    tm = min(total, 1024)
    rows = 512
    o_s5_re, o_s5_im, o_lru, o_lru_conv, o_gdn, o_gdn_conv, o_ffn_conv = [], [], [], [], [], [], []
    depth = p['norm_mix'].shape[0]
    for i in range(depth):
        kind, j = i % 3, i // 3
        g_mix = p['norm_mix'][i].reshape(1, D_MODEL)
        if kind == 0:
            bre, bim, cre, cim, are, aim = p['s5_disc'][j]
            y, hre, him = _s5_core(x, g_mix, p['s5_w_in'][j].astype(BF16),
                                   s5_re[j].reshape(nb, S5_COLS), s5_im[j].reshape(nb, S5_COLS),
                                   bre, bim, cre, cim, are, aim, p['s5_d'][j].reshape(1, D_MODEL), nb, rows,
                                   seq_major_io and i == 0)
            mix = (y, p['s5_w_glu'][j].astype(BF16), "glu")
            o_s5_re.append(hre.reshape(nb, S5_GROUPS, S5_STATE))
            o_s5_im.append(him.reshape(nb, S5_GROUPS, S5_STATE))
        elif kind == 1:
            w_in = p['lru_w_in'][j].astype(BF16).reshape(D_MODEL, 2, LRU_BLOCKS, LRU_BLOCK)
            w_in = w_in.transpose(2, 0, 1, 3).reshape(LRU_BLOCKS, D_MODEL, 2 * LRU_BLOCK)
            y, h_new, conv_new = _lru_core(
                x, g_mix, w_in, _to_time_major(lru_conv[j]), lru_h[j],
                p['lru_conv_w'][j], p['lru_conv_b'][j].reshape(1, LRU_WIDTH),
                p['lru_w_gate_a'][j].astype(BF16), p['lru_b_gate_a'][j].reshape(1, LRU_WIDTH),
                p['lru_w_gate_x'][j].astype(BF16), p['lru_b_gate_x'][j].reshape(1, LRU_WIDTH),
                p['lru_lambda'][j].reshape(1, LRU_WIDTH), nb, rows)
            mix = (y, p['lru_w_out'][j].astype(BF16), "plain")
            o_lru.append(h_new)
            o_lru_conv.append(_from_time_major(conv_new, nb))
        else:
            w_in = p['gdn_w_in'][j]
            w_pad = jnp.pad(w_in, ((0, 0), (0, GDN_PROJ_PAD - w_in.shape[1]))).astype(BF16)
            w_pad = w_pad.reshape(D_MODEL, GDN_PROJ_PAD // 256, 256).transpose(1, 0, 2)
            pad8 = lambda a: jnp.pad(a.reshape(1, GDN_HEADS), ((0, 0), (0, 128 - GDN_HEADS)))
            chunk = GDN_CHUNK if seq >= GDN_CHUNK else 8
            batch_major = nb == 8 and seq % chunk == 0
            *qkvzgb, conv_new = _gdn_prep(x, g_mix, w_pad, _to_time_major(gdn_conv[j]), p['gdn_conv_w'][j],
                                          pad8(p['gdn_a_log'][j]), pad8(p['gdn_dt_bias'][j]), nb, rows,
                                          batch_major)
            nw = p['gdn_norm'][j].reshape(1, GDN_DV)
            w_out = p['gdn_w_out'][j].astype(BF16)
            if batch_major:
                o, s_new = _gdn_core(*qkvzgb, gdn_s[j], nw, chunk, 4)
                mix = (o, w_out, "seq_major")
            else:
                lp = -(-seq // chunk) * chunk

                def bm(a):
                    a = a.reshape(seq, nb, a.shape[1]).transpose(1, 0, 2)
                    return jnp.pad(a, ((0, 0), (0, lp - seq), (0, 0)))

                o, s_new = _gdn_core(*[bm(a) for a in qkvzgb], gdn_s[j], nw, chunk, 8)
                o = o[:, :seq].transpose(1, 0, 2).reshape(total, GDN_KEY_DIM).astype(BF16)
                mix = (o, w_out, "plain")
            o_gdn.append(s_new)
            o_gdn_conv.append(_from_time_major(conv_new, nb))
        x, ca, cb = _ffn(x, *mix, p['norm_ffn'][i].reshape(1, D_MODEL), p['ffn_w_up'].astype(BF16),
                         p['ffn_conv_w'][i], p['ffn_conv_b'][i].reshape(1, 2 * FFN_HIDDEN),
                         _to_time_major(ffn_conv[i]), p['ffn_w_down'].astype(BF16), i,
                         p['norm_final'].reshape(1, D_MODEL), nb, tm, i == depth - 1,
                         seq_major_io and i == 0, seq_major_io and i == depth - 1)
        hist = (FFN_CONV_WIDTH - 1) * nb
        o_ffn_conv.append(_from_time_major(jnp.concatenate([ca[-hist:], cb[-hist:]], axis=1), nb))
    return (x, jnp.stack(o_s5_re), jnp.stack(o_s5_im), jnp.stack(o_lru), jnp.stack(o_lru_conv),
            jnp.stack(o_gdn), jnp.stack(o_gdn_conv), jnp.stack(o_ffn_conv))


def kernel(x_prompt, x_sample, state_s5_re, state_s5_im, state_lru, state_lru_conv, state_gdn, state_gdn_conv, state_ffn_conv, norm_mix, norm_ffn, norm_final, s5_w_in, s5_a_re, s5_a_im, s5_log_dt, s5_b_re, s5_b_im, s5_c_re, s5_c_im, s5_d, s5_w_glu, lru_w_in, lru_conv_w, lru_conv_b, lru_w_gate_a, lru_b_gate_a, lru_w_gate_x, lru_b_gate_x, lru_lambda, lru_w_out, gdn_w_in, gdn_conv_w, gdn_a_log, gdn_dt_bias, gdn_norm, gdn_w_out, ffn_w_up, ffn_conv_w, ffn_conv_b, ffn_w_down):
    p = dict(norm_mix=norm_mix, norm_ffn=norm_ffn, norm_final=norm_final, s5_w_in=s5_w_in, s5_a_re=s5_a_re,
             s5_a_im=s5_a_im, s5_log_dt=s5_log_dt, s5_b_re=s5_b_re, s5_b_im=s5_b_im, s5_c_re=s5_c_re,
             s5_c_im=s5_c_im, s5_d=s5_d, s5_w_glu=s5_w_glu, lru_w_in=lru_w_in, lru_conv_w=lru_conv_w,
             lru_conv_b=lru_conv_b, lru_w_gate_a=lru_w_gate_a, lru_b_gate_a=lru_b_gate_a,
             lru_w_gate_x=lru_w_gate_x, lru_b_gate_x=lru_b_gate_x, lru_lambda=lru_lambda, lru_w_out=lru_w_out,
             gdn_w_in=gdn_w_in, gdn_conv_w=gdn_conv_w, gdn_a_log=gdn_a_log, gdn_dt_bias=gdn_dt_bias,
             gdn_norm=gdn_norm, gdn_w_out=gdn_w_out, ffn_w_up=ffn_w_up, ffn_conv_w=ffn_conv_w,
             ffn_conv_b=ffn_conv_b, ffn_w_down=ffn_w_down)
    p['s5_disc'] = [_s5_params(s5_a_re[j], s5_a_im[j], s5_log_dt[j], s5_b_re[j], s5_b_im[j], s5_c_re[j],
                               s5_c_im[j]) for j in range(s5_a_re.shape[0])]
    outs = []
    for x, states in (
            (x_prompt, None),
            (x_sample, (state_s5_re, state_s5_im, state_lru, state_lru_conv, state_gdn, state_gdn_conv,
                        state_ffn_conv))):
        nb, seq, _ = x.shape
        if states is None:
            states = tuple(jnp.zeros((s.shape[0], nb) + s.shape[2:], F32) for s in (
                state_s5_re, state_s5_im, state_lru, state_lru_conv, state_gdn, state_gdn_conv, state_ffn_conv))
        seq_major_io = nb == 8 and seq % 128 == 0
        if seq_major_io:
            res = _trunk(x, nb, seq, *states, p, True)
            outs.append(tuple(res))
        else:
            res = _trunk(_to_time_major(x), nb, seq, *states, p, False)
            outs.append((_from_time_major(res[0], nb),) + tuple(res[1:]))
    (y_p, *st_p), (y_s, *st_s) = outs
    return (y_p, y_s, *st_p, *st_s)
```

```python
import functools
import math

import jax
import jax.numpy as jnp
from jax import lax
from jax.experimental import pallas as pl
from jax.experimental.pallas import tpu as pltpu

F32 = jnp.float32
BF16 = jnp.bfloat16

D_MODEL = 1024
RMS_EPS = 1e-6
L2_EPS = 1e-6
S5_GROUPS = 64
S5_STATE = 64
S5_GROUP_CH = 16
S5_COLS = S5_GROUPS * S5_STATE
S5_KB = 8
S5_SCAN_LANES = 1024
LRU_WIDTH = 1280
LRU_BLOCK = 128
LRU_BLOCKS = LRU_WIDTH // LRU_BLOCK
LRU_C = 8.0
CONV_WIDTH = 4
GDN_HEADS = 8
GDN_DK = 128
GDN_DV = 128
GDN_KEY_DIM = GDN_HEADS * GDN_DK
GDN_CONV_DIM = 3 * GDN_KEY_DIM
GDN_CHUNK = 64
GDN_PROJ_PAD = 4352
FFN_HIDDEN = 2816
FFN_CONV_WIDTH = 3
FFN_TN = 256
SUB_ROWS = 256
VMEM_LIMIT_BYTES = 56 * 1024 * 1024


def _cparams(sem):
    return pltpu.CompilerParams(dimension_semantics=sem, vmem_limit_bytes=VMEM_LIMIT_BYTES)


def _rms(x, g):
    ms = jnp.mean(x * x, axis=-1, keepdims=True)
    return x * lax.rsqrt(ms + RMS_EPS) * g


def _softplus(x):
    return jnp.maximum(x, 0.0) + jnp.log1p(jnp.exp(-jnp.abs(x)))


def _stage_time_major(src_ref, slab_ref, nb):
    steps = src_ref.shape[1]
    for b in range(nb):
        for kt in range(src_ref.shape[2] // 128):
            slab_ref[kt, pl.ds(b, steps, stride=nb), :] = src_ref[b, :, kt * 128:(kt + 1) * 128]


def _gelu(x):
    c = math.sqrt(2.0 / math.pi)
    half = 0.5 * x
    return half + half * jnp.tanh(x * (c + (c * 0.044715) * (x * x)))


def _dot(a, b):
    return jnp.dot(a.astype(BF16), b.astype(BF16), preferred_element_type=F32)


def _dot_nt(a, b):
    return lax.dot_general(a.astype(BF16), b.astype(BF16), (((1,), (1,)), ((), ())),
                           preferred_element_type=F32)


def _split3(a):
    hi = a.astype(BF16)
    r = a - hi.astype(F32)
    mid = r.astype(BF16)
    lo = (r - mid.astype(F32)).astype(BF16)
    return hi, mid, lo


def _s5_core_kernel(x_ref, g_ref, win_ref, h0re_ref, h0im_ref, bre_ref, bim_ref, cre_ref, cim_ref, are_ref,
                    aim_ref, d_ref, y_ref, hre_out, him_out, xn_s, u_ref, hre_s, him_s, *stage,
                    nb, rows, seq_major_in):
    i = pl.program_id(0)

    @pl.when(i == 0)
    def _():
        hre_s[0:nb, :] = h0re_ref[...]
        him_s[0:nb, :] = h0im_ref[...]

    kw = S5_COLS // S5_KB
    uw = D_MODEL // S5_KB
    per_grp = S5_SCAN_LANES // kw
    n_grp = S5_COLS // S5_SCAN_LANES
    if seq_major_in:
        x_st, = stage
        _stage_time_major(x_ref, x_st, nb)
        x = jnp.concatenate([x_st[kt] for kt in range(D_MODEL // 128)], axis=1)
    else:
        x = x_ref[...]
    xn_s[...] = _rms(x, g_ref[...]).astype(BF16)

    def project_in(grp):
        ucols = slice(grp * per_grp * uw, (grp + 1) * per_grp * uw)
        u_ref[:, ucols] = jnp.dot(xn_s[...], win_ref[:, ucols], preferred_element_type=F32)
        for kb in range(grp * per_grp, (grp + 1) * per_grp):
            ukb = u_ref[:, kb * uw:(kb + 1) * uw].astype(BF16)
            hre_s[nb:nb + rows, kb * kw:(kb + 1) * kw] = jnp.dot(ukb, bre_ref[kb], preferred_element_type=F32)
            him_s[nb:nb + rows, kb * kw:(kb + 1) * kw] = jnp.dot(ukb, bim_ref[kb], preferred_element_type=F32)

    def scan(grp):
        cols = slice(grp * S5_SCAN_LANES, (grp + 1) * S5_SCAN_LANES)
        are = jnp.broadcast_to(are_ref[:, cols], (nb, S5_SCAN_LANES))
        aim = jnp.broadcast_to(aim_ref[:, cols], (nb, S5_SCAN_LANES))
        hr, hi = hre_s[0:nb, cols], him_s[0:nb, cols]
        for t in range(rows // nb):
            r = slice(nb + t * nb, 2 * nb + t * nb)
            hr, hi = (are * hr - aim * hi + hre_s[r, cols],
                      are * hi + aim * hr + him_s[r, cols])
            hre_s[r, cols] = hr
            him_s[r, cols] = hi

    def project_out(grp):
        for kb in range(grp * per_grp, (grp + 1) * per_grp):
            hr = hre_s[nb:nb + rows, kb * kw:(kb + 1) * kw].astype(BF16)
            hi = him_s[nb:nb + rows, kb * kw:(kb + 1) * kw].astype(BF16)
            yk = (jnp.dot(hr, cre_ref[kb], preferred_element_type=F32)
                  - jnp.dot(hi, cim_ref[kb], preferred_element_type=F32))
            yk = yk + d_ref[:, kb * uw:(kb + 1) * uw] * u_ref[:, kb * uw:(kb + 1) * uw]
            y_ref[:, kb * uw:(kb + 1) * uw] = _gelu(yk).astype(BF16)

    project_in(0)
    for grp in range(n_grp):
        if grp + 1 < n_grp:
            project_in(grp + 1)
        scan(grp)
        project_out(grp)

    last_re = hre_s[rows:rows + nb, :]
    last_im = him_s[rows:rows + nb, :]
    hre_s[0:nb, :] = last_re
    him_s[0:nb, :] = last_im
    hre_out[...] = last_re
    him_out[...] = last_im


def _s5_core(x, g, w_in, h0re, h0im, bre, bim, cre, cim, are, aim, d, nb, rows, seq_major_in=False):
    full = lambda shape: pl.BlockSpec(shape, lambda i: (0,) * len(shape))
    if seq_major_in:
        total = x.shape[0] * x.shape[1]
        x_spec = pl.BlockSpec((nb, rows // nb, D_MODEL), lambda i: (0, i, 0))
        stage = [pltpu.VMEM((D_MODEL // 128, rows, 128), F32)]
    else:
        total = x.shape[0]
        x_spec = pl.BlockSpec((rows, D_MODEL), lambda i: (i, 0))
        stage = []
    return pl.pallas_call(
        functools.partial(_s5_core_kernel, nb=nb, rows=rows, seq_major_in=seq_major_in),
        grid=(total // rows,),
        in_specs=[x_spec, full((1, D_MODEL)), full(w_in.shape),
                  full((nb, S5_COLS)), full((nb, S5_COLS)),
                  full(bre.shape), full(bim.shape), full(cre.shape), full(cim.shape),
                  full((1, S5_COLS)), full((1, S5_COLS)), full((1, D_MODEL))],
        out_specs=[pl.BlockSpec((rows, D_MODEL), lambda i: (i, 0)),
                   full((nb, S5_COLS)), full((nb, S5_COLS))],
        out_shape=[jax.ShapeDtypeStruct((total, D_MODEL), BF16),
                   jax.ShapeDtypeStruct((nb, S5_COLS), F32),
                   jax.ShapeDtypeStruct((nb, S5_COLS), F32)],
        scratch_shapes=[pltpu.VMEM((rows, D_MODEL), BF16), pltpu.VMEM((rows, D_MODEL), F32),
                        pltpu.VMEM((nb + rows, S5_COLS), F32),
                        pltpu.VMEM((nb + rows, S5_COLS), F32)] + stage,
        compiler_params=_cparams(("arbitrary",)),
        name="s5_core",
    )(x, g, w_in, h0re, h0im, bre, bim, cre, cim, are, aim, d)


def _lru_core_kernel(x_ref, g_ref, win_ref, prev_ref, h0_ref, cw_ref, cb_ref, wga_ref, bga_ref, wgx_ref,
                     bgx_ref, lam_ref, y_ref, hout_ref, cout_ref, xn_s, gate_s, xp_s, h_s, a_s, *, nb, rows):
    i = pl.program_id(0)
    hist = (CONV_WIDTH - 1) * nb

    @pl.when(i == 0)
    def _():
        xp_s[0:hist, :] = prev_ref[...]
        h_s[0:nb, :] = h0_ref[...]

    xn_s[...] = _rms(x_ref[...], g_ref[...]).astype(BF16)
    c8 = -LRU_C * _softplus(-lam_ref[...])

    def project(n):
        pg = jnp.dot(xn_s[...], win_ref[n], preferred_element_type=F32)
        sl = slice(n * LRU_BLOCK, (n + 1) * LRU_BLOCK)
        gate_s[:, sl] = pg[:, :LRU_BLOCK]
        xp_s[hist:hist + rows, sl] = pg[:, LRU_BLOCK:]

    def gates(n):
        sl = slice(n * LRU_BLOCK, (n + 1) * LRU_BLOCK)
        xcn = xp_s[0:rows, sl] * cw_ref[0:1, sl]
        for k in range(1, CONV_WIDTH):
            xcn = xcn + xp_s[k * nb:k * nb + rows, sl] * cw_ref[k:k + 1, sl]
        xcn = xcn + cb_ref[:, sl]
        xcb = xcn.astype(BF16)
        r = jax.nn.sigmoid(jnp.dot(xcb, wga_ref[n], preferred_element_type=F32) + bga_ref[:, sl])
        ig = jax.nn.sigmoid(jnp.dot(xcb, wgx_ref[n], preferred_element_type=F32) + bgx_ref[:, sl])
        log_a = c8[:, sl] * r
        a_s[:, sl] = jnp.exp(log_a)
        t = jnp.tanh(log_a)
        h_s[nb:nb + rows, sl] = jnp.sqrt(-2.0 * t / (1.0 - t)) * ig * xcn

    project(0)
    for n in range(LRU_BLOCKS):
        if n + 1 < LRU_BLOCKS:
            project(n + 1)
        gates(n)

    def step(t, carry):
        r0 = pl.multiple_of(t * nb, nb)
        r1 = pl.multiple_of(t * nb + nb, nb)
        h_s[pl.ds(r1, nb), :] = a_s[pl.ds(r0, nb), :] * h_s[pl.ds(r0, nb), :] + h_s[pl.ds(r1, nb), :]
        return carry

    lax.fori_loop(0, rows // nb, step, 0)

    y_ref[...] = (_gelu(gate_s[...]) * h_s[nb:nb + rows, :]).astype(BF16)
    tail = xp_s[rows:rows + hist, :]
    last = h_s[rows:rows + nb, :]
    xp_s[0:hist, :] = tail
    h_s[0:nb, :] = last
    cout_ref[...] = tail
    hout_ref[...] = last


def _lru_core(x, g, w_in, prev, h0, cw, cb, wga, bga, wgx, bgx, lam, nb, rows):
    total = x.shape[0]
    hist = (CONV_WIDTH - 1) * nb
    full = lambda shape: pl.BlockSpec(shape, lambda i: (0,) * len(shape))
    return pl.pallas_call(
        functools.partial(_lru_core_kernel, nb=nb, rows=rows),
        grid=(total // rows,),
        in_specs=[pl.BlockSpec((rows, D_MODEL), lambda i: (i, 0)), full((1, D_MODEL)),
                  pl.BlockSpec(w_in.shape, lambda i: (0, 0, 0), pipeline_mode=pl.Buffered(1)),
                  full((hist, LRU_WIDTH)), full((nb, LRU_WIDTH)),
                  full((CONV_WIDTH, LRU_WIDTH)), full((1, LRU_WIDTH)),
                  full(wga.shape), full((1, LRU_WIDTH)), full(wgx.shape), full((1, LRU_WIDTH)),
                  full((1, LRU_WIDTH))],
        out_specs=[pl.BlockSpec((rows, LRU_WIDTH), lambda i: (i, 0)),
                   full((nb, LRU_WIDTH)), full((hist, LRU_WIDTH))],
        out_shape=[jax.ShapeDtypeStruct((total, LRU_WIDTH), BF16),
                   jax.ShapeDtypeStruct((nb, LRU_WIDTH), F32),
                   jax.ShapeDtypeStruct((hist, LRU_WIDTH), F32)],
        scratch_shapes=[pltpu.VMEM((rows, D_MODEL), BF16),
                        pltpu.VMEM((rows, LRU_WIDTH), F32),
                        pltpu.VMEM((hist + rows, LRU_WIDTH), F32),
                        pltpu.VMEM((nb + rows, LRU_WIDTH), F32),
                        pltpu.VMEM((rows, LRU_WIDTH), F32)],
        compiler_params=_cparams(("arbitrary",)),
        name="lru_core",
    )(x, g, w_in, prev, h0, cw, cb, wga, bga, wgx, bgx, lam)


def _gdn_prep_kernel(x_ref, gn_ref, win_ref, prev_ref, cw_ref, alog_ref, dtb_ref,
                     q_ref, k_ref, v_ref, zo_ref, g_ref, beta_ref, cout_ref, xn_s, xp_s, ab_s, st_s,
                     *, nb, rows, batch_major):
    i = pl.program_id(0)
    hist = (CONV_WIDTH - 1) * nb
    n_qkv = 3 * GDN_HEADS
    n_z = GDN_HEADS

    @pl.when(i == 0)
    def _():
        xp_s[0:hist, :] = prev_ref[...]

    xn_s[...] = _rms(x_ref[...], gn_ref[...]).astype(BF16)

    def project(p):
        pg = jnp.dot(xn_s[...], win_ref[p], preferred_element_type=F32)
        for half in range(2):
            s = 2 * p + half
            col = pg[:, half * 128:(half + 1) * 128]
            if s < n_qkv:
                xp_s[hist:hist + rows, s * 128:(s + 1) * 128] = col
            elif s < n_qkv + n_z:
                if batch_major:
                    st_s[s] = col
                else:
                    zo_ref[:, (s - n_qkv) * 128:(s - n_qkv + 1) * 128] = col
            elif s == n_qkv + n_z:
                ab_s[...] = col

    def activate(s):
        part, h = divmod(s, GDN_HEADS)
        sl = slice(s * 128, (s + 1) * 128)
        acc = xp_s[0:rows, sl] * cw_ref[0:1, sl]
        for k in range(1, CONV_WIDTH):
            acc = acc + xp_s[k * nb:k * nb + rows, sl] * cw_ref[k:k + 1, sl]
        y = acc * jax.nn.sigmoid(acc)
        if part < 2:
            y = y * lax.rsqrt(jnp.sum(y * y, axis=-1, keepdims=True) + L2_EPS)
        if part == 0:
            y = y * (GDN_DK ** -0.5)
        if batch_major:
            st_s[s] = y
        else:
            (q_ref, k_ref, v_ref)[part][:, h * GDN_DK:(h + 1) * GDN_DK] = y

    n_pairs = win_ref.shape[0]
    project(0)
    for p in range(n_pairs):
        if p + 1 < n_pairs:
            project(p + 1)
        for s in (2 * p, 2 * p + 1):
            if s < n_qkv:
                activate(s)

    ab = ab_s[...]
    g = -jnp.exp(alog_ref[...]) * _softplus(ab + dtb_ref[...])
    beta = jax.nn.sigmoid(ab)
    if batch_major:
        st_s[4 * GDN_HEADS] = g
        st_s[4 * GDN_HEADS + 1] = beta
        steps = rows // nb
        for b in range(nb):
            pick = pl.ds(b, steps, stride=nb)
            for part, out in enumerate((q_ref, k_ref, v_ref, zo_ref)):
                for h in range(GDN_HEADS):
                    out[b, :, h * GDN_DK:(h + 1) * GDN_DK] = st_s[part * GDN_HEADS + h, pick, :]
            g_ref[b] = st_s[4 * GDN_HEADS, pick, :]
            beta_ref[b] = st_s[4 * GDN_HEADS + 1, pick, :]
    else:
        g_ref[...] = g
        beta_ref[...] = beta
    tail = xp_s[rows:rows + hist, :]
    xp_s[0:hist, :] = tail
    cout_ref[...] = tail


def _gdn_prep(x, gn, w_in, prev, cw, alog, dtb, nb, rows, batch_major):
    total = x.shape[0]
    hist = (CONV_WIDTH - 1) * nb
    full = lambda shape: pl.BlockSpec(shape, lambda i: (0,) * len(shape))
    tile = lambda n: pl.BlockSpec((rows, n), lambda i: (i, 0))
    if batch_major:
        out_tile = lambda n: pl.BlockSpec((nb, rows // nb, n), lambda i: (0, i, 0))
        out_sds = lambda n: jax.ShapeDtypeStruct((nb, total // nb, n), F32)
    else:
        out_tile = tile
        out_sds = lambda n: jax.ShapeDtypeStruct((total, n), F32)
    widths = (GDN_KEY_DIM,) * 4 + (128, 128)
    return pl.pallas_call(
        functools.partial(_gdn_prep_kernel, nb=nb, rows=rows, batch_major=batch_major),
        grid=(total // rows,),
        in_specs=[tile(D_MODEL), full((1, D_MODEL)),
                  pl.BlockSpec(w_in.shape, lambda i: (0, 0, 0), pipeline_mode=pl.Buffered(1)),
                  full((hist, GDN_CONV_DIM)), full((CONV_WIDTH, GDN_CONV_DIM)),
                  full((1, 128)), full((1, 128))],
        out_specs=[out_tile(n) for n in widths] + [full((hist, GDN_CONV_DIM))],
        out_shape=[out_sds(n) for n in widths] + [jax.ShapeDtypeStruct((hist, GDN_CONV_DIM), F32)],
        scratch_shapes=[pltpu.VMEM((rows, D_MODEL), BF16),
                        pltpu.VMEM((hist + rows, GDN_CONV_DIM), F32),
                        pltpu.VMEM((rows, 128), F32),
                        pltpu.VMEM((4 * GDN_HEADS + 2, rows if batch_major else 8, 128), F32)],
        compiler_params=_cparams(("arbitrary",)),
        name="gdn_prep",
    )(x, gn, w_in, prev, cw, alog, dtb)


def _unit_lower_inverses(ms, ri, ci, chunk):
    eye = (ri == ci).astype(F32)
    blk = (ri >> 3) == (ci >> 3)
    n1 = [jnp.where(blk, -m, 0.0) for m in ms]
    n2 = [_dot(a, a) for a in n1]
    n4 = [_dot(a, a) for a in n2]
    ts = [_dot(eye + a, eye + b) for a, b in zip(n1, n2)]
    ts = [_dot(t, eye + a) for t, a in zip(ts, n4)]
    shift = 3
    while (1 << shift) < chunk:
        pair = ((ri >> (shift + 1)) == (ci >> (shift + 1))) & ((ri >> shift) != (ci >> shift))
        left = [_dot(t, jnp.where(pair, m, 0.0)) for t, m in zip(ts, ms)]
        ts = [t - _dot(a, t) for t, a in zip(ts, left)]
        shift += 1
    return ts


def _gdn_core_kernel(q_ref, k_ref, v_ref, z_ref, g_ref, beta_ref, s0_ref, nw_ref, o_ref, s_ref, *, chunk, bb):
    c = pl.program_id(1)

    @pl.when(c == 0)
    def _():
        s_ref[...] = s0_ref[...]

    ri = lax.broadcasted_iota(jnp.int32, (chunk, chunk), 0)
    ci = lax.broadcasted_iota(jnp.int32, (chunk, chunk), 1)
    causal = ri >= ci
    strict = ri > ci
    tril = causal.astype(BF16)
    e_r = lax.broadcasted_iota(jnp.int32, (128, 128), 0)
    e_c = lax.broadcasted_iota(jnp.int32, (128, 128), 1)
    eye128 = (e_r == e_c).astype(BF16)
    dotf = functools.partial(jnp.dot, preferred_element_type=F32)
    nt = lambda a, b: lax.dot_general(a, b, (((1,), (1,)), ((), ())), preferred_element_type=F32)
    nw = nw_ref[...]

    cums, cum_ts, ecums, e_lasts, e_rests, betas = [], [], [], [], [], []
    for bi in range(bb):
        g3 = _split3(g_ref[bi])
        cum = dotf(tril, g3[0]) + dotf(tril, g3[1]) + dotf(tril, g3[2])
        c3 = _split3(cum)
        cums.append(cum)
        cum_ts.append(nt(eye128, c3[0]) + nt(eye128, c3[1]) + nt(eye128, c3[2]))
        ecums.append(jnp.exp(cum))
        g_last = cum[chunk - 1:chunk, :]
        e_lasts.append(jnp.exp(g_last))
        e_rests.append(jnp.exp(g_last - cum))
        betas.append(beta_ref[bi])

    units = [(bi, h) for bi in range(bb) for h in range(GDN_HEADS)]
    col = lambda a, h: a[:, h:h + 1]
    sl = lambda h: slice(h * GDN_DK, (h + 1) * GDN_DK)
    q = [q_ref[bi, :, sl(h)] for bi, h in units]
    k = [k_ref[bi, :, sl(h)] for bi, h in units]
    decay = [jnp.exp(jnp.where(causal, col(cums[bi], h) - cum_ts[bi][h:h + 1, :], -jnp.inf)) for bi, h in units]
    k_beta = [kk * col(betas[bi], GDN_HEADS + h) for kk, (bi, h) in zip(k, units)]
    ak = [_dot_nt(jnp.concatenate([kb, qq], axis=0), kk) for kb, qq, kk in zip(k_beta, q, k)]
    ms = [jnp.where(strict, a[:chunk] * d, 0.0) for a, d in zip(ak, decay)]
    ts = _unit_lower_inverses(ms, ri, ci, chunk)
    rhs = [jnp.concatenate([v_ref[bi, :, sl(h)] * col(betas[bi], GDN_HEADS + h), kb * col(ecums[bi], h)], axis=1)
           for kb, (bi, h) in zip(k_beta, units)]
    sol = [_dot(t, r) for t, r in zip(ts, rhs)]
    s_old = [s_ref[bi, h] for bi, h in units]
    ws = [_dot(jnp.concatenate([so[:, GDN_DV:], qq * col(ecums[bi], h)], axis=0), s)
          for so, qq, s, (bi, h) in zip(sol, q, s_old, units)]
    v_new = [so[:, :GDN_DV] - w[:chunk] for so, w in zip(sol, ws)]
    o = [w[chunk:] + _dot(a[chunk:] * d, vn) for w, a, d, vn in zip(ws, ak, decay, v_new)]
    k_dec_t = [nt(eye128, (kk * col(e_rests[bi], h)).astype(BF16)) for kk, (bi, h) in zip(k, units)]
    for (bi, h), s, kt, vn, oo in zip(units, s_old, k_dec_t, v_new, o):
        s_ref[bi, h] = s * col(e_lasts[bi], h) + _dot(kt, vn)
        on = oo * lax.rsqrt(jnp.mean(oo * oo, axis=-1, keepdims=True) + RMS_EPS) * nw
        zh = z_ref[bi, :, sl(h)]
        o_ref[bi, :, sl(h)] = on * (zh * jax.nn.sigmoid(zh))


def _gdn_core(q, k, v, z, g, beta, s0, nw, chunk, bb):
    nb, lp = q.shape[0], q.shape[1]
    seq = lambda n: pl.BlockSpec((bb, chunk, n), lambda b, c: (b, c, 0))
    st = pl.BlockSpec((bb, GDN_HEADS, GDN_DK, GDN_DV), lambda b, c: (b, 0, 0, 0))
    return pl.pallas_call(
        functools.partial(_gdn_core_kernel, chunk=chunk, bb=bb),
        grid=(nb // bb, lp // chunk),
        in_specs=[seq(GDN_KEY_DIM), seq(GDN_KEY_DIM), seq(GDN_KEY_DIM), seq(GDN_KEY_DIM),
                  seq(128), seq(128), st, pl.BlockSpec((1, GDN_DV), lambda b, c: (0, 0))],
        out_specs=[seq(GDN_KEY_DIM), st],
        out_shape=[jax.ShapeDtypeStruct((nb, lp, GDN_KEY_DIM), F32),
                   jax.ShapeDtypeStruct(s0.shape, F32)],
        compiler_params=_cparams(("parallel", "arbitrary")),
        name="gdn_core",
    )(q, k, v, z, g, beta, s0, nw)


def _ffn_kernel(r_ref, a_ref, wo_ref, g_ref, wa_ref, wb_ref, cwa_ref, cwb_ref, cba_ref, cbb_ref, pa_ref, pb_ref,
                wd_ref, gf_ref, o_ref, ca_out, cb_out, x_s, xn_s, acc_s, hpa_s, hpb_s, cara_s, carb_s, *stage,
                nb, tm, final_norm, mixer_out, res_seq_major, out_seq_major):
    i = pl.program_id(0)
    j = pl.program_id(1)
    hist = (FFN_CONV_WIDTH - 1) * nb
    rs = min(SUB_ROWS, tm)
    nsub = tm // rs
    sub = lambda r: slice(r * rs, (r + 1) * rs)
    stage = list(stage)
    a_s = stage.pop(0) if mixer_out == "seq_major" else None
    io_s = stage.pop(0) if (res_seq_major or out_seq_major) else None
    nk = D_MODEL // 128

    @pl.when(j == 0)
    def _():
        if mixer_out == "seq_major":
            _stage_time_major(a_ref, a_s, nb)
        if res_seq_major:
            _stage_time_major(r_ref, io_s, nb)

        def project(r):
            if mixer_out == "seq_major":
                a = jnp.concatenate([a_s[kt, sub(r), :].astype(BF16) for kt in range(nk)], axis=1)
            else:
                a = a_ref[sub(r), :]
            if res_seq_major:
                res = jnp.concatenate([io_s[kt, sub(r), :] for kt in range(nk)], axis=1)
            else:
                res = r_ref[sub(r), :]
            if mixer_out == "glu":
                half = wo_ref.shape[1] // 2
                val = jnp.dot(a, wo_ref[:, :half], preferred_element_type=F32)
                gate = jnp.dot(a, wo_ref[:, half:], preferred_element_type=F32)
                y = val * jax.nn.sigmoid(gate)
            else:
                y = jnp.dot(a, wo_ref[...], preferred_element_type=F32)
            x_s[sub(r), :] = res + y

        project(0)
        for r in range(nsub):
            if r + 1 < nsub:
                project(r + 1)
            xn_s[sub(r), :] = _rms(x_s[sub(r), :], g_ref[...]).astype(BF16)
        acc_s[...] = jnp.zeros_like(acc_s)

    @pl.when(i == 0)
    def _():
        hpa_s[0:hist, :] = pa_ref[...]
        hpb_s[0:hist, :] = pb_ref[...]

    @pl.when(i > 0)
    def _():
        hpa_s[0:hist, :] = cara_s[j]
        hpb_s[0:hist, :] = carb_s[j]

    def up(r):
        xr = xn_s[r * rs:(r + 1) * rs, :]
        hpa_s[hist + r * rs:hist + (r + 1) * rs, :] = jnp.dot(xr, wa_ref[...], preferred_element_type=F32)
        hpb_s[hist + r * rs:hist + (r + 1) * rs, :] = jnp.dot(xr, wb_ref[...], preferred_element_type=F32)

    def conv(hp_s, cw_ref, cb_ref, r):
        y = hp_s[r * rs:(r + 1) * rs, :] * cw_ref[0:1, :]
        for k in range(1, FFN_CONV_WIDTH):
            y = y + hp_s[k * nb + r * rs:k * nb + (r + 1) * rs, :] * cw_ref[k:k + 1, :]
        return y + cb_ref[...]

    def down(r):
        act = (_gelu(conv(hpa_s, cwa_ref, cba_ref, r)) * conv(hpb_s, cwb_ref, cbb_ref, r)).astype(BF16)
        acc_s[r * rs:(r + 1) * rs, :] += jnp.dot(act, wd_ref[...], preferred_element_type=F32)

    up(0)
    for r in range(nsub):
        if r + 1 < nsub:
            up(r + 1)
        down(r)

    tail_a = hpa_s[tm:tm + hist, :]
    tail_b = hpb_s[tm:tm + hist, :]
    cara_s[j] = tail_a
    carb_s[j] = tail_b
    ca_out[...] = tail_a
    cb_out[...] = tail_b

    @pl.when(j == pl.num_programs(1) - 1)
    def _():
        y = x_s[...] + acc_s[...]
        if final_norm:
            y = _rms(y, gf_ref[...])
        if out_seq_major:
            for kt in range(nk):
                io_s[kt] = y[:, kt * 128:(kt + 1) * 128]
            for b in range(nb):
                for kt in range(nk):
                    o_ref[b, :, kt * 128:(kt + 1) * 128] = io_s[kt, pl.ds(b, tm // nb, stride=nb), :]
        else:
            o_ref[...] = y


def _ffn(res, a, w_out, mixer_out, g, w_up, cw, cb, prev, w_down, layer, g_final, nb, tm, final_norm,
         res_seq_major=False, out_seq_major=False):
    rows = res.shape[0] * res.shape[1] if res_seq_major else res.shape[0]
    row_tile = pl.BlockSpec((tm, D_MODEL), lambda i, j: (i, 0))
    seq_tile = pl.BlockSpec((nb, tm // nb, D_MODEL), lambda i, j: (0, i, 0))
    tn = FFN_TN
    nj = FFN_HIDDEN // tn
    hist = (FFN_CONV_WIDTH - 1) * nb
    col_a = lambda r: pl.BlockSpec((r, tn), lambda i, j: (0, j))
    col_b = lambda r: pl.BlockSpec((r, tn), lambda i, j: (0, nj + j))
    vec = pl.BlockSpec((1, D_MODEL), lambda i, j: (0, 0))
    k = w_out.shape[0]
    if mixer_out == "seq_major":
        a_spec = pl.BlockSpec((nb, tm // nb, k), lambda i, j: (0, i, 0))
        stage = [pltpu.VMEM((k // 128, tm, 128), F32)]
    else:
        a_spec = pl.BlockSpec((tm, k), lambda i, j: (i, 0))
        stage = []
    if res_seq_major or out_seq_major:
        stage = stage + [pltpu.VMEM((D_MODEL // 128, tm, 128), F32)]
    out_sds = (jax.ShapeDtypeStruct((nb, rows // nb, D_MODEL), F32) if out_seq_major
               else jax.ShapeDtypeStruct((rows, D_MODEL), F32))
    return pl.pallas_call(
        functools.partial(_ffn_kernel, nb=nb, tm=tm, final_norm=final_norm, mixer_out=mixer_out,
                          res_seq_major=res_seq_major, out_seq_major=out_seq_major),
        grid=(rows // tm, nj),
        in_specs=[seq_tile if res_seq_major else row_tile, a_spec,
                  pl.BlockSpec(w_out.shape, lambda i, j: (0, 0)), vec,
                  pl.BlockSpec((None, None, D_MODEL, tn), lambda i, j: (layer, j, 0, 0)),
                  pl.BlockSpec((None, None, D_MODEL, tn), lambda i, j: (layer, nj + j, 0, 0)),
                  col_a(FFN_CONV_WIDTH), col_b(FFN_CONV_WIDTH), col_a(1), col_b(1),
                  col_a(hist), col_b(hist),
                  pl.BlockSpec((None, tn, D_MODEL), lambda i, j: (layer, j, 0)), vec],
        out_specs=[seq_tile if out_seq_major else row_tile,
                   pl.BlockSpec((hist, tn), lambda i, j: (i, j)),
                   pl.BlockSpec((hist, tn), lambda i, j: (i, j))],
        out_shape=[out_sds,
                   jax.ShapeDtypeStruct((rows // tm * hist, FFN_HIDDEN), F32),
                   jax.ShapeDtypeStruct((rows // tm * hist, FFN_HIDDEN), F32)],
        scratch_shapes=[pltpu.VMEM((tm, D_MODEL), F32), pltpu.VMEM((tm, D_MODEL), BF16),
                        pltpu.VMEM((tm, D_MODEL), F32),
                        pltpu.VMEM((hist + tm, tn), F32), pltpu.VMEM((hist + tm, tn), F32),
                        pltpu.VMEM((nj, hist, tn), F32), pltpu.VMEM((nj, hist, tn), F32)] + stage,
        compiler_params=_cparams(("arbitrary", "arbitrary")),
        name="conv_ffn",
    )(res, a, w_out, g, w_up, w_up, cw, cw, cb, cb, prev, prev, w_down, g_final)


def _s5_disc_kernel(are_ref, aim_ref, ldt_ref, bre_ref, bim_ref, abr_ref, abi_ref, bbr_ref, bbi_ref):
    a_re, a_im = are_ref[...], aim_ref[...]
    dt = jnp.exp(ldt_ref[...])
    mag = jnp.exp(a_re * dt)
    ar = mag * jnp.cos(a_im * dt)
    ai = mag * jnp.sin(a_im * dt)
    den = a_re * a_re + a_im * a_im
    nr = ar - 1.0
    cr = (nr * a_re + ai * a_im) / den
    ci = (ai * a_re - nr * a_im) / den
    b_re, b_im = bre_ref[...], bim_ref[...]
    abr_ref[...] = ar
    abi_ref[...] = ai
    bbr_ref[...] = cr * b_re - ci * b_im
    bbi_ref[...] = cr * b_im + ci * b_re


def _s5_params(a_re, a_im, log_dt, b_re, b_im, c_re, c_im):
    rep = lambda a: jnp.repeat(a.astype(F32), S5_GROUP_CH, axis=0)
    rows_gc = lambda b: b.astype(F32).transpose(0, 2, 1).reshape(D_MODEL, S5_STATE)
    ldt = jnp.broadcast_to(log_dt.astype(F32)[:, None], (S5_GROUPS, S5_STATE))
    sds = jax.ShapeDtypeStruct((D_MODEL, S5_STATE), F32)
    abr, abi, bbr, bbi = pl.pallas_call(_s5_disc_kernel, out_shape=[sds] * 4, name="s5_discretize")(
        rep(a_re), rep(a_im), rep(ldt), rows_gc(b_re), rows_gc(b_im))
    eye = jnp.eye(S5_GROUPS // S5_KB, dtype=F32)

    def b_blocks(b):
        b = b.reshape(S5_KB, S5_GROUPS // S5_KB, S5_GROUP_CH, S5_STATE)
        return jnp.einsum('kgcp,gh->kgchp', b, eye).reshape(S5_KB, D_MODEL // S5_KB, S5_COLS // S5_KB).astype(BF16)

    def c_blocks(c):
        c = c.astype(F32).reshape(S5_KB, S5_GROUPS // S5_KB, S5_GROUP_CH, S5_STATE)
        return jnp.einsum('kgcp,gh->kgphc', c, eye).reshape(S5_KB, S5_COLS // S5_KB, D_MODEL // S5_KB).astype(BF16)

    return (b_blocks(bbr), b_blocks(bbi), c_blocks(c_re), c_blocks(c_im),
            abr[::S5_GROUP_CH].reshape(1, S5_COLS), abi[::S5_GROUP_CH].reshape(1, S5_COLS))


def _to_time_major(a):
    return a.transpose(1, 0, 2).reshape(a.shape[0] * a.shape[1], a.shape[2])


def _from_time_major(a, nb):
    return a.reshape(a.shape[0] // nb, nb, a.shape[1]).transpose(1, 0, 2)


def _trunk(x, nb, seq, s5_re, s5_im, lru_h, lru_conv, gdn_s, gdn_conv, ffn_conv, p, seq_major_io):
    total = seq * nb
    tm = min(total, 1024)
    rows = 512
    o_s5_re, o_s5_im, o_lru, o_lru_conv, o_gdn, o_gdn_conv, o_ffn_conv = [], [], [], [], [], [], []
    depth = p['norm_mix'].shape[0]
    for i in range(depth):
        kind, j = i % 3, i // 3
        g_mix = p['norm_mix'][i].reshape(1, D_MODEL)
        if kind == 0:
            bre, bim, cre, cim, are, aim = p['s5_disc'][j]
            y, hre, him = _s5_core(x, g_mix, p['s5_w_in'][j].astype(BF16),
                                   s5_re[j].reshape(nb, S5_COLS), s5_im[j].reshape(nb, S5_COLS),
                                   bre, bim, cre, cim, are, aim, p['s5_d'][j].reshape(1, D_MODEL), nb, rows,
                                   seq_major_io and i == 0)
            mix = (y, p['s5_w_glu'][j].astype(BF16), "glu")
            o_s5_re.append(hre.reshape(nb, S5_GROUPS, S5_STATE))
            o_s5_im.append(him.reshape(nb, S5_GROUPS, S5_STATE))
        elif kind == 1:
            w_in = p['lru_w_in'][j].astype(BF16).reshape(D_MODEL, 2, LRU_BLOCKS, LRU_BLOCK)
            w_in = w_in.transpose(2, 0, 1, 3).reshape(LRU_BLOCKS, D_MODEL, 2 * LRU_BLOCK)
            y, h_new, conv_new = _lru_core(
                x, g_mix, w_in, _to_time_major(lru_conv[j]), lru_h[j],
                p['lru_conv_w'][j], p['lru_conv_b'][j].reshape(1, LRU_WIDTH),
                p['lru_w_gate_a'][j].astype(BF16), p['lru_b_gate_a'][j].reshape(1, LRU_WIDTH),
                p['lru_w_gate_x'][j].astype(BF16), p['lru_b_gate_x'][j].reshape(1, LRU_WIDTH),
                p['lru_lambda'][j].reshape(1, LRU_WIDTH), nb, rows)
            mix = (y, p['lru_w_out'][j].astype(BF16), "plain")
            o_lru.append(h_new)
            o_lru_conv.append(_from_time_major(conv_new, nb))
        else:
            w_in = p['gdn_w_in'][j]
            w_pad = jnp.pad(w_in, ((0, 0), (0, GDN_PROJ_PAD - w_in.shape[1]))).astype(BF16)
            w_pad = w_pad.reshape(D_MODEL, GDN_PROJ_PAD // 256, 256).transpose(1, 0, 2)
            pad8 = lambda a: jnp.pad(a.reshape(1, GDN_HEADS), ((0, 0), (0, 128 - GDN_HEADS)))
            chunk = GDN_CHUNK if seq >= GDN_CHUNK else 8
            batch_major = nb == 8 and seq % chunk == 0
            *qkvzgb, conv_new = _gdn_prep(x, g_mix, w_pad, _to_time_major(gdn_conv[j]), p['gdn_conv_w'][j],
                                          pad8(p['gdn_a_log'][j]), pad8(p['gdn_dt_bias'][j]), nb, rows,
                                          batch_major)
            nw = p['gdn_norm'][j].reshape(1, GDN_DV)
            w_out = p['gdn_w_out'][j].astype(BF16)
            if batch_major:
                o, s_new = _gdn_core(*qkvzgb, gdn_s[j], nw, chunk, 4)
                mix = (o, w_out, "seq_major")
            else:
                lp = -(-seq // chunk) * chunk

                def bm(a):
                    a = a.reshape(seq, nb, a.shape[1]).transpose(1, 0, 2)
                    return jnp.pad(a, ((0, 0), (0, lp - seq), (0, 0)))

                o, s_new = _gdn_core(*[bm(a) for a in qkvzgb], gdn_s[j], nw, chunk, 8)
                o = o[:, :seq].transpose(1, 0, 2).reshape(total, GDN_KEY_DIM).astype(BF16)
                mix = (o, w_out, "plain")
            o_gdn.append(s_new)
            o_gdn_conv.append(_from_time_major(conv_new, nb))
        x, ca, cb = _ffn(x, *mix, p['norm_ffn'][i].reshape(1, D_MODEL), p['ffn_w_up_tiles'],
                         p['ffn_conv_w'][i], p['ffn_conv_b'][i].reshape(1, 2 * FFN_HIDDEN),
                         _to_time_major(ffn_conv[i]), p['ffn_w_down'].astype(BF16), i,
                         p['norm_final'].reshape(1, D_MODEL), nb, tm, i == depth - 1,
                         seq_major_io and i == 0, seq_major_io and i == depth - 1)
        hist = (FFN_CONV_WIDTH - 1) * nb
        o_ffn_conv.append(_from_time_major(jnp.concatenate([ca[-hist:], cb[-hist:]], axis=1), nb))
    return (x, jnp.stack(o_s5_re), jnp.stack(o_s5_im), jnp.stack(o_lru), jnp.stack(o_lru_conv),
            jnp.stack(o_gdn), jnp.stack(o_gdn_conv), jnp.stack(o_ffn_conv))


def kernel(x_prompt, x_sample, state_s5_re, state_s5_im, state_lru, state_lru_conv, state_gdn, state_gdn_conv, state_ffn_conv, norm_mix, norm_ffn, norm_final, s5_w_in, s5_a_re, s5_a_im, s5_log_dt, s5_b_re, s5_b_im, s5_c_re, s5_c_im, s5_d, s5_w_glu, lru_w_in, lru_conv_w, lru_conv_b, lru_w_gate_a, lru_b_gate_a, lru_w_gate_x, lru_b_gate_x, lru_lambda, lru_w_out, gdn_w_in, gdn_conv_w, gdn_a_log, gdn_dt_bias, gdn_norm, gdn_w_out, ffn_w_up, ffn_conv_w, ffn_conv_b, ffn_w_down):
    p = dict(norm_mix=norm_mix, norm_ffn=norm_ffn, norm_final=norm_final, s5_w_in=s5_w_in, s5_a_re=s5_a_re,
             s5_a_im=s5_a_im, s5_log_dt=s5_log_dt, s5_b_re=s5_b_re, s5_b_im=s5_b_im, s5_c_re=s5_c_re,
             s5_c_im=s5_c_im, s5_d=s5_d, s5_w_glu=s5_w_glu, lru_w_in=lru_w_in, lru_conv_w=lru_conv_w,
             lru_conv_b=lru_conv_b, lru_w_gate_a=lru_w_gate_a, lru_b_gate_a=lru_b_gate_a,
             lru_w_gate_x=lru_w_gate_x, lru_b_gate_x=lru_b_gate_x, lru_lambda=lru_lambda, lru_w_out=lru_w_out,
             gdn_w_in=gdn_w_in, gdn_conv_w=gdn_conv_w, gdn_a_log=gdn_a_log, gdn_dt_bias=gdn_dt_bias,
             gdn_norm=gdn_norm, gdn_w_out=gdn_w_out, ffn_w_up=ffn_w_up, ffn_conv_w=ffn_conv_w,
             ffn_conv_b=ffn_conv_b, ffn_w_down=ffn_w_down)
    p['s5_disc'] = [_s5_params(s5_a_re[j], s5_a_im[j], s5_log_dt[j], s5_b_re[j], s5_b_im[j], s5_c_re[j],
                               s5_c_im[j]) for j in range(s5_a_re.shape[0])]
    w_up = ffn_w_up.astype(BF16)
    p['ffn_w_up_tiles'] = w_up.reshape(w_up.shape[0], D_MODEL, -1, FFN_TN).transpose(0, 2, 1, 3)
    outs = []
    for x, states in (
            (x_prompt, None),
            (x_sample, (state_s5_re, state_s5_im, state_lru, state_lru_conv, state_gdn, state_gdn_conv,
                        state_ffn_conv))):
        nb, seq, _ = x.shape
        if states is None:
            states = tuple(jnp.zeros((s.shape[0], nb) + s.shape[2:], F32) for s in (
                state_s5_re, state_s5_im, state_lru, state_lru_conv, state_gdn, state_gdn_conv, state_ffn_conv))
        seq_major_io = nb == 8 and seq % 128 == 0
        if seq_major_io:
            res = _trunk(x, nb, seq, *states, p, True)
            outs.append(tuple(res))
        else:
            res = _trunk(_to_time_major(x), nb, seq, *states, p, False)
            outs.append((_from_time_major(res[0], nb),) + tuple(res[1:]))
    (y_p, *st_p), (y_s, *st_s) = outs
    return (y_p, y_s, *st_p, *st_s)
```

```python
import functools
import math

import jax
import jax.numpy as jnp
from jax import lax
from jax.experimental import pallas as pl
from jax.experimental.pallas import tpu as pltpu

F32 = jnp.float32
BF16 = jnp.bfloat16

D_MODEL = 1024
RMS_EPS = 1e-6
L2_EPS = 1e-6
S5_GROUPS = 64
S5_STATE = 64
S5_GROUP_CH = 16
S5_COLS = S5_GROUPS * S5_STATE
S5_KB = 8
S5_SCAN_LANES = 1024
LRU_WIDTH = 1280
LRU_BLOCK = 128
LRU_BLOCKS = LRU_WIDTH // LRU_BLOCK
LRU_C = 8.0
CONV_WIDTH = 4
GDN_HEADS = 8
GDN_DK = 128
GDN_DV = 128
GDN_KEY_DIM = GDN_HEADS * GDN_DK
GDN_CONV_DIM = 3 * GDN_KEY_DIM
GDN_CHUNK = 64
GDN_PROJ_PAD = 4352
FFN_HIDDEN = 2816
FFN_CONV_WIDTH = 3
FFN_TN = 256
SUB_ROWS = 256
VMEM_LIMIT_BYTES = 56 * 1024 * 1024


def _cparams(sem):
    return pltpu.CompilerParams(dimension_semantics=sem, vmem_limit_bytes=VMEM_LIMIT_BYTES)


def _rms(x, g):
    ms = jnp.mean(x * x, axis=-1, keepdims=True)
    return x * lax.rsqrt(ms + RMS_EPS) * g


def _softplus(x):
    return jnp.maximum(x, 0.0) + jnp.log1p(jnp.exp(-jnp.abs(x)))


def _stage_time_major(src_ref, slab_ref, nb):
    steps = src_ref.shape[1]
    for b in range(nb):
        for kt in range(src_ref.shape[2] // 128):
            slab_ref[kt, pl.ds(b, steps, stride=nb), :] = src_ref[b, :, kt * 128:(kt + 1) * 128]


def _gelu(x):
    c = math.sqrt(2.0 / math.pi)
    half = 0.5 * x
    return half + half * jnp.tanh(x * (c + (c * 0.044715) * (x * x)))


def _dot(a, b):
    return jnp.dot(a.astype(BF16), b.astype(BF16), preferred_element_type=F32)


def _dot_nt(a, b):
    return lax.dot_general(a.astype(BF16), b.astype(BF16), (((1,), (1,)), ((), ())),
                           preferred_element_type=F32)


def _split3(a):
    hi = a.astype(BF16)
    r = a - hi.astype(F32)
    mid = r.astype(BF16)
    lo = (r - mid.astype(F32)).astype(BF16)
    return hi, mid, lo


def _s5_core_kernel(x_ref, g_ref, win_ref, h0re_ref, h0im_ref, bre_ref, bim_ref, cre_ref, cim_ref, are_ref,
                    aim_ref, d_ref, y_ref, hre_out, him_out, xn_s, u_ref, hre_s, him_s, *stage,
                    nb, rows, seq_major_in):
    i = pl.program_id(0)

    @pl.when(i == 0)
    def _():
        hre_s[0:nb, :] = h0re_ref[...]
        him_s[0:nb, :] = h0im_ref[...]

    kw = S5_COLS // S5_KB
    uw = D_MODEL // S5_KB
    per_grp = S5_SCAN_LANES // kw
    n_grp = S5_COLS // S5_SCAN_LANES
    if seq_major_in:
        x_st, = stage
        _stage_time_major(x_ref, x_st, nb)
        x = jnp.concatenate([x_st[kt] for kt in range(D_MODEL // 128)], axis=1)
    else:
        x = x_ref[...]
    xn_s[...] = _rms(x, g_ref[...]).astype(BF16)

    def project_in(grp):
        ucols = slice(grp * per_grp * uw, (grp + 1) * per_grp * uw)
        u_ref[:, ucols] = jnp.dot(xn_s[...], win_ref[:, ucols], preferred_element_type=F32)
        for kb in range(grp * per_grp, (grp + 1) * per_grp):
            ukb = u_ref[:, kb * uw:(kb + 1) * uw].astype(BF16)
            hre_s[nb:nb + rows, kb * kw:(kb + 1) * kw] = jnp.dot(ukb, bre_ref[kb], preferred_element_type=F32)
            him_s[nb:nb + rows, kb * kw:(kb + 1) * kw] = jnp.dot(ukb, bim_ref[kb], preferred_element_type=F32)

    def scan(grp):
        cols = slice(grp * S5_SCAN_LANES, (grp + 1) * S5_SCAN_LANES)
        are = jnp.broadcast_to(are_ref[:, cols], (nb, S5_SCAN_LANES))
        aim = jnp.broadcast_to(aim_ref[:, cols], (nb, S5_SCAN_LANES))
        hr, hi = hre_s[0:nb, cols], him_s[0:nb, cols]
        for t in range(rows // nb):
            r = slice(nb + t * nb, 2 * nb + t * nb)
            hr, hi = (are * hr - aim * hi + hre_s[r, cols],
                      are * hi + aim * hr + him_s[r, cols])
            hre_s[r, cols] = hr
            him_s[r, cols] = hi

    def project_out(grp):
        for kb in range(grp * per_grp, (grp + 1) * per_grp):
            hr = hre_s[nb:nb + rows, kb * kw:(kb + 1) * kw].astype(BF16)
            hi = him_s[nb:nb + rows, kb * kw:(kb + 1) * kw].astype(BF16)
            yk = (jnp.dot(hr, cre_ref[kb], preferred_element_type=F32)
                  - jnp.dot(hi, cim_ref[kb], preferred_element_type=F32))
            yk = yk + d_ref[:, kb * uw:(kb + 1) * uw] * u_ref[:, kb * uw:(kb + 1) * uw]
            y_ref[:, kb * uw:(kb + 1) * uw] = _gelu(yk).astype(BF16)

    project_in(0)
    for grp in range(n_grp):
        if grp + 1 < n_grp:
            project_in(grp + 1)
        scan(grp)
        project_out(grp)

    last_re = hre_s[rows:rows + nb, :]
    last_im = him_s[rows:rows + nb, :]
    hre_s[0:nb, :] = last_re
    him_s[0:nb, :] = last_im
    hre_out[...] = last_re
    him_out[...] = last_im


def _s5_core(x, g, w_in, h0re, h0im, bre, bim, cre, cim, are, aim, d, nb, rows, seq_major_in=False):
    full = lambda shape: pl.BlockSpec(shape, lambda i: (0,) * len(shape))
    if seq_major_in:
        total = x.shape[0] * x.shape[1]
        x_spec = pl.BlockSpec((nb, rows // nb, D_MODEL), lambda i: (0, i, 0))
        stage = [pltpu.VMEM((D_MODEL // 128, rows, 128), F32)]
    else:
        total = x.shape[0]
        x_spec = pl.BlockSpec((rows, D_MODEL), lambda i: (i, 0))
        stage = []
    return pl.pallas_call(
        functools.partial(_s5_core_kernel, nb=nb, rows=rows, seq_major_in=seq_major_in),
        grid=(total // rows,),
        in_specs=[x_spec, full((1, D_MODEL)), full(w_in.shape),
                  full((nb, S5_COLS)), full((nb, S5_COLS)),
                  full(bre.shape), full(bim.shape), full(cre.shape), full(cim.shape),
                  full((1, S5_COLS)), full((1, S5_COLS)), full((1, D_MODEL))],
        out_specs=[pl.BlockSpec((rows, D_MODEL), lambda i: (i, 0)),
                   full((nb, S5_COLS)), full((nb, S5_COLS))],
        out_shape=[jax.ShapeDtypeStruct((total, D_MODEL), BF16),
                   jax.ShapeDtypeStruct((nb, S5_COLS), F32),
                   jax.ShapeDtypeStruct((nb, S5_COLS), F32)],
        scratch_shapes=[pltpu.VMEM((rows, D_MODEL), BF16), pltpu.VMEM((rows, D_MODEL), F32),
                        pltpu.VMEM((nb + rows, S5_COLS), F32),
                        pltpu.VMEM((nb + rows, S5_COLS), F32)] + stage,
        compiler_params=_cparams(("arbitrary",)),
        name="s5_core",
    )(x, g, w_in, h0re, h0im, bre, bim, cre, cim, are, aim, d)


def _lru_core_kernel(x_ref, g_ref, win_ref, prev_ref, h0_ref, cw_ref, cb_ref, wga_ref, bga_ref, wgx_ref,
                     bgx_ref, lam_ref, y_ref, hout_ref, cout_ref, xn_s, gate_s, xp_s, h_s, a_s, *, nb, rows):
    i = pl.program_id(0)
    hist = (CONV_WIDTH - 1) * nb

    @pl.when(i == 0)
    def _():
        xp_s[0:hist, :] = prev_ref[...]
        h_s[0:nb, :] = h0_ref[...]

    xn_s[...] = _rms(x_ref[...], g_ref[...]).astype(BF16)
    c8 = -LRU_C * _softplus(-lam_ref[...])

    def project(n):
        pg = jnp.dot(xn_s[...], win_ref[n], preferred_element_type=F32)
        sl = slice(n * LRU_BLOCK, (n + 1) * LRU_BLOCK)
        gate_s[:, sl] = pg[:, :LRU_BLOCK]
        xp_s[hist:hist + rows, sl] = pg[:, LRU_BLOCK:]

    def gates(n):
        sl = slice(n * LRU_BLOCK, (n + 1) * LRU_BLOCK)
        xcn = xp_s[0:rows, sl] * cw_ref[0:1, sl]
        for k in range(1, CONV_WIDTH):
            xcn = xcn + xp_s[k * nb:k * nb + rows, sl] * cw_ref[k:k + 1, sl]
        xcn = xcn + cb_ref[:, sl]
        xcb = xcn.astype(BF16)
        r = jax.nn.sigmoid(jnp.dot(xcb, wga_ref[n], preferred_element_type=F32) + bga_ref[:, sl])
        ig = jax.nn.sigmoid(jnp.dot(xcb, wgx_ref[n], preferred_element_type=F32) + bgx_ref[:, sl])
        log_a = c8[:, sl] * r
        a_s[:, sl] = jnp.exp(log_a)
        t = jnp.tanh(log_a)
        h_s[nb:nb + rows, sl] = jnp.sqrt(-2.0 * t / (1.0 - t)) * ig * xcn

    project(0)
    for n in range(LRU_BLOCKS):
        if n + 1 < LRU_BLOCKS:
            project(n + 1)
        gates(n)

    def step(t, carry):
        r0 = pl.multiple_of(t * nb, nb)
        r1 = pl.multiple_of(t * nb + nb, nb)
        h_s[pl.ds(r1, nb), :] = a_s[pl.ds(r0, nb), :] * h_s[pl.ds(r0, nb), :] + h_s[pl.ds(r1, nb), :]
        return carry

    lax.fori_loop(0, rows // nb, step, 0)

    y_ref[...] = (_gelu(gate_s[...]) * h_s[nb:nb + rows, :]).astype(BF16)
    tail = xp_s[rows:rows + hist, :]
    last = h_s[rows:rows + nb, :]
    xp_s[0:hist, :] = tail
    h_s[0:nb, :] = last
    cout_ref[...] = tail
    hout_ref[...] = last


def _lru_core(x, g, w_in, prev, h0, cw, cb, wga, bga, wgx, bgx, lam, nb, rows):
    total = x.shape[0]
    hist = (CONV_WIDTH - 1) * nb
    full = lambda shape: pl.BlockSpec(shape, lambda i: (0,) * len(shape))
    return pl.pallas_call(
        functools.partial(_lru_core_kernel, nb=nb, rows=rows),
        grid=(total // rows,),
        in_specs=[pl.BlockSpec((rows, D_MODEL), lambda i: (i, 0)), full((1, D_MODEL)),
                  pl.BlockSpec(w_in.shape, lambda i: (0, 0, 0), pipeline_mode=pl.Buffered(1)),
                  full((hist, LRU_WIDTH)), full((nb, LRU_WIDTH)),
                  full((CONV_WIDTH, LRU_WIDTH)), full((1, LRU_WIDTH)),
                  full(wga.shape), full((1, LRU_WIDTH)), full(wgx.shape), full((1, LRU_WIDTH)),
                  full((1, LRU_WIDTH))],
        out_specs=[pl.BlockSpec((rows, LRU_WIDTH), lambda i: (i, 0)),
                   full((nb, LRU_WIDTH)), full((hist, LRU_WIDTH))],
        out_shape=[jax.ShapeDtypeStruct((total, LRU_WIDTH), BF16),
                   jax.ShapeDtypeStruct((nb, LRU_WIDTH), F32),
                   jax.ShapeDtypeStruct((hist, LRU_WIDTH), F32)],
        scratch_shapes=[pltpu.VMEM((rows, D_MODEL), BF16),
                        pltpu.VMEM((rows, LRU_WIDTH), F32),
                        pltpu.VMEM((hist + rows, LRU_WIDTH), F32),
                        pltpu.VMEM((nb + rows, LRU_WIDTH), F32),
                        pltpu.VMEM((rows, LRU_WIDTH), F32)],
        compiler_params=_cparams(("arbitrary",)),
        name="lru_core",
    )(x, g, w_in, prev, h0, cw, cb, wga, bga, wgx, bgx, lam)


def _gdn_prep_kernel(x_ref, gn_ref, win_ref, prev_ref, cw_ref, alog_ref, dtb_ref,
                     q_ref, k_ref, v_ref, zo_ref, g_ref, beta_ref, cout_ref, xn_s, xp_s, ab_s, st_s,
                     *, nb, rows, batch_major):
    i = pl.program_id(0)
    hist = (CONV_WIDTH - 1) * nb
    n_qkv = 3 * GDN_HEADS
    n_z = GDN_HEADS

    @pl.when(i == 0)
    def _():
        xp_s[0:hist, :] = prev_ref[...]

    xn_s[...] = _rms(x_ref[...], gn_ref[...]).astype(BF16)

    def project(p):
        pg = jnp.dot(xn_s[...], win_ref[p], preferred_element_type=F32)
        for half in range(2):
            s = 2 * p + half
            col = pg[:, half * 128:(half + 1) * 128]
            if s < n_qkv:
                xp_s[hist:hist + rows, s * 128:(s + 1) * 128] = col
            elif s < n_qkv + n_z:
                if batch_major:
                    st_s[s] = col
                else:
                    zo_ref[:, (s - n_qkv) * 128:(s - n_qkv + 1) * 128] = col
            elif s == n_qkv + n_z:
                ab_s[...] = col

    def activate(s):
        part, h = divmod(s, GDN_HEADS)
        sl = slice(s * 128, (s + 1) * 128)
        acc = xp_s[0:rows, sl] * cw_ref[0:1, sl]
        for k in range(1, CONV_WIDTH):
            acc = acc + xp_s[k * nb:k * nb + rows, sl] * cw_ref[k:k + 1, sl]
        y = acc * jax.nn.sigmoid(acc)
        if part < 2:
            y = y * lax.rsqrt(jnp.sum(y * y, axis=-1, keepdims=True) + L2_EPS)
        if part == 0:
            y = y * (GDN_DK ** -0.5)
        if batch_major:
            st_s[s] = y
        else:
            (q_ref, k_ref, v_ref)[part][:, h * GDN_DK:(h + 1) * GDN_DK] = y

    n_pairs = win_ref.shape[0]
    project(0)
    for p in range(n_pairs):
        if p + 1 < n_pairs:
            project(p + 1)
        for s in (2 * p, 2 * p + 1):
            if s < n_qkv:
                activate(s)

    ab = ab_s[...]
    g = -jnp.exp(alog_ref[...]) * _softplus(ab + dtb_ref[...])
    beta = jax.nn.sigmoid(ab)
    if batch_major:
        st_s[4 * GDN_HEADS] = g
        st_s[4 * GDN_HEADS + 1] = beta
        steps = rows // nb
        for b in range(nb):
            pick = pl.ds(b, steps, stride=nb)
            for part, out in enumerate((q_ref, k_ref, v_ref, zo_ref)):
                for h in range(GDN_HEADS):
                    out[b, :, h * GDN_DK:(h + 1) * GDN_DK] = st_s[part * GDN_HEADS + h, pick, :]
            g_ref[b] = st_s[4 * GDN_HEADS, pick, :]
            beta_ref[b] = st_s[4 * GDN_HEADS + 1, pick, :]
    else:
        g_ref[...] = g
        beta_ref[...] = beta
    tail = xp_s[rows:rows + hist, :]
    xp_s[0:hist, :] = tail
    cout_ref[...] = tail


def _gdn_prep(x, gn, w_in, prev, cw, alog, dtb, nb, rows, batch_major):
    total = x.shape[0]
    hist = (CONV_WIDTH - 1) * nb
    full = lambda shape: pl.BlockSpec(shape, lambda i: (0,) * len(shape))
    tile = lambda n: pl.BlockSpec((rows, n), lambda i: (i, 0))
    if batch_major:
        out_tile = lambda n: pl.BlockSpec((nb, rows // nb, n), lambda i: (0, i, 0))
        out_sds = lambda n: jax.ShapeDtypeStruct((nb, total // nb, n), F32)
    else:
        out_tile = tile
        out_sds = lambda n: jax.ShapeDtypeStruct((total, n), F32)
    widths = (GDN_KEY_DIM,) * 4 + (128, 128)
    return pl.pallas_call(
        functools.partial(_gdn_prep_kernel, nb=nb, rows=rows, batch_major=batch_major),
        grid=(total // rows,),
        in_specs=[tile(D_MODEL), full((1, D_MODEL)),
                  pl.BlockSpec(w_in.shape, lambda i: (0, 0, 0), pipeline_mode=pl.Buffered(1)),
                  full((hist, GDN_CONV_DIM)), full((CONV_WIDTH, GDN_CONV_DIM)),
                  full((1, 128)), full((1, 128))],
        out_specs=[out_tile(n) for n in widths] + [full((hist, GDN_CONV_DIM))],
        out_shape=[out_sds(n) for n in widths] + [jax.ShapeDtypeStruct((hist, GDN_CONV_DIM), F32)],
        scratch_shapes=[pltpu.VMEM((rows, D_MODEL), BF16),
                        pltpu.VMEM((hist + rows, GDN_CONV_DIM), F32),
                        pltpu.VMEM((rows, 128), F32),
                        pltpu.VMEM((4 * GDN_HEADS + 2, rows if batch_major else 8, 128), F32)],
        compiler_params=_cparams(("arbitrary",)),
        name="gdn_prep",
    )(x, gn, w_in, prev, cw, alog, dtb)


def _unit_lower_inverses(ms, ri, ci, chunk):
    eye = (ri == ci).astype(F32)
    blk = (ri >> 3) == (ci >> 3)
    n1 = [jnp.where(blk, -m, 0.0) for m in ms]
    n2 = [_dot(a, a) for a in n1]
    n4 = [_dot(a, a) for a in n2]
    ts = [_dot(eye + a, eye + b) for a, b in zip(n1, n2)]
    ts = [_dot(t, eye + a) for t, a in zip(ts, n4)]
    shift = 3
    while (1 << shift) < chunk:
        pair = ((ri >> (shift + 1)) == (ci >> (shift + 1))) & ((ri >> shift) != (ci >> shift))
        left = [_dot(t, jnp.where(pair, m, 0.0)) for t, m in zip(ts, ms)]
        ts = [t - _dot(a, t) for t, a in zip(ts, left)]
        shift += 1
    return ts


def _gdn_core_kernel(q_ref, k_ref, v_ref, z_ref, g_ref, beta_ref, s0_ref, nw_ref, o_ref, s_ref, *, chunk, bb):
    c = pl.program_id(1)

    @pl.when(c == 0)
    def _():
        s_ref[...] = s0_ref[...]

    ri = lax.broadcasted_iota(jnp.int32, (chunk, chunk), 0)
    ci = lax.broadcasted_iota(jnp.int32, (chunk, chunk), 1)
    causal = ri >= ci
    strict = ri > ci
    tril = causal.astype(BF16)
    dotf = functools.partial(jnp.dot, preferred_element_type=F32)
    nw = nw_ref[...]

    cums, cum_ts, ecums, e_lasts, e_rests, betas = [], [], [], [], [], []
    for bi in range(bb):
        g3 = _split3(g_ref[bi])
        cum = dotf(tril, g3[0]) + dotf(tril, g3[1]) + dotf(tril, g3[2])
        cums.append(cum)
        cum_ts.append(cum.T)
        ecums.append(jnp.exp(cum))
        g_last = cum[chunk - 1:chunk, :]
        e_lasts.append(jnp.exp(g_last))
        e_rests.append(jnp.exp(g_last - cum))
        betas.append(beta_ref[bi])

    units = [(bi, h) for bi in range(bb) for h in range(GDN_HEADS)]
    col = lambda a, h: a[:, h:h + 1]
    sl = lambda h: slice(h * GDN_DK, (h + 1) * GDN_DK)
    q = [q_ref[bi, :, sl(h)] for bi, h in units]
    k = [k_ref[bi, :, sl(h)] for bi, h in units]
    decay = [jnp.exp(jnp.where(causal, col(cums[bi], h) - cum_ts[bi][h:h + 1, :], -jnp.inf)) for bi, h in units]
    k_beta = [kk * col(betas[bi], GDN_HEADS + h) for kk, (bi, h) in zip(k, units)]
    ak = [_dot_nt(jnp.concatenate([kb, qq], axis=0), kk) for kb, qq, kk in zip(k_beta, q, k)]
    ms = [jnp.where(strict, a[:chunk] * d, 0.0) for a, d in zip(ak, decay)]
    ts = _unit_lower_inverses(ms, ri, ci, chunk)
    rhs = [jnp.concatenate([v_ref[bi, :, sl(h)] * col(betas[bi], GDN_HEADS + h), kb * col(ecums[bi], h)], axis=1)
           for kb, (bi, h) in zip(k_beta, units)]
    sol = [_dot(t, r) for t, r in zip(ts, rhs)]
    s_old = [s_ref[bi, h] for bi, h in units]
    ws = [_dot(jnp.concatenate([so[:, GDN_DV:], qq * col(ecums[bi], h)], axis=0), s)
          for so, qq, s, (bi, h) in zip(sol, q, s_old, units)]
    v_new = [so[:, :GDN_DV] - w[:chunk] for so, w in zip(sol, ws)]
    o = [w[chunk:] + _dot(a[chunk:] * d, vn) for w, a, d, vn in zip(ws, ak, decay, v_new)]
    k_dec_t = [(kk * col(e_rests[bi], h)).T for kk, (bi, h) in zip(k, units)]
    for (bi, h), s, kt, vn, oo in zip(units, s_old, k_dec_t, v_new, o):
        s_ref[bi, h] = s * col(e_lasts[bi], h) + _dot(kt, vn)
        on = oo * lax.rsqrt(jnp.mean(oo * oo, axis=-1, keepdims=True) + RMS_EPS) * nw
        zh = z_ref[bi, :, sl(h)]
        o_ref[bi, :, sl(h)] = on * (zh * jax.nn.sigmoid(zh))


def _gdn_core(q, k, v, z, g, beta, s0, nw, chunk, bb):
    nb, lp = q.shape[0], q.shape[1]
    seq = lambda n: pl.BlockSpec((bb, chunk, n), lambda b, c: (b, c, 0))
    st = pl.BlockSpec((bb, GDN_HEADS, GDN_DK, GDN_DV), lambda b, c: (b, 0, 0, 0))
    return pl.pallas_call(
        functools.partial(_gdn_core_kernel, chunk=chunk, bb=bb),
        grid=(nb // bb, lp // chunk),
        in_specs=[seq(GDN_KEY_DIM), seq(GDN_KEY_DIM), seq(GDN_KEY_DIM), seq(GDN_KEY_DIM),
                  seq(128), seq(128), st, pl.BlockSpec((1, GDN_DV), lambda b, c: (0, 0))],
        out_specs=[seq(GDN_KEY_DIM), st],
        out_shape=[jax.ShapeDtypeStruct((nb, lp, GDN_KEY_DIM), F32),
                   jax.ShapeDtypeStruct(s0.shape, F32)],
        compiler_params=_cparams(("parallel", "arbitrary")),
        name="gdn_core",
    )(q, k, v, z, g, beta, s0, nw)


def _ffn_kernel(r_ref, a_ref, wo_ref, g_ref, wa_ref, wb_ref, cwa_ref, cwb_ref, cba_ref, cbb_ref, pa_ref, pb_ref,
                wd_ref, gf_ref, o_ref, ca_out, cb_out, x_s, xn_s, acc_s, hpa_s, hpb_s, cara_s, carb_s, *stage,
                nb, tm, final_norm, mixer_out, res_seq_major, out_seq_major):
    i = pl.program_id(0)
    j = pl.program_id(1)
    hist = (FFN_CONV_WIDTH - 1) * nb
    rs = min(SUB_ROWS, tm)
    nsub = tm // rs
    sub = lambda r: slice(r * rs, (r + 1) * rs)
    stage = list(stage)
    a_s = stage.pop(0) if mixer_out == "seq_major" else None
    io_s = stage.pop(0) if (res_seq_major or out_seq_major) else None
    nk = D_MODEL // 128

    @pl.when(j == 0)
    def _():
        if mixer_out == "seq_major":
            _stage_time_major(a_ref, a_s, nb)
        if res_seq_major:
            _stage_time_major(r_ref, io_s, nb)

        def project(r):
            if mixer_out == "seq_major":
                a = jnp.concatenate([a_s[kt, sub(r), :].astype(BF16) for kt in range(nk)], axis=1)
            else:
                a = a_ref[sub(r), :]
            if res_seq_major:
                res = jnp.concatenate([io_s[kt, sub(r), :] for kt in range(nk)], axis=1)
            else:
                res = r_ref[sub(r), :]
            if mixer_out == "glu":
                half = wo_ref.shape[1] // 2
                val = jnp.dot(a, wo_ref[:, :half], preferred_element_type=F32)
                gate = jnp.dot(a, wo_ref[:, half:], preferred_element_type=F32)
                y = val * jax.nn.sigmoid(gate)
            else:
                y = jnp.dot(a, wo_ref[...], preferred_element_type=F32)
            x_s[sub(r), :] = res + y

        project(0)
        for r in range(nsub):
            if r + 1 < nsub:
                project(r + 1)
            xn_s[sub(r), :] = _rms(x_s[sub(r), :], g_ref[...]).astype(BF16)
        acc_s[...] = jnp.zeros_like(acc_s)

    @pl.when(i == 0)
    def _():
        hpa_s[0:hist, :] = pa_ref[...]
        hpb_s[0:hist, :] = pb_ref[...]

    @pl.when(i > 0)
    def _():
        hpa_s[0:hist, :] = cara_s[j]
        hpb_s[0:hist, :] = carb_s[j]

    def up(r):
        xr = xn_s[r * rs:(r + 1) * rs, :]
        hpa_s[hist + r * rs:hist + (r + 1) * rs, :] = jnp.dot(xr, wa_ref[...], preferred_element_type=F32)
        hpb_s[hist + r * rs:hist + (r + 1) * rs, :] = jnp.dot(xr, wb_ref[...], preferred_element_type=F32)

    def conv(hp_s, cw_ref, cb_ref, r):
        y = hp_s[r * rs:(r + 1) * rs, :] * cw_ref[0:1, :]
        for k in range(1, FFN_CONV_WIDTH):
            y = y + hp_s[k * nb + r * rs:k * nb + (r + 1) * rs, :] * cw_ref[k:k + 1, :]
        return y + cb_ref[...]

    def down(r):
        act = (_gelu(conv(hpa_s, cwa_ref, cba_ref, r)) * conv(hpb_s, cwb_ref, cbb_ref, r)).astype(BF16)
        acc_s[r * rs:(r + 1) * rs, :] += jnp.dot(act, wd_ref[...], preferred_element_type=F32)

    up(0)
    for r in range(nsub):
        if r + 1 < nsub:
            up(r + 1)
        down(r)

    tail_a = hpa_s[tm:tm + hist, :]
    tail_b = hpb_s[tm:tm + hist, :]
    cara_s[j] = tail_a
    carb_s[j] = tail_b
    ca_out[...] = tail_a
    cb_out[...] = tail_b

    @pl.when(j == pl.num_programs(1) - 1)
    def _():
        y = x_s[...] + acc_s[...]
        if final_norm:
            y = _rms(y, gf_ref[...])
        if out_seq_major:
            for kt in range(nk):
                io_s[kt] = y[:, kt * 128:(kt + 1) * 128]
            for b in range(nb):
                for kt in range(nk):
                    o_ref[b, :, kt * 128:(kt + 1) * 128] = io_s[kt, pl.ds(b, tm // nb, stride=nb), :]
        else:
            o_ref[...] = y


def _ffn(res, a, w_out, mixer_out, g, w_up, cw, cb, prev, w_down, layer, g_final, nb, tm, final_norm,
         res_seq_major=False, out_seq_major=False):
    rows = res.shape[0] * res.shape[1] if res_seq_major else res.shape[0]
    row_tile = pl.BlockSpec((tm, D_MODEL), lambda i, j: (i, 0))
    seq_tile = pl.BlockSpec((nb, tm // nb, D_MODEL), lambda i, j: (0, i, 0))
    tn = FFN_TN
    nj = FFN_HIDDEN // tn
    hist = (FFN_CONV_WIDTH - 1) * nb
    col_a = lambda r: pl.BlockSpec((r, tn), lambda i, j: (0, j))
    col_b = lambda r: pl.BlockSpec((r, tn), lambda i, j: (0, nj + j))
    vec = pl.BlockSpec((1, D_MODEL), lambda i, j: (0, 0))
    k = w_out.shape[0]
    if mixer_out == "seq_major":
        a_spec = pl.BlockSpec((nb, tm // nb, k), lambda i, j: (0, i, 0))
        stage = [pltpu.VMEM((k // 128, tm, 128), F32)]
    else:
        a_spec = pl.BlockSpec((tm, k), lambda i, j: (i, 0))
        stage = []
    if res_seq_major or out_seq_major:
        stage = stage + [pltpu.VMEM((D_MODEL // 128, tm, 128), F32)]
    out_sds = (jax.ShapeDtypeStruct((nb, rows // nb, D_MODEL), F32) if out_seq_major
               else jax.ShapeDtypeStruct((rows, D_MODEL), F32))
    return pl.pallas_call(
        functools.partial(_ffn_kernel, nb=nb, tm=tm, final_norm=final_norm, mixer_out=mixer_out,
                          res_seq_major=res_seq_major, out_seq_major=out_seq_major),
        grid=(rows // tm, nj),
        in_specs=[seq_tile if res_seq_major else row_tile, a_spec,
                  pl.BlockSpec(w_out.shape, lambda i, j: (0, 0)), vec,
                  pl.BlockSpec((None, D_MODEL, tn), lambda i, j: (layer, 0, j)),
                  pl.BlockSpec((None, D_MODEL, tn), lambda i, j: (layer, 0, nj + j)),
                  col_a(FFN_CONV_WIDTH), col_b(FFN_CONV_WIDTH), col_a(1), col_b(1),
                  col_a(hist), col_b(hist),
                  pl.BlockSpec((None, tn, D_MODEL), lambda i, j: (layer, j, 0)), vec],
        out_specs=[seq_tile if out_seq_major else row_tile,
                   pl.BlockSpec((hist, tn), lambda i, j: (i, j)),
                   pl.BlockSpec((hist, tn), lambda i, j: (i, j))],
        out_shape=[out_sds,
                   jax.ShapeDtypeStruct((rows // tm * hist, FFN_HIDDEN), F32),
                   jax.ShapeDtypeStruct((rows // tm * hist, FFN_HIDDEN), F32)],
        scratch_shapes=[pltpu.VMEM((tm, D_MODEL), F32), pltpu.VMEM((tm, D_MODEL), BF16),
                        pltpu.VMEM((tm, D_MODEL), F32),
                        pltpu.VMEM((hist + tm, tn), F32), pltpu.VMEM((hist + tm, tn), F32),
                        pltpu.VMEM((nj, hist, tn), F32), pltpu.VMEM((nj, hist, tn), F32)] + stage,
        compiler_params=_cparams(("arbitrary", "arbitrary")),
        name="conv_ffn",
    )(res, a, w_out, g, w_up, w_up, cw, cw, cb, cb, prev, prev, w_down, g_final)


def _s5_disc_kernel(are_ref, aim_ref, ldt_ref, bre_ref, bim_ref, abr_ref, abi_ref, bbr_ref, bbi_ref):
    a_re, a_im = are_ref[...], aim_ref[...]
    dt = jnp.exp(ldt_ref[...])
    mag = jnp.exp(a_re * dt)
    ar = mag * jnp.cos(a_im * dt)
    ai = mag * jnp.sin(a_im * dt)
    den = a_re * a_re + a_im * a_im
    nr = ar - 1.0
    cr = (nr * a_re + ai * a_im) / den
    ci = (ai * a_re - nr * a_im) / den
    b_re, b_im = bre_ref[...], bim_ref[...]
    abr_ref[...] = ar
    abi_ref[...] = ai
    bbr_ref[...] = cr * b_re - ci * b_im
    bbi_ref[...] = cr * b_im + ci * b_re


def _s5_params(a_re, a_im, log_dt, b_re, b_im, c_re, c_im):
    rep = lambda a: jnp.repeat(a.astype(F32), S5_GROUP_CH, axis=0)
    rows_gc = lambda b: b.astype(F32).transpose(0, 2, 1).reshape(D_MODEL, S5_STATE)
    ldt = jnp.broadcast_to(log_dt.astype(F32)[:, None], (S5_GROUPS, S5_STATE))
    sds = jax.ShapeDtypeStruct((D_MODEL, S5_STATE), F32)
    abr, abi, bbr, bbi = pl.pallas_call(_s5_disc_kernel, out_shape=[sds] * 4, name="s5_discretize")(
        rep(a_re), rep(a_im), rep(ldt), rows_gc(b_re), rows_gc(b_im))
    eye = jnp.eye(S5_GROUPS // S5_KB, dtype=F32)

    def b_blocks(b):
        b = b.reshape(S5_KB, S5_GROUPS // S5_KB, S5_GROUP_CH, S5_STATE)
        return jnp.einsum('kgcp,gh->kgchp', b, eye).reshape(S5_KB, D_MODEL // S5_KB, S5_COLS // S5_KB).astype(BF16)

    def c_blocks(c):
        c = c.astype(F32).reshape(S5_KB, S5_GROUPS // S5_KB, S5_GROUP_CH, S5_STATE)
        return jnp.einsum('kgcp,gh->kgphc', c, eye).reshape(S5_KB, S5_COLS // S5_KB, D_MODEL // S5_KB).astype(BF16)

    return (b_blocks(bbr), b_blocks(bbi), c_blocks(c_re), c_blocks(c_im),
            abr[::S5_GROUP_CH].reshape(1, S5_COLS), abi[::S5_GROUP_CH].reshape(1, S5_COLS))


def _to_time_major(a):
    return a.transpose(1, 0, 2).reshape(a.shape[0] * a.shape[1], a.shape[2])


def _from_time_major(a, nb):
    return a.reshape(a.shape[0] // nb, nb, a.shape[1]).transpose(1, 0, 2)


def _trunk(x, nb, seq, s5_re, s5_im, lru_h, lru_conv, gdn_s, gdn_conv, ffn_conv, p, seq_major_io):
    total = seq * nb
    tm = min(total, 1024)
    rows = 512
    o_s5_re, o_s5_im, o_lru, o_lru_conv, o_gdn, o_gdn_conv, o_ffn_conv = [], [], [], [], [], [], []
    depth = p['norm_mix'].shape[0]
    for i in range(depth):
        kind, j = i % 3, i // 3
        g_mix = p['norm_mix'][i].reshape(1, D_MODEL)
        if kind == 0:
            bre, bim, cre, cim, are, aim = p['s5_disc'][j]
            y, hre, him = _s5_core(x, g_mix, p['s5_w_in'][j].astype(BF16),
                                   s5_re[j].reshape(nb, S5_COLS), s5_im[j].reshape(nb, S5_COLS),
                                   bre, bim, cre, cim, are, aim, p['s5_d'][j].reshape(1, D_MODEL), nb, rows,
                                   seq_major_io and i == 0)
            mix = (y, p['s5_w_glu'][j].astype(BF16), "glu")
            o_s5_re.append(hre.reshape(nb, S5_GROUPS, S5_STATE))
            o_s5_im.append(him.reshape(nb, S5_GROUPS, S5_STATE))
        elif kind == 1:
            w_in = p['lru_w_in'][j].astype(BF16).reshape(D_MODEL, 2, LRU_BLOCKS, LRU_BLOCK)
            w_in = w_in.transpose(2, 0, 1, 3).reshape(LRU_BLOCKS, D_MODEL, 2 * LRU_BLOCK)
            y, h_new, conv_new = _lru_core(
                x, g_mix, w_in, _to_time_major(lru_conv[j]), lru_h[j],
                p['lru_conv_w'][j], p['lru_conv_b'][j].reshape(1, LRU_WIDTH),
                p['lru_w_gate_a'][j].astype(BF16), p['lru_b_gate_a'][j].reshape(1, LRU_WIDTH),
                p['lru_w_gate_x'][j].astype(BF16), p['lru_b_gate_x'][j].reshape(1, LRU_WIDTH),
                p['lru_lambda'][j].reshape(1, LRU_WIDTH), nb, rows)
            mix = (y, p['lru_w_out'][j].astype(BF16), "plain")
            o_lru.append(h_new)
            o_lru_conv.append(_from_time_major(conv_new, nb))
        else:
            w_in = p['gdn_w_in'][j]
            w_pad = jnp.pad(w_in, ((0, 0), (0, GDN_PROJ_PAD - w_in.shape[1]))).astype(BF16)
            w_pad = w_pad.reshape(D_MODEL, GDN_PROJ_PAD // 256, 256).transpose(1, 0, 2)
            pad8 = lambda a: jnp.pad(a.reshape(1, GDN_HEADS), ((0, 0), (0, 128 - GDN_HEADS)))
            chunk = GDN_CHUNK if seq >= GDN_CHUNK else 8
            batch_major = nb == 8 and seq % chunk == 0
            *qkvzgb, conv_new = _gdn_prep(x, g_mix, w_pad, _to_time_major(gdn_conv[j]), p['gdn_conv_w'][j],
                                          pad8(p['gdn_a_log'][j]), pad8(p['gdn_dt_bias'][j]), nb, rows,
                                          batch_major)
            nw = p['gdn_norm'][j].reshape(1, GDN_DV)
            w_out = p['gdn_w_out'][j].astype(BF16)
            if batch_major:
                o, s_new = _gdn_core(*qkvzgb, gdn_s[j], nw, chunk, 4)
                mix = (o, w_out, "seq_major")
            else:
                lp = -(-seq // chunk) * chunk

                def bm(a):
                    a = a.reshape(seq, nb, a.shape[1]).transpose(1, 0, 2)
                    return jnp.pad(a, ((0, 0), (0, lp - seq), (0, 0)))

                o, s_new = _gdn_core(*[bm(a) for a in qkvzgb], gdn_s[j], nw, chunk, 8)
                o = o[:, :seq].transpose(1, 0, 2).reshape(total, GDN_KEY_DIM).astype(BF16)
                mix = (o, w_out, "plain")
            o_gdn.append(s_new)
            o_gdn_conv.append(_from_time_major(conv_new, nb))
        x, ca, cb = _ffn(x, *mix, p['norm_ffn'][i].reshape(1, D_MODEL), p['ffn_w_up'].astype(BF16),
                         p['ffn_conv_w'][i], p['ffn_conv_b'][i].reshape(1, 2 * FFN_HIDDEN),
                         _to_time_major(ffn_conv[i]), p['ffn_w_down'].astype(BF16), i,
                         p['norm_final'].reshape(1, D_MODEL), nb, tm, i == depth - 1,
                         seq_major_io and i == 0, seq_major_io and i == depth - 1)
        hist = (FFN_CONV_WIDTH - 1) * nb
        o_ffn_conv.append(_from_time_major(jnp.concatenate([ca[-hist:], cb[-hist:]], axis=1), nb))
    return (x, jnp.stack(o_s5_re), jnp.stack(o_s5_im), jnp.stack(o_lru), jnp.stack(o_lru_conv),
            jnp.stack(o_gdn), jnp.stack(o_gdn_conv), jnp.stack(o_ffn_conv))


def kernel(x_prompt, x_sample, state_s5_re, state_s5_im, state_lru, state_lru_conv, state_gdn, state_gdn_conv, state_ffn_conv, norm_mix, norm_ffn, norm_final, s5_w_in, s5_a_re, s5_a_im, s5_log_dt, s5_b_re, s5_b_im, s5_c_re, s5_c_im, s5_d, s5_w_glu, lru_w_in, lru_conv_w, lru_conv_b, lru_w_gate_a, lru_b_gate_a, lru_w_gate_x, lru_b_gate_x, lru_lambda, lru_w_out, gdn_w_in, gdn_conv_w, gdn_a_log, gdn_dt_bias, gdn_norm, gdn_w_out, ffn_w_up, ffn_conv_w, ffn_conv_b, ffn_w_down):
    p = dict(norm_mix=norm_mix, norm_ffn=norm_ffn, norm_final=norm_final, s5_w_in=s5_w_in, s5_a_re=s5_a_re,
             s5_a_im=s5_a_im, s5_log_dt=s5_log_dt, s5_b_re=s5_b_re, s5_b_im=s5_b_im, s5_c_re=s5_c_re,
             s5_c_im=s5_c_im, s5_d=s5_d, s5_w_glu=s5_w_glu, lru_w_in=lru_w_in, lru_conv_w=lru_conv_w,
             lru_conv_b=lru_conv_b, lru_w_gate_a=lru_w_gate_a, lru_b_gate_a=lru_b_gate_a,
             lru_w_gate_x=lru_w_gate_x, lru_b_gate_x=lru_b_gate_x, lru_lambda=lru_lambda, lru_w_out=lru_w_out,
             gdn_w_in=gdn_w_in, gdn_conv_w=gdn_conv_w, gdn_a_log=gdn_a_log, gdn_dt_bias=gdn_dt_bias,
             gdn_norm=gdn_norm, gdn_w_out=gdn_w_out, ffn_w_up=ffn_w_up, ffn_conv_w=ffn_conv_w,
             ffn_conv_b=ffn_conv_b, ffn_w_down=ffn_w_down)
    p['s5_disc'] = [_s5_params(s5_a_re[j], s5_a_im[j], s5_log_dt[j], s5_b_re[j], s5_b_im[j], s5_c_re[j],
                               s5_c_im[j]) for j in range(s5_a_re.shape[0])]
    outs = []
    for x, states in (
            (x_prompt, None),
            (x_sample, (state_s5_re, state_s5_im, state_lru, state_lru_conv, state_gdn, state_gdn_conv,
                        state_ffn_conv))):
        nb, seq, _ = x.shape
        if states is None:
            states = tuple(jnp.zeros((s.shape[0], nb) + s.shape[2:], F32) for s in (
                state_s5_re, state_s5_im, state_lru, state_lru_conv, state_gdn, state_gdn_conv, state_ffn_conv))
        seq_major_io = nb == 8 and seq % 128 == 0
        if seq_major_io:
            res = _trunk(x, nb, seq, *states, p, True)
            outs.append(tuple(res))
        else:
            res = _trunk(_to_time_major(x), nb, seq, *states, p, False)
            outs.append((_from_time_major(res[0], nb),) + tuple(res[1:]))
    (y_p, *st_p), (y_s, *st_s) = outs
    return (y_p, y_s, *st_p, *st_s)
```

```python
import functools
import math

import jax
import jax.numpy as jnp
from jax import lax
from jax.experimental import pallas as pl
from jax.experimental.pallas import tpu as pltpu

F32 = jnp.float32
BF16 = jnp.bfloat16

D_MODEL = 1024
RMS_EPS = 1e-6
L2_EPS = 1e-6
S5_GROUPS = 64
S5_STATE = 64
S5_GROUP_CH = 16
S5_COLS = S5_GROUPS * S5_STATE
S5_KB = 8
S5_SCAN_LANES = 1024
LRU_WIDTH = 1280
LRU_BLOCK = 128
LRU_BLOCKS = LRU_WIDTH // LRU_BLOCK
LRU_C = 8.0
CONV_WIDTH = 4
GDN_HEADS = 8
GDN_DK = 128
GDN_DV = 128
GDN_KEY_DIM = GDN_HEADS * GDN_DK
GDN_CONV_DIM = 3 * GDN_KEY_DIM
GDN_CHUNK = 64
GDN_PROJ_PAD = 4352
FFN_HIDDEN = 2816
FFN_CONV_WIDTH = 3
FFN_TN = 256
WEIGHT_SLOTS = 3
SUB_ROWS = 256
VMEM_LIMIT_BYTES = 56 * 1024 * 1024


def _cparams(sem):
    return pltpu.CompilerParams(dimension_semantics=sem, vmem_limit_bytes=VMEM_LIMIT_BYTES)


def _rms(x, g):
    ms = jnp.mean(x * x, axis=-1, keepdims=True)
    return x * lax.rsqrt(ms + RMS_EPS) * g


def _softplus(x):
    return jnp.maximum(x, 0.0) + jnp.log1p(jnp.exp(-jnp.abs(x)))


def _stage_time_major(src_ref, slab_ref, nb):
    steps = src_ref.shape[1]
    for b in range(nb):
        for kt in range(src_ref.shape[2] // 128):
            slab_ref[kt, pl.ds(b, steps, stride=nb), :] = src_ref[b, :, kt * 128:(kt + 1) * 128]


def _gelu(x):
    c = math.sqrt(2.0 / math.pi)
    half = 0.5 * x
    return half + half * jnp.tanh(x * (c + (c * 0.044715) * (x * x)))


def _dot(a, b):
    return jnp.dot(a.astype(BF16), b.astype(BF16), preferred_element_type=F32)


def _dot_nt(a, b):
    return lax.dot_general(a.astype(BF16), b.astype(BF16), (((1,), (1,)), ((), ())),
                           preferred_element_type=F32)


def _split3(a):
    hi = a.astype(BF16)
    r = a - hi.astype(F32)
    mid = r.astype(BF16)
    lo = (r - mid.astype(F32)).astype(BF16)
    return hi, mid, lo


def _s5_core_kernel(x_ref, g_ref, win_ref, h0re_ref, h0im_ref, bre_ref, bim_ref, cre_ref, cim_ref, are_ref,
                    aim_ref, d_ref, y_ref, hre_out, him_out, xn_s, u_ref, hre_s, him_s, *stage,
                    nb, rows, seq_major_in):
    i = pl.program_id(0)

    @pl.when(i == 0)
    def _():
        hre_s[0:nb, :] = h0re_ref[...]
        him_s[0:nb, :] = h0im_ref[...]

    kw = S5_COLS // S5_KB
    uw = D_MODEL // S5_KB
    per_grp = S5_SCAN_LANES // kw
    n_grp = S5_COLS // S5_SCAN_LANES
    if seq_major_in:
        x_st, = stage
        _stage_time_major(x_ref, x_st, nb)
        x = jnp.concatenate([x_st[kt] for kt in range(D_MODEL // 128)], axis=1)
    else:
        x = x_ref[...]
    xn_s[...] = _rms(x, g_ref[...]).astype(BF16)

    def project_in(grp):
        ucols = slice(grp * per_grp * uw, (grp + 1) * per_grp * uw)
        u_ref[:, ucols] = jnp.dot(xn_s[...], win_ref[:, ucols], preferred_element_type=F32)
        for kb in range(grp * per_grp, (grp + 1) * per_grp):
            ukb = u_ref[:, kb * uw:(kb + 1) * uw].astype(BF16)
            hre_s[nb:nb + rows, kb * kw:(kb + 1) * kw] = jnp.dot(ukb, bre_ref[kb], preferred_element_type=F32)
            him_s[nb:nb + rows, kb * kw:(kb + 1) * kw] = jnp.dot(ukb, bim_ref[kb], preferred_element_type=F32)

    def scan(grp):
        cols = slice(grp * S5_SCAN_LANES, (grp + 1) * S5_SCAN_LANES)
        are = jnp.broadcast_to(are_ref[:, cols], (nb, S5_SCAN_LANES))
        aim = jnp.broadcast_to(aim_ref[:, cols], (nb, S5_SCAN_LANES))
        hr, hi = hre_s[0:nb, cols], him_s[0:nb, cols]
        for t in range(rows // nb):
            r = slice(nb + t * nb, 2 * nb + t * nb)
            hr, hi = (are * hr - aim * hi + hre_s[r, cols],
                      are * hi + aim * hr + him_s[r, cols])
            hre_s[r, cols] = hr
            him_s[r, cols] = hi

    def project_out(grp):
        for kb in range(grp * per_grp, (grp + 1) * per_grp):
            hr = hre_s[nb:nb + rows, kb * kw:(kb + 1) * kw].astype(BF16)
            hi = him_s[nb:nb + rows, kb * kw:(kb + 1) * kw].astype(BF16)
            yk = (jnp.dot(hr, cre_ref[kb], preferred_element_type=F32)
                  - jnp.dot(hi, cim_ref[kb], preferred_element_type=F32))
            yk = yk + d_ref[:, kb * uw:(kb + 1) * uw] * u_ref[:, kb * uw:(kb + 1) * uw]
            y_ref[:, kb * uw:(kb + 1) * uw] = _gelu(yk).astype(BF16)

    project_in(0)
    for grp in range(n_grp):
        if grp + 1 < n_grp:
            project_in(grp + 1)
        scan(grp)
        project_out(grp)

    last_re = hre_s[rows:rows + nb, :]
    last_im = him_s[rows:rows + nb, :]
    hre_s[0:nb, :] = last_re
    him_s[0:nb, :] = last_im
    hre_out[...] = last_re
    him_out[...] = last_im


def _s5_core(x, g, w_in, h0re, h0im, bre, bim, cre, cim, are, aim, d, nb, rows, seq_major_in=False):
    full = lambda shape: pl.BlockSpec(shape, lambda i: (0,) * len(shape))
    if seq_major_in:
        total = x.shape[0] * x.shape[1]
        x_spec = pl.BlockSpec((nb, rows // nb, D_MODEL), lambda i: (0, i, 0))
        stage = [pltpu.VMEM((D_MODEL // 128, rows, 128), F32)]
    else:
        total = x.shape[0]
        x_spec = pl.BlockSpec((rows, D_MODEL), lambda i: (i, 0))
        stage = []
    return pl.pallas_call(
        functools.partial(_s5_core_kernel, nb=nb, rows=rows, seq_major_in=seq_major_in),
        grid=(total // rows,),
        in_specs=[x_spec, full((1, D_MODEL)), full(w_in.shape),
                  full((nb, S5_COLS)), full((nb, S5_COLS)),
                  full(bre.shape), full(bim.shape), full(cre.shape), full(cim.shape),
                  full((1, S5_COLS)), full((1, S5_COLS)), full((1, D_MODEL))],
        out_specs=[pl.BlockSpec((rows, D_MODEL), lambda i: (i, 0)),
                   full((nb, S5_COLS)), full((nb, S5_COLS))],
        out_shape=[jax.ShapeDtypeStruct((total, D_MODEL), BF16),
                   jax.ShapeDtypeStruct((nb, S5_COLS), F32),
                   jax.ShapeDtypeStruct((nb, S5_COLS), F32)],
        scratch_shapes=[pltpu.VMEM((rows, D_MODEL), BF16), pltpu.VMEM((rows, D_MODEL), F32),
                        pltpu.VMEM((nb + rows, S5_COLS), F32),
                        pltpu.VMEM((nb + rows, S5_COLS), F32)] + stage,
        compiler_params=_cparams(("arbitrary",)),
        name="s5_core",
    )(x, g, w_in, h0re, h0im, bre, bim, cre, cim, are, aim, d)


def _lru_core_kernel(x_ref, g_ref, win_ref, prev_ref, h0_ref, cw_ref, cb_ref, wga_ref, bga_ref, wgx_ref,
                     bgx_ref, lam_ref, y_ref, hout_ref, cout_ref, xn_s, gate_s, xp_s, h_s, a_s, *, nb, rows):
    i = pl.program_id(0)
    hist = (CONV_WIDTH - 1) * nb

    @pl.when(i == 0)
    def _():
        xp_s[0:hist, :] = prev_ref[...]
        h_s[0:nb, :] = h0_ref[...]

    xn_s[...] = _rms(x_ref[...], g_ref[...]).astype(BF16)
    c8 = -LRU_C * _softplus(-lam_ref[...])

    def project(n):
        pg = jnp.dot(xn_s[...], win_ref[n], preferred_element_type=F32)
        sl = slice(n * LRU_BLOCK, (n + 1) * LRU_BLOCK)
        gate_s[:, sl] = pg[:, :LRU_BLOCK]
        xp_s[hist:hist + rows, sl] = pg[:, LRU_BLOCK:]

    def gates(n):
        sl = slice(n * LRU_BLOCK, (n + 1) * LRU_BLOCK)
        xcn = xp_s[0:rows, sl] * cw_ref[0:1, sl]
        for k in range(1, CONV_WIDTH):
            xcn = xcn + xp_s[k * nb:k * nb + rows, sl] * cw_ref[k:k + 1, sl]
        xcn = xcn + cb_ref[:, sl]
        xcb = xcn.astype(BF16)
        r = jax.nn.sigmoid(jnp.dot(xcb, wga_ref[n], preferred_element_type=F32) + bga_ref[:, sl])
        ig = jax.nn.sigmoid(jnp.dot(xcb, wgx_ref[n], preferred_element_type=F32) + bgx_ref[:, sl])
        log_a = c8[:, sl] * r
        a_s[:, sl] = jnp.exp(log_a)
        t = jnp.tanh(log_a)
        h_s[nb:nb + rows, sl] = jnp.sqrt(-2.0 * t / (1.0 - t)) * ig * xcn

    project(0)
    for n in range(LRU_BLOCKS):
        if n + 1 < LRU_BLOCKS:
            project(n + 1)
        gates(n)

    def step(t, carry):
        r0 = pl.multiple_of(t * nb, nb)
        r1 = pl.multiple_of(t * nb + nb, nb)
        h_s[pl.ds(r1, nb), :] = a_s[pl.ds(r0, nb), :] * h_s[pl.ds(r0, nb), :] + h_s[pl.ds(r1, nb), :]
        return carry

    lax.fori_loop(0, rows // nb, step, 0)

    y_ref[...] = (_gelu(gate_s[...]) * h_s[nb:nb + rows, :]).astype(BF16)
    tail = xp_s[rows:rows + hist, :]
    last = h_s[rows:rows + nb, :]
    xp_s[0:hist, :] = tail
    h_s[0:nb, :] = last
    cout_ref[...] = tail
    hout_ref[...] = last


def _lru_core(x, g, w_in, prev, h0, cw, cb, wga, bga, wgx, bgx, lam, nb, rows):
    total = x.shape[0]
    hist = (CONV_WIDTH - 1) * nb
    full = lambda shape: pl.BlockSpec(shape, lambda i: (0,) * len(shape))
    return pl.pallas_call(
        functools.partial(_lru_core_kernel, nb=nb, rows=rows),
        grid=(total // rows,),
        in_specs=[pl.BlockSpec((rows, D_MODEL), lambda i: (i, 0)), full((1, D_MODEL)),
                  pl.BlockSpec(w_in.shape, lambda i: (0, 0, 0), pipeline_mode=pl.Buffered(1)),
                  full((hist, LRU_WIDTH)), full((nb, LRU_WIDTH)),
                  full((CONV_WIDTH, LRU_WIDTH)), full((1, LRU_WIDTH)),
                  full(wga.shape), full((1, LRU_WIDTH)), full(wgx.shape), full((1, LRU_WIDTH)),
                  full((1, LRU_WIDTH))],
        out_specs=[pl.BlockSpec((rows, LRU_WIDTH), lambda i: (i, 0)),
                   full((nb, LRU_WIDTH)), full((hist, LRU_WIDTH))],
        out_shape=[jax.ShapeDtypeStruct((total, LRU_WIDTH), BF16),
                   jax.ShapeDtypeStruct((nb, LRU_WIDTH), F32),
                   jax.ShapeDtypeStruct((hist, LRU_WIDTH), F32)],
        scratch_shapes=[pltpu.VMEM((rows, D_MODEL), BF16),
                        pltpu.VMEM((rows, LRU_WIDTH), F32),
                        pltpu.VMEM((hist + rows, LRU_WIDTH), F32),
                        pltpu.VMEM((nb + rows, LRU_WIDTH), F32),
                        pltpu.VMEM((rows, LRU_WIDTH), F32)],
        compiler_params=_cparams(("arbitrary",)),
        name="lru_core",
    )(x, g, w_in, prev, h0, cw, cb, wga, bga, wgx, bgx, lam)


def _gdn_prep_kernel(x_ref, gn_ref, win_ref, prev_ref, cw_ref, alog_ref, dtb_ref,
                     q_ref, k_ref, v_ref, zo_ref, g_ref, beta_ref, cout_ref, xn_s, xp_s, ab_s, st_s,
                     *, nb, rows, batch_major):
    i = pl.program_id(0)
    hist = (CONV_WIDTH - 1) * nb
    n_qkv = 3 * GDN_HEADS
    n_z = GDN_HEADS

    @pl.when(i == 0)
    def _():
        xp_s[0:hist, :] = prev_ref[...]

    xn_s[...] = _rms(x_ref[...], gn_ref[...]).astype(BF16)

    def project(p):
        pg = jnp.dot(xn_s[...], win_ref[p], preferred_element_type=F32)
        for half in range(2):
            s = 2 * p + half
            col = pg[:, half * 128:(half + 1) * 128]
            if s < n_qkv:
                xp_s[hist:hist + rows, s * 128:(s + 1) * 128] = col
            elif s < n_qkv + n_z:
                if batch_major:
                    st_s[s] = col
                else:
                    zo_ref[:, (s - n_qkv) * 128:(s - n_qkv + 1) * 128] = col
            elif s == n_qkv + n_z:
                ab_s[...] = col

    def activate(s):
        part, h = divmod(s, GDN_HEADS)
        sl = slice(s * 128, (s + 1) * 128)
        acc = xp_s[0:rows, sl] * cw_ref[0:1, sl]
        for k in range(1, CONV_WIDTH):
            acc = acc + xp_s[k * nb:k * nb + rows, sl] * cw_ref[k:k + 1, sl]
        y = acc * jax.nn.sigmoid(acc)
        if part < 2:
            y = y * lax.rsqrt(jnp.sum(y * y, axis=-1, keepdims=True) + L2_EPS)
        if part == 0:
            y = y * (GDN_DK ** -0.5)
        if batch_major:
            st_s[s] = y
        else:
            (q_ref, k_ref, v_ref)[part][:, h * GDN_DK:(h + 1) * GDN_DK] = y

    n_pairs = win_ref.shape[0]
    project(0)
    for p in range(n_pairs):
        if p + 1 < n_pairs:
            project(p + 1)
        for s in (2 * p, 2 * p + 1):
            if s < n_qkv:
                activate(s)

    ab = ab_s[...]
    g = -jnp.exp(alog_ref[...]) * _softplus(ab + dtb_ref[...])
    beta = jax.nn.sigmoid(ab)
    if batch_major:
        st_s[4 * GDN_HEADS] = g
        st_s[4 * GDN_HEADS + 1] = beta
        steps = rows // nb
        for b in range(nb):
            pick = pl.ds(b, steps, stride=nb)
            for part, out in enumerate((q_ref, k_ref, v_ref, zo_ref)):
                for h in range(GDN_HEADS):
                    out[b, :, h * GDN_DK:(h + 1) * GDN_DK] = st_s[part * GDN_HEADS + h, pick, :]
            g_ref[b] = st_s[4 * GDN_HEADS, pick, :]
            beta_ref[b] = st_s[4 * GDN_HEADS + 1, pick, :]
    else:
        g_ref[...] = g
        beta_ref[...] = beta
    tail = xp_s[rows:rows + hist, :]
    xp_s[0:hist, :] = tail
    cout_ref[...] = tail


def _gdn_prep(x, gn, w_in, prev, cw, alog, dtb, nb, rows, batch_major):
    total = x.shape[0]
    hist = (CONV_WIDTH - 1) * nb
    full = lambda shape: pl.BlockSpec(shape, lambda i: (0,) * len(shape))
    tile = lambda n: pl.BlockSpec((rows, n), lambda i: (i, 0))
    if batch_major:
        out_tile = lambda n: pl.BlockSpec((nb, rows // nb, n), lambda i: (0, i, 0))
        out_sds = lambda n: jax.ShapeDtypeStruct((nb, total // nb, n), F32)
    else:
        out_tile = tile
        out_sds = lambda n: jax.ShapeDtypeStruct((total, n), F32)
    widths = (GDN_KEY_DIM,) * 4 + (128, 128)
    return pl.pallas_call(
        functools.partial(_gdn_prep_kernel, nb=nb, rows=rows, batch_major=batch_major),
        grid=(total // rows,),
        in_specs=[tile(D_MODEL), full((1, D_MODEL)),
                  pl.BlockSpec(w_in.shape, lambda i: (0, 0, 0), pipeline_mode=pl.Buffered(1)),
                  full((hist, GDN_CONV_DIM)), full((CONV_WIDTH, GDN_CONV_DIM)),
                  full((1, 128)), full((1, 128))],
        out_specs=[out_tile(n) for n in widths] + [full((hist, GDN_CONV_DIM))],
        out_shape=[out_sds(n) for n in widths] + [jax.ShapeDtypeStruct((hist, GDN_CONV_DIM), F32)],
        scratch_shapes=[pltpu.VMEM((rows, D_MODEL), BF16),
                        pltpu.VMEM((hist + rows, GDN_CONV_DIM), F32),
                        pltpu.VMEM((rows, 128), F32),
                        pltpu.VMEM((4 * GDN_HEADS + 2, rows if batch_major else 8, 128), F32)],
        compiler_params=_cparams(("arbitrary",)),
        name="gdn_prep",
    )(x, gn, w_in, prev, cw, alog, dtb)


def _unit_lower_inverses(ms, ri, ci, chunk):
    eye = (ri == ci).astype(F32)
    blk = (ri >> 3) == (ci >> 3)
    n1 = [jnp.where(blk, -m, 0.0) for m in ms]
    n2 = [_dot(a, a) for a in n1]
    n4 = [_dot(a, a) for a in n2]
    ts = [_dot(eye + a, eye + b) for a, b in zip(n1, n2)]
    ts = [_dot(t, eye + a) for t, a in zip(ts, n4)]
    shift = 3
    while (1 << shift) < chunk:
        pair = ((ri >> (shift + 1)) == (ci >> (shift + 1))) & ((ri >> shift) != (ci >> shift))
        left = [_dot(t, jnp.where(pair, m, 0.0)) for t, m in zip(ts, ms)]
        ts = [t - _dot(a, t) for t, a in zip(ts, left)]
        shift += 1
    return ts


def _gdn_core_kernel(q_ref, k_ref, v_ref, z_ref, g_ref, beta_ref, s0_ref, nw_ref, o_ref, s_ref, *, chunk, bb):
    c = pl.program_id(1)

    @pl.when(c == 0)
    def _():
        s_ref[...] = s0_ref[...]

    ri = lax.broadcasted_iota(jnp.int32, (chunk, chunk), 0)
    ci = lax.broadcasted_iota(jnp.int32, (chunk, chunk), 1)
    causal = ri >= ci
    strict = ri > ci
    tril = causal.astype(BF16)
    e_r = lax.broadcasted_iota(jnp.int32, (128, 128), 0)
    e_c = lax.broadcasted_iota(jnp.int32, (128, 128), 1)
    eye128 = (e_r == e_c).astype(BF16)
    dotf = functools.partial(jnp.dot, preferred_element_type=F32)
    nt = lambda a, b: lax.dot_general(a, b, (((1,), (1,)), ((), ())), preferred_element_type=F32)
    nw = nw_ref[...]

    cums, cum_ts, ecums, e_lasts, e_rests, betas = [], [], [], [], [], []
    for bi in range(bb):
        g3 = _split3(g_ref[bi])
        cum = dotf(tril, g3[0]) + dotf(tril, g3[1]) + dotf(tril, g3[2])
        c3 = _split3(cum)
        cums.append(cum)
        cum_ts.append(nt(eye128, c3[0]) + nt(eye128, c3[1]) + nt(eye128, c3[2]))
        ecums.append(jnp.exp(cum))
        g_last = cum[chunk - 1:chunk, :]
        e_lasts.append(jnp.exp(g_last))
        e_rests.append(jnp.exp(g_last - cum))
        betas.append(beta_ref[bi])

    units = [(bi, h) for bi in range(bb) for h in range(GDN_HEADS)]
    col = lambda a, h: a[:, h:h + 1]
    sl = lambda h: slice(h * GDN_DK, (h + 1) * GDN_DK)
    q = [q_ref[bi, :, sl(h)] for bi, h in units]
    k = [k_ref[bi, :, sl(h)] for bi, h in units]
    decay = [jnp.exp(jnp.where(causal, col(cums[bi], h) - cum_ts[bi][h:h + 1, :], -jnp.inf)) for bi, h in units]
    k_beta = [kk * col(betas[bi], GDN_HEADS + h) for kk, (bi, h) in zip(k, units)]
    ak = [_dot_nt(jnp.concatenate([kb, qq], axis=0), kk) for kb, qq, kk in zip(k_beta, q, k)]
    ms = [jnp.where(strict, a[:chunk] * d, 0.0) for a, d in zip(ak, decay)]
    ts = _unit_lower_inverses(ms, ri, ci, chunk)
    rhs = [jnp.concatenate([v_ref[bi, :, sl(h)] * col(betas[bi], GDN_HEADS + h), kb * col(ecums[bi], h)], axis=1)
           for kb, (bi, h) in zip(k_beta, units)]
    sol = [_dot(t, r) for t, r in zip(ts, rhs)]
    s_old = [s_ref[bi, h] for bi, h in units]
    ws = [_dot(jnp.concatenate([so[:, GDN_DV:], qq * col(ecums[bi], h)], axis=0), s)
          for so, qq, s, (bi, h) in zip(sol, q, s_old, units)]
    v_new = [so[:, :GDN_DV] - w[:chunk] for so, w in zip(sol, ws)]
    o = [w[chunk:] + _dot(a[chunk:] * d, vn) for w, a, d, vn in zip(ws, ak, decay, v_new)]
    k_dec_t = [nt(eye128, (kk * col(e_rests[bi], h)).astype(BF16)) for kk, (bi, h) in zip(k, units)]
    for (bi, h), s, kt, vn, oo in zip(units, s_old, k_dec_t, v_new, o):
        s_ref[bi, h] = s * col(e_lasts[bi], h) + _dot(kt, vn)
        on = oo * lax.rsqrt(jnp.mean(oo * oo, axis=-1, keepdims=True) + RMS_EPS) * nw
        zh = z_ref[bi, :, sl(h)]
        o_ref[bi, :, sl(h)] = on * (zh * jax.nn.sigmoid(zh))


def _gdn_core(q, k, v, z, g, beta, s0, nw, chunk, bb):
    nb, lp = q.shape[0], q.shape[1]
    seq = lambda n: pl.BlockSpec((bb, chunk, n), lambda b, c: (b, c, 0))
    st = pl.BlockSpec((bb, GDN_HEADS, GDN_DK, GDN_DV), lambda b, c: (b, 0, 0, 0))
    return pl.pallas_call(
        functools.partial(_gdn_core_kernel, chunk=chunk, bb=bb),
        grid=(nb // bb, lp // chunk),
        in_specs=[seq(GDN_KEY_DIM), seq(GDN_KEY_DIM), seq(GDN_KEY_DIM), seq(GDN_KEY_DIM),
                  seq(128), seq(128), st, pl.BlockSpec((1, GDN_DV), lambda b, c: (0, 0))],
        out_specs=[seq(GDN_KEY_DIM), st],
        out_shape=[jax.ShapeDtypeStruct((nb, lp, GDN_KEY_DIM), F32),
                   jax.ShapeDtypeStruct(s0.shape, F32)],
        compiler_params=_cparams(("parallel", "arbitrary")),
        name="gdn_core",
    )(q, k, v, z, g, beta, s0, nw)


def _ffn_kernel(r_ref, a_ref, wo_ref, g_ref, wup_hbm, cwa_ref, cwb_ref, cba_ref, cbb_ref, pa_ref, pb_ref,
                wd_ref, gf_ref, o_ref, ca_out, cb_out, x_s, xn_s, acc_s, hpa_s, hpb_s, cara_s, carb_s,
                wbuf, wsem, *stage,
                nb, tm, final_norm, mixer_out, res_seq_major, out_seq_major, layer):
    i = pl.program_id(0)
    j = pl.program_id(1)
    nj = pl.num_programs(1)
    n_steps = pl.num_programs(0) * nj
    step = i * nj + j
    tn = wbuf.shape[3]

    def w_copy(s, half):
        col = pl.multiple_of((half * nj + s % nj) * tn, tn)
        slot = s % WEIGHT_SLOTS
        return pltpu.make_async_copy(wup_hbm.at[layer, :, pl.ds(col, tn)], wbuf.at[half, slot],
                                     wsem.at[half, slot])

    @pl.when(step == 0)
    def _():
        for s in range(WEIGHT_SLOTS - 1):
            w_copy(s, 0).start()
            w_copy(s, 1).start()

    @pl.when(step + WEIGHT_SLOTS - 1 < n_steps)
    def _():
        w_copy(step + WEIGHT_SLOTS - 1, 0).start()
        w_copy(step + WEIGHT_SLOTS - 1, 1).start()

    w_copy(step, 0).wait()
    w_copy(step, 1).wait()
    w_slot = step % WEIGHT_SLOTS
    hist = (FFN_CONV_WIDTH - 1) * nb
    rs = min(SUB_ROWS, tm)
    nsub = tm // rs
    sub = lambda r: slice(r * rs, (r + 1) * rs)
    stage = list(stage)
    a_s = stage.pop(0) if mixer_out == "seq_major" else None
    io_s = stage.pop(0) if (res_seq_major or out_seq_major) else None
    nk = D_MODEL // 128

    @pl.when(j == 0)
    def _():
        if mixer_out == "seq_major":
            _stage_time_major(a_ref, a_s, nb)
        if res_seq_major:
            _stage_time_major(r_ref, io_s, nb)

        def project(r):
            if mixer_out == "seq_major":
                a = jnp.concatenate([a_s[kt, sub(r), :].astype(BF16) for kt in range(nk)], axis=1)
            else:
                a = a_ref[sub(r), :]
            if res_seq_major:
                res = jnp.concatenate([io_s[kt, sub(r), :] for kt in range(nk)], axis=1)
            else:
                res = r_ref[sub(r), :]
            if mixer_out == "glu":
                half = wo_ref.shape[1] // 2
                val = jnp.dot(a, wo_ref[:, :half], preferred_element_type=F32)
                gate = jnp.dot(a, wo_ref[:, half:], preferred_element_type=F32)
                y = val * jax.nn.sigmoid(gate)
            else:
                y = jnp.dot(a, wo_ref[...], preferred_element_type=F32)
            x_s[sub(r), :] = res + y

        project(0)
        for r in range(nsub):
            if r + 1 < nsub:
                project(r + 1)
            xn_s[sub(r), :] = _rms(x_s[sub(r), :], g_ref[...]).astype(BF16)
        acc_s[...] = jnp.zeros_like(acc_s)

    @pl.when(i == 0)
    def _():
        hpa_s[0:hist, :] = pa_ref[...]
        hpb_s[0:hist, :] = pb_ref[...]

    @pl.when(i > 0)
    def _():
        hpa_s[0:hist, :] = cara_s[j]
        hpb_s[0:hist, :] = carb_s[j]

    def up(r):
        xr = xn_s[r * rs:(r + 1) * rs, :]
        hpa_s[hist + r * rs:hist + (r + 1) * rs, :] = jnp.dot(xr, wbuf[0, w_slot], preferred_element_type=F32)
        hpb_s[hist + r * rs:hist + (r + 1) * rs, :] = jnp.dot(xr, wbuf[1, w_slot], preferred_element_type=F32)

    def conv(hp_s, cw_ref, cb_ref, r):
        y = hp_s[r * rs:(r + 1) * rs, :] * cw_ref[0:1, :]
        for k in range(1, FFN_CONV_WIDTH):
            y = y + hp_s[k * nb + r * rs:k * nb + (r + 1) * rs, :] * cw_ref[k:k + 1, :]
        return y + cb_ref[...]

    def down(r):
        act = (_gelu(conv(hpa_s, cwa_ref, cba_ref, r)) * conv(hpb_s, cwb_ref, cbb_ref, r)).astype(BF16)
        acc_s[r * rs:(r + 1) * rs, :] += jnp.dot(act, wd_ref[...], preferred_element_type=F32)

    up(0)
    for r in range(nsub):
        if r + 1 < nsub:
            up(r + 1)
        down(r)

    tail_a = hpa_s[tm:tm + hist, :]
    tail_b = hpb_s[tm:tm + hist, :]
    cara_s[j] = tail_a
    carb_s[j] = tail_b
    ca_out[...] = tail_a
    cb_out[...] = tail_b

    @pl.when(j == pl.num_programs(1) - 1)
    def _():
        y = x_s[...] + acc_s[...]
        if final_norm:
            y = _rms(y, gf_ref[...])
        if out_seq_major:
            for kt in range(nk):
                io_s[kt] = y[:, kt * 128:(kt + 1) * 128]
            for b in range(nb):
                for kt in range(nk):
                    o_ref[b, :, kt * 128:(kt + 1) * 128] = io_s[kt, pl.ds(b, tm // nb, stride=nb), :]
        else:
            o_ref[...] = y


def _ffn(res, a, w_out, mixer_out, g, w_up, cw, cb, prev, w_down, layer, g_final, nb, tm, final_norm,
         res_seq_major=False, out_seq_major=False):
    rows = res.shape[0] * res.shape[1] if res_seq_major else res.shape[0]
    row_tile = pl.BlockSpec((tm, D_MODEL), lambda i, j: (i, 0))
    seq_tile = pl.BlockSpec((nb, tm // nb, D_MODEL), lambda i, j: (0, i, 0))
    tn = FFN_TN
    nj = FFN_HIDDEN // tn
    hist = (FFN_CONV_WIDTH - 1) * nb
    col_a = lambda r: pl.BlockSpec((r, tn), lambda i, j: (0, j))
    col_b = lambda r: pl.BlockSpec((r, tn), lambda i, j: (0, nj + j))
    vec = pl.BlockSpec((1, D_MODEL), lambda i, j: (0, 0))
    k = w_out.shape[0]
    if mixer_out == "seq_major":
        a_spec = pl.BlockSpec((nb, tm // nb, k), lambda i, j: (0, i, 0))
        stage = [pltpu.VMEM((k // 128, tm, 128), F32)]
    else:
        a_spec = pl.BlockSpec((tm, k), lambda i, j: (i, 0))
        stage = []
    if res_seq_major or out_seq_major:
        stage = stage + [pltpu.VMEM((D_MODEL // 128, tm, 128), F32)]
    out_sds = (jax.ShapeDtypeStruct((nb, rows // nb, D_MODEL), F32) if out_seq_major
               else jax.ShapeDtypeStruct((rows, D_MODEL), F32))
    return pl.pallas_call(
        functools.partial(_ffn_kernel, nb=nb, tm=tm, final_norm=final_norm, mixer_out=mixer_out,
                          res_seq_major=res_seq_major, out_seq_major=out_seq_major, layer=layer),
        grid=(rows // tm, nj),
        in_specs=[seq_tile if res_seq_major else row_tile, a_spec,
                  pl.BlockSpec(w_out.shape, lambda i, j: (0, 0)), vec,
                  pl.BlockSpec(memory_space=pl.ANY),
                  col_a(FFN_CONV_WIDTH), col_b(FFN_CONV_WIDTH), col_a(1), col_b(1),
                  col_a(hist), col_b(hist),
                  pl.BlockSpec((None, tn, D_MODEL), lambda i, j: (layer, j, 0)), vec],
        out_specs=[seq_tile if out_seq_major else row_tile,
                   pl.BlockSpec((hist, tn), lambda i, j: (i, j)),
                   pl.BlockSpec((hist, tn), lambda i, j: (i, j))],
        out_shape=[out_sds,
                   jax.ShapeDtypeStruct((rows // tm * hist, FFN_HIDDEN), F32),
                   jax.ShapeDtypeStruct((rows // tm * hist, FFN_HIDDEN), F32)],
        scratch_shapes=[pltpu.VMEM((tm, D_MODEL), F32), pltpu.VMEM((tm, D_MODEL), BF16),
                        pltpu.VMEM((tm, D_MODEL), F32),
                        pltpu.VMEM((hist + tm, tn), F32), pltpu.VMEM((hist + tm, tn), F32),
                        pltpu.VMEM((nj, hist, tn), F32), pltpu.VMEM((nj, hist, tn), F32),
                        pltpu.VMEM((2, WEIGHT_SLOTS, D_MODEL, tn), BF16),
                        pltpu.SemaphoreType.DMA((2, WEIGHT_SLOTS))] + stage,
        compiler_params=_cparams(("arbitrary", "arbitrary")),
        name="conv_ffn",
    )(res, a, w_out, g, w_up, cw, cw, cb, cb, prev, prev, w_down, g_final)


def _s5_disc_kernel(are_ref, aim_ref, ldt_ref, bre_ref, bim_ref, abr_ref, abi_ref, bbr_ref, bbi_ref):
    a_re, a_im = are_ref[...], aim_ref[...]
    dt = jnp.exp(ldt_ref[...])
    mag = jnp.exp(a_re * dt)
    ar = mag * jnp.cos(a_im * dt)
    ai = mag * jnp.sin(a_im * dt)
    den = a_re * a_re + a_im * a_im
    nr = ar - 1.0
    cr = (nr * a_re + ai * a_im) / den
    ci = (ai * a_re - nr * a_im) / den
    b_re, b_im = bre_ref[...], bim_ref[...]
    abr_ref[...] = ar
    abi_ref[...] = ai
    bbr_ref[...] = cr * b_re - ci * b_im
    bbi_ref[...] = cr * b_im + ci * b_re


def _s5_params(a_re, a_im, log_dt, b_re, b_im, c_re, c_im):
    rep = lambda a: jnp.repeat(a.astype(F32), S5_GROUP_CH, axis=0)
    rows_gc = lambda b: b.astype(F32).transpose(0, 2, 1).reshape(D_MODEL, S5_STATE)
    ldt = jnp.broadcast_to(log_dt.astype(F32)[:, None], (S5_GROUPS, S5_STATE))
    sds = jax.ShapeDtypeStruct((D_MODEL, S5_STATE), F32)
    abr, abi, bbr, bbi = pl.pallas_call(_s5_disc_kernel, out_shape=[sds] * 4, name="s5_discretize")(
        rep(a_re), rep(a_im), rep(ldt), rows_gc(b_re), rows_gc(b_im))
    eye = jnp.eye(S5_GROUPS // S5_KB, dtype=F32)

    def b_blocks(b):
        b = b.reshape(S5_KB, S5_GROUPS // S5_KB, S5_GROUP_CH, S5_STATE)
        return jnp.einsum('kgcp,gh->kgchp', b, eye).reshape(S5_KB, D_MODEL // S5_KB, S5_COLS // S5_KB).astype(BF16)

    def c_blocks(c):
        c = c.astype(F32).reshape(S5_KB, S5_GROUPS // S5_KB, S5_GROUP_CH, S5_STATE)
        return jnp.einsum('kgcp,gh->kgphc', c, eye).reshape(S5_KB, S5_COLS // S5_KB, D_MODEL // S5_KB).astype(BF16)

    return (b_blocks(bbr), b_blocks(bbi), c_blocks(c_re), c_blocks(c_im),
            abr[::S5_GROUP_CH].reshape(1, S5_COLS), abi[::S5_GROUP_CH].reshape(1, S5_COLS))


def _to_time_major(a):
    return a.transpose(1, 0, 2).reshape(a.shape[0] * a.shape[1], a.shape[2])


def _from_time_major(a, nb):
    return a.reshape(a.shape[0] // nb, nb, a.shape[1]).transpose(1, 0, 2)


def _trunk(x, nb, seq, s5_re, s5_im, lru_h, lru_conv, gdn_s, gdn_conv, ffn_conv, p, seq_major_io):
    total = seq * nb
    tm = min(total, 1024)
    rows = 512
    o_s5_re, o_s5_im, o_lru, o_lru_conv, o_gdn, o_gdn_conv, o_ffn_conv = [], [], [], [], [], [], []
    depth = p['norm_mix'].shape[0]
    for i in range(depth):
        kind, j = i % 3, i // 3
        g_mix = p['norm_mix'][i].reshape(1, D_MODEL)
        if kind == 0:
            bre, bim, cre, cim, are, aim = p['s5_disc'][j]
            y, hre, him = _s5_core(x, g_mix, p['s5_w_in'][j].astype(BF16),
                                   s5_re[j].reshape(nb, S5_COLS), s5_im[j].reshape(nb, S5_COLS),
                                   bre, bim, cre, cim, are, aim, p['s5_d'][j].reshape(1, D_MODEL), nb, rows,
                                   seq_major_io and i == 0)
            mix = (y, p['s5_w_glu'][j].astype(BF16), "glu")
            o_s5_re.append(hre.reshape(nb, S5_GROUPS, S5_STATE))
            o_s5_im.append(him.reshape(nb, S5_GROUPS, S5_STATE))
        elif kind == 1:
            w_in = p['lru_w_in'][j].astype(BF16).reshape(D_MODEL, 2, LRU_BLOCKS, LRU_BLOCK)
            w_in = w_in.transpose(2, 0, 1, 3).reshape(LRU_BLOCKS, D_MODEL, 2 * LRU_BLOCK)
            y, h_new, conv_new = _lru_core(
                x, g_mix, w_in, _to_time_major(lru_conv[j]), lru_h[j],
                p['lru_conv_w'][j], p['lru_conv_b'][j].reshape(1, LRU_WIDTH),
                p['lru_w_gate_a'][j].astype(BF16), p['lru_b_gate_a'][j].reshape(1, LRU_WIDTH),
                p['lru_w_gate_x'][j].astype(BF16), p['lru_b_gate_x'][j].reshape(1, LRU_WIDTH),
                p['lru_lambda'][j].reshape(1, LRU_WIDTH), nb, rows)
            mix = (y, p['lru_w_out'][j].astype(BF16), "plain")
            o_lru.append(h_new)
            o_lru_conv.append(_from_time_major(conv_new, nb))
        else:
            w_in = p['gdn_w_in'][j]
            w_pad = jnp.pad(w_in, ((0, 0), (0, GDN_PROJ_PAD - w_in.shape[1]))).astype(BF16)
            w_pad = w_pad.reshape(D_MODEL, GDN_PROJ_PAD // 256, 256).transpose(1, 0, 2)
            pad8 = lambda a: jnp.pad(a.reshape(1, GDN_HEADS), ((0, 0), (0, 128 - GDN_HEADS)))
            chunk = GDN_CHUNK if seq >= GDN_CHUNK else 8
            batch_major = nb == 8 and seq % chunk == 0
            *qkvzgb, conv_new = _gdn_prep(x, g_mix, w_pad, _to_time_major(gdn_conv[j]), p['gdn_conv_w'][j],
                                          pad8(p['gdn_a_log'][j]), pad8(p['gdn_dt_bias'][j]), nb, rows,
                                          batch_major)
            nw = p['gdn_norm'][j].reshape(1, GDN_DV)
            w_out = p['gdn_w_out'][j].astype(BF16)
            if batch_major:
                o, s_new = _gdn_core(*qkvzgb, gdn_s[j], nw, chunk, 4)
                mix = (o, w_out, "seq_major")
            else:
                lp = -(-seq // chunk) * chunk

                def bm(a):
                    a = a.reshape(seq, nb, a.shape[1]).transpose(1, 0, 2)
                    return jnp.pad(a, ((0, 0), (0, lp - seq), (0, 0)))

                o, s_new = _gdn_core(*[bm(a) for a in qkvzgb], gdn_s[j], nw, chunk, 8)
                o = o[:, :seq].transpose(1, 0, 2).reshape(total, GDN_KEY_DIM).astype(BF16)
                mix = (o, w_out, "plain")
            o_gdn.append(s_new)
            o_gdn_conv.append(_from_time_major(conv_new, nb))
        x, ca, cb = _ffn(x, *mix, p['norm_ffn'][i].reshape(1, D_MODEL), p['ffn_w_up'].astype(BF16),
                         p['ffn_conv_w'][i], p['ffn_conv_b'][i].reshape(1, 2 * FFN_HIDDEN),
                         _to_time_major(ffn_conv[i]), p['ffn_w_down'].astype(BF16), i,
                         p['norm_final'].reshape(1, D_MODEL), nb, tm, i == depth - 1,
                         seq_major_io and i == 0, seq_major_io and i == depth - 1)
        hist = (FFN_CONV_WIDTH - 1) * nb
        o_ffn_conv.append(_from_time_major(jnp.concatenate([ca[-hist:], cb[-hist:]], axis=1), nb))
    return (x, jnp.stack(o_s5_re), jnp.stack(o_s5_im), jnp.stack(o_lru), jnp.stack(o_lru_conv),
            jnp.stack(o_gdn), jnp.stack(o_gdn_conv), jnp.stack(o_ffn_conv))


def kernel(x_prompt, x_sample, state_s5_re, state_s5_im, state_lru, state_lru_conv, state_gdn, state_gdn_conv, state_ffn_conv, norm_mix, norm_ffn, norm_final, s5_w_in, s5_a_re, s5_a_im, s5_log_dt, s5_b_re, s5_b_im, s5_c_re, s5_c_im, s5_d, s5_w_glu, lru_w_in, lru_conv_w, lru_conv_b, lru_w_gate_a, lru_b_gate_a, lru_w_gate_x, lru_b_gate_x, lru_lambda, lru_w_out, gdn_w_in, gdn_conv_w, gdn_a_log, gdn_dt_bias, gdn_norm, gdn_w_out, ffn_w_up, ffn_conv_w, ffn_conv_b, ffn_w_down):
    p = dict(norm_mix=norm_mix, norm_ffn=norm_ffn, norm_final=norm_final, s5_w_in=s5_w_in, s5_a_re=s5_a_re,
             s5_a_im=s5_a_im, s5_log_dt=s5_log_dt, s5_b_re=s5_b_re, s5_b_im=s5_b_im, s5_c_re=s5_c_re,
             s5_c_im=s5_c_im, s5_d=s5_d, s5_w_glu=s5_w_glu, lru_w_in=lru_w_in, lru_conv_w=lru_conv_w,
             lru_conv_b=lru_conv_b, lru_w_gate_a=lru_w_gate_a, lru_b_gate_a=lru_b_gate_a,
             lru_w_gate_x=lru_w_gate_x, lru_b_gate_x=lru_b_gate_x, lru_lambda=lru_lambda, lru_w_out=lru_w_out,
             gdn_w_in=gdn_w_in, gdn_conv_w=gdn_conv_w, gdn_a_log=gdn_a_log, gdn_dt_bias=gdn_dt_bias,
             gdn_norm=gdn_norm, gdn_w_out=gdn_w_out, ffn_w_up=ffn_w_up, ffn_conv_w=ffn_conv_w,
             ffn_conv_b=ffn_conv_b, ffn_w_down=ffn_w_down)
    p['s5_disc'] = [_s5_params(s5_a_re[j], s5_a_im[j], s5_log_dt[j], s5_b_re[j], s5_b_im[j], s5_c_re[j],
                               s5_c_im[j]) for j in range(s5_a_re.shape[0])]
    outs = []
    for x, states in (
            (x_prompt, None),
            (x_sample, (state_s5_re, state_s5_im, state_lru, state_lru_conv, state_gdn, state_gdn_conv,
                        state_ffn_conv))):
        nb, seq, _ = x.shape
        if states is None:
            states = tuple(jnp.zeros((s.shape[0], nb) + s.shape[2:], F32) for s in (
                state_s5_re, state_s5_im, state_lru, state_lru_conv, state_gdn, state_gdn_conv, state_ffn_conv))
        seq_major_io = nb == 8 and seq % 128 == 0
        if seq_major_io:
            res = _trunk(x, nb, seq, *states, p, True)
            outs.append(tuple(res))
        else:
            res = _trunk(_to_time_major(x), nb, seq, *states, p, False)
            outs.append((_from_time_major(res[0], nb),) + tuple(res[1:]))
    (y_p, *st_p), (y_s, *st_s) = outs
    return (y_p, y_s, *st_p, *st_s)
```
